```python
import math
import jax, jax.numpy as jnp
from jax import lax
import numpy as np

D_MODEL = 2048
BATCH = 1
SEQ = 16384
DEPTH = 2
DEC_BATCH = 8
DEC_SEQ = 16
PAST_LEN = 1024

CHUNK = 64
ATTN_HEADS = 16
ATTN_KV_HEADS = 2
ATTN_HEAD_DIM = 64
ATTN_GROUP = ATTN_HEADS // ATTN_KV_HEADS
ATTN_WIDTH = ATTN_HEADS * ATTN_HEAD_DIM
KV_WIDTH = ATTN_KV_HEADS * ATTN_HEAD_DIM
WINDOW = 128
N_BAND = WINDOW // CHUNK + 1
ROPE_THETA = 500000.0
ROPE_DIM = ATTN_HEAD_DIM // 4
DN_HEADS = 8
DN_KEY_DIM = 128
DN_VAL_DIM = 128
DN_WIDTH = DN_HEADS * DN_VAL_DIM
CONV_WIDTH = 4
DN_CONV_CH = DN_HEADS * (2 * DN_KEY_DIM + DN_VAL_DIM)
MIX_WIDTH = ATTN_WIDTH + DN_WIDTH
PROJ_WIDTH = ATTN_WIDTH + 2 * KV_WIDTH + DN_CONV_CH + DN_WIDTH + 2 * DN_HEADS
D_FF = 5632
N_EXPERTS = 8
TOP_K = 2
D_FF_EXPERT = 2816
N_DENSE = (DEPTH + 1) // 2
N_MOE = DEPTH // 2
EPS = 1e-6

kernel_name = 'hybrid_swa_sink_gated_deltanet_streaming_step'


def _rmsnorm(x, w):
    x32 = x.astype(jnp.float32)
    y = x32 * lax.rsqrt(jnp.mean(x32 * x32, -1, keepdims=True) + EPS)
    return (y * w.astype(jnp.float32)).astype(x.dtype)


def _l2norm(x):
    x32 = x.astype(jnp.float32)
    return x32 * lax.rsqrt(jnp.sum(x32 * x32, -1, keepdims=True) + EPS)


def _rope_partial(x, pos):
    half = ROPE_DIM // 2
    inv_freq = jnp.power(ROPE_THETA, -2.0 * jnp.arange(half, dtype=jnp.float32) / ROPE_DIM)
    ang = pos.astype(jnp.float32)[:, None] * inv_freq[None, :]
    cos = jnp.cos(ang)[None, :, None, :]
    sin = jnp.sin(ang)[None, :, None, :]
    x32 = x.astype(jnp.float32)
    x1 = x32[..., :half]
    x2 = x32[..., half:ROPE_DIM]
    out = jnp.concatenate([x1 * cos - x2 * sin, x2 * cos + x1 * sin, x32[..., ROPE_DIM:]], -1)
    return out.astype(x.dtype)


def _window_mask(q_pos, k_pos):
    q_chunk = q_pos // CHUNK
    k_chunk = k_pos // CHUNK
    return (k_pos >= 0) & (k_chunk <= q_chunk) & (k_pos >= q_chunk * CHUNK - WINDOW)


def _sink_attend(s, mask, sinks, v, eq_av):
    s = jnp.where(mask, s, -jnp.inf)
    sink = sinks.astype(jnp.float32)[:, :, None, None]
    m = jnp.maximum(jnp.max(s, -1, keepdims=True), sink)
    p = jnp.exp(s - m)
    denom = jnp.sum(p, -1, keepdims=True) + jnp.exp(sink - m)
    p = (p / denom).astype(v.dtype)
    return jnp.einsum(eq_av, p, v)


def _attn_prompt(q, k, v, sinks):
    B, T = q.shape[:2]
    nc = T // CHUNK
    qb = q.reshape(B, nc, CHUNK, ATTN_KV_HEADS, ATTN_GROUP, ATTN_HEAD_DIM).astype(jnp.float32)
    pad = ((0, 0), (WINDOW, 0), (0, 0), (0, 0))
    kp = jnp.pad(k, pad).reshape(B, nc + N_BAND - 1, CHUNK, ATTN_KV_HEADS, ATTN_HEAD_DIM)
    vp = jnp.pad(v, pad).reshape(B, nc + N_BAND - 1, CHUNK, ATTN_KV_HEADS, ATTN_HEAD_DIM)
    kb = jnp.concatenate([kp[:, j:j + nc] for j in range(N_BAND)], axis=2)
    vb = jnp.concatenate([vp[:, j:j + nc] for j in range(N_BAND)], axis=2)
    s = jnp.einsum('bnqhgd,bnkhd->bnhgqk', qb, kb.astype(jnp.float32)) * (ATTN_HEAD_DIM ** -0.5)
    blk = jnp.arange(nc, dtype=jnp.int32)[:, None] * CHUNK
    q_pos = blk + jnp.arange(CHUNK, dtype=jnp.int32)[None, :]
    k_pos = blk - WINDOW + jnp.arange(N_BAND * CHUNK, dtype=jnp.int32)[None, :]
    mask = _window_mask(q_pos[:, :, None], k_pos[:, None, :])
    o = _sink_attend(s, mask[:, None, None], sinks, vb, 'bnhgqk,bnkhd->bnqhgd')
    return o.reshape(B, T, ATTN_WIDTH)


def _attn_sample(q, k, v, k_cache, v_cache, sinks):
    B, S = q.shape[:2]
    lc = k_cache.shape[1]
    kk = jnp.concatenate([k_cache.astype(k.dtype), k], 1)
    vv = jnp.concatenate([v_cache.astype(v.dtype), v], 1)
    qg = q.reshape(B, S, ATTN_KV_HEADS, ATTN_GROUP, ATTN_HEAD_DIM).astype(jnp.float32)
    s = jnp.einsum('bqhgd,bkhd->bhgqk', qg, kk.astype(jnp.float32)) * (ATTN_HEAD_DIM ** -0.5)
    q_pos = PAST_LEN + jnp.arange(S, dtype=jnp.int32)
    k_pos = jnp.concatenate([PAST_LEN - lc + jnp.arange(lc, dtype=jnp.int32), q_pos])
    mask = _window_mask(q_pos[:, None], k_pos[None, :])
    o = _sink_attend(s, mask, sinks, vv, 'bhgqk,bkhd->bqhgd')
    return o.reshape(B, S, ATTN_WIDTH)


def _causal_conv(x, buf, w):
    T = x.shape[1]
    xp = jnp.concatenate([buf.astype(x.dtype), x], 1)
    y = xp[:, 0:T] * w[0]
    for i in range(1, CONV_WIDTH):
        y = y + xp[:, i:i + T] * w[i]
    return y, xp[:, -(CONV_WIDTH - 1):]


def _gated_delta(q, k, v, g, beta, s0, chunk):
    B, T, H, _ = q.shape
    Dv = v.shape[-1]
    n = T // chunk

    def blocks(a):
        a = a.astype(jnp.float32).reshape((B, n, chunk, H) + a.shape[3:])
        return jnp.moveaxis(a, 3, 1)

    qc, kc, vc, gc, bc = blocks(q), blocks(k), blocks(v), blocks(g), blocks(beta)
    G = jnp.cumsum(gc, -1)
    idx = jnp.arange(chunk)
    incl = idx[:, None] >= idx[None, :]
    strict = idx[:, None] > idx[None, :]
    decay = jnp.exp(jnp.where(incl, G[..., :, None] - G[..., None, :], -jnp.inf))
    kk = jnp.einsum('bhnld,bhnmd->bhnlm', kc, kc)
    a_mat = jnp.eye(chunk, dtype=jnp.float32) + jnp.where(strict, bc[..., None] * decay * kk, 0.0)
    rhs = jnp.concatenate([bc[..., None] * vc, (bc * jnp.exp(G))[..., None] * kc], -1)
    w = lax.linalg.triangular_solve(a_mat, rhs, left_side=True, lower=True, unit_diagonal=True)
    w_v, w_k = w[..., :Dv], w[..., Dv:]
    p = jnp.einsum('bhnld,bhnmd->bhnlm', qc, kc) * decay
    q_dec = jnp.exp(G)[..., None] * qc
    k_end = jnp.exp(G[..., -1:] - G)[..., None] * kc
    g_end = jnp.exp(G[..., -1])
    xs = tuple(jnp.moveaxis(a, 2, 0) for a in (w_v, w_k, p, q_dec, k_end, g_end))

    def step(S, xs_t):
        w_v_t, w_k_t, p_t, q_t, k_t, g_t = xs_t
        u = w_v_t - jnp.einsum('bhld,bhde->bhle', w_k_t, S)
        o = jnp.einsum('bhld,bhde->bhle', q_t, S) + jnp.einsum('bhlm,bhme->bhle', p_t, u)
        S = g_t[..., None, None] * S + jnp.einsum('bhld,bhle->bhde', k_t, u)
        return S, o

    S, o = lax.scan(step, s0.astype(jnp.float32), xs)
    o = jnp.transpose(o, (1, 0, 3, 2, 4)).reshape(B, T, H, Dv)
    return o.astype(v.dtype), S.astype(s0.dtype)


def _mixer(h, pos, k_cache, v_cache, s0, conv_buf, w_in, w_conv, sinks, a_log, dt_bias, w_onorm, w_out, delta_chunk):
    B, T, _ = h.shape
    p = h @ w_in
    o1 = ATTN_WIDTH
    o2 = o1 + KV_WIDTH
    o3 = o2 + KV_WIDTH
    o4 = o3 + DN_CONV_CH
    o5 = o4 + DN_WIDTH
    o6 = o5 + DN_HEADS
    q_a = _rope_partial(p[..., :o1].reshape(B, T, ATTN_HEADS, ATTN_HEAD_DIM), pos)
    k_a = _rope_partial(p[..., o1:o2].reshape(B, T, ATTN_KV_HEADS, ATTN_HEAD_DIM), pos)
    v_a = p[..., o2:o3].reshape(B, T, ATTN_KV_HEADS, ATTN_HEAD_DIM)
    if k_cache is None:
        attn = _attn_prompt(q_a, k_a, v_a, sinks)
    else:
        attn = _attn_sample(q_a, k_a, v_a, k_cache, v_cache, sinks)
    conv_out, new_buf = _causal_conv(p[..., o3:o4], conv_buf, w_conv)
    conv_out = jax.nn.silu(conv_out)
    nk = DN_HEADS * DN_KEY_DIM
    q_d = _l2norm(conv_out[..., :nk].reshape(B, T, DN_HEADS, DN_KEY_DIM)) * (DN_KEY_DIM ** -0.5)
    k_d = _l2norm(conv_out[..., nk:2 * nk].reshape(B, T, DN_HEADS, DN_KEY_DIM))
    v_d = conv_out[..., 2 * nk:].reshape(B, T, DN_HEADS, DN_VAL_DIM)
    beta = jax.nn.sigmoid(p[..., o5:o6].astype(jnp.float32))
    g = -jnp.exp(a_log.astype(jnp.float32)) * jax.nn.softplus(p[..., o6:].astype(jnp.float32) + dt_bias.astype(jnp.float32))
    o_d, s_new = _gated_delta(q_d, k_d, v_d, g, beta, s0, delta_chunk)
    gate = p[..., o4:o5].reshape(B, T, DN_HEADS, DN_VAL_DIM)
    o_d = _rmsnorm(o_d, w_onorm) * jax.nn.silu(gate)
    out = jnp.concatenate([attn, o_d.reshape(B, T, DN_WIDTH)], -1) @ w_out
    return out, k_a, v_a, s_new, new_buf


def _swiglu(h, w_gate, w_up, w_down):
    return (jax.nn.silu(h @ w_gate) * (h @ w_up)) @ w_down


def _moe(h, w_router, w_gate, w_up, w_down):
    logits = (h @ w_router).astype(jnp.float32)
    top_val, top_idx = lax.top_k(logits, TOP_K)
    top_w = jax.nn.softmax(top_val, -1)
    gates = jnp.sum(jax.nn.one_hot(top_idx, N_EXPERTS, dtype=jnp.float32) * top_w[..., None], -2).astype(h.dtype)
    out = jnp.zeros_like(h)
    for e in range(N_EXPERTS):
        out = out + gates[..., e:e + 1] * _swiglu(h, w_gate[e], w_up[e], w_down[e])
    return out


def _trunk(x, c, pos, past, delta_chunk, w_in, w_conv, attn_sinks, dn_a_log, dn_dt_bias, dn_norm, w_out, w_mod, b_mod, norm_gains, ffn_gate, ffn_up, ffn_down, moe_router, moe_gate, moe_up, moe_down):
    B = x.shape[0]
    ks, vs, ss, bufs = [], [], [], []
    for l in range(DEPTH):
        mod = jax.nn.silu(c) @ w_mod[l] + b_mod[l]
        sh_a, sc_a, g_a, sh_f, sc_f, g_f = [m[:, None, :] for m in jnp.split(mod, 6, -1)]
        if past is None:
            kc = None
            vc = None
            s0 = jnp.zeros((B, DN_HEADS, DN_KEY_DIM, DN_VAL_DIM), x.dtype)
            buf = jnp.zeros((B, CONV_WIDTH - 1, DN_CONV_CH), x.dtype)
        else:
            kc, vc, s0, buf = past[0][l], past[1][l], past[2][l], past[3][l]
        h = _rmsnorm(x, norm_gains[l, 0]) * (1 + sc_a) + sh_a
        y, k_new, v_new, s_new, buf_new = _mixer(
            h, pos, kc, vc, s0, buf, w_in[l], w_conv[l],
            attn_sinks[l].reshape(ATTN_KV_HEADS, ATTN_GROUP), dn_a_log[l], dn_dt_bias[l],
            dn_norm[l], w_out[l], delta_chunk)
        x = x + g_a * _rmsnorm(y, norm_gains[l, 1])
        h = _rmsnorm(x, norm_gains[l, 2]) * (1 + sc_f) + sh_f
        if l % 2 == 0:
            y = _swiglu(h, ffn_gate[l // 2], ffn_up[l // 2], ffn_down[l // 2])
        else:
            y = _moe(h, moe_router[l // 2], moe_gate[l // 2], moe_up[l // 2], moe_down[l // 2])
        x = x + g_f * _rmsnorm(y, norm_gains[l, 3])
        if past is None:
            k_new = k_new[:, -WINDOW:]
            v_new = v_new[:, -WINDOW:]
        ks.append(k_new)
        vs.append(v_new)
        ss.append(s_new)
        bufs.append(buf_new)
    return x, jnp.stack(ks), jnp.stack(vs), jnp.stack(ss), jnp.stack(bufs)


def setup_inputs(seed: int = 0) -> dict:
    key = jax.random.key(seed)
    k = jax.random.split(key, 26)

    def nrm(i, shape, scale):
        return scale * jax.random.normal(k[i], shape, jnp.float32)

    lc = min(WINDOW, PAST_LEN)
    a_init = jax.random.uniform(k[0], (DEPTH, DN_HEADS), jnp.float32, 1.0, 16.0)
    dt = jnp.exp(jax.random.uniform(k[1], (DEPTH, DN_HEADS), jnp.float32, math.log(1e-3), math.log(1e-1)))
    return {
        'x_prompt': nrm(2, (BATCH, SEQ, D_MODEL), 1.0),
        'x_sample': nrm(3, (DEC_BATCH, DEC_SEQ, D_MODEL), 1.0),
        'cache_attn_k': nrm(4, (DEPTH, DEC_BATCH, lc, ATTN_KV_HEADS, ATTN_HEAD_DIM), 1.0),
        'cache_attn_v': nrm(5, (DEPTH, DEC_BATCH, lc, ATTN_KV_HEADS, ATTN_HEAD_DIM), 1.0),
        'state_delta': nrm(6, (DEPTH, DEC_BATCH, DN_HEADS, DN_KEY_DIM, DN_VAL_DIM), 0.1),
        'state_conv': nrm(7, (DEPTH, DEC_BATCH, CONV_WIDTH - 1, DN_CONV_CH), 1.0),
        'c_prompt': nrm(8, (BATCH, D_MODEL), 1.0),
        'c_sample': nrm(9, (DEC_BATCH, D_MODEL), 1.0),
        'w_in': nrm(10, (DEPTH, D_MODEL, PROJ_WIDTH), D_MODEL ** -0.5),
        'w_conv': nrm(11, (DEPTH, CONV_WIDTH, DN_CONV_CH), CONV_WIDTH ** -0.5),
        'attn_sinks': nrm(12, (DEPTH, ATTN_HEADS), 0.5),
        'dn_a_log': jnp.log(a_init),
        'dn_dt_bias': dt + jnp.log(-jnp.expm1(-dt)),
        'dn_norm': 1.0 + nrm(13, (DEPTH, DN_VAL_DIM), 0.02),
        'w_out': nrm(14, (DEPTH, MIX_WIDTH, D_MODEL), MIX_WIDTH ** -0.5),
        'w_mod': nrm(15, (DEPTH, D_MODEL, 6 * D_MODEL), 0.5 * D_MODEL ** -0.5),
        'b_mod': nrm(16, (DEPTH, 6 * D_MODEL), 0.01),
        'norm_gains': 1.0 + nrm(17, (DEPTH, 4, D_MODEL), 0.02),
        'ffn_gate': nrm(18, (N_DENSE, D_MODEL, D_FF), D_MODEL ** -0.5),
        'ffn_up': nrm(19, (N_DENSE, D_MODEL, D_FF), D_MODEL ** -0.5),
        'ffn_down': nrm(20, (N_DENSE, D_FF, D_MODEL), D_FF ** -0.5),
        'moe_router': nrm(21, (N_MOE, D_MODEL, N_EXPERTS), D_MODEL ** -0.5),
        'moe_gate': nrm(22, (N_MOE, N_EXPERTS, D_MODEL, D_FF_EXPERT), D_MODEL ** -0.5),
        'moe_up': nrm(23, (N_MOE, N_EXPERTS, D_MODEL, D_FF_EXPERT), D_MODEL ** -0.5),
        'moe_down': nrm(24, (N_MOE, N_EXPERTS, D_FF_EXPERT, D_MODEL), D_FF_EXPERT ** -0.5),
    }


def reference(x_prompt, x_sample, cache_attn_k, cache_attn_v, state_delta, state_conv, c_prompt, c_sample, w_in, w_conv, attn_sinks, dn_a_log, dn_dt_bias, dn_norm, w_out, w_mod, b_mod, norm_gains, ffn_gate, ffn_up, ffn_down, moe_router, moe_gate, moe_up, moe_down):
    weights = (w_in, w_conv, attn_sinks, dn_a_log, dn_dt_bias, dn_norm, w_out, w_mod, b_mod, norm_gains,
               ffn_gate, ffn_up, ffn_down, moe_router, moe_gate, moe_up, moe_down)
    pos_p = jnp.arange(x_prompt.shape[1], dtype=jnp.int32)
    y_prompt, k_p, v_p, s_p, conv_p = _trunk(x_prompt, c_prompt, pos_p, None, CHUNK, *weights)
    pos_s = PAST_LEN + jnp.arange(x_sample.shape[1], dtype=jnp.int32)
    past = (cache_attn_k, cache_attn_v, state_delta, state_conv)
    y_sample, k_s, v_s, s_s, conv_s = _trunk(x_sample, c_sample, pos_s, past, x_sample.shape[1], *weights)
    return (y_prompt, y_sample, k_p, v_p, s_p, conv_p, k_s, v_s, s_s, conv_s)
```

```python
import functools
import math

import numpy as np
import jax
import jax.numpy as jnp
from jax import lax
from jax.experimental import pallas as pl
from jax.experimental.pallas import tpu as pltpu

D_MODEL = 2048
DEPTH = 2
PAST_LEN = 1024
CHUNK = 64
ATTN_HEADS = 16
ATTN_KV_HEADS = 2
ATTN_HEAD_DIM = 64
ATTN_WIDTH = 1024
KV_WIDTH = 128
WINDOW = 128
ROPE_THETA = 500000.0
ROPE_DIM = 16
DN_HEADS = 8
DN_KEY_DIM = 128
DN_VAL_DIM = 128
DN_WIDTH = 1024
CONV_WIDTH = 4
DN_CONV_CH = 3072
D_FF = 5632
N_EXPERTS = 8
D_FF_EXPERT = 2816
EPS = 1e-6

F32 = jnp.float32
BF16 = jnp.bfloat16
LANES = 128
V7X_VMEM_LIMIT = 56 * 1024 * 1024

P_Q = 0
P_GATE = 1024
P_CONV = 2048
P_K = 5120
P_V = 5248
P_BA = 5376
P_WIDTH = 5632


def _cparams(*sem):
    return pltpu.CompilerParams(dimension_semantics=sem, vmem_limit_bytes=V7X_VMEM_LIMIT)


def _silu(x):
    return x * jax.nn.sigmoid(x)


def _rms(x, gain):
    return x * lax.rsqrt(jnp.mean(x * x, -1, keepdims=True) + EPS) * gain


def _dot(a, b):
    return jnp.dot(a, b, preferred_element_type=F32)


def _dot_nt(a, b):
    return lax.dot_general(a, b, (((1,), (1,)), ((), ())), preferred_element_type=F32)


def _dot_tn(a, b):
    return lax.dot_general(a, b, (((0,), (0,)), ((), ())), preferred_element_type=F32)


def _split_bf16(a):
    hi = a.astype(BF16)
    lo = (a - hi.astype(F32)).astype(BF16)
    return hi, lo


def _dot_x3(a, b, dot=_dot, out_axis=0):
    a_hi, a_lo = _split_bf16(a)
    b_hi, b_lo = _split_bf16(b)
    n = a.shape[out_axis]
    top = dot(jnp.concatenate([a_hi, a_lo], out_axis), b_hi)
    return top[:n] + top[n:] + dot(a_hi, b_lo)


def _dot_any(a, b, precise, dot=_dot, out_axis=0):
    if precise:
        return _dot_x3(a, b, dot, out_axis)
    return dot(a.astype(BF16), b.astype(BF16))


def _mm(a, w):
    if len(w) == 1:
        return _dot(a.astype(BF16), w[0])
    a_hi, a_lo = _split_bf16(a)
    n = a.shape[0]
    top = _dot(jnp.concatenate([a_hi, a_lo], 0), w[0])
    return top[:n] + top[n:] + _dot(a_hi, w[1])


def _split_weight(w):
    hi, lo = _split_bf16(w)
    return (hi, lo)


def _mod_spec(rows, tm, d):
    if rows == 1:
        return pl.BlockSpec((1, d), lambda i, *_: (0, 0))
    return pl.BlockSpec((tm, d), lambda i, *_: (i, 0))


def _mod_kernel(c_ref, w_ref, b_ref, o_ref):
    o_ref[0] = _dot_x3(_silu(c_ref[...]), w_ref[0]) + b_ref[0]


def _modulation(c_all, w_mod, b_mod):
    rows = c_all.shape[0]
    n = w_mod.shape[2]
    tn = 1024
    return pl.pallas_call(
        _mod_kernel,
        grid=(DEPTH, n // tn),
        in_specs=[
            pl.BlockSpec((rows, D_MODEL), lambda l, j: (0, 0)),
            pl.BlockSpec((1, D_MODEL, tn), lambda l, j: (l, 0, j)),
            pl.BlockSpec((1, 1, tn), lambda l, j: (l, 0, j)),
        ],
        out_specs=pl.BlockSpec((1, rows, tn), lambda l, j: (l, 0, j)),
        out_shape=jax.ShapeDtypeStruct((DEPTH, rows, n), F32),
        compiler_params=_cparams("parallel", "parallel"),
        name="modulation",
    )(c_all, w_mod, b_mod.reshape(DEPTH, 1, n))


def _norm_proj_kernel(nw, x_ref, g_ref, sc_ref, sh_ref, *refs):
    w_refs, (o_ref, h_ref) = refs[:nw], refs[nw:]

    @pl.when(pl.program_id(1) == 0)
    def _():
        h = _rms(x_ref[...], g_ref[...]) * (1.0 + sc_ref[...]) + sh_ref[...]
        h_ref[...] = h.astype(h_ref.dtype)

    o_ref[...] = _mm(h_ref[...], tuple(r[...] for r in w_refs))


def _norm_proj(x, gain, scale, shift, w, tm, tn):
    m, d = x.shape
    n = w[0].shape[1]
    return pl.pallas_call(
        functools.partial(_norm_proj_kernel, len(w)),
        grid=(m // tm, n // tn),
        in_specs=[
            pl.BlockSpec((tm, d), lambda i, j: (i, 0)),
            pl.BlockSpec((1, d), lambda i, j: (0, 0)),
            _mod_spec(scale.shape[0], tm, d),
            _mod_spec(shift.shape[0], tm, d),
        ] + [pl.BlockSpec((d, tn), lambda i, j: (0, j))] * len(w),
        out_specs=pl.BlockSpec((tm, tn), lambda i, j: (i, j)),
        out_shape=jax.ShapeDtypeStruct((m, n), F32),
        scratch_shapes=[pltpu.VMEM((tm, d), BF16 if len(w) == 1 else F32)],
        compiler_params=_cparams("parallel", "arbitrary"),
        name="norm_proj",
    )(x, gain, scale, shift, *w)


def _rope(x, cos, sa, sb):
    return x * cos + pltpu.roll(x, LANES - 8, 1) * sa + pltpu.roll(x, 8, 1) * sb


def _kv_variants(k, v):
    lo = lax.broadcasted_iota(jnp.int32, k.shape, 1) < ATTN_HEAD_DIM
    kr = pltpu.roll(k, ATTN_HEAD_DIM, 1)
    vr = pltpu.roll(v, ATTN_HEAD_DIM, 1)
    zero = jnp.zeros_like(k)
    k_lo = (jnp.where(lo, k, zero), jnp.where(lo, kr, zero))
    k_hi = (jnp.where(lo, zero, kr), jnp.where(lo, zero, k))
    v_lo = (jnp.where(lo, v, zero), jnp.where(lo, vr, zero))
    v_hi = (jnp.where(lo, zero, vr), jnp.where(lo, zero, v))
    return k_lo, k_hi, v_lo, v_hi


def _sink_softmax(s, sink):
    m = jnp.maximum(jnp.max(s, -1, keepdims=True), sink)
    p = jnp.exp(s - m)
    den = jnp.sum(p, -1, keepdims=True) + jnp.exp(sink - m)
    return p / den


def _attn_core(qb, k_lo, k_hi, v_lo, v_hi, bias, sink_even, sink_odd, precise=False):
    p_even = _sink_softmax(_dot_any(qb, k_lo, precise, _dot_nt) + bias, sink_even)
    p_odd = _sink_softmax(_dot_any(qb, k_hi, precise, _dot_nt) + bias, sink_odd)
    return _dot_any(p_even, v_lo, precise) + _dot_any(p_odd, v_hi, precise)


def _sink_columns(sink_ref, rows_per_pair):
    n = 4 * rows_per_pair
    pair = lax.broadcasted_iota(jnp.int32, (n, 1), 0) // rows_per_pair
    out = []
    for j in range(ATTN_KV_HEADS):
        cols = []
        for par in range(2):
            col = jnp.zeros((n, 1), F32)
            for a in range(4):
                col = jnp.where(pair == a, sink_ref[8 * j + 2 * a + par], col)
            cols.append(col)
        out.append(cols)
    return out


def _attn_prompt_kernel(sink_ref, q_ref, kv_ref, cos_ref, sa_ref, sb_ref, o_ref, knew_ref,
                        qs_ref, klo_ref, khi_ref, vlo_ref, vhi_ref):
    i = pl.program_id(0)
    tb = q_ref.shape[0]
    bufs = (klo_ref, khi_ref, vlo_ref, vhi_ref)

    @pl.when(i == 0)
    def _():
        for r in bufs:
            r[:, 0:WINDOW, :] = jnp.zeros((ATTN_KV_HEADS, WINDOW, LANES), BF16)

    @pl.when(i > 0)
    def _():
        for r in bufs:
            r[:, 0:WINDOW, :] = r[:, tb:tb + WINDOW, :]

    cos, sa, sb = cos_ref[...], sa_ref[...], sb_ref[...]
    k = _rope(kv_ref[:, 0:LANES], cos, sa, sb)
    knew_ref[...] = k
    variants = _kv_variants(k, kv_ref[:, LANES:2 * LANES])
    for r, var in zip(bufs, variants):
        for j in range(ATTN_KV_HEADS):
            r[j, WINDOW:, :] = var[j].astype(BF16)
    scale = ATTN_HEAD_DIM ** -0.5
    for a in range(ATTN_WIDTH // LANES):
        cols = slice(a * LANES, (a + 1) * LANES)
        qs_ref[:, cols] = (_rope(q_ref[:, cols], cos, sa, sb) * scale).astype(BF16)

    sinks = _sink_columns(sink_ref, CHUNK)
    nk = WINDOW + CHUNK

    def chunk_body(c, carry):
        r0 = pl.multiple_of(c * CHUNK, CHUNK)
        kpos = i * tb - WINDOW + r0 + lax.broadcasted_iota(jnp.int32, (1, nk), 1)
        bias = jnp.where(kpos >= 0, 0.0, -jnp.inf).astype(F32)
        for j in range(ATTN_KV_HEADS):
            qb = jnp.concatenate(
                [qs_ref[pl.ds(r0, CHUNK), (4 * j + a) * LANES:(4 * j + a + 1) * LANES] for a in range(4)], 0)
            o = _attn_core(qb, klo_ref[j, pl.ds(r0, nk), :], khi_ref[j, pl.ds(r0, nk), :],
                           vlo_ref[j, pl.ds(r0, nk), :], vhi_ref[j, pl.ds(r0, nk), :],
                           bias, sinks[j][0], sinks[j][1])
            for a in range(4):
                o_ref[pl.ds(r0, CHUNK), (4 * j + a) * LANES:(4 * j + a + 1) * LANES] = (
                    o[a * CHUNK:(a + 1) * CHUNK].astype(BF16))
        return carry

    lax.fori_loop(0, tb // CHUNK, chunk_body, 0)


def _attn_prompt(p, sinks, cos, sa, sb, tb):
    t = p.shape[0]
    kv_blk = P_K // (2 * LANES)
    row = lambda i: (i, 0)
    return pl.pallas_call(
        _attn_prompt_kernel,
        grid=(t // tb,),
        in_specs=[
            pl.BlockSpec(memory_space=pltpu.SMEM),
            pl.BlockSpec((tb, ATTN_WIDTH), row),
            pl.BlockSpec((tb, 2 * LANES), lambda i: (i, kv_blk)),
            pl.BlockSpec((tb, LANES), row),
            pl.BlockSpec((tb, LANES), row),
            pl.BlockSpec((tb, LANES), row),
        ],
        out_specs=[pl.BlockSpec((tb, ATTN_WIDTH), row), pl.BlockSpec((tb, LANES), row)],
        out_shape=[jax.ShapeDtypeStruct((t, ATTN_WIDTH), BF16), jax.ShapeDtypeStruct((t, LANES), F32)],
        scratch_shapes=[pltpu.VMEM((tb, ATTN_WIDTH), BF16)]
        + [pltpu.VMEM((ATTN_KV_HEADS, tb + WINDOW, LANES), BF16) for _ in range(4)],
        compiler_params=_cparams("arbitrary"),
        name="attn_prompt",
    )(sinks, p, p, cos, sa, sb)


def _attn_sample_kernel(sink_ref, q_ref, kv_ref, ck_ref, cv_ref, cos_ref, sa_ref, sb_ref, bias_ref,
                        o_ref, knew_ref):
    s = q_ref.shape[0]
    cos, sa, sb = cos_ref[...], sa_ref[...], sb_ref[...]
    k = _rope(kv_ref[:, 0:LANES], cos, sa, sb)
    knew_ref[...] = k
    kk = jnp.concatenate([ck_ref[0], k], 0)
    vv = jnp.concatenate([cv_ref[0], kv_ref[:, LANES:2 * LANES]], 0)
    k_lo, k_hi, v_lo, v_hi = _kv_variants(kk, vv)
    sinks = _sink_columns(sink_ref, s)
    scale = ATTN_HEAD_DIM ** -0.5
    bias = bias_ref[...]
    for j in range(ATTN_KV_HEADS):
        qb = jnp.concatenate(
            [_rope(q_ref[:, (4 * j + a) * LANES:(4 * j + a + 1) * LANES], cos, sa, sb) * scale for a in range(4)], 0)
        o = _attn_core(qb, k_lo[j], k_hi[j], v_lo[j], v_hi[j], bias, sinks[j][0], sinks[j][1], precise=True)
        for a in range(4):
            o_ref[:, (4 * j + a) * LANES:(4 * j + a + 1) * LANES] = o[a * s:(a + 1) * s]


def _attn_sample(p, cache_k, cache_v, sinks, cos, sa, sb, bias, batch, s):
    lc = cache_k.shape[1]
    kv_blk = P_K // (2 * LANES)
    row = lambda b: (b, 0)
    const = lambda b: (0, 0)
    return pl.pallas_call(
        _attn_sample_kernel,
        grid=(batch,),
        in_specs=[
            pl.BlockSpec(memory_space=pltpu.SMEM),
            pl.BlockSpec((s, ATTN_WIDTH), row),
            pl.BlockSpec((s, 2 * LANES), lambda b: (b, kv_blk)),
            pl.BlockSpec((1, lc, LANES), lambda b: (b, 0, 0)),
            pl.BlockSpec((1, lc, LANES), lambda b: (b, 0, 0)),
            pl.BlockSpec((s, LANES), const),
            pl.BlockSpec((s, LANES), const),
            pl.BlockSpec((s, LANES), const),
            pl.BlockSpec((4 * s, lc + s), const),
        ],
        out_specs=[pl.BlockSpec((s, ATTN_WIDTH), row), pl.BlockSpec((s, LANES), row)],
        out_shape=[jax.ShapeDtypeStruct((batch * s, ATTN_WIDTH), F32),
                   jax.ShapeDtypeStruct((batch * s, LANES), F32)],
        compiler_params=_cparams("parallel"),
        name="attn_sample",
    )(sinks, p, p, cache_k, cache_v, cos, sa, sb, bias)


def _unit_lower_inverse(a):
    n = a.shape[0]
    eye = (lax.broadcasted_iota(jnp.int32, (n, n), 0) == lax.broadcasted_iota(jnp.int32, (n, n), 1)).astype(F32)
    power = -a
    x = eye + power
    for _ in range(int(math.log2(n)) - 1):
        power = _dot_x3(power, power)
        x = x + _dot_x3(x, power)
    return x


def _softplus(x):
    return jnp.maximum(x, 0.0) + jnp.log1p(jnp.exp(-jnp.abs(x)))


def _dn_prep_kernel(chunk, precise, qd_ref, kd_ref, vd_ref, ba_ref, hq_ref, hk_ref, hv_ref, wq_ref, wk_ref, wv_ref,
                    alog_ref, dtb_ref,
                    wv_out, wk_out, qdec_out, kend_out, p_out, gend_out,
                    q_scr, k_scr, v_scr, beta_scr, g_scr):
    h = pl.program_id(1)
    tb = qd_ref.shape[0]

    def conv_silu(x_ref, halo_ref, w_ref):
        xp = jnp.concatenate([halo_ref[0], x_ref[...]], 0)
        w = w_ref[...]
        y = xp[5:5 + tb] * w[0:1]
        for tap in range(1, CONV_WIDTH):
            y = y + xp[5 + tap:5 + tap + tb] * w[tap:tap + 1]
        return _silu(y)

    q = conv_silu(qd_ref, hq_ref, wq_ref)
    k = conv_silu(kd_ref, hk_ref, wk_ref)
    v_scr[...] = conv_silu(vd_ref, hv_ref, wv_ref)
    q_scr[...] = q * lax.rsqrt(jnp.sum(q * q, -1, keepdims=True) + EPS) * (DN_KEY_DIM ** -0.5)
    k_scr[...] = k * lax.rsqrt(jnp.sum(k * k, -1, keepdims=True) + EPS)

    ba = ba_ref[...]
    lane = lax.broadcasted_iota(jnp.int32, ba.shape, 1)
    beta_all = jax.nn.sigmoid(ba)
    g_all = -jnp.exp(alog_ref[...]) * _softplus(ba + dtb_ref[...])
    beta_scr[...] = jnp.sum(jnp.where(lane == h, beta_all, 0.0), -1, keepdims=True)
    g_scr[...] = jnp.sum(jnp.where(lane == h + DN_HEADS, g_all, 0.0), -1, keepdims=True)

    li = lax.broadcasted_iota(jnp.int32, (chunk, chunk), 0)
    mi = lax.broadcasted_iota(jnp.int32, (chunk, chunk), 1)
    op_dtype = wk_out.dtype
    pad = jnp.zeros((chunk, LANES - chunk), op_dtype)

    def chunk_body(c, carry):
        r0 = pl.multiple_of(c * chunk, chunk)
        rows = pl.ds(r0, chunk)
        qc, kc, vc = q_scr[rows, :], k_scr[rows, :], v_scr[rows, :]
        bc, gc = beta_scr[rows, :], g_scr[rows, :]
        g_row = jnp.sum(jnp.where(li <= mi, gc, 0.0), 0, keepdims=True)
        g_col = jnp.sum(jnp.where(li == mi, g_row, 0.0), 1, keepdims=True)
        decay = jnp.exp(jnp.where(li >= mi, g_col - g_row, -jnp.inf))
        qk_kk = _dot_any(jnp.concatenate([qc, kc], 0), kc, precise, _dot_nt)
        qk, kk = qk_kk[:chunk], qk_kk[chunk:]
        a = jnp.where(li > mi, bc * decay * kk, 0.0)
        t_inv = _unit_lower_inverse(a)
        e_g = jnp.exp(g_col)
        rhs = jnp.concatenate([bc * vc, (bc * e_g) * kc], 1)
        w = _dot_x3(t_inv, rhs)
        g_end = g_row[:, chunk - 1:chunk]
        wv_out[rows, :] = w[:, :DN_VAL_DIM]
        wk_out[rows, :] = w[:, DN_VAL_DIM:].astype(op_dtype)
        qdec_out[rows, :] = (e_g * qc).astype(op_dtype)
        kend_out[rows, :] = (jnp.exp(g_end - g_col) * kc).astype(op_dtype)
        p_out[rows, 0:chunk] = (qk * decay).astype(op_dtype)
        p_out[rows, chunk:] = pad
        gend_out[c] = jnp.broadcast_to(jnp.exp(g_end), (1, LANES))
        return carry

    lax.fori_loop(0, tb // chunk, chunk_body, 0)


def _dn_prep(p, halo, w_conv8, alog_row, dtb_row, chunk, tb, precise):
    m = p.shape[0]
    op_dtype = F32 if precise else BF16
    nh = DN_HEADS
    cq, ck, cv = P_CONV // LANES, P_CONV // LANES + nh, P_CONV // LANES + 2 * nh
    col = lambda base: (lambda i, h: (i, base + h))
    halo_spec = lambda base: pl.BlockSpec((1, 8, LANES), lambda i, h: (i, 0, base + h))
    w_spec = lambda base: pl.BlockSpec((8, LANES), lambda i, h: (0, base + h))
    const = pl.BlockSpec((1, LANES), lambda i, h: (0, 0))
    head_blk = pl.BlockSpec((tb, LANES), lambda i, h: (i, h))
    out_shape = [
        jax.ShapeDtypeStruct((m, DN_WIDTH), F32),
        jax.ShapeDtypeStruct((m, DN_WIDTH), op_dtype),
        jax.ShapeDtypeStruct((m, DN_WIDTH), op_dtype),
        jax.ShapeDtypeStruct((m, DN_WIDTH), op_dtype),
        jax.ShapeDtypeStruct((m, DN_WIDTH), op_dtype),
        jax.ShapeDtypeStruct((m // chunk, 1, DN_WIDTH), F32),
    ]
    return pl.pallas_call(
        functools.partial(_dn_prep_kernel, chunk, precise),
        grid=(m // tb, nh),
        in_specs=[
            pl.BlockSpec((tb, LANES), col(cq)),
            pl.BlockSpec((tb, LANES), col(ck)),
            pl.BlockSpec((tb, LANES), col(cv)),
            pl.BlockSpec((tb, LANES), lambda i, h: (i, P_BA // LANES)),
            halo_spec(0), halo_spec(nh), halo_spec(2 * nh),
            w_spec(0), w_spec(nh), w_spec(2 * nh),
            const, const,
        ],
        out_specs=[head_blk] * 5 + [pl.BlockSpec((tb // chunk, 1, LANES), lambda i, h: (i, 0, h))],
        out_shape=out_shape,
        scratch_shapes=[pltpu.VMEM((tb, LANES), F32)] * 3 + [pltpu.VMEM((tb, 1), F32)] * 2,
        compiler_params=_cparams("parallel", "parallel"),
        name="dn_prep",
    )(p, p, p, p, halo, halo, halo, w_conv8, w_conv8, w_conv8, alog_row, dtb_row)


def _dn_scan_kernel(chunk, n_chunks, wv_ref, wk_ref, qd_ref, ke_ref, p_ref, ge_ref, gate_ref, s0_ref, onorm_ref,
                    od_ref, sout_ref, s_scr):
    n = pl.program_id(1)
    precise = wk_ref.dtype == F32

    @pl.when(n == 0)
    def _():
        s_scr[...] = s0_ref[0]

    onorm = onorm_ref[...]
    for c in range(n_chunks):
        rows = slice(c * chunk, (c + 1) * chunk)
        for h in range(DN_HEADS):
            cols = slice(h * LANES, (h + 1) * LANES)
            s = s_scr[h]
            if not precise:
                s = s.astype(BF16)
            u = wv_ref[rows, cols] - _dot_any(wk_ref[rows, cols], s, precise)
            if not precise:
                u = u.astype(BF16)
            o = (_dot_any(qd_ref[rows, cols], s, precise)
                 + _dot_any(p_ref[rows, h * LANES:h * LANES + chunk], u, precise))
            s_scr[h] = ge_ref[c, :, cols] * s_scr[h] + _dot_any(ke_ref[rows, cols], u, precise, _dot_tn, 1)
            gate = gate_ref[rows, cols]
            od_ref[rows, cols] = (_rms(o, onorm) * _silu(gate)).astype(od_ref.dtype)

    @pl.when(n == pl.num_programs(1) - 1)
    def _():
        sout_ref[0] = s_scr[...]


def _dn_scan(prep, p, s0, onorm_row, chunk, n_chunks, batch):
    wv, wk, qdec, kend, pm, gend = prep
    m = wv.shape[0]
    rows = chunk * n_chunks
    steps = m // batch // rows
    blk = lambda b, n: (b * steps + n, 0)
    wide = pl.BlockSpec((rows, DN_WIDTH), blk)
    state = pl.BlockSpec((1, DN_HEADS, DN_KEY_DIM, DN_VAL_DIM), lambda b, n: (b, 0, 0, 0))
    return pl.pallas_call(
        functools.partial(_dn_scan_kernel, chunk, n_chunks),
        grid=(batch, steps),
        in_specs=[
            wide, wide, wide, wide, wide,
            pl.BlockSpec((n_chunks, 1, DN_WIDTH), lambda b, n: (b * steps + n, 0, 0)),
            pl.BlockSpec((rows, DN_WIDTH), lambda b, n: (b * steps + n, P_GATE // DN_WIDTH)),
            state,
            pl.BlockSpec((1, LANES), lambda b, n: (0, 0)),
        ],
        out_specs=[wide, state],
        out_shape=[jax.ShapeDtypeStruct((m, DN_WIDTH), wk.dtype),
                   jax.ShapeDtypeStruct((batch, DN_HEADS, DN_KEY_DIM, DN_VAL_DIM), F32)],
        scratch_shapes=[pltpu.VMEM((DN_HEADS, DN_KEY_DIM, DN_VAL_DIM), F32)],
        compiler_params=_cparams("parallel", "arbitrary"),
        name="dn_scan",
    )(wv, wk, qdec, kend, pm, gend, p, s0, onorm_row)


def _out_proj_kernel(nw, attn_ref, od_ref, *refs):
    w_refs, (x_ref, g_ref, gate_ref, o_ref) = refs[:nw], refs[nw:]
    y = (_mm(attn_ref[...], tuple(r[0:ATTN_WIDTH, :] for r in w_refs))
         + _mm(od_ref[...], tuple(r[ATTN_WIDTH:, :] for r in w_refs)))
    o_ref[...] = x_ref[...] + gate_ref[...] * _rms(y, g_ref[...])


def _out_proj(attn, od, w, x, gain, gate, tm):
    m, d = x.shape
    row = lambda i: (i, 0)
    return pl.pallas_call(
        functools.partial(_out_proj_kernel, len(w)),
        grid=(m // tm,),
        in_specs=[
            pl.BlockSpec((tm, ATTN_WIDTH), row),
            pl.BlockSpec((tm, DN_WIDTH), row),
        ] + [pl.BlockSpec((ATTN_WIDTH + DN_WIDTH, d), lambda i: (0, 0))] * len(w) + [
            pl.BlockSpec((tm, d), row),
            pl.BlockSpec((1, d), lambda i: (0, 0)),
            _mod_spec(gate.shape[0], tm, d),
        ],
        out_specs=pl.BlockSpec((tm, d), row),
        out_shape=jax.ShapeDtypeStruct((m, d), F32),
        compiler_params=_cparams("parallel"),
        name="out_proj",
    )(attn, od, *w, x, gain, gate)


def _ffn_kernel(nw, x_ref, g_ref, sc_ref, sh_ref, *refs):
    wg_refs, wu_refs, wd_refs = refs[:nw], refs[nw:2 * nw], refs[2 * nw:3 * nw]
    g2_ref, gate_ref, o_ref, h_ref, acc_ref = refs[3 * nw:]
    j = pl.program_id(1)

    @pl.when(j == 0)
    def _():
        h = _rms(x_ref[...], g_ref[...]) * (1.0 + sc_ref[...]) + sh_ref[...]
        h_ref[...] = h.astype(h_ref.dtype)
        acc_ref[...] = jnp.zeros_like(acc_ref)

    h = h_ref[...]
    act = _silu(_mm(h, tuple(r[...] for r in wg_refs))) * _mm(h, tuple(r[...] for r in wu_refs))
    acc_ref[...] += _mm(act, tuple(r[...] for r in wd_refs))

    @pl.when(j == pl.num_programs(1) - 1)
    def _():
        o_ref[...] = x_ref[...] + gate_ref[...] * _rms(acc_ref[...], g2_ref[...])


def _ffn(x, gain, scale, shift, wg, wu, wd, gain2, gate, tm, tf):
    m, d = x.shape
    nw = len(wg)
    f = wg[0].shape[1]
    row = lambda i, j: (i, 0)
    vec = pl.BlockSpec((1, d), lambda i, j: (0, 0))
    return pl.pallas_call(
        functools.partial(_ffn_kernel, nw),
        grid=(m // tm, f // tf),
        in_specs=[
            pl.BlockSpec((tm, d), row), vec,
            _mod_spec(scale.shape[0], tm, d), _mod_spec(shift.shape[0], tm, d),
        ] + [pl.BlockSpec((d, tf), lambda i, j: (0, j))] * (2 * nw)
        + [pl.BlockSpec((tf, d), lambda i, j: (j, 0))] * nw
        + [vec, _mod_spec(gate.shape[0], tm, d)],
        out_specs=pl.BlockSpec((tm, d), row),
        out_shape=jax.ShapeDtypeStruct((m, d), F32),
        scratch_shapes=[pltpu.VMEM((tm, d), BF16 if nw == 1 else F32), pltpu.VMEM((tm, d), F32)],
        compiler_params=_cparams("parallel", "arbitrary"),
        name="ffn_dense",
    )(x, gain, scale, shift, *wg, *wu, *wd, gain2, gate)


def _router_kernel(x_ref, g_ref, sc_ref, sh_ref, wr_ref, h_ref, gates_ref, idx_ref, w12_ref):
    h = _rms(x_ref[...], g_ref[...]) * (1.0 + sc_ref[...]) + sh_ref[...]
    h_ref[...] = h
    logits = jnp.dot(h, wr_ref[...], preferred_element_type=F32, precision=lax.Precision.HIGHEST)
    lane = lax.broadcasted_iota(jnp.int32, logits.shape, 1).astype(F32)
    logits = jnp.where(lane < N_EXPERTS, logits, -jnp.inf)
    m1 = jnp.max(logits, -1, keepdims=True)
    i1 = jnp.min(jnp.where(logits == m1, lane, float(LANES)), -1, keepdims=True)
    rest = jnp.where(lane == i1, -jnp.inf, logits)
    m2 = jnp.max(rest, -1, keepdims=True)
    i2 = jnp.min(jnp.where(rest == m2, lane, float(LANES)), -1, keepdims=True)
    t = jnp.exp(m2 - m1)
    w1 = 1.0 / (1.0 + t)
    w2 = t / (1.0 + t)
    gates_ref[...] = jnp.where(lane == i1, w1, 0.0) + jnp.where(lane == i2, w2, 0.0)
    idx_ref[...] = jnp.where(lane == 0.0, i1, jnp.where(lane == 1.0, i2, 0.0)).astype(jnp.int32)
    w12_ref[...] = jnp.where(lane == 0.0, w1, jnp.where(lane == 1.0, w2, 0.0))


def _router(x, gain, scale, shift, w_router_pad, tm):
    m, d = x.shape
    row = lambda i: (i, 0)
    vec = pl.BlockSpec((1, d), lambda i: (0, 0))
    small = pl.BlockSpec((tm, LANES), row)
    return pl.pallas_call(
        _router_kernel,
        grid=(m // tm,),
        in_specs=[pl.BlockSpec((tm, d), row), vec,
                  _mod_spec(scale.shape[0], tm, d), _mod_spec(shift.shape[0], tm, d),
                  pl.BlockSpec((d, LANES), lambda i: (0, 0))],
        out_specs=[pl.BlockSpec((tm, d), row), small, small, small],
        out_shape=[jax.ShapeDtypeStruct((m, d), F32), jax.ShapeDtypeStruct((m, LANES), F32),
                   jax.ShapeDtypeStruct((m, LANES), jnp.int32), jax.ShapeDtypeStruct((m, LANES), F32)],
        compiler_params=_cparams("parallel"),
        name="moe_router",
    )(x, gain, scale, shift, w_router_pad)


def _moe_gemm_kernel(te_ref, tot_ref, rt_ref, h_hbm, wg_ref, wu_ref, wd_ref, ys_ref, xs_ref, xb_ref, acc_ref, sem):
    r = pl.program_id(0)
    j = pl.program_id(1)
    tm = xs_ref.shape[0]
    active = r < tot_ref[0]

    def row_copy(t):
        return pltpu.make_async_copy(h_hbm.at[pl.ds(rt_ref[r * tm + t], 1), :], xs_ref.at[pl.ds(t, 1), :], sem)

    @pl.when(jnp.logical_and(active, j == 0))
    def _():
        def start(t, carry):
            row_copy(t).start()
            return carry

        def wait(t, carry):
            row_copy(t).wait()
            return carry

        lax.fori_loop(0, tm, start, 0)
        lax.fori_loop(0, tm, wait, 0)
        xb_ref[...] = xs_ref[...].astype(BF16)
        acc_ref[...] = jnp.zeros_like(acc_ref)

    @pl.when(active)
    def _():
        xb = xb_ref[...]
        act = (_silu(_dot(xb, wg_ref[0])) * _dot(xb, wu_ref[0])).astype(BF16)
        acc_ref[...] += _dot(act, wd_ref[0])

    @pl.when(j == pl.num_programs(1) - 1)
    def _():
        ys_ref[...] = jnp.where(active, acc_ref[...], 0.0)


def _moe_gemm(tile_expert, total_tiles, row_token, h, wg, wu, wd, tm, tf):
    n_tiles = tile_expert.shape[0]
    d = h.shape[1]
    f = wg.shape[2]
    nj = f // tf

    def w_col(r, j, te, tot, rt):
        return (te[r], 0, jnp.where(r < tot[0], j, nj - 1))

    def w_row(r, j, te, tot, rt):
        return (te[r], jnp.where(r < tot[0], j, nj - 1), 0)

    grid_spec = pltpu.PrefetchScalarGridSpec(
        num_scalar_prefetch=3,
        grid=(n_tiles, nj),
        in_specs=[
            pl.BlockSpec(memory_space=pl.ANY),
            pl.BlockSpec((1, d, tf), w_col),
            pl.BlockSpec((1, d, tf), w_col),
            pl.BlockSpec((1, tf, d), w_row),
        ],
        out_specs=pl.BlockSpec((tm, d), lambda r, j, te, tot, rt: (r, 0)),
        scratch_shapes=[pltpu.VMEM((tm, d), F32), pltpu.VMEM((tm, d), BF16), pltpu.VMEM((tm, d), F32),
                        pltpu.SemaphoreType.DMA],
    )
    return pl.pallas_call(
        _moe_gemm_kernel,
        grid_spec=grid_spec,
        out_shape=jax.ShapeDtypeStruct((n_tiles * tm, d), F32),
        compiler_params=_cparams("arbitrary", "arbitrary"),
        name="moe_gemm",
    )(tile_expert, total_tiles, row_token, h, wg, wu, wd)


def _moe_combine_kernel(dest_ref, x_ref, w12_ref, g_ref, gate_ref, ys_hbm, o_ref, buf_ref, sem):
    i = pl.program_id(0)
    tb = x_ref.shape[0]

    def row_copy(t, k):
        src = dest_ref[2 * (i * tb + t) + k]
        return pltpu.make_async_copy(ys_hbm.at[pl.ds(src, 1), :], buf_ref.at[k, pl.ds(t, 1), :], sem)

    def start(t, carry):
        row_copy(t, 0).start()
        row_copy(t, 1).start()
        return carry

    def wait(t, carry):
        row_copy(t, 0).wait()
        row_copy(t, 1).wait()
        return carry

    lax.fori_loop(0, tb, start, 0)
    lax.fori_loop(0, tb, wait, 0)
    w12 = w12_ref[...]
    y = w12[:, 0:1] * buf_ref[0] + w12[:, 1:2] * buf_ref[1]
    o_ref[...] = x_ref[...] + gate_ref[...] * _rms(y, g_ref[...])


def _moe_combine(dest, x, w12, gain, gate, ys, tb):
    m, d = x.shape
    row = lambda i, dst: (i, 0)
    grid_spec = pltpu.PrefetchScalarGridSpec(
        num_scalar_prefetch=1,
        grid=(m // tb,),
        in_specs=[
            pl.BlockSpec((tb, d), row),
            pl.BlockSpec((tb, LANES), row),
            pl.BlockSpec((1, d), lambda i, dst: (0, 0)),
            pl.BlockSpec((1, d), lambda i, dst: (0, 0)),
            pl.BlockSpec(memory_space=pl.ANY),
        ],
        out_specs=pl.BlockSpec((tb, d), row),
        scratch_shapes=[pltpu.VMEM((2, tb, d), F32), pltpu.SemaphoreType.DMA],
    )
    return pl.pallas_call(
        _moe_combine_kernel,
        grid_spec=grid_spec,
        out_shape=jax.ShapeDtypeStruct((m, d), F32),
        compiler_params=_cparams("arbitrary"),
        name="moe_combine",
    )(dest, x, w12, gain, gate, ys)


def _route_tables(idx2, tm, n_tiles):
    m = idx2.shape[0]
    n_assign = 2 * m
    e_flat = idx2.reshape(n_assign)
    onehot = (e_flat[:, None] == jnp.arange(N_EXPERTS, dtype=jnp.int32)[None, :]).astype(jnp.int32)
    csum = jnp.cumsum(onehot, 0)
    counts = csum[-1]
    rank = jnp.sum(csum * onehot, -1) - 1
    padded = ((counts + tm - 1) // tm) * tm
    pend = jnp.cumsum(padded)
    pstart = pend - padded
    ustart = jnp.cumsum(counts) - counts
    dest = (pstart[e_flat] + rank).astype(jnp.int32)
    total_tiles = (pend[-1] // tm).astype(jnp.int32).reshape(1)
    tile_expert = jnp.minimum(
        jnp.searchsorted(pend // tm, jnp.arange(n_tiles, dtype=jnp.int32), side="right"), N_EXPERTS - 1
    ).astype(jnp.int32)
    order = jnp.argsort(e_flat, stable=True).astype(jnp.int32)
    rows = jnp.arange(n_tiles * tm, dtype=jnp.int32)
    te = tile_expert[rows // tm]
    off = rows - pstart[te]
    src = order[jnp.clip(ustart[te] + off, 0, n_assign - 1)]
    row_token = jnp.where(off < counts[te], src // 2, 0).astype(jnp.int32)
    return tile_expert, total_tiles, row_token, dest


def _moe_dense_kernel(h_ref, gates_ref, wg_ref, wu_ref, wd_ref, x_ref, g_ref, gate_ref, o_ref, acc_ref, tot_ref):
    e = pl.program_id(0)
    j = pl.program_id(1)
    nj = pl.num_programs(1)

    @pl.when(jnp.logical_and(e == 0, j == 0))
    def _():
        tot_ref[...] = jnp.zeros_like(tot_ref)

    @pl.when(j == 0)
    def _():
        acc_ref[...] = jnp.zeros_like(acc_ref)

    h = h_ref[...].astype(BF16)
    act = (_silu(_dot(h, wg_ref[0])) * _dot(h, wu_ref[0])).astype(BF16)
    acc_ref[...] += _dot(act, wd_ref[0])

    @pl.when(j == nj - 1)
    def _():
        gates = gates_ref[...]
        lane = lax.broadcasted_iota(jnp.int32, gates.shape, 1)
        ge = jnp.sum(jnp.where(lane == e, gates, 0.0), -1, keepdims=True)
        tot_ref[...] += ge * acc_ref[...]

    @pl.when(jnp.logical_and(e == pl.num_programs(0) - 1, j == nj - 1))
    def _():
        o_ref[...] = x_ref[...] + gate_ref[...] * _rms(tot_ref[...], g_ref[...])


def _moe_dense(h, gates, wg, wu, wd, x, gain, gate, tf):
    m, d = x.shape
    f = wg.shape[2]
    full = pl.BlockSpec((m, d), lambda e, j: (0, 0))
    return pl.pallas_call(
        _moe_dense_kernel,
        grid=(N_EXPERTS, f // tf),
        in_specs=[
            full,
            pl.BlockSpec((m, LANES), lambda e, j: (0, 0)),
            pl.BlockSpec((1, d, tf), lambda e, j: (e, 0, j)),
            pl.BlockSpec((1, d, tf), lambda e, j: (e, 0, j)),
            pl.BlockSpec((1, tf, d), lambda e, j: (e, j, 0)),
            full,
            pl.BlockSpec((1, d), lambda e, j: (0, 0)),
            full,
        ],
        out_specs=full,
        out_shape=jax.ShapeDtypeStruct((m, d), F32),
        scratch_shapes=[pltpu.VMEM((m, d), F32), pltpu.VMEM((m, d), F32)],
        compiler_params=_cparams("arbitrary", "arbitrary"),
        name="moe_dense",
    )(h, gates, wg, wu, wd, x, gain, gate)


def _rope_tables(pos):
    half = ROPE_DIM // 2
    inv_freq = jnp.power(ROPE_THETA, -2.0 * jnp.arange(half, dtype=F32) / ROPE_DIM)
    ang = pos.astype(F32)[:, None] * inv_freq[None, :]
    cos, sin = jnp.cos(ang), jnp.sin(ang)
    t = pos.shape[0]
    rest = ATTN_HEAD_DIM - ROPE_DIM
    cos_h = jnp.concatenate([cos, cos, jnp.ones((t, rest), F32)], 1)
    sa_h = jnp.concatenate([-sin, jnp.zeros((t, half + rest), F32)], 1)
    sb_h = jnp.concatenate([jnp.zeros((t, half), F32), sin, jnp.zeros((t, rest), F32)], 1)
    rep = LANES // ATTN_HEAD_DIM
    return tuple(jnp.tile(a, (1, rep)) for a in (cos_h, sa_h, sb_h))


def _permute_w_in(w):
    o1 = ATTN_WIDTH
    o2 = o1 + KV_WIDTH
    o3 = o2 + KV_WIDTH
    o4 = o3 + DN_CONV_CH
    o5 = o4 + DN_WIDTH
    parts = [w[:, :o1], w[:, o4:o5], w[:, o3:o4], w[:, o1:o2], w[:, o2:o3], w[:, o5:]]
    used = sum(a.shape[1] for a in parts)
    parts.append(jnp.zeros((w.shape[0], P_WIDTH - used), w.dtype))
    return jnp.concatenate(parts, 1)


def _sample_mask_bias(s, lc):
    q_pos = PAST_LEN + np.arange(s)
    k_pos = np.concatenate([PAST_LEN - lc + np.arange(lc), q_pos])
    q_chunk = q_pos[:, None] // CHUNK
    k_chunk = k_pos[None, :] // CHUNK
    mask = (k_pos[None, :] >= 0) & (k_chunk <= q_chunk) & (k_pos[None, :] >= q_chunk * CHUNK - WINDOW)
    bias = np.where(mask, 0.0, -np.inf).astype(np.float32)
    return jnp.asarray(np.tile(bias, (4, 1)))


def _conv_halo(p, init, tb, seq):
    m = p.shape[0]
    batch = m // seq
    nb = seq // tb
    conv = p[:, P_CONV:P_CONV + DN_CONV_CH].reshape(batch, nb, tb, DN_CONV_CH)
    prev = jnp.concatenate([init[:, None], conv[:, :-1, tb - (CONV_WIDTH - 1):, :]], 1)
    prev = prev.reshape(batch * nb, CONV_WIDTH - 1, DN_CONV_CH)
    return jnp.pad(prev, ((0, 0), (8 - (CONV_WIDTH - 1), 0), (0, 0)))


def _trunk(x, mods, layer_w, rope, past, cfg):
    m = x.shape[0]
    batch, seq = cfg["batch"], cfg["seq"]
    precise = cfg["precise"]
    nw = 2 if precise else 1
    ks, vs, ss, bufs = [], [], [], []
    cos, sa, sb = rope
    for l in range(DEPTH):
        w = layer_w[l]
        sh_a, sc_a, g_a, sh_f, sc_f, g_f = mods[l]
        p = _norm_proj(x, w["gain"][0], sc_a, sh_a, w["w_in"][:nw], cfg["tm_proj"], cfg["tn_proj"])
        if past is None:
            attn, k_new = _attn_prompt(p, w["sinks"], cos, sa, sb, cfg["tb_attn"])
            s0 = jnp.zeros((batch, DN_HEADS, DN_KEY_DIM, DN_VAL_DIM), F32)
            conv_init = jnp.zeros((batch, CONV_WIDTH - 1, DN_CONV_CH), F32)
        else:
            ck = past[0][l].reshape(batch, -1, KV_WIDTH)
            cv = past[1][l].reshape(batch, -1, KV_WIDTH)
            attn, k_new = _attn_sample(p, ck, cv, w["sinks"], cos, sa, sb, cfg["bias"], batch, seq)
            s0 = past[2][l]
            conv_init = past[3][l]
        halo = _conv_halo(p, conv_init, cfg["tb_dn"], seq)
        prep = _dn_prep(p, halo, w["w_conv"], w["alog"], w["dtb"], cfg["chunk"], cfg["tb_dn"], precise)
        od, s_new = _dn_scan(prep, p, s0, w["onorm"], cfg["chunk"], cfg["scan_chunks"], batch)
        x = _out_proj(attn, od, w["w_out"][:nw], x, w["gain"][1], g_a, cfg["tm_out"])
        if l % 2 == 0:
            x = _ffn(x, w["gain"][2], sc_f, sh_f, w["ffn_gate"][:nw], w["ffn_up"][:nw], w["ffn_down"][:nw],
                     w["gain"][3], g_f, cfg["tm_ffn"], cfg["tf_ffn"])
        else:
            h, gates, idx, w12 = _router(x, w["gain"][2], sc_f, sh_f, w["router"], cfg["tm_router"])
            if cfg["routed"]:
                tm = cfg["tm_moe"]
                n_tiles = 2 * m // tm + N_EXPERTS
                tile_expert, total_tiles, row_token, dest = _route_tables(idx[:, :2], tm, n_tiles)
                ys = _moe_gemm(tile_expert, total_tiles, row_token, h, w["moe_gate"], w["moe_up"], w["moe_down"],
                               tm, cfg["tf_moe"])
                x = _moe_combine(dest, x, w12, w["gain"][3], g_f, ys, cfg["tb_combine"])
            else:
                x = _moe_dense(h, gates, w["moe_gate"], w["moe_up"], w["moe_down"], x, w["gain"][3], g_f,
                               cfg["tf_moe"])
        pb = p.reshape(batch, seq, P_WIDTH)
        keep = min(WINDOW, seq) if past is None else seq
        ks.append(k_new.reshape(batch, seq, ATTN_KV_HEADS, ATTN_HEAD_DIM)[:, seq - keep:])
        vs.append(pb[:, seq - keep:, P_V:P_V + KV_WIDTH].reshape(batch, keep, ATTN_KV_HEADS, ATTN_HEAD_DIM))
        ss.append(s_new)
        assert seq >= CONV_WIDTH - 1
        bufs.append(pb[:, seq - (CONV_WIDTH - 1):, P_CONV:P_CONV + DN_CONV_CH])
    return x, jnp.stack(ks), jnp.stack(vs), jnp.stack(ss), jnp.stack(bufs)


def kernel(x_prompt, x_sample, cache_attn_k, cache_attn_v, state_delta, state_conv, c_prompt, c_sample, w_in, w_conv, attn_sinks, dn_a_log, dn_dt_bias, dn_norm, w_out, w_mod, b_mod, norm_gains, ffn_gate, ffn_up, ffn_down, moe_router, moe_gate, moe_up, moe_down):
    bp, tp, d = x_prompt.shape
    bs, ts, _ = x_sample.shape
    assert bp == 1 and d == D_MODEL

    c_all = jnp.concatenate([c_prompt, c_sample, jnp.zeros((16 - bp - bs, d), F32)], 0)
    mod = _modulation(c_all, w_mod, b_mod)
    mods_p, mods_s = [], []
    for l in range(DEPTH):
        six = jnp.split(mod[l], 6, -1)
        mods_p.append([a[0:bp] for a in six])
        mods_s.append([jnp.repeat(a[bp:bp + bs], ts, axis=0) for a in six])

    def pad_lanes(v, at):
        return jnp.zeros((1, LANES), F32).at[0, at:at + v.shape[0]].set(v)

    layer_w = []
    for l in range(DEPTH):
        w = {
            "gain": [norm_gains[l, i].reshape(1, d) for i in range(4)],
            "w_in": _split_weight(_permute_w_in(w_in[l])),
            "sinks": attn_sinks[l],
            "w_conv": jnp.pad(w_conv[l], ((0, 8 - CONV_WIDTH), (0, 0))),
            "alog": pad_lanes(dn_a_log[l], DN_HEADS),
            "dtb": pad_lanes(dn_dt_bias[l], DN_HEADS),
            "onorm": dn_norm[l].reshape(1, DN_VAL_DIM),
            "w_out": _split_weight(w_out[l]),
        }
        if l % 2 == 0:
            w["ffn_gate"] = _split_weight(ffn_gate[l // 2])
            w["ffn_up"] = _split_weight(ffn_up[l // 2])
            w["ffn_down"] = _split_weight(ffn_down[l // 2])
        else:
            w["router"] = jnp.pad(moe_router[l // 2], ((0, 0), (0, LANES - N_EXPERTS)))
            w["moe_gate"] = moe_gate[l // 2].astype(BF16)
            w["moe_up"] = moe_up[l // 2].astype(BF16)
            w["moe_down"] = moe_down[l // 2].astype(BF16)
        layer_w.append(w)

    cfg_p = dict(batch=bp, seq=tp, precise=False, chunk=CHUNK, scan_chunks=4, tm_proj=1024, tn_proj=512, tb_attn=512,
                 tb_dn=512, tm_out=512, tm_ffn=512, tf_ffn=512, tm_router=512, routed=True, tm_moe=512,
                 tf_moe=256, tb_combine=256)
    rope_p = _rope_tables(jnp.arange(tp, dtype=jnp.int32))
    y_p, k_p, v_p, s_p, conv_p = _trunk(x_prompt.reshape(bp * tp, d), mods_p, layer_w, rope_p, None, cfg_p)

    ms = bs * ts
    cfg_s = dict(batch=bs, seq=ts, precise=True, chunk=ts, scan_chunks=1, tm_proj=ms, tn_proj=512, tb_dn=ts, tm_out=ms,
                 tm_ffn=ms, tf_ffn=512, tm_router=ms, routed=False, tf_moe=256,
                 bias=_sample_mask_bias(ts, cache_attn_k.shape[2]))
    rope_s = _rope_tables(PAST_LEN + jnp.arange(ts, dtype=jnp.int32))
    past = (cache_attn_k, cache_attn_v, state_delta, state_conv)
    y_s, k_s, v_s, s_s, conv_s = _trunk(x_sample.reshape(ms, d), mods_s, layer_w, rope_s, past, cfg_s)

    return (y_p.reshape(bp, tp, d), y_s.reshape(bs, ts, d), k_p, v_p, s_p, conv_p, k_s, v_s, s_s, conv_s)
```

```python
import functools
import math

import numpy as np
import jax
import jax.numpy as jnp
from jax import lax
from jax.experimental import pallas as pl
from jax.experimental.pallas import tpu as pltpu

D_MODEL = 2048
DEPTH = 2
PAST_LEN = 1024
CHUNK = 64
ATTN_HEADS = 16
ATTN_KV_HEADS = 2
ATTN_HEAD_DIM = 64
ATTN_WIDTH = 1024
KV_WIDTH = 128
WINDOW = 128
ROPE_THETA = 500000.0
ROPE_DIM = 16
DN_HEADS = 8
DN_KEY_DIM = 128
DN_VAL_DIM = 128
DN_WIDTH = 1024
CONV_WIDTH = 4
DN_CONV_CH = 3072
D_FF = 5632
N_EXPERTS = 8
D_FF_EXPERT = 2816
EPS = 1e-6

F32 = jnp.float32
BF16 = jnp.bfloat16
LANES = 128
V7X_VMEM_LIMIT = 56 * 1024 * 1024

P_Q = 0
P_GATE = 1024
P_CONV = 2048
P_K = 5120
P_V = 5248
P_BA = 5376
P_WIDTH = 5632


def _cparams(*sem):
    return pltpu.CompilerParams(dimension_semantics=sem, vmem_limit_bytes=V7X_VMEM_LIMIT)


def _silu(x):
    return x * jax.nn.sigmoid(x)


def _rms(x, gain):
    return x * lax.rsqrt(jnp.mean(x * x, -1, keepdims=True) + EPS) * gain


def _dot(a, b):
    return jnp.dot(a, b, preferred_element_type=F32)


def _dot_nt(a, b):
    return lax.dot_general(a, b, (((1,), (1,)), ((), ())), preferred_element_type=F32)


def _dot_tn(a, b):
    return lax.dot_general(a, b, (((0,), (0,)), ((), ())), preferred_element_type=F32)


def _split_bf16(a):
    hi = a.astype(BF16)
    lo = (a - hi.astype(F32)).astype(BF16)
    return hi, lo


def _dot_x3(a, b, dot=_dot, out_axis=0):
    a_hi, a_lo = _split_bf16(a)
    b_hi, b_lo = _split_bf16(b)
    n = a.shape[out_axis]
    top = dot(jnp.concatenate([a_hi, a_lo], out_axis), b_hi)
    return top[:n] + top[n:] + dot(a_hi, b_lo)


def _dot_any(a, b, precise, dot=_dot, out_axis=0):
    if precise:
        return _dot_x3(a, b, dot, out_axis)
    return dot(a.astype(BF16), b.astype(BF16))


def _mm(a, w):
    if len(w) == 1:
        return _dot(a.astype(BF16), w[0])
    a_hi, a_lo = _split_bf16(a)
    n = a.shape[0]
    top = _dot(jnp.concatenate([a_hi, a_lo], 0), w[0])
    return top[:n] + top[n:] + _dot(a_hi, w[1])


def _split_weight(w):
    hi, lo = _split_bf16(w)
    return (hi, lo)


def _mod_spec(rows, tm, d):
    if rows == 1:
        return pl.BlockSpec((1, d), lambda i, *_: (0, 0))
    return pl.BlockSpec((tm, d), lambda i, *_: (i, 0))


def _mod_kernel(c_ref, w_ref, b_ref, o_ref):
    o_ref[0] = _dot_x3(_silu(c_ref[...]), w_ref[0]) + b_ref[0]


def _modulation(c_all, w_mod, b_mod):
    rows = c_all.shape[0]
    n = w_mod.shape[2]
    tn = 1024
    return pl.pallas_call(
        _mod_kernel,
        grid=(DEPTH, n // tn),
        in_specs=[
            pl.BlockSpec((rows, D_MODEL), lambda l, j: (0, 0)),
            pl.BlockSpec((1, D_MODEL, tn), lambda l, j: (l, 0, j)),
            pl.BlockSpec((1, 1, tn), lambda l, j: (l, 0, j)),
        ],
        out_specs=pl.BlockSpec((1, rows, tn), lambda l, j: (l, 0, j)),
        out_shape=jax.ShapeDtypeStruct((DEPTH, rows, n), F32),
        compiler_params=_cparams("parallel", "parallel"),
        name="modulation",
    )(c_all, w_mod, b_mod.reshape(DEPTH, 1, n))


def _norm_proj_kernel(nw, x_ref, g_ref, sc_ref, sh_ref, *refs):
    w_refs, (o_ref, h_ref) = refs[:nw], refs[nw:]

    @pl.when(pl.program_id(1) == 0)
    def _():
        h = _rms(x_ref[...], g_ref[...]) * (1.0 + sc_ref[...]) + sh_ref[...]
        h_ref[...] = h.astype(h_ref.dtype)

    o_ref[...] = _mm(h_ref[...], tuple(r[...] for r in w_refs))


def _norm_proj(x, gain, scale, shift, w, tm, tn):
    m, d = x.shape
    n = w[0].shape[1]
    return pl.pallas_call(
        functools.partial(_norm_proj_kernel, len(w)),
        grid=(m // tm, n // tn),
        in_specs=[
            pl.BlockSpec((tm, d), lambda i, j: (i, 0)),
            pl.BlockSpec((1, d), lambda i, j: (0, 0)),
            _mod_spec(scale.shape[0], tm, d),
            _mod_spec(shift.shape[0], tm, d),
        ] + [pl.BlockSpec((d, tn), lambda i, j: (0, j))] * len(w),
        out_specs=pl.BlockSpec((tm, tn), lambda i, j: (i, j)),
        out_shape=jax.ShapeDtypeStruct((m, n), F32),
        scratch_shapes=[pltpu.VMEM((tm, d), BF16 if len(w) == 1 else F32)],
        compiler_params=_cparams("parallel", "arbitrary"),
        name="norm_proj",
    )(x, gain, scale, shift, *w)


def _rope(x, cos, sa, sb):
    return x * cos + pltpu.roll(x, LANES - 8, 1) * sa + pltpu.roll(x, 8, 1) * sb


def _kv_variants(k, v):
    lo = lax.broadcasted_iota(jnp.int32, k.shape, 1) < ATTN_HEAD_DIM
    kr = pltpu.roll(k, ATTN_HEAD_DIM, 1)
    vr = pltpu.roll(v, ATTN_HEAD_DIM, 1)
    zero = jnp.zeros_like(k)
    k_lo = (jnp.where(lo, k, zero), jnp.where(lo, kr, zero))
    k_hi = (jnp.where(lo, zero, kr), jnp.where(lo, zero, k))
    v_lo = (jnp.where(lo, v, zero), jnp.where(lo, vr, zero))
    v_hi = (jnp.where(lo, zero, vr), jnp.where(lo, zero, v))
    return k_lo, k_hi, v_lo, v_hi


def _sink_softmax(s, sink):
    m = jnp.maximum(jnp.max(s, -1, keepdims=True), sink)
    p = jnp.exp(s - m)
    den = jnp.sum(p, -1, keepdims=True) + jnp.exp(sink - m)
    return p / den


def _attn_core(qb, k_lo, k_hi, v_lo, v_hi, bias, sink_even, sink_odd, precise=False):
    p_even = _sink_softmax(_dot_any(qb, k_lo, precise, _dot_nt) + bias, sink_even)
    p_odd = _sink_softmax(_dot_any(qb, k_hi, precise, _dot_nt) + bias, sink_odd)
    return _dot_any(p_even, v_lo, precise) + _dot_any(p_odd, v_hi, precise)


def _sink_columns(sink_ref, rows_per_pair):
    n = 4 * rows_per_pair
    pair = lax.broadcasted_iota(jnp.int32, (n, 1), 0) // rows_per_pair
    out = []
    for j in range(ATTN_KV_HEADS):
        cols = []
        for par in range(2):
            col = jnp.zeros((n, 1), F32)
            for a in range(4):
                col = jnp.where(pair == a, sink_ref[8 * j + 2 * a + par], col)
            cols.append(col)
        out.append(cols)
    return out


def _attn_prompt_kernel(sink_ref, q_ref, kv_ref, cos_ref, sa_ref, sb_ref, o_ref, knew_ref,
                        qs_ref, klo_ref, khi_ref, vlo_ref, vhi_ref):
    i = pl.program_id(0)
    tb = q_ref.shape[0]
    bufs = (klo_ref, khi_ref, vlo_ref, vhi_ref)

    @pl.when(i == 0)
    def _():
        for r in bufs:
            r[:, 0:WINDOW, :] = jnp.zeros((ATTN_KV_HEADS, WINDOW, LANES), BF16)

    @pl.when(i > 0)
    def _():
        for r in bufs:
            r[:, 0:WINDOW, :] = r[:, tb:tb + WINDOW, :]

    cos, sa, sb = cos_ref[...], sa_ref[...], sb_ref[...]
    k = _rope(kv_ref[:, 0:LANES], cos, sa, sb)
    knew_ref[...] = k
    variants = _kv_variants(k, kv_ref[:, LANES:2 * LANES])
    for r, var in zip(bufs, variants):
        for j in range(ATTN_KV_HEADS):
            r[j, WINDOW:, :] = var[j].astype(BF16)
    scale = ATTN_HEAD_DIM ** -0.5
    for a in range(ATTN_WIDTH // LANES):
        cols = slice(a * LANES, (a + 1) * LANES)
        qs_ref[:, cols] = (_rope(q_ref[:, cols], cos, sa, sb) * scale).astype(BF16)

    sinks = _sink_columns(sink_ref, CHUNK)
    nk = WINDOW + CHUNK

    def chunk_body(c, carry):
        r0 = pl.multiple_of(c * CHUNK, CHUNK)
        kpos = i * tb - WINDOW + r0 + lax.broadcasted_iota(jnp.int32, (1, nk), 1)
        bias = jnp.where(kpos >= 0, 0.0, -jnp.inf).astype(F32)
        for j in range(ATTN_KV_HEADS):
            qb = jnp.concatenate(
                [qs_ref[pl.ds(r0, CHUNK), (4 * j + a) * LANES:(4 * j + a + 1) * LANES] for a in range(4)], 0)
            o = _attn_core(qb, klo_ref[j, pl.ds(r0, nk), :], khi_ref[j, pl.ds(r0, nk), :],
                           vlo_ref[j, pl.ds(r0, nk), :], vhi_ref[j, pl.ds(r0, nk), :],
                           bias, sinks[j][0], sinks[j][1])
            for a in range(4):
                o_ref[pl.ds(r0, CHUNK), (4 * j + a) * LANES:(4 * j + a + 1) * LANES] = (
                    o[a * CHUNK:(a + 1) * CHUNK].astype(BF16))
        return carry

    lax.fori_loop(0, tb // CHUNK, chunk_body, 0)


def _attn_prompt(p, sinks, cos, sa, sb, tb):
    t = p.shape[0]
    kv_blk = P_K // (2 * LANES)
    row = lambda i: (i, 0)
    return pl.pallas_call(
        _attn_prompt_kernel,
        grid=(t // tb,),
        in_specs=[
            pl.BlockSpec(memory_space=pltpu.SMEM),
            pl.BlockSpec((tb, ATTN_WIDTH), row),
            pl.BlockSpec((tb, 2 * LANES), lambda i: (i, kv_blk)),
            pl.BlockSpec((tb, LANES), row),
            pl.BlockSpec((tb, LANES), row),
            pl.BlockSpec((tb, LANES), row),
        ],
        out_specs=[pl.BlockSpec((tb, ATTN_WIDTH), row), pl.BlockSpec((tb, LANES), row)],
        out_shape=[jax.ShapeDtypeStruct((t, ATTN_WIDTH), BF16), jax.ShapeDtypeStruct((t, LANES), F32)],
        scratch_shapes=[pltpu.VMEM((tb, ATTN_WIDTH), BF16)]
        + [pltpu.VMEM((ATTN_KV_HEADS, tb + WINDOW, LANES), BF16) for _ in range(4)],
        compiler_params=_cparams("arbitrary"),
        name="attn_prompt",
    )(sinks, p, p, cos, sa, sb)


def _attn_sample_kernel(sink_ref, q_ref, kv_ref, ck_ref, cv_ref, cos_ref, sa_ref, sb_ref, bias_ref,
                        o_ref, knew_ref):
    s = q_ref.shape[0]
    cos, sa, sb = cos_ref[...], sa_ref[...], sb_ref[...]
    k = _rope(kv_ref[:, 0:LANES], cos, sa, sb)
    knew_ref[...] = k
    kk = jnp.concatenate([ck_ref[0], k], 0)
    vv = jnp.concatenate([cv_ref[0], kv_ref[:, LANES:2 * LANES]], 0)
    k_lo, k_hi, v_lo, v_hi = _kv_variants(kk, vv)
    sinks = _sink_columns(sink_ref, s)
    scale = ATTN_HEAD_DIM ** -0.5
    bias = bias_ref[...]
    for j in range(ATTN_KV_HEADS):
        qb = jnp.concatenate(
            [_rope(q_ref[:, (4 * j + a) * LANES:(4 * j + a + 1) * LANES], cos, sa, sb) * scale for a in range(4)], 0)
        o = _attn_core(qb, k_lo[j], k_hi[j], v_lo[j], v_hi[j], bias, sinks[j][0], sinks[j][1], precise=True)
        for a in range(4):
            o_ref[:, (4 * j + a) * LANES:(4 * j + a + 1) * LANES] = o[a * s:(a + 1) * s]


def _attn_sample(p, cache_k, cache_v, sinks, cos, sa, sb, bias, batch, s):
    lc = cache_k.shape[1]
    kv_blk = P_K // (2 * LANES)
    row = lambda b: (b, 0)
    const = lambda b: (0, 0)
    return pl.pallas_call(
        _attn_sample_kernel,
        grid=(batch,),
        in_specs=[
            pl.BlockSpec(memory_space=pltpu.SMEM),
            pl.BlockSpec((s, ATTN_WIDTH), row),
            pl.BlockSpec((s, 2 * LANES), lambda b: (b, kv_blk)),
            pl.BlockSpec((1, lc, LANES), lambda b: (b, 0, 0)),
            pl.BlockSpec((1, lc, LANES), lambda b: (b, 0, 0)),
            pl.BlockSpec((s, LANES), const),
            pl.BlockSpec((s, LANES), const),
            pl.BlockSpec((s, LANES), const),
            pl.BlockSpec((4 * s, lc + s), const),
        ],
        out_specs=[pl.BlockSpec((s, ATTN_WIDTH), row), pl.BlockSpec((s, LANES), row)],
        out_shape=[jax.ShapeDtypeStruct((batch * s, ATTN_WIDTH), F32),
                   jax.ShapeDtypeStruct((batch * s, LANES), F32)],
        compiler_params=_cparams("parallel"),
        name="attn_sample",
    )(sinks, p, p, cache_k, cache_v, cos, sa, sb, bias)


class _BlockDiag:
    def __init__(self, chunk, group):
        self.chunk, self.group = chunk, group
        n = chunk * group
        lane = lax.broadcasted_iota(jnp.int32, (chunk, n), 1)
        self.lane_block = lane // chunk
        self.eye = (lax.broadcasted_iota(jnp.int32, (chunk, n), 0) == lane % chunk).astype(F32)

    def wide(self, tall):
        c = self.chunk
        out = tall[0:c]
        for b in range(1, self.group):
            out = out + tall[b * c:(b + 1) * c]
        return out

    def expand(self, wide):
        if self.group == 1:
            return wide
        zero = jnp.zeros_like(wide)
        return jnp.concatenate([jnp.where(self.lane_block == b, wide, zero) for b in range(self.group)], 0)

    def rmul(self, lhs, wide):
        l_hi, l_lo = _split_bf16(lhs)
        w_hi, w_lo = _split_bf16(wide)
        m = lhs.shape[0]
        top = _dot(jnp.concatenate([l_hi, l_lo], 0), self.expand(w_hi))
        return top[:m] + top[m:] + _dot(l_hi, self.expand(w_lo))

    def lmul(self, wide, rhs):
        w_hi, w_lo = _split_bf16(wide)
        r_hi, r_lo = _split_bf16(rhs)
        n = self.chunk * self.group
        e_hi = self.expand(w_hi)
        top = _dot(jnp.concatenate([e_hi, self.expand(w_lo)], 0), r_hi)
        return top[:n] + top[n:] + _dot(e_hi, r_lo)

    def unit_lower_inverse(self, a_tall):
        c = self.chunk
        neg = -self.wide(a_tall)
        x = self.eye + neg
        power = self.rmul(neg, neg)
        iters = int(math.log2(c)) - 1
        for it in range(iters):
            last = it == iters - 1
            r = self.rmul(x if last else jnp.concatenate([x, power], 0), power)
            x = x + r[:c]
            if not last:
                power = r[c:]
        return x


def _softplus(x):
    return jnp.maximum(x, 0.0) + jnp.log1p(jnp.exp(-jnp.abs(x)))


def _dn_prep_kernel(chunk, group, precise, qd_ref, kd_ref, vd_ref, ba_ref, hq_ref, hk_ref, hv_ref,
                    wq_ref, wk_ref, wv_ref, alog_ref, dtb_ref,
                    wv_out, wk_out, qdec_out, kend_out, p_out, gend_out):
    h = pl.program_id(1)
    tb = qd_ref.shape[0]
    n = chunk * group

    def conv_silu(x_ref, halo_ref, w_ref):
        xp = jnp.concatenate([halo_ref[0], x_ref[...]], 0)
        w = w_ref[...]
        y = xp[5:5 + tb] * w[0:1]
        for tap in range(1, CONV_WIDTH):
            y = y + xp[5 + tap:5 + tap + tb] * w[tap:tap + 1]
        return _silu(y)

    q = conv_silu(qd_ref, hq_ref, wq_ref)
    k = conv_silu(kd_ref, hk_ref, wk_ref)
    v = conv_silu(vd_ref, hv_ref, wv_ref)
    q = q * lax.rsqrt(jnp.sum(q * q, -1, keepdims=True) + EPS) * (DN_KEY_DIM ** -0.5)
    k = k * lax.rsqrt(jnp.sum(k * k, -1, keepdims=True) + EPS)

    ba = ba_ref[...]
    lane = lax.broadcasted_iota(jnp.int32, ba.shape, 1)
    beta_all = jax.nn.sigmoid(ba)
    g_all = -jnp.exp(alog_ref[...]) * _softplus(ba + dtb_ref[...])
    beta = jnp.sum(jnp.where(lane == h, beta_all, 0.0), -1, keepdims=True)
    g = jnp.sum(jnp.where(lane == h + DN_HEADS, g_all, 0.0), -1, keepdims=True)

    li = lax.broadcasted_iota(jnp.int32, (n, n), 0)
    mi = lax.broadcasted_iota(jnp.int32, (n, n), 1)
    same = (li // chunk) == (mi // chunk)
    upto = jnp.logical_and(same, li <= mi)
    since = jnp.logical_and(same, li >= mi)
    chunk_end = mi == (li // chunk) * chunk + (chunk - 1)
    blocks = _BlockDiag(chunk, group)
    op_dtype = wk_out.dtype
    slot = p_out.shape[1]

    for gi in range(tb // n):
        rows = slice(gi * n, (gi + 1) * n)
        qc, kc, vc, bc, gc = q[rows], k[rows], v[rows], beta[rows], g[rows]
        g_row = jnp.sum(jnp.where(upto, gc, 0.0), 0, keepdims=True)
        g_col = jnp.sum(jnp.where(li == mi, g_row, 0.0), 1, keepdims=True)
        g_end = jnp.sum(jnp.where(chunk_end, g_row, 0.0), 1, keepdims=True)
        decay = jnp.exp(jnp.where(since, g_col - g_row, -jnp.inf))
        qk_kk = _dot_any(jnp.concatenate([qc, kc], 0), kc, precise, _dot_nt)
        qk, kk = qk_kk[:n], qk_kk[n:]
        a = jnp.where(li > mi, bc * decay * kk, 0.0)
        t_inv = blocks.unit_lower_inverse(a)
        e_g = jnp.exp(g_col)
        rhs = jnp.concatenate([bc * vc, (bc * e_g) * kc], 1)
        w = blocks.lmul(t_inv, rhs)
        wv_out[rows, :] = w[:, :DN_VAL_DIM]
        wk_out[rows, :] = w[:, DN_VAL_DIM:].astype(op_dtype)
        qdec_out[rows, :] = (e_g * qc).astype(op_dtype)
        kend_out[rows, :] = (jnp.exp(g_end - g_col) * kc).astype(op_dtype)
        p_out[rows, 0:n] = (qk * decay).astype(op_dtype)
        if slot > n:
            p_out[rows, n:] = jnp.zeros((n, slot - n), op_dtype)
        for c in range(group):
            last = c * chunk + chunk - 1
            gend_out[gi * group + c] = jnp.broadcast_to(jnp.exp(g_end[last:last + 1]), (1, LANES))


def _dn_score_slot(chunk, group):
    return max(chunk * group, LANES)


def _dn_prep(p, halo, w_conv8, alog_row, dtb_row, chunk, group, tb, precise):
    m = p.shape[0]
    op_dtype = F32 if precise else BF16
    slot = _dn_score_slot(chunk, group)
    nh = DN_HEADS
    cq, ck, cv = P_CONV // LANES, P_CONV // LANES + nh, P_CONV // LANES + 2 * nh
    col = lambda base: (lambda i, h: (i, base + h))
    halo_spec = lambda base: pl.BlockSpec((1, 8, LANES), lambda i, h: (i, 0, base + h))
    w_spec = lambda base: pl.BlockSpec((8, LANES), lambda i, h: (0, base + h))
    const = pl.BlockSpec((1, LANES), lambda i, h: (0, 0))
    head_blk = pl.BlockSpec((tb, LANES), lambda i, h: (i, h))
    out_shape = [
        jax.ShapeDtypeStruct((m, DN_WIDTH), F32),
        jax.ShapeDtypeStruct((m, DN_WIDTH), op_dtype),
        jax.ShapeDtypeStruct((m, DN_WIDTH), op_dtype),
        jax.ShapeDtypeStruct((m, DN_WIDTH), op_dtype),
        jax.ShapeDtypeStruct((m, DN_HEADS * slot), op_dtype),
        jax.ShapeDtypeStruct((m // chunk, 1, DN_WIDTH), F32),
    ]
    return pl.pallas_call(
        functools.partial(_dn_prep_kernel, chunk, group, precise),
        grid=(m // tb, nh),
        in_specs=[
            pl.BlockSpec((tb, LANES), col(cq)),
            pl.BlockSpec((tb, LANES), col(ck)),
            pl.BlockSpec((tb, LANES), col(cv)),
            pl.BlockSpec((tb, LANES), lambda i, h: (i, P_BA // LANES)),
            halo_spec(0), halo_spec(nh), halo_spec(2 * nh),
            w_spec(0), w_spec(nh), w_spec(2 * nh),
            const, const,
        ],
        out_specs=[head_blk] * 4 + [pl.BlockSpec((tb, slot), lambda i, h: (i, h)),
                                    pl.BlockSpec((tb // chunk, 1, LANES), lambda i, h: (i, 0, h))],
        out_shape=out_shape,
        compiler_params=_cparams("parallel", "parallel"),
        name="dn_prep",
    )(p, p, p, p, halo, halo, halo, w_conv8, w_conv8, w_conv8, alog_row, dtb_row)


def _dn_scan_kernel(chunk, n_chunks, wv_ref, wk_ref, qd_ref, ke_ref, p_ref, ge_ref, gate_ref, s0_ref, onorm_ref,
                    od_ref, sout_ref, s_scr, u_scr):
    n = pl.program_id(1)
    precise = wk_ref.dtype == F32
    group_rows = chunk * n_chunks
    slot = _dn_score_slot(chunk, n_chunks)

    @pl.when(n == 0)
    def _():
        s_scr[...] = s0_ref[0]
        u_scr[...] = jnp.zeros_like(u_scr)

    onorm = onorm_ref[...]
    for c in range(n_chunks):
        rows = slice(c * chunk, (c + 1) * chunk)
        for h in range(DN_HEADS):
            cols = slice(h * LANES, (h + 1) * LANES)
            s = s_scr[h]
            if not precise:
                s = s.astype(BF16)
            u = wv_ref[rows, cols] - _dot_any(wk_ref[rows, cols], s, precise)
            if not precise:
                u = u.astype(BF16)
            u_scr[h, rows, :] = u
            o = (_dot_any(qd_ref[rows, cols], s, precise)
                 + _dot_any(p_ref[rows, h * slot:h * slot + group_rows], u_scr[h], precise))
            s_scr[h] = ge_ref[c, :, cols] * s_scr[h] + _dot_any(ke_ref[rows, cols], u, precise, _dot_tn, 1)
            gate = gate_ref[rows, cols]
            od_ref[rows, cols] = (_rms(o, onorm) * _silu(gate)).astype(od_ref.dtype)

    @pl.when(n == pl.num_programs(1) - 1)
    def _():
        sout_ref[0] = s_scr[...]


def _dn_scan(prep, p, s0, onorm_row, chunk, n_chunks, batch):
    wv, wk, qdec, kend, pm, gend = prep
    m = wv.shape[0]
    rows = chunk * n_chunks
    steps = m // batch // rows
    blk = lambda b, n: (b * steps + n, 0)
    wide = pl.BlockSpec((rows, DN_WIDTH), blk)
    state = pl.BlockSpec((1, DN_HEADS, DN_KEY_DIM, DN_VAL_DIM), lambda b, n: (b, 0, 0, 0))
    slot = _dn_score_slot(chunk, n_chunks)
    assert pm.shape[1] == DN_HEADS * slot
    return pl.pallas_call(
        functools.partial(_dn_scan_kernel, chunk, n_chunks),
        grid=(batch, steps),
        in_specs=[
            wide, wide, wide, wide, pl.BlockSpec((rows, DN_HEADS * slot), blk),
            pl.BlockSpec((n_chunks, 1, DN_WIDTH), lambda b, n: (b * steps + n, 0, 0)),
            pl.BlockSpec((rows, DN_WIDTH), lambda b, n: (b * steps + n, P_GATE // DN_WIDTH)),
            state,
            pl.BlockSpec((1, LANES), lambda b, n: (0, 0)),
        ],
        out_specs=[wide, state],
        out_shape=[jax.ShapeDtypeStruct((m, DN_WIDTH), wk.dtype),
                   jax.ShapeDtypeStruct((batch, DN_HEADS, DN_KEY_DIM, DN_VAL_DIM), F32)],
        scratch_shapes=[pltpu.VMEM((DN_HEADS, DN_KEY_DIM, DN_VAL_DIM), F32),
                        pltpu.VMEM((DN_HEADS, rows, DN_VAL_DIM), wk.dtype)],
        compiler_params=_cparams("parallel", "arbitrary"),
        name="dn_scan",
    )(wv, wk, qdec, kend, pm, gend, p, s0, onorm_row)


def _out_proj_kernel(nw, attn_ref, od_ref, *refs):
    w_refs, (x_ref, g_ref, gate_ref, o_ref) = refs[:nw], refs[nw:]
    y = (_mm(attn_ref[...], tuple(r[0:ATTN_WIDTH, :] for r in w_refs))
         + _mm(od_ref[...], tuple(r[ATTN_WIDTH:, :] for r in w_refs)))
    o_ref[...] = x_ref[...] + gate_ref[...] * _rms(y, g_ref[...])


def _out_proj(attn, od, w, x, gain, gate, tm):
    m, d = x.shape
    row = lambda i: (i, 0)
    return pl.pallas_call(
        functools.partial(_out_proj_kernel, len(w)),
        grid=(m // tm,),
        in_specs=[
            pl.BlockSpec((tm, ATTN_WIDTH), row),
            pl.BlockSpec((tm, DN_WIDTH), row),
        ] + [pl.BlockSpec((ATTN_WIDTH + DN_WIDTH, d), lambda i: (0, 0))] * len(w) + [
            pl.BlockSpec((tm, d), row),
            pl.BlockSpec((1, d), lambda i: (0, 0)),
            _mod_spec(gate.shape[0], tm, d),
        ],
        out_specs=pl.BlockSpec((tm, d), row),
        out_shape=jax.ShapeDtypeStruct((m, d), F32),
        compiler_params=_cparams("parallel"),
        name="out_proj",
    )(attn, od, *w, x, gain, gate)


def _ffn_kernel(nw, x_ref, g_ref, sc_ref, sh_ref, *refs):
    wg_refs, wu_refs, wd_refs = refs[:nw], refs[nw:2 * nw], refs[2 * nw:3 * nw]
    g2_ref, gate_ref, o_ref, h_ref, acc_ref = refs[3 * nw:]
    j = pl.program_id(1)

    @pl.when(j == 0)
    def _():
        h = _rms(x_ref[...], g_ref[...]) * (1.0 + sc_ref[...]) + sh_ref[...]
        h_ref[...] = h.astype(h_ref.dtype)
        acc_ref[...] = jnp.zeros_like(acc_ref)

    h = h_ref[...]
    act = _silu(_mm(h, tuple(r[...] for r in wg_refs))) * _mm(h, tuple(r[...] for r in wu_refs))
    acc_ref[...] += _mm(act, tuple(r[...] for r in wd_refs))

    @pl.when(j == pl.num_programs(1) - 1)
    def _():
        o_ref[...] = x_ref[...] + gate_ref[...] * _rms(acc_ref[...], g2_ref[...])


def _ffn(x, gain, scale, shift, wg, wu, wd, gain2, gate, tm, tf):
    m, d = x.shape
    nw = len(wg)
    f = wg[0].shape[1]
    row = lambda i, j: (i, 0)
    vec = pl.BlockSpec((1, d), lambda i, j: (0, 0))
    return pl.pallas_call(
        functools.partial(_ffn_kernel, nw),
        grid=(m // tm, f // tf),
        in_specs=[
            pl.BlockSpec((tm, d), row), vec,
            _mod_spec(scale.shape[0], tm, d), _mod_spec(shift.shape[0], tm, d),
        ] + [pl.BlockSpec((d, tf), lambda i, j: (0, j))] * (2 * nw)
        + [pl.BlockSpec((tf, d), lambda i, j: (j, 0))] * nw
        + [vec, _mod_spec(gate.shape[0], tm, d)],
        out_specs=pl.BlockSpec((tm, d), row),
        out_shape=jax.ShapeDtypeStruct((m, d), F32),
        scratch_shapes=[pltpu.VMEM((tm, d), BF16 if nw == 1 else F32), pltpu.VMEM((tm, d), F32)],
        compiler_params=_cparams("parallel", "arbitrary"),
        name="ffn_dense",
    )(x, gain, scale, shift, *wg, *wu, *wd, gain2, gate)


def _router_kernel(x_ref, g_ref, sc_ref, sh_ref, wr_ref, h_ref, gates_ref, idx_ref, w12_ref):
    h = _rms(x_ref[...], g_ref[...]) * (1.0 + sc_ref[...]) + sh_ref[...]
    h_ref[...] = h
    logits = jnp.dot(h, wr_ref[...], preferred_element_type=F32, precision=lax.Precision.HIGHEST)
    lane = lax.broadcasted_iota(jnp.int32, logits.shape, 1).astype(F32)
    logits = jnp.where(lane < N_EXPERTS, logits, -jnp.inf)
    m1 = jnp.max(logits, -1, keepdims=True)
    i1 = jnp.min(jnp.where(logits == m1, lane, float(LANES)), -1, keepdims=True)
    rest = jnp.where(lane == i1, -jnp.inf, logits)
    m2 = jnp.max(rest, -1, keepdims=True)
    i2 = jnp.min(jnp.where(rest == m2, lane, float(LANES)), -1, keepdims=True)
    t = jnp.exp(m2 - m1)
    w1 = 1.0 / (1.0 + t)
    w2 = t / (1.0 + t)
    gates_ref[...] = jnp.where(lane == i1, w1, 0.0) + jnp.where(lane == i2, w2, 0.0)
    idx_ref[...] = jnp.where(lane == 0.0, i1, jnp.where(lane == 1.0, i2, 0.0)).astype(jnp.int32)
    w12_ref[...] = jnp.where(lane == 0.0, w1, jnp.where(lane == 1.0, w2, 0.0))


def _router(x, gain, scale, shift, w_router_pad, tm):
    m, d = x.shape
    row = lambda i: (i, 0)
    vec = pl.BlockSpec((1, d), lambda i: (0, 0))
    small = pl.BlockSpec((tm, LANES), row)
    return pl.pallas_call(
        _router_kernel,
        grid=(m // tm,),
        in_specs=[pl.BlockSpec((tm, d), row), vec,
                  _mod_spec(scale.shape[0], tm, d), _mod_spec(shift.shape[0], tm, d),
                  pl.BlockSpec((d, LANES), lambda i: (0, 0))],
        out_specs=[pl.BlockSpec((tm, d), row), small, small, small],
        out_shape=[jax.ShapeDtypeStruct((m, d), F32), jax.ShapeDtypeStruct((m, LANES), F32),
                   jax.ShapeDtypeStruct((m, LANES), jnp.int32), jax.ShapeDtypeStruct((m, LANES), F32)],
        compiler_params=_cparams("parallel"),
        name="moe_router",
    )(x, gain, scale, shift, w_router_pad)


def _moe_gemm_kernel(te_ref, tot_ref, rt_ref, h_hbm, wg_ref, wu_ref, wd_ref, ys_ref, xs_ref, xb_ref, acc_ref, sem):
    r = pl.program_id(0)
    j = pl.program_id(1)
    tm = xs_ref.shape[0]
    active = r < tot_ref[0]

    def row_copy(t):
        return pltpu.make_async_copy(h_hbm.at[pl.ds(rt_ref[r * tm + t], 1), :], xs_ref.at[pl.ds(t, 1), :], sem)

    @pl.when(jnp.logical_and(active, j == 0))
    def _():
        def start(t, carry):
            row_copy(t).start()
            return carry

        def wait(t, carry):
            row_copy(t).wait()
            return carry

        lax.fori_loop(0, tm, start, 0)
        lax.fori_loop(0, tm, wait, 0)
        xb_ref[...] = xs_ref[...].astype(BF16)
        acc_ref[...] = jnp.zeros_like(acc_ref)

    @pl.when(active)
    def _():
        xb = xb_ref[...]
        act = (_silu(_dot(xb, wg_ref[0])) * _dot(xb, wu_ref[0])).astype(BF16)
        acc_ref[...] += _dot(act, wd_ref[0])

    @pl.when(j == pl.num_programs(1) - 1)
    def _():
        ys_ref[...] = jnp.where(active, acc_ref[...], 0.0)


def _moe_gemm(tile_expert, total_tiles, row_token, h, wg, wu, wd, tm, tf):
    n_tiles = tile_expert.shape[0]
    d = h.shape[1]
    f = wg.shape[2]
    nj = f // tf

    def w_col(r, j, te, tot, rt):
        return (te[r], 0, jnp.where(r < tot[0], j, nj - 1))

    def w_row(r, j, te, tot, rt):
        return (te[r], jnp.where(r < tot[0], j, nj - 1), 0)

    grid_spec = pltpu.PrefetchScalarGridSpec(
        num_scalar_prefetch=3,
        grid=(n_tiles, nj),
        in_specs=[
            pl.BlockSpec(memory_space=pl.ANY),
            pl.BlockSpec((1, d, tf), w_col),
            pl.BlockSpec((1, d, tf), w_col),
            pl.BlockSpec((1, tf, d), w_row),
        ],
        out_specs=pl.BlockSpec((tm, d), lambda r, j, te, tot, rt: (r, 0)),
        scratch_shapes=[pltpu.VMEM((tm, d), F32), pltpu.VMEM((tm, d), BF16), pltpu.VMEM((tm, d), F32),
                        pltpu.SemaphoreType.DMA],
    )
    return pl.pallas_call(
        _moe_gemm_kernel,
        grid_spec=grid_spec,
        out_shape=jax.ShapeDtypeStruct((n_tiles * tm, d), F32),
        compiler_params=_cparams("arbitrary", "arbitrary"),
        name="moe_gemm",
    )(tile_expert, total_tiles, row_token, h, wg, wu, wd)


def _moe_combine_kernel(dest_ref, x_ref, w12_ref, g_ref, gate_ref, ys_hbm, o_ref, buf_ref, sem):
    i = pl.program_id(0)
    tb = x_ref.shape[0]

    def row_copy(t, k):
        src = dest_ref[2 * (i * tb + t) + k]
        return pltpu.make_async_copy(ys_hbm.at[pl.ds(src, 1), :], buf_ref.at[k, pl.ds(t, 1), :], sem)

    def start(t, carry):
        row_copy(t, 0).start()
        row_copy(t, 1).start()
        return carry

    def wait(t, carry):
        row_copy(t, 0).wait()
        row_copy(t, 1).wait()
        return carry

    lax.fori_loop(0, tb, start, 0)
    lax.fori_loop(0, tb, wait, 0)
    w12 = w12_ref[...]
    y = w12[:, 0:1] * buf_ref[0] + w12[:, 1:2] * buf_ref[1]
    o_ref[...] = x_ref[...] + gate_ref[...] * _rms(y, g_ref[...])


def _moe_combine(dest, x, w12, gain, gate, ys, tb):
    m, d = x.shape
    row = lambda i, dst: (i, 0)
    grid_spec = pltpu.PrefetchScalarGridSpec(
        num_scalar_prefetch=1,
        grid=(m // tb,),
        in_specs=[
            pl.BlockSpec((tb, d), row),
            pl.BlockSpec((tb, LANES), row),
            pl.BlockSpec((1, d), lambda i, dst: (0, 0)),
            pl.BlockSpec((1, d), lambda i, dst: (0, 0)),
            pl.BlockSpec(memory_space=pl.ANY),
        ],
        out_specs=pl.BlockSpec((tb, d), row),
        scratch_shapes=[pltpu.VMEM((2, tb, d), F32), pltpu.SemaphoreType.DMA],
    )
    return pl.pallas_call(
        _moe_combine_kernel,
        grid_spec=grid_spec,
        out_shape=jax.ShapeDtypeStruct((m, d), F32),
        compiler_params=_cparams("arbitrary"),
        name="moe_combine",
    )(dest, x, w12, gain, gate, ys)


def _route_tables(idx2, tm, n_tiles):
    m = idx2.shape[0]
    n_assign = 2 * m
    assert n_tiles * tm == n_assign + N_EXPERTS * tm
    experts = jnp.arange(N_EXPERTS, dtype=jnp.int32)
    e_flat = idx2.reshape(n_assign)
    onehot = (e_flat[:, None] == experts[None, :]).astype(jnp.int32)
    csum = jnp.cumsum(onehot, 0)
    counts = csum[-1]
    padded = ((counts + tm - 1) // tm) * tm
    pend = jnp.cumsum(padded)
    pstart = pend - padded
    dest = jnp.sum(onehot * (pstart[None, :] + csum - 1), -1).astype(jnp.int32)
    total_tiles = (pend[-1] // tm).astype(jnp.int32).reshape(1)
    tile_expert = jnp.minimum(
        jnp.searchsorted(pend // tm, jnp.arange(n_tiles, dtype=jnp.int32), side="right"), N_EXPERTS - 1
    ).astype(jnp.int32)
    filler_key = jnp.where(jnp.arange(tm, dtype=jnp.int32)[None, :] < (padded - counts)[:, None],
                           experts[:, None], N_EXPERTS).reshape(-1)
    keys = jnp.concatenate([e_flat, filler_key])
    tokens = jnp.concatenate([jnp.arange(n_assign, dtype=jnp.int32) // 2,
                              jnp.zeros((N_EXPERTS * tm,), jnp.int32)])
    _, row_token = lax.sort((keys, tokens), num_keys=1, is_stable=True)
    return tile_expert, total_tiles, row_token, dest


def _moe_dense_kernel(h_ref, gates_ref, wg_ref, wu_ref, wd_ref, x_ref, g_ref, gate_ref, o_ref, acc_ref, tot_ref):
    e = pl.program_id(0)
    j = pl.program_id(1)
    nj = pl.num_programs(1)

    @pl.when(jnp.logical_and(e == 0, j == 0))
    def _():
        tot_ref[...] = jnp.zeros_like(tot_ref)

    @pl.when(j == 0)
    def _():
        acc_ref[...] = jnp.zeros_like(acc_ref)

    h = h_ref[...].astype(BF16)
    act = (_silu(_dot(h, wg_ref[0])) * _dot(h, wu_ref[0])).astype(BF16)
    acc_ref[...] += _dot(act, wd_ref[0])

    @pl.when(j == nj - 1)
    def _():
        gates = gates_ref[...]
        lane = lax.broadcasted_iota(jnp.int32, gates.shape, 1)
        ge = jnp.sum(jnp.where(lane == e, gates, 0.0), -1, keepdims=True)
        tot_ref[...] += ge * acc_ref[...]

    @pl.when(jnp.logical_and(e == pl.num_programs(0) - 1, j == nj - 1))
    def _():
        o_ref[...] = x_ref[...] + gate_ref[...] * _rms(tot_ref[...], g_ref[...])


def _moe_dense(h, gates, wg, wu, wd, x, gain, gate, tf):
    m, d = x.shape
    f = wg.shape[2]
    full = pl.BlockSpec((m, d), lambda e, j: (0, 0))
    return pl.pallas_call(
        _moe_dense_kernel,
        grid=(N_EXPERTS, f // tf),
        in_specs=[
            full,
            pl.BlockSpec((m, LANES), lambda e, j: (0, 0)),
            pl.BlockSpec((1, d, tf), lambda e, j: (e, 0, j)),
            pl.BlockSpec((1, d, tf), lambda e, j: (e, 0, j)),
            pl.BlockSpec((1, tf, d), lambda e, j: (e, j, 0)),
            full,
            pl.BlockSpec((1, d), lambda e, j: (0, 0)),
            full,
        ],
        out_specs=full,
        out_shape=jax.ShapeDtypeStruct((m, d), F32),
        scratch_shapes=[pltpu.VMEM((m, d), F32), pltpu.VMEM((m, d), F32)],
        compiler_params=_cparams("arbitrary", "arbitrary"),
        name="moe_dense",
    )(h, gates, wg, wu, wd, x, gain, gate)


def _rope_tables(pos):
    half = ROPE_DIM // 2
    inv_freq = jnp.power(ROPE_THETA, -2.0 * jnp.arange(half, dtype=F32) / ROPE_DIM)
    ang = pos.astype(F32)[:, None] * inv_freq[None, :]
    cos, sin = jnp.cos(ang), jnp.sin(ang)
    t = pos.shape[0]
    rest = ATTN_HEAD_DIM - ROPE_DIM
    cos_h = jnp.concatenate([cos, cos, jnp.ones((t, rest), F32)], 1)
    sa_h = jnp.concatenate([-sin, jnp.zeros((t, half + rest), F32)], 1)
    sb_h = jnp.concatenate([jnp.zeros((t, half), F32), sin, jnp.zeros((t, rest), F32)], 1)
    rep = LANES // ATTN_HEAD_DIM
    return tuple(jnp.tile(a, (1, rep)) for a in (cos_h, sa_h, sb_h))


def _permute_w_in(w):
    o1 = ATTN_WIDTH
    o2 = o1 + KV_WIDTH
    o3 = o2 + KV_WIDTH
    o4 = o3 + DN_CONV_CH
    o5 = o4 + DN_WIDTH
    parts = [w[:, :o1], w[:, o4:o5], w[:, o3:o4], w[:, o1:o2], w[:, o2:o3], w[:, o5:]]
    used = sum(a.shape[1] for a in parts)
    parts.append(jnp.zeros((w.shape[0], P_WIDTH - used), w.dtype))
    return jnp.concatenate(parts, 1)


def _sample_mask_bias(s, lc):
    q_pos = PAST_LEN + np.arange(s)
    k_pos = np.concatenate([PAST_LEN - lc + np.arange(lc), q_pos])
    q_chunk = q_pos[:, None] // CHUNK
    k_chunk = k_pos[None, :] // CHUNK
    mask = (k_pos[None, :] >= 0) & (k_chunk <= q_chunk) & (k_pos[None, :] >= q_chunk * CHUNK - WINDOW)
    bias = np.where(mask, 0.0, -np.inf).astype(np.float32)
    return jnp.asarray(np.tile(bias, (4, 1)))


def _conv_halo(p, init, tb, seq):
    m = p.shape[0]
    batch = m // seq
    nb = seq // tb
    tails = p.reshape(batch, nb, tb, P_WIDTH)[:, :nb - 1, tb - (CONV_WIDTH - 1):, P_CONV:P_CONV + DN_CONV_CH]
    prev = jnp.concatenate([init[:, None], tails], 1)
    prev = prev.reshape(batch * nb, CONV_WIDTH - 1, DN_CONV_CH)
    return jnp.pad(prev, ((0, 0), (8 - (CONV_WIDTH - 1), 0), (0, 0)))


def _trunk(x, mods, layer_w, rope, past, cfg):
    m = x.shape[0]
    batch, seq = cfg["batch"], cfg["seq"]
    precise = cfg["precise"]
    nw = 2 if precise else 1
    ks, vs, ss, bufs = [], [], [], []
    cos, sa, sb = rope
    for l in range(DEPTH):
        w = layer_w[l]
        sh_a, sc_a, g_a, sh_f, sc_f, g_f = mods[l]
        p = _norm_proj(x, w["gain"][0], sc_a, sh_a, w["w_in"][:nw], cfg["tm_proj"], cfg["tn_proj"])
        if past is None:
            attn, k_new = _attn_prompt(p, w["sinks"], cos, sa, sb, cfg["tb_attn"])
            s0 = jnp.zeros((batch, DN_HEADS, DN_KEY_DIM, DN_VAL_DIM), F32)
            conv_init = jnp.zeros((batch, CONV_WIDTH - 1, DN_CONV_CH), F32)
        else:
            ck = past[0][l].reshape(batch, -1, KV_WIDTH)
            cv = past[1][l].reshape(batch, -1, KV_WIDTH)
            attn, k_new = _attn_sample(p, ck, cv, w["sinks"], cos, sa, sb, cfg["bias"], batch, seq)
            s0 = past[2][l]
            conv_init = past[3][l]
        halo = _conv_halo(p, conv_init, cfg["tb_dn"], seq)
        prep = _dn_prep(p, halo, w["w_conv"], w["alog"], w["dtb"], cfg["chunk"], cfg["group"], cfg["tb_dn"],
                        precise)
        od, s_new = _dn_scan(prep, p, s0, w["onorm"], cfg["chunk"], cfg["group"], batch)
        x = _out_proj(attn, od, w["w_out"][:nw], x, w["gain"][1], g_a, cfg["tm_out"])
        if l % 2 == 0:
            x = _ffn(x, w["gain"][2], sc_f, sh_f, w["ffn_gate"][:nw], w["ffn_up"][:nw], w["ffn_down"][:nw],
                     w["gain"][3], g_f, cfg["tm_ffn"], cfg["tf_ffn"])
        else:
            h, gates, idx, w12 = _router(x, w["gain"][2], sc_f, sh_f, w["router"], cfg["tm_router"])
            if cfg["routed"]:
                tm = cfg["tm_moe"]
                n_tiles = 2 * m // tm + N_EXPERTS
                tile_expert, total_tiles, row_token, dest = _route_tables(idx[:, :2], tm, n_tiles)
                ys = _moe_gemm(tile_expert, total_tiles, row_token, h, w["moe_gate"], w["moe_up"], w["moe_down"],
                               tm, cfg["tf_moe"])
                x = _moe_combine(dest, x, w12, w["gain"][3], g_f, ys, cfg["tb_combine"])
            else:
                x = _moe_dense(h, gates, w["moe_gate"], w["moe_up"], w["moe_down"], x, w["gain"][3], g_f,
                               cfg["tf_moe"])
        pb = p.reshape(batch, seq, P_WIDTH)
        keep = min(WINDOW, seq) if past is None else seq
        ks.append(k_new.reshape(batch, seq, ATTN_KV_HEADS, ATTN_HEAD_DIM)[:, seq - keep:])
        vs.append(pb[:, seq - keep:, P_V:P_V + KV_WIDTH].reshape(batch, keep, ATTN_KV_HEADS, ATTN_HEAD_DIM))
        ss.append(s_new)
        assert seq >= CONV_WIDTH - 1
        bufs.append(pb[:, seq - (CONV_WIDTH - 1):, P_CONV:P_CONV + DN_CONV_CH])
    return x, jnp.stack(ks), jnp.stack(vs), jnp.stack(ss), jnp.stack(bufs)


def kernel(x_prompt, x_sample, cache_attn_k, cache_attn_v, state_delta, state_conv, c_prompt, c_sample, w_in, w_conv, attn_sinks, dn_a_log, dn_dt_bias, dn_norm, w_out, w_mod, b_mod, norm_gains, ffn_gate, ffn_up, ffn_down, moe_router, moe_gate, moe_up, moe_down):
    bp, tp, d = x_prompt.shape
    bs, ts, _ = x_sample.shape
    assert bp == 1 and d == D_MODEL

    c_all = jnp.concatenate([c_prompt, c_sample, jnp.zeros((16 - bp - bs, d), F32)], 0)
    mod = _modulation(c_all, w_mod, b_mod)
    mods_p, mods_s = [], []
    for l in range(DEPTH):
        six = jnp.split(mod[l], 6, -1)
        mods_p.append([a[0:bp] for a in six])
        mods_s.append([jnp.repeat(a[bp:bp + bs], ts, axis=0) for a in six])

    def pad_lanes(v, at):
        return jnp.zeros((1, LANES), F32).at[0, at:at + v.shape[0]].set(v)

    layer_w = []
    for l in range(DEPTH):
        w = {
            "gain": [norm_gains[l, i].reshape(1, d) for i in range(4)],
            "w_in": _split_weight(_permute_w_in(w_in[l])),
            "sinks": attn_sinks[l],
            "w_conv": jnp.pad(w_conv[l], ((0, 8 - CONV_WIDTH), (0, 0))),
            "alog": pad_lanes(dn_a_log[l], DN_HEADS),
            "dtb": pad_lanes(dn_dt_bias[l], DN_HEADS),
            "onorm": dn_norm[l].reshape(1, DN_VAL_DIM),
            "w_out": _split_weight(w_out[l]),
        }
        if l % 2 == 0:
            w["ffn_gate"] = _split_weight(ffn_gate[l // 2])
            w["ffn_up"] = _split_weight(ffn_up[l // 2])
            w["ffn_down"] = _split_weight(ffn_down[l // 2])
        else:
            w["router"] = jnp.pad(moe_router[l // 2], ((0, 0), (0, LANES - N_EXPERTS)))
            w["moe_gate"] = moe_gate[l // 2].astype(BF16)
            w["moe_up"] = moe_up[l // 2].astype(BF16)
            w["moe_down"] = moe_down[l // 2].astype(BF16)
        layer_w.append(w)

    cfg_p = dict(batch=bp, seq=tp, precise=False, chunk=CHUNK, group=4, tm_proj=1024, tn_proj=512, tb_attn=512,
                 tb_dn=512, tm_out=512, tm_ffn=512, tf_ffn=512, tm_router=512, routed=True, tm_moe=512,
                 tf_moe=256, tb_combine=256)
    rope_p = _rope_tables(jnp.arange(tp, dtype=jnp.int32))
    y_p, k_p, v_p, s_p, conv_p = _trunk(x_prompt.reshape(bp * tp, d), mods_p, layer_w, rope_p, None, cfg_p)

    ms = bs * ts
    cfg_s = dict(batch=bs, seq=ts, precise=True, chunk=ts, group=1, tm_proj=ms, tn_proj=512, tb_dn=ts, tm_out=ms,
                 tm_ffn=ms, tf_ffn=512, tm_router=ms, routed=False, tf_moe=256,
                 bias=_sample_mask_bias(ts, cache_attn_k.shape[2]))
    rope_s = _rope_tables(PAST_LEN + jnp.arange(ts, dtype=jnp.int32))
    past = (cache_attn_k, cache_attn_v, state_delta, state_conv)
    y_s, k_s, v_s, s_s, conv_s = _trunk(x_sample.reshape(ms, d), mods_s, layer_w, rope_s, past, cfg_s)

    return (y_p.reshape(bp, tp, d), y_s.reshape(bs, ts, d), k_p, v_p, s_p, conv_p, k_s, v_s, s_s, conv_s)
```

```python
import functools
import math

import numpy as np
import jax
import jax.numpy as jnp
from jax import lax
from jax.experimental import pallas as pl
from jax.experimental.pallas import tpu as pltpu

D_MODEL = 2048
DEPTH = 2
PAST_LEN = 1024
CHUNK = 64
ATTN_HEADS = 16
ATTN_KV_HEADS = 2
ATTN_HEAD_DIM = 64
ATTN_WIDTH = 1024
KV_WIDTH = 128
WINDOW = 128
ROPE_THETA = 500000.0
ROPE_DIM = 16
DN_HEADS = 8
DN_KEY_DIM = 128
DN_VAL_DIM = 128
DN_WIDTH = 1024
CONV_WIDTH = 4
DN_CONV_CH = 3072
D_FF = 5632
N_EXPERTS = 8
D_FF_EXPERT = 2816
EPS = 1e-6

F32 = jnp.float32
BF16 = jnp.bfloat16
LANES = 128
V7X_VMEM_LIMIT = 56 * 1024 * 1024

P_Q = 0
P_GATE = 1024
P_CONV = 2048
P_K = 5120
P_V = 5248
P_BA = 5376
P_WIDTH = 5632


def _cparams(*sem):
    return pltpu.CompilerParams(dimension_semantics=sem, vmem_limit_bytes=V7X_VMEM_LIMIT)


def _silu(x):
    return x * jax.nn.sigmoid(x)


def _rms(x, gain):
    return x * lax.rsqrt(jnp.mean(x * x, -1, keepdims=True) + EPS) * gain


def _dot(a, b):
    return jnp.dot(a, b, preferred_element_type=F32)


def _dot_nt(a, b):
    return lax.dot_general(a, b, (((1,), (1,)), ((), ())), preferred_element_type=F32)


def _dot_tn(a, b):
    return lax.dot_general(a, b, (((0,), (0,)), ((), ())), preferred_element_type=F32)


def _split_bf16(a):
    hi = a.astype(BF16)
    lo = (a - hi.astype(F32)).astype(BF16)
    return hi, lo


def _dot_x3(a, b, dot=_dot, out_axis=0):
    a_hi, a_lo = _split_bf16(a)
    b_hi, b_lo = _split_bf16(b)
    n = a.shape[out_axis]
    top = dot(jnp.concatenate([a_hi, a_lo], out_axis), b_hi)
    return top[:n] + top[n:] + dot(a_hi, b_lo)


def _dot_any(a, b, precise, dot=_dot, out_axis=0):
    if precise:
        return _dot_x3(a, b, dot, out_axis)
    return dot(a.astype(BF16), b.astype(BF16))


def _mm(a, w):
    if len(w) == 1:
        return _dot(a.astype(BF16), w[0])
    a_hi, a_lo = _split_bf16(a)
    n = a.shape[0]
    top = _dot(jnp.concatenate([a_hi, a_lo], 0), w[0])
    return top[:n] + top[n:] + _dot(a_hi, w[1])


def _split_weight(w):
    hi, lo = _split_bf16(w)
    return (hi, lo)


def _mod_spec(rows, tm, d):
    if rows == 1:
        return pl.BlockSpec((1, d), lambda i, *_: (0, 0))
    return pl.BlockSpec((tm, d), lambda i, *_: (i, 0))


def _mod_kernel(c_ref, w_ref, b_ref, o_ref):
    o_ref[0] = _dot_x3(_silu(c_ref[...]), w_ref[0]) + b_ref[0]


def _modulation(c_all, w_mod, b_mod):
    rows = c_all.shape[0]
    n = w_mod.shape[2]
    tn = 1024
    return pl.pallas_call(
        _mod_kernel,
        grid=(DEPTH, n // tn),
        in_specs=[
            pl.BlockSpec((rows, D_MODEL), lambda l, j: (0, 0)),
            pl.BlockSpec((1, D_MODEL, tn), lambda l, j: (l, 0, j)),
            pl.BlockSpec((1, 1, tn), lambda l, j: (l, 0, j)),
        ],
        out_specs=pl.BlockSpec((1, rows, tn), lambda l, j: (l, 0, j)),
        out_shape=jax.ShapeDtypeStruct((DEPTH, rows, n), F32),
        compiler_params=_cparams("parallel", "parallel"),
        name="modulation",
    )(c_all, w_mod, b_mod.reshape(DEPTH, 1, n))


def _norm_proj_kernel(nw, x_ref, g_ref, sc_ref, sh_ref, *refs):
    w_refs, (o_ref, h_ref) = refs[:nw], refs[nw:]

    @pl.when(pl.program_id(1) == 0)
    def _():
        h = _rms(x_ref[...], g_ref[...]) * (1.0 + sc_ref[...]) + sh_ref[...]
        h_ref[...] = h.astype(h_ref.dtype)

    o_ref[...] = _mm(h_ref[...], tuple(r[...] for r in w_refs))


def _norm_proj(x, gain, scale, shift, w, tm, tn):
    m, d = x.shape
    n = w[0].shape[1]
    return pl.pallas_call(
        functools.partial(_norm_proj_kernel, len(w)),
        grid=(m // tm, n // tn),
        in_specs=[
            pl.BlockSpec((tm, d), lambda i, j: (i, 0)),
            pl.BlockSpec((1, d), lambda i, j: (0, 0)),
            _mod_spec(scale.shape[0], tm, d),
            _mod_spec(shift.shape[0], tm, d),
        ] + [pl.BlockSpec((d, tn), lambda i, j: (0, j))] * len(w),
        out_specs=pl.BlockSpec((tm, tn), lambda i, j: (i, j)),
        out_shape=jax.ShapeDtypeStruct((m, n), F32),
        scratch_shapes=[pltpu.VMEM((tm, d), BF16 if len(w) == 1 else F32)],
        compiler_params=_cparams("parallel", "arbitrary"),
        name="norm_proj",
    )(x, gain, scale, shift, *w)


def _rope(x, cos, sa, sb):
    return x * cos + pltpu.roll(x, LANES - 8, 1) * sa + pltpu.roll(x, 8, 1) * sb


def _kv_variants(k, v):
    lo = lax.broadcasted_iota(jnp.int32, k.shape, 1) < ATTN_HEAD_DIM
    kr = pltpu.roll(k, ATTN_HEAD_DIM, 1)
    vr = pltpu.roll(v, ATTN_HEAD_DIM, 1)
    zero = jnp.zeros_like(k)
    k_lo = (jnp.where(lo, k, zero), jnp.where(lo, kr, zero))
    k_hi = (jnp.where(lo, zero, kr), jnp.where(lo, zero, k))
    v_lo = (jnp.where(lo, v, zero), jnp.where(lo, vr, zero))
    v_hi = (jnp.where(lo, zero, vr), jnp.where(lo, zero, v))
    return k_lo, k_hi, v_lo, v_hi


def _sink_softmax(s, sink):
    m = jnp.maximum(jnp.max(s, -1, keepdims=True), sink)
    p = jnp.exp(s - m)
    den = jnp.sum(p, -1, keepdims=True) + jnp.exp(sink - m)
    return p / den


def _attn_core(qb, k_lo, k_hi, v_lo, v_hi, bias, sink_even, sink_odd, precise=False):
    p_even = _sink_softmax(_dot_any(qb, k_lo, precise, _dot_nt) + bias, sink_even)
    p_odd = _sink_softmax(_dot_any(qb, k_hi, precise, _dot_nt) + bias, sink_odd)
    return _dot_any(p_even, v_lo, precise) + _dot_any(p_odd, v_hi, precise)


def _sink_columns(sink_ref, rows_per_pair):
    n = 4 * rows_per_pair
    pair = lax.broadcasted_iota(jnp.int32, (n, 1), 0) // rows_per_pair
    out = []
    for j in range(ATTN_KV_HEADS):
        cols = []
        for par in range(2):
            col = jnp.zeros((n, 1), F32)
            for a in range(4):
                col = jnp.where(pair == a, sink_ref[8 * j + 2 * a + par], col)
            cols.append(col)
        out.append(cols)
    return out


def _attn_prompt_kernel(sink_ref, q_ref, kv_ref, cos_ref, sa_ref, sb_ref, o_ref, knew_ref,
                        qs_ref, klo_ref, khi_ref, vlo_ref, vhi_ref):
    i = pl.program_id(0)
    tb = q_ref.shape[0]
    bufs = (klo_ref, khi_ref, vlo_ref, vhi_ref)

    @pl.when(i == 0)
    def _():
        for r in bufs:
            r[:, 0:WINDOW, :] = jnp.zeros((ATTN_KV_HEADS, WINDOW, LANES), BF16)

    @pl.when(i > 0)
    def _():
        for r in bufs:
            r[:, 0:WINDOW, :] = r[:, tb:tb + WINDOW, :]

    cos, sa, sb = cos_ref[...], sa_ref[...], sb_ref[...]
    k = _rope(kv_ref[:, 0:LANES], cos, sa, sb)
    knew_ref[...] = k
    variants = _kv_variants(k, kv_ref[:, LANES:2 * LANES])
    for r, var in zip(bufs, variants):
        for j in range(ATTN_KV_HEADS):
            r[j, WINDOW:, :] = var[j].astype(BF16)
    scale = ATTN_HEAD_DIM ** -0.5
    for a in range(ATTN_WIDTH // LANES):
        cols = slice(a * LANES, (a + 1) * LANES)
        qs_ref[:, cols] = (_rope(q_ref[:, cols], cos, sa, sb) * scale).astype(BF16)

    sinks = _sink_columns(sink_ref, CHUNK)
    nk = WINDOW + CHUNK

    def chunk_body(c, carry):
        r0 = pl.multiple_of(c * CHUNK, CHUNK)
        kpos = i * tb - WINDOW + r0 + lax.broadcasted_iota(jnp.int32, (1, nk), 1)
        bias = jnp.where(kpos >= 0, 0.0, -jnp.inf).astype(F32)
        for j in range(ATTN_KV_HEADS):
            qb = jnp.concatenate(
                [qs_ref[pl.ds(r0, CHUNK), (4 * j + a) * LANES:(4 * j + a + 1) * LANES] for a in range(4)], 0)
            o = _attn_core(qb, klo_ref[j, pl.ds(r0, nk), :], khi_ref[j, pl.ds(r0, nk), :],
                           vlo_ref[j, pl.ds(r0, nk), :], vhi_ref[j, pl.ds(r0, nk), :],
                           bias, sinks[j][0], sinks[j][1])
            for a in range(4):
                o_ref[pl.ds(r0, CHUNK), (4 * j + a) * LANES:(4 * j + a + 1) * LANES] = (
                    o[a * CHUNK:(a + 1) * CHUNK].astype(BF16))
        return carry

    lax.fori_loop(0, tb // CHUNK, chunk_body, 0)


def _attn_prompt(p, sinks, cos, sa, sb, tb):
    t = p.shape[0]
    kv_blk = P_K // (2 * LANES)
    row = lambda i: (i, 0)
    return pl.pallas_call(
        _attn_prompt_kernel,
        grid=(t // tb,),
        in_specs=[
            pl.BlockSpec(memory_space=pltpu.SMEM),
            pl.BlockSpec((tb, ATTN_WIDTH), row),
            pl.BlockSpec((tb, 2 * LANES), lambda i: (i, kv_blk)),
            pl.BlockSpec((tb, LANES), row),
            pl.BlockSpec((tb, LANES), row),
            pl.BlockSpec((tb, LANES), row),
        ],
        out_specs=[pl.BlockSpec((tb, ATTN_WIDTH), row), pl.BlockSpec((tb, LANES), row)],
        out_shape=[jax.ShapeDtypeStruct((t, ATTN_WIDTH), BF16), jax.ShapeDtypeStruct((t, LANES), F32)],
        scratch_shapes=[pltpu.VMEM((tb, ATTN_WIDTH), BF16)]
        + [pltpu.VMEM((ATTN_KV_HEADS, tb + WINDOW, LANES), BF16) for _ in range(4)],
        compiler_params=_cparams("arbitrary"),
        name="attn_prompt",
    )(sinks, p, p, cos, sa, sb)


def _attn_sample_kernel(sink_ref, q_ref, kv_ref, ck_ref, cv_ref, cos_ref, sa_ref, sb_ref, bias_ref,
                        o_ref, knew_ref):
    s = q_ref.shape[0]
    cos, sa, sb = cos_ref[...], sa_ref[...], sb_ref[...]
    k = _rope(kv_ref[:, 0:LANES], cos, sa, sb)
    knew_ref[...] = k
    kk = jnp.concatenate([ck_ref[0], k], 0)
    vv = jnp.concatenate([cv_ref[0], kv_ref[:, LANES:2 * LANES]], 0)
    k_lo, k_hi, v_lo, v_hi = _kv_variants(kk, vv)
    sinks = _sink_columns(sink_ref, s)
    scale = ATTN_HEAD_DIM ** -0.5
    bias = bias_ref[...]
    for j in range(ATTN_KV_HEADS):
        qb = jnp.concatenate(
            [_rope(q_ref[:, (4 * j + a) * LANES:(4 * j + a + 1) * LANES], cos, sa, sb) * scale for a in range(4)], 0)
        o = _attn_core(qb, k_lo[j], k_hi[j], v_lo[j], v_hi[j], bias, sinks[j][0], sinks[j][1], precise=True)
        for a in range(4):
            o_ref[:, (4 * j + a) * LANES:(4 * j + a + 1) * LANES] = o[a * s:(a + 1) * s]


def _attn_sample(p, cache_k, cache_v, sinks, cos, sa, sb, bias, batch, s):
    lc = cache_k.shape[1]
    kv_blk = P_K // (2 * LANES)
    row = lambda b: (b, 0)
    const = lambda b: (0, 0)
    return pl.pallas_call(
        _attn_sample_kernel,
        grid=(batch,),
        in_specs=[
            pl.BlockSpec(memory_space=pltpu.SMEM),
            pl.BlockSpec((s, ATTN_WIDTH), row),
            pl.BlockSpec((s, 2 * LANES), lambda b: (b, kv_blk)),
            pl.BlockSpec((1, lc, LANES), lambda b: (b, 0, 0)),
            pl.BlockSpec((1, lc, LANES), lambda b: (b, 0, 0)),
            pl.BlockSpec((s, LANES), const),
            pl.BlockSpec((s, LANES), const),
            pl.BlockSpec((s, LANES), const),
            pl.BlockSpec((4 * s, lc + s), const),
        ],
        out_specs=[pl.BlockSpec((s, ATTN_WIDTH), row), pl.BlockSpec((s, LANES), row)],
        out_shape=[jax.ShapeDtypeStruct((batch * s, ATTN_WIDTH), F32),
                   jax.ShapeDtypeStruct((batch * s, LANES), F32)],
        compiler_params=_cparams("parallel"),
        name="attn_sample",
    )(sinks, p, p, cache_k, cache_v, cos, sa, sb, bias)


class _BlockDiag:
    def __init__(self, chunk, group):
        self.chunk, self.group = chunk, group
        n = chunk * group
        lane = lax.broadcasted_iota(jnp.int32, (chunk, n), 1)
        self.lane_block = lane // chunk
        self.eye = (lax.broadcasted_iota(jnp.int32, (chunk, n), 0) == lane % chunk).astype(F32)

    def wide(self, tall):
        c = self.chunk
        out = tall[0:c]
        for b in range(1, self.group):
            out = out + tall[b * c:(b + 1) * c]
        return out

    def expand(self, wide):
        if self.group == 1:
            return wide
        zero = jnp.zeros_like(wide)
        return jnp.concatenate([jnp.where(self.lane_block == b, wide, zero) for b in range(self.group)], 0)

    def rmul(self, lhs, wide):
        l_hi, l_lo = _split_bf16(lhs)
        w_hi, w_lo = _split_bf16(wide)
        m = lhs.shape[0]
        top = _dot(jnp.concatenate([l_hi, l_lo], 0), self.expand(w_hi))
        return top[:m] + top[m:] + _dot(l_hi, self.expand(w_lo))

    def lmul(self, wide, rhs):
        w_hi, w_lo = _split_bf16(wide)
        r_hi, r_lo = _split_bf16(rhs)
        n = self.chunk * self.group
        e_hi = self.expand(w_hi)
        top = _dot(jnp.concatenate([e_hi, self.expand(w_lo)], 0), r_hi)
        return top[:n] + top[n:] + _dot(e_hi, r_lo)

    def unit_lower_inverse(self, a_talls):
        c = self.chunk
        negs = [-self.wide(a) for a in a_talls]
        xs = [self.eye + neg for neg in negs]
        powers = [self.rmul(neg, neg) for neg in negs]
        iters = int(math.log2(c)) - 1
        for it in range(iters):
            last = it == iters - 1
            rs = [self.rmul(x if last else jnp.concatenate([x, p], 0), p) for x, p in zip(xs, powers)]
            xs = [x + r[:c] for x, r in zip(xs, rs)]
            if not last:
                powers = [r[c:] for r in rs]
        return xs


def _softplus(x):
    return jnp.maximum(x, 0.0) + jnp.log1p(jnp.exp(-jnp.abs(x)))


def _dn_prep_kernel(chunk, group, precise, qd_ref, kd_ref, vd_ref, ba_ref, hq_ref, hk_ref, hv_ref,
                    wq_ref, wk_ref, wv_ref, alog_ref, dtb_ref,
                    wv_out, wk_out, qdec_out, kend_out, p_out, gend_out):
    h = pl.program_id(1)
    tb = qd_ref.shape[0]
    n = chunk * group

    def conv_silu(x_ref, halo_ref, w_ref):
        xp = jnp.concatenate([halo_ref[0], x_ref[...]], 0)
        w = w_ref[...]
        y = xp[5:5 + tb] * w[0:1]
        for tap in range(1, CONV_WIDTH):
            y = y + xp[5 + tap:5 + tap + tb] * w[tap:tap + 1]
        return _silu(y)

    q = conv_silu(qd_ref, hq_ref, wq_ref)
    k = conv_silu(kd_ref, hk_ref, wk_ref)
    v = conv_silu(vd_ref, hv_ref, wv_ref)
    q = q * lax.rsqrt(jnp.sum(q * q, -1, keepdims=True) + EPS) * (DN_KEY_DIM ** -0.5)
    k = k * lax.rsqrt(jnp.sum(k * k, -1, keepdims=True) + EPS)

    ba = ba_ref[...]
    lane = lax.broadcasted_iota(jnp.int32, ba.shape, 1)
    beta_all = jax.nn.sigmoid(ba)
    g_all = -jnp.exp(alog_ref[...]) * _softplus(ba + dtb_ref[...])
    beta = jnp.sum(jnp.where(lane == h, beta_all, 0.0), -1, keepdims=True)
    g = jnp.sum(jnp.where(lane == h + DN_HEADS, g_all, 0.0), -1, keepdims=True)

    li = lax.broadcasted_iota(jnp.int32, (n, n), 0)
    mi = lax.broadcasted_iota(jnp.int32, (n, n), 1)
    same = (li // chunk) == (mi // chunk)
    upto = jnp.logical_and(same, li <= mi)
    since = jnp.logical_and(same, li >= mi)
    chunk_end = mi == (li // chunk) * chunk + (chunk - 1)
    blocks = _BlockDiag(chunk, group)
    op_dtype = wk_out.dtype
    slot = p_out.shape[1]

    row_groups = [slice(gi * n, (gi + 1) * n) for gi in range(tb // n)]
    a_mats, stash = [], []
    for rows in row_groups:
        qc, kc, bc, gc = q[rows], k[rows], beta[rows], g[rows]
        g_row = jnp.sum(jnp.where(upto, gc, 0.0), 0, keepdims=True)
        g_col = jnp.sum(jnp.where(li == mi, g_row, 0.0), 1, keepdims=True)
        g_end = jnp.sum(jnp.where(chunk_end, g_row, 0.0), 1, keepdims=True)
        decay = jnp.exp(jnp.where(since, g_col - g_row, -jnp.inf))
        qk_kk = _dot_any(jnp.concatenate([qc, kc], 0), kc, precise, _dot_nt)
        a_mats.append(jnp.where(li > mi, bc * decay * qk_kk[n:], 0.0))
        e_g = jnp.exp(g_col)
        qdec_out[rows, :] = (e_g * qc).astype(op_dtype)
        kend_out[rows, :] = (jnp.exp(g_end - g_col) * kc).astype(op_dtype)
        p_out[rows, 0:n] = (qk_kk[:n] * decay).astype(op_dtype)
        if slot > n:
            p_out[rows, n:] = jnp.zeros((n, slot - n), op_dtype)
        stash.append((e_g, g_end))
    t_invs = blocks.unit_lower_inverse(a_mats)
    for gi, (rows, t_inv, (e_g, g_end)) in enumerate(zip(row_groups, t_invs, stash)):
        kc, vc, bc = k[rows], v[rows], beta[rows]
        w = blocks.lmul(t_inv, jnp.concatenate([bc * vc, (bc * e_g) * kc], 1))
        wv_out[rows, :] = w[:, :DN_VAL_DIM]
        wk_out[rows, :] = w[:, DN_VAL_DIM:].astype(op_dtype)
        for c in range(group):
            last = c * chunk + chunk - 1
            gend_out[gi * group + c] = jnp.broadcast_to(jnp.exp(g_end[last:last + 1]), (1, LANES))


def _dn_score_slot(chunk, group):
    return max(chunk * group, LANES)


def _dn_prep(p, halo, w_conv8, alog_row, dtb_row, chunk, group, tb, precise):
    m = p.shape[0]
    op_dtype = F32 if precise else BF16
    slot = _dn_score_slot(chunk, group)
    nh = DN_HEADS
    cq, ck, cv = P_CONV // LANES, P_CONV // LANES + nh, P_CONV // LANES + 2 * nh
    col = lambda base: (lambda i, h: (i, base + h))
    halo_spec = lambda base: pl.BlockSpec((1, 8, LANES), lambda i, h: (i, 0, base + h))
    w_spec = lambda base: pl.BlockSpec((8, LANES), lambda i, h: (0, base + h))
    const = pl.BlockSpec((1, LANES), lambda i, h: (0, 0))
    head_blk = pl.BlockSpec((tb, LANES), lambda i, h: (i, h))
    out_shape = [
        jax.ShapeDtypeStruct((m, DN_WIDTH), F32),
        jax.ShapeDtypeStruct((m, DN_WIDTH), op_dtype),
        jax.ShapeDtypeStruct((m, DN_WIDTH), op_dtype),
        jax.ShapeDtypeStruct((m, DN_WIDTH), op_dtype),
        jax.ShapeDtypeStruct((m, DN_HEADS * slot), op_dtype),
        jax.ShapeDtypeStruct((m // chunk, 1, DN_WIDTH), F32),
    ]
    return pl.pallas_call(
        functools.partial(_dn_prep_kernel, chunk, group, precise),
        grid=(m // tb, nh),
        in_specs=[
            pl.BlockSpec((tb, LANES), col(cq)),
            pl.BlockSpec((tb, LANES), col(ck)),
            pl.BlockSpec((tb, LANES), col(cv)),
            pl.BlockSpec((tb, LANES), lambda i, h: (i, P_BA // LANES)),
            halo_spec(0), halo_spec(nh), halo_spec(2 * nh),
            w_spec(0), w_spec(nh), w_spec(2 * nh),
            const, const,
        ],
        out_specs=[head_blk] * 4 + [pl.BlockSpec((tb, slot), lambda i, h: (i, h)),
                                    pl.BlockSpec((tb // chunk, 1, LANES), lambda i, h: (i, 0, h))],
        out_shape=out_shape,
        compiler_params=_cparams("parallel", "parallel"),
        name="dn_prep",
    )(p, p, p, p, halo, halo, halo, w_conv8, w_conv8, w_conv8, alog_row, dtb_row)


def _dn_scan_kernel(chunk, group, n_chunks, wv_ref, wk_ref, qd_ref, ke_ref, p_ref, ge_ref, gate_ref, s0_ref,
                    onorm_ref, od_ref, sout_ref, s_scr, u_scr):
    n = pl.program_id(1)
    precise = wk_ref.dtype == F32
    group_rows = chunk * group
    slot = _dn_score_slot(chunk, group)

    @pl.when(n == 0)
    def _():
        s_scr[...] = s0_ref[0]
        u_scr[...] = jnp.zeros_like(u_scr)

    onorm = onorm_ref[...]
    for c in range(n_chunks):
        rows = slice(c * chunk, (c + 1) * chunk)
        group_slot = slice((c % group) * chunk, (c % group + 1) * chunk)
        for h in range(DN_HEADS):
            cols = slice(h * LANES, (h + 1) * LANES)
            s = s_scr[h]
            if not precise:
                s = s.astype(BF16)
            u = wv_ref[rows, cols] - _dot_any(wk_ref[rows, cols], s, precise)
            if not precise:
                u = u.astype(BF16)
            u_scr[h, group_slot, :] = u
            o = (_dot_any(qd_ref[rows, cols], s, precise)
                 + _dot_any(p_ref[rows, h * slot:h * slot + group_rows], u_scr[h], precise))
            s_scr[h] = ge_ref[c, :, cols] * s_scr[h] + _dot_any(ke_ref[rows, cols], u, precise, _dot_tn, 1)
            gate = gate_ref[rows, cols]
            od_ref[rows, cols] = (_rms(o, onorm) * _silu(gate)).astype(od_ref.dtype)

    @pl.when(n == pl.num_programs(1) - 1)
    def _():
        sout_ref[0] = s_scr[...]


def _dn_scan(prep, p, s0, onorm_row, chunk, group, n_chunks, batch):
    wv, wk, qdec, kend, pm, gend = prep
    m = wv.shape[0]
    assert n_chunks % group == 0
    rows = chunk * n_chunks
    steps = m // batch // rows
    blk = lambda b, n: (b * steps + n, 0)
    wide = pl.BlockSpec((rows, DN_WIDTH), blk)
    state = pl.BlockSpec((1, DN_HEADS, DN_KEY_DIM, DN_VAL_DIM), lambda b, n: (b, 0, 0, 0))
    slot = _dn_score_slot(chunk, group)
    assert pm.shape[1] == DN_HEADS * slot
    return pl.pallas_call(
        functools.partial(_dn_scan_kernel, chunk, group, n_chunks),
        grid=(batch, steps),
        in_specs=[
            wide, wide, wide, wide, pl.BlockSpec((rows, DN_HEADS * slot), blk),
            pl.BlockSpec((n_chunks, 1, DN_WIDTH), lambda b, n: (b * steps + n, 0, 0)),
            pl.BlockSpec((rows, DN_WIDTH), lambda b, n: (b * steps + n, P_GATE // DN_WIDTH)),
            state,
            pl.BlockSpec((1, LANES), lambda b, n: (0, 0)),
        ],
        out_specs=[wide, state],
        out_shape=[jax.ShapeDtypeStruct((m, DN_WIDTH), wk.dtype),
                   jax.ShapeDtypeStruct((batch, DN_HEADS, DN_KEY_DIM, DN_VAL_DIM), F32)],
        scratch_shapes=[pltpu.VMEM((DN_HEADS, DN_KEY_DIM, DN_VAL_DIM), F32),
                        pltpu.VMEM((DN_HEADS, chunk * group, DN_VAL_DIM), wk.dtype)],
        compiler_params=_cparams("parallel", "arbitrary"),
        name="dn_scan",
    )(wv, wk, qdec, kend, pm, gend, p, s0, onorm_row)


def _out_proj_kernel(nw, attn_ref, od_ref, *refs):
    w_refs, (x_ref, g_ref, gate_ref, o_ref) = refs[:nw], refs[nw:]
    y = (_mm(attn_ref[...], tuple(r[0:ATTN_WIDTH, :] for r in w_refs))
         + _mm(od_ref[...], tuple(r[ATTN_WIDTH:, :] for r in w_refs)))
    o_ref[...] = x_ref[...] + gate_ref[...] * _rms(y, g_ref[...])


def _out_proj(attn, od, w, x, gain, gate, tm):
    m, d = x.shape
    row = lambda i: (i, 0)
    return pl.pallas_call(
        functools.partial(_out_proj_kernel, len(w)),
        grid=(m // tm,),
        in_specs=[
            pl.BlockSpec((tm, ATTN_WIDTH), row),
            pl.BlockSpec((tm, DN_WIDTH), row),
        ] + [pl.BlockSpec((ATTN_WIDTH + DN_WIDTH, d), lambda i: (0, 0))] * len(w) + [
            pl.BlockSpec((tm, d), row),
            pl.BlockSpec((1, d), lambda i: (0, 0)),
            _mod_spec(gate.shape[0], tm, d),
        ],
        out_specs=pl.BlockSpec((tm, d), row),
        out_shape=jax.ShapeDtypeStruct((m, d), F32),
        compiler_params=_cparams("parallel"),
        name="out_proj",
    )(attn, od, *w, x, gain, gate)


def _ffn_kernel(nw, x_ref, g_ref, sc_ref, sh_ref, *refs):
    wg_refs, wu_refs, wd_refs = refs[:nw], refs[nw:2 * nw], refs[2 * nw:3 * nw]
    g2_ref, gate_ref, o_ref, h_ref, acc_ref = refs[3 * nw:]
    j = pl.program_id(1)

    @pl.when(j == 0)
    def _():
        h = _rms(x_ref[...], g_ref[...]) * (1.0 + sc_ref[...]) + sh_ref[...]
        h_ref[...] = h.astype(h_ref.dtype)
        acc_ref[...] = jnp.zeros_like(acc_ref)

    h = h_ref[...]
    act = _silu(_mm(h, tuple(r[...] for r in wg_refs))) * _mm(h, tuple(r[...] for r in wu_refs))
    acc_ref[...] += _mm(act, tuple(r[...] for r in wd_refs))

    @pl.when(j == pl.num_programs(1) - 1)
    def _():
        o_ref[...] = x_ref[...] + gate_ref[...] * _rms(acc_ref[...], g2_ref[...])


def _ffn(x, gain, scale, shift, wg, wu, wd, gain2, gate, tm, tf):
    m, d = x.shape
    nw = len(wg)
    f = wg[0].shape[1]
    row = lambda i, j: (i, 0)
    vec = pl.BlockSpec((1, d), lambda i, j: (0, 0))
    return pl.pallas_call(
        functools.partial(_ffn_kernel, nw),
        grid=(m // tm, f // tf),
        in_specs=[
            pl.BlockSpec((tm, d), row), vec,
            _mod_spec(scale.shape[0], tm, d), _mod_spec(shift.shape[0], tm, d),
        ] + [pl.BlockSpec((d, tf), lambda i, j: (0, j))] * (2 * nw)
        + [pl.BlockSpec((tf, d), lambda i, j: (j, 0))] * nw
        + [vec, _mod_spec(gate.shape[0], tm, d)],
        out_specs=pl.BlockSpec((tm, d), row),
        out_shape=jax.ShapeDtypeStruct((m, d), F32),
        scratch_shapes=[pltpu.VMEM((tm, d), BF16 if nw == 1 else F32), pltpu.VMEM((tm, d), F32)],
        compiler_params=_cparams("parallel", "arbitrary"),
        name="ffn_dense",
    )(x, gain, scale, shift, *wg, *wu, *wd, gain2, gate)


def _router_kernel(x_ref, g_ref, sc_ref, sh_ref, wr_ref, h_ref, gates_ref, idx_ref, w12_ref):
    h = _rms(x_ref[...], g_ref[...]) * (1.0 + sc_ref[...]) + sh_ref[...]
    h_ref[...] = h
    logits = jnp.dot(h, wr_ref[...], preferred_element_type=F32, precision=lax.Precision.HIGHEST)
    lane = lax.broadcasted_iota(jnp.int32, logits.shape, 1).astype(F32)
    logits = jnp.where(lane < N_EXPERTS, logits, -jnp.inf)
    m1 = jnp.max(logits, -1, keepdims=True)
    i1 = jnp.min(jnp.where(logits == m1, lane, float(LANES)), -1, keepdims=True)
    rest = jnp.where(lane == i1, -jnp.inf, logits)
    m2 = jnp.max(rest, -1, keepdims=True)
    i2 = jnp.min(jnp.where(rest == m2, lane, float(LANES)), -1, keepdims=True)
    t = jnp.exp(m2 - m1)
    w1 = 1.0 / (1.0 + t)
    w2 = t / (1.0 + t)
    gates_ref[...] = jnp.where(lane == i1, w1, 0.0) + jnp.where(lane == i2, w2, 0.0)
    idx_ref[...] = jnp.where(lane == 0.0, i1, jnp.where(lane == 1.0, i2, 0.0)).astype(jnp.int32)
    w12_ref[...] = jnp.where(lane == 0.0, w1, jnp.where(lane == 1.0, w2, 0.0))


def _router(x, gain, scale, shift, w_router_pad, tm):
    m, d = x.shape
    row = lambda i: (i, 0)
    vec = pl.BlockSpec((1, d), lambda i: (0, 0))
    small = pl.BlockSpec((tm, LANES), row)
    return pl.pallas_call(
        _router_kernel,
        grid=(m // tm,),
        in_specs=[pl.BlockSpec((tm, d), row), vec,
                  _mod_spec(scale.shape[0], tm, d), _mod_spec(shift.shape[0], tm, d),
                  pl.BlockSpec((d, LANES), lambda i: (0, 0))],
        out_specs=[pl.BlockSpec((tm, d), row), small, small, small],
        out_shape=[jax.ShapeDtypeStruct((m, d), F32), jax.ShapeDtypeStruct((m, LANES), F32),
                   jax.ShapeDtypeStruct((m, LANES), jnp.int32), jax.ShapeDtypeStruct((m, LANES), F32)],
        compiler_params=_cparams("parallel"),
        name="moe_router",
    )(x, gain, scale, shift, w_router_pad)


def _moe_gemm_kernel(nj, te_ref, tot_ref, rt_ref, h_hbm, wg_ref, wu_ref, wd_ref, ys_ref, xs_ref, xb_ref, acc_ref,
                     sems):
    r = pl.program_id(0)
    j = pl.program_id(1)
    tm = xb_ref.shape[0]
    total = tot_ref[0]
    active = r < total
    slot = r % 2
    share = -(-tm // nj)

    def row_copy(tile, t, s):
        return pltpu.make_async_copy(h_hbm.at[pl.ds(rt_ref[tile * tm + t], 1), :],
                                     xs_ref.at[s, pl.ds(t, 1), :], sems.at[s])

    def start_rows(tile, s, lo, hi):
        def body(t, carry):
            row_copy(tile, t, s).start()
            return carry

        lax.fori_loop(lo, hi, body, 0)

    @pl.when(jnp.logical_and(r == 0, j == 0))
    def _():
        start_rows(0, 0, 0, tm)

    @pl.when(jnp.logical_and(active, j == 0))
    def _():
        def wait(t, carry):
            row_copy(r, t, slot).wait()
            return carry

        lax.fori_loop(0, tm, wait, 0)
        xb_ref[...] = xs_ref[slot].astype(BF16)
        acc_ref[...] = jnp.zeros_like(acc_ref)

    @pl.when(r + 1 < total)
    def _():
        lo = j * share
        start_rows(r + 1, 1 - slot, lo, jnp.minimum(lo + share, tm))

    @pl.when(active)
    def _():
        xb = xb_ref[...]
        act = (_silu(_dot(xb, wg_ref[0])) * _dot(xb, wu_ref[0])).astype(BF16)
        acc_ref[...] += _dot(act, wd_ref[0])

    @pl.when(j == pl.num_programs(1) - 1)
    def _():
        ys_ref[...] = jnp.where(active, acc_ref[...], 0.0)


def _moe_gemm(tile_expert, total_tiles, row_token, h, wg, wu, wd, tm, tf):
    n_tiles = tile_expert.shape[0]
    d = h.shape[1]
    f = wg.shape[2]
    nj = f // tf

    def w_col(r, j, te, tot, rt):
        return (te[r], 0, jnp.where(r < tot[0], j, nj - 1))

    def w_row(r, j, te, tot, rt):
        return (te[r], jnp.where(r < tot[0], j, nj - 1), 0)

    grid_spec = pltpu.PrefetchScalarGridSpec(
        num_scalar_prefetch=3,
        grid=(n_tiles, nj),
        in_specs=[
            pl.BlockSpec(memory_space=pl.ANY),
            pl.BlockSpec((1, d, tf), w_col),
            pl.BlockSpec((1, d, tf), w_col),
            pl.BlockSpec((1, tf, d), w_row),
        ],
        out_specs=pl.BlockSpec((tm, d), lambda r, j, te, tot, rt: (r, 0)),
        scratch_shapes=[pltpu.VMEM((2, tm, d), F32), pltpu.VMEM((tm, d), BF16), pltpu.VMEM((tm, d), F32),
                        pltpu.SemaphoreType.DMA((2,))],
    )
    return pl.pallas_call(
        functools.partial(_moe_gemm_kernel, nj),
        grid_spec=grid_spec,
        out_shape=jax.ShapeDtypeStruct((n_tiles * tm, d), F32),
        compiler_params=_cparams("arbitrary", "arbitrary"),
        name="moe_gemm",
    )(tile_expert, total_tiles, row_token, h, wg, wu, wd)


def _moe_combine_kernel(dest_ref, x_ref, w12_ref, g_ref, gate_ref, ys_hbm, o_ref, buf_ref, sems):
    i = pl.program_id(0)
    tb = x_ref.shape[0]
    slot = i % 2

    def row_copy(blk, t, k, s):
        src = dest_ref[2 * (blk * tb + t) + k]
        return pltpu.make_async_copy(ys_hbm.at[pl.ds(src, 1), :], buf_ref.at[s, k, pl.ds(t, 1), :], sems.at[s])

    def start_block(blk, s):
        def body(t, carry):
            row_copy(blk, t, 0, s).start()
            row_copy(blk, t, 1, s).start()
            return carry

        lax.fori_loop(0, tb, body, 0)

    @pl.when(i == 0)
    def _():
        start_block(0, 0)

    @pl.when(i + 1 < pl.num_programs(0))
    def _():
        start_block(i + 1, 1 - slot)

    def wait(t, carry):
        row_copy(i, t, 0, slot).wait()
        row_copy(i, t, 1, slot).wait()
        return carry

    lax.fori_loop(0, tb, wait, 0)
    w12 = w12_ref[...]
    y = w12[:, 0:1] * buf_ref[slot, 0] + w12[:, 1:2] * buf_ref[slot, 1]
    o_ref[...] = x_ref[...] + gate_ref[...] * _rms(y, g_ref[...])


def _moe_combine(dest, x, w12, gain, gate, ys, tb):
    m, d = x.shape
    row = lambda i, dst: (i, 0)
    grid_spec = pltpu.PrefetchScalarGridSpec(
        num_scalar_prefetch=1,
        grid=(m // tb,),
        in_specs=[
            pl.BlockSpec((tb, d), row),
            pl.BlockSpec((tb, LANES), row),
            pl.BlockSpec((1, d), lambda i, dst: (0, 0)),
            pl.BlockSpec((1, d), lambda i, dst: (0, 0)),
            pl.BlockSpec(memory_space=pl.ANY),
        ],
        out_specs=pl.BlockSpec((tb, d), row),
        scratch_shapes=[pltpu.VMEM((2, 2, tb, d), F32), pltpu.SemaphoreType.DMA((2,))],
    )
    return pl.pallas_call(
        _moe_combine_kernel,
        grid_spec=grid_spec,
        out_shape=jax.ShapeDtypeStruct((m, d), F32),
        compiler_params=_cparams("arbitrary"),
        name="moe_combine",
    )(dest, x, w12, gain, gate, ys)


def _route_tables(idx2, tm, n_tiles):
    m = idx2.shape[0]
    n_assign = 2 * m
    assert n_tiles * tm == n_assign + N_EXPERTS * tm
    experts = jnp.arange(N_EXPERTS, dtype=jnp.int32)
    e_flat = idx2.reshape(n_assign)
    onehot = (e_flat[:, None] == experts[None, :]).astype(jnp.int32)
    csum = jnp.cumsum(onehot, 0)
    counts = csum[-1]
    padded = ((counts + tm - 1) // tm) * tm
    pend = jnp.cumsum(padded)
    pstart = pend - padded
    dest = jnp.sum(onehot * (pstart[None, :] + csum - 1), -1).astype(jnp.int32)
    total_tiles = (pend[-1] // tm).astype(jnp.int32).reshape(1)
    tile_expert = jnp.minimum(
        jnp.searchsorted(pend // tm, jnp.arange(n_tiles, dtype=jnp.int32), side="right"), N_EXPERTS - 1
    ).astype(jnp.int32)
    filler_key = jnp.where(jnp.arange(tm, dtype=jnp.int32)[None, :] < (padded - counts)[:, None],
                           experts[:, None], N_EXPERTS).reshape(-1)
    keys = jnp.concatenate([e_flat, filler_key])
    tokens = jnp.concatenate([jnp.arange(n_assign, dtype=jnp.int32) // 2,
                              jnp.zeros((N_EXPERTS * tm,), jnp.int32)])
    _, row_token = lax.sort((keys, tokens), num_keys=1, is_stable=True)
    return tile_expert, total_tiles, row_token, dest


def _moe_dense_kernel(h_ref, gates_ref, wg_ref, wu_ref, wd_ref, x_ref, g_ref, gate_ref, o_ref, acc_ref, tot_ref):
    e = pl.program_id(0)
    j = pl.program_id(1)
    nj = pl.num_programs(1)

    @pl.when(jnp.logical_and(e == 0, j == 0))
    def _():
        tot_ref[...] = jnp.zeros_like(tot_ref)

    @pl.when(j == 0)
    def _():
        acc_ref[...] = jnp.zeros_like(acc_ref)

    h = h_ref[...].astype(BF16)
    act = (_silu(_dot(h, wg_ref[0])) * _dot(h, wu_ref[0])).astype(BF16)
    acc_ref[...] += _dot(act, wd_ref[0])

    @pl.when(j == nj - 1)
    def _():
        gates = gates_ref[...]
        lane = lax.broadcasted_iota(jnp.int32, gates.shape, 1)
        ge = jnp.sum(jnp.where(lane == e, gates, 0.0), -1, keepdims=True)
        tot_ref[...] += ge * acc_ref[...]

    @pl.when(jnp.logical_and(e == pl.num_programs(0) - 1, j == nj - 1))
    def _():
        o_ref[...] = x_ref[...] + gate_ref[...] * _rms(tot_ref[...], g_ref[...])


def _moe_dense(h, gates, wg, wu, wd, x, gain, gate, tf):
    m, d = x.shape
    f = wg.shape[2]
    full = pl.BlockSpec((m, d), lambda e, j: (0, 0))
    return pl.pallas_call(
        _moe_dense_kernel,
        grid=(N_EXPERTS, f // tf),
        in_specs=[
            full,
            pl.BlockSpec((m, LANES), lambda e, j: (0, 0)),
            pl.BlockSpec((1, d, tf), lambda e, j: (e, 0, j)),
            pl.BlockSpec((1, d, tf), lambda e, j: (e, 0, j)),
            pl.BlockSpec((1, tf, d), lambda e, j: (e, j, 0)),
            full,
            pl.BlockSpec((1, d), lambda e, j: (0, 0)),
            full,
        ],
        out_specs=full,
        out_shape=jax.ShapeDtypeStruct((m, d), F32),
        scratch_shapes=[pltpu.VMEM((m, d), F32), pltpu.VMEM((m, d), F32)],
        compiler_params=_cparams("arbitrary", "arbitrary"),
        name="moe_dense",
    )(h, gates, wg, wu, wd, x, gain, gate)


def _rope_tables(pos):
    half = ROPE_DIM // 2
    inv_freq = jnp.power(ROPE_THETA, -2.0 * jnp.arange(half, dtype=F32) / ROPE_DIM)
    ang = pos.astype(F32)[:, None] * inv_freq[None, :]
    cos, sin = jnp.cos(ang), jnp.sin(ang)
    t = pos.shape[0]
    rest = ATTN_HEAD_DIM - ROPE_DIM
    cos_h = jnp.concatenate([cos, cos, jnp.ones((t, rest), F32)], 1)
    sa_h = jnp.concatenate([-sin, jnp.zeros((t, half + rest), F32)], 1)
    sb_h = jnp.concatenate([jnp.zeros((t, half), F32), sin, jnp.zeros((t, rest), F32)], 1)
    rep = LANES // ATTN_HEAD_DIM
    return tuple(jnp.tile(a, (1, rep)) for a in (cos_h, sa_h, sb_h))


def _permute_w_in(w):
    o1 = ATTN_WIDTH
    o2 = o1 + KV_WIDTH
    o3 = o2 + KV_WIDTH
    o4 = o3 + DN_CONV_CH
    o5 = o4 + DN_WIDTH
    parts = [w[:, :o1], w[:, o4:o5], w[:, o3:o4], w[:, o1:o2], w[:, o2:o3], w[:, o5:]]
    used = sum(a.shape[1] for a in parts)
    parts.append(jnp.zeros((w.shape[0], P_WIDTH - used), w.dtype))
    return jnp.concatenate(parts, 1)


def _sample_mask_bias(s, lc):
    q_pos = PAST_LEN + np.arange(s)
    k_pos = np.concatenate([PAST_LEN - lc + np.arange(lc), q_pos])
    q_chunk = q_pos[:, None] // CHUNK
    k_chunk = k_pos[None, :] // CHUNK
    mask = (k_pos[None, :] >= 0) & (k_chunk <= q_chunk) & (k_pos[None, :] >= q_chunk * CHUNK - WINDOW)
    bias = np.where(mask, 0.0, -np.inf).astype(np.float32)
    return jnp.asarray(np.tile(bias, (4, 1)))


def _conv_halo(p, init, tb, seq):
    m = p.shape[0]
    batch = m // seq
    nb = seq // tb
    tails = p.reshape(batch, nb, tb, P_WIDTH)[:, :nb - 1, tb - (CONV_WIDTH - 1):, P_CONV:P_CONV + DN_CONV_CH]
    prev = jnp.concatenate([init[:, None], tails], 1)
    prev = prev.reshape(batch * nb, CONV_WIDTH - 1, DN_CONV_CH)
    return jnp.pad(prev, ((0, 0), (8 - (CONV_WIDTH - 1), 0), (0, 0)))


def _trunk(x, mods, layer_w, rope, past, cfg):
    m = x.shape[0]
    batch, seq = cfg["batch"], cfg["seq"]
    precise = cfg["precise"]
    nw = 2 if precise else 1
    ks, vs, ss, bufs = [], [], [], []
    cos, sa, sb = rope
    for l in range(DEPTH):
        w = layer_w[l]
        sh_a, sc_a, g_a, sh_f, sc_f, g_f = mods[l]
        p = _norm_proj(x, w["gain"][0], sc_a, sh_a, w["w_in"][:nw], cfg["tm_proj"], cfg["tn_proj"])
        if past is None:
            attn, k_new = _attn_prompt(p, w["sinks"], cos, sa, sb, cfg["tb_attn"])
            s0 = jnp.zeros((batch, DN_HEADS, DN_KEY_DIM, DN_VAL_DIM), F32)
            conv_init = jnp.zeros((batch, CONV_WIDTH - 1, DN_CONV_CH), F32)
        else:
            ck = past[0][l].reshape(batch, -1, KV_WIDTH)
            cv = past[1][l].reshape(batch, -1, KV_WIDTH)
            attn, k_new = _attn_sample(p, ck, cv, w["sinks"], cos, sa, sb, cfg["bias"], batch, seq)
            s0 = past[2][l]
            conv_init = past[3][l]
        halo = _conv_halo(p, conv_init, cfg["tb_dn"], seq)
        prep = _dn_prep(p, halo, w["w_conv"], w["alog"], w["dtb"], cfg["chunk"], cfg["group"], cfg["tb_dn"],
                        precise)
        od, s_new = _dn_scan(prep, p, s0, w["onorm"], cfg["chunk"], cfg["group"], cfg["scan_chunks"], batch)
        x = _out_proj(attn, od, w["w_out"][:nw], x, w["gain"][1], g_a, cfg["tm_out"])
        if l % 2 == 0:
            x = _ffn(x, w["gain"][2], sc_f, sh_f, w["ffn_gate"][:nw], w["ffn_up"][:nw], w["ffn_down"][:nw],
                     w["gain"][3], g_f, cfg["tm_ffn"], cfg["tf_ffn"])
        else:
            h, gates, idx, w12 = _router(x, w["gain"][2], sc_f, sh_f, w["router"], cfg["tm_router"])
            if cfg["routed"]:
                tm = cfg["tm_moe"]
                n_tiles = 2 * m // tm + N_EXPERTS
                tile_expert, total_tiles, row_token, dest = _route_tables(idx[:, :2], tm, n_tiles)
                ys = _moe_gemm(tile_expert, total_tiles, row_token, h, w["moe_gate"], w["moe_up"], w["moe_down"],
                               tm, cfg["tf_moe"])
                x = _moe_combine(dest, x, w12, w["gain"][3], g_f, ys, cfg["tb_combine"])
            else:
                x = _moe_dense(h, gates, w["moe_gate"], w["moe_up"], w["moe_down"], x, w["gain"][3], g_f,
                               cfg["tf_moe"])
        pb = p.reshape(batch, seq, P_WIDTH)
        keep = min(WINDOW, seq) if past is None else seq
        ks.append(k_new.reshape(batch, seq, ATTN_KV_HEADS, ATTN_HEAD_DIM)[:, seq - keep:])
        vs.append(pb[:, seq - keep:, P_V:P_V + KV_WIDTH].reshape(batch, keep, ATTN_KV_HEADS, ATTN_HEAD_DIM))
        ss.append(s_new)
        assert seq >= CONV_WIDTH - 1
        bufs.append(pb[:, seq - (CONV_WIDTH - 1):, P_CONV:P_CONV + DN_CONV_CH])
    return x, jnp.stack(ks), jnp.stack(vs), jnp.stack(ss), jnp.stack(bufs)


def kernel(x_prompt, x_sample, cache_attn_k, cache_attn_v, state_delta, state_conv, c_prompt, c_sample, w_in, w_conv, attn_sinks, dn_a_log, dn_dt_bias, dn_norm, w_out, w_mod, b_mod, norm_gains, ffn_gate, ffn_up, ffn_down, moe_router, moe_gate, moe_up, moe_down):
    bp, tp, d = x_prompt.shape
    bs, ts, _ = x_sample.shape
    assert bp == 1 and d == D_MODEL

    c_all = jnp.concatenate([c_prompt, c_sample, jnp.zeros((16 - bp - bs, d), F32)], 0)
    mod = _modulation(c_all, w_mod, b_mod)
    mods_p, mods_s = [], []
    for l in range(DEPTH):
        six = jnp.split(mod[l], 6, -1)
        mods_p.append([a[0:bp] for a in six])
        mods_s.append([jnp.repeat(a[bp:bp + bs], ts, axis=0) for a in six])

    def pad_lanes(v, at):
        return jnp.zeros((1, LANES), F32).at[0, at:at + v.shape[0]].set(v)

    layer_w = []
    for l in range(DEPTH):
        w = {
            "gain": [norm_gains[l, i].reshape(1, d) for i in range(4)],
            "w_in": _split_weight(_permute_w_in(w_in[l])),
            "sinks": attn_sinks[l],
            "w_conv": jnp.pad(w_conv[l], ((0, 8 - CONV_WIDTH), (0, 0))),
            "alog": pad_lanes(dn_a_log[l], DN_HEADS),
            "dtb": pad_lanes(dn_dt_bias[l], DN_HEADS),
            "onorm": dn_norm[l].reshape(1, DN_VAL_DIM),
            "w_out": _split_weight(w_out[l]),
        }
        if l % 2 == 0:
            w["ffn_gate"] = _split_weight(ffn_gate[l // 2])
            w["ffn_up"] = _split_weight(ffn_up[l // 2])
            w["ffn_down"] = _split_weight(ffn_down[l // 2])
        else:
            w["router"] = jnp.pad(moe_router[l // 2], ((0, 0), (0, LANES - N_EXPERTS)))
            w["moe_gate"] = moe_gate[l // 2].astype(BF16)
            w["moe_up"] = moe_up[l // 2].astype(BF16)
            w["moe_down"] = moe_down[l // 2].astype(BF16)
        layer_w.append(w)

    cfg_p = dict(batch=bp, seq=tp, precise=False, chunk=CHUNK, group=2, scan_chunks=4, tm_proj=1024, tn_proj=512, tb_attn=512,
                 tb_dn=512, tm_out=512, tm_ffn=512, tf_ffn=512, tm_router=512, routed=True, tm_moe=512,
                 tf_moe=256, tb_combine=256)
    rope_p = _rope_tables(jnp.arange(tp, dtype=jnp.int32))
    y_p, k_p, v_p, s_p, conv_p = _trunk(x_prompt.reshape(bp * tp, d), mods_p, layer_w, rope_p, None, cfg_p)

    ms = bs * ts
    cfg_s = dict(batch=bs, seq=ts, precise=True, chunk=ts, group=1, scan_chunks=1, tm_proj=ms, tn_proj=512, tb_dn=ts, tm_out=ms,
                 tm_ffn=ms, tf_ffn=512, tm_router=ms, routed=False, tf_moe=1408,
                 bias=_sample_mask_bias(ts, cache_attn_k.shape[2]))
    rope_s = _rope_tables(PAST_LEN + jnp.arange(ts, dtype=jnp.int32))
    past = (cache_attn_k, cache_attn_v, state_delta, state_conv)
    y_s, k_s, v_s, s_s, conv_s = _trunk(x_sample.reshape(ms, d), mods_s, layer_w, rope_s, past, cfg_s)

    return (y_p.reshape(bp, tp, d), y_s.reshape(bs, ts, d), k_p, v_p, s_p, conv_p, k_s, v_s, s_s, conv_s)
```

```python
import functools
import math

import numpy as np
import jax
import jax.numpy as jnp
from jax import lax
from jax.experimental import pallas as pl
from jax.experimental.pallas import tpu as pltpu

D_MODEL = 2048
DEPTH = 2
PAST_LEN = 1024
CHUNK = 64
ATTN_HEADS = 16
ATTN_KV_HEADS = 2
ATTN_HEAD_DIM = 64
ATTN_WIDTH = 1024
KV_WIDTH = 128
WINDOW = 128
ROPE_THETA = 500000.0
ROPE_DIM = 16
DN_HEADS = 8
DN_KEY_DIM = 128
DN_VAL_DIM = 128
DN_WIDTH = 1024
CONV_WIDTH = 4
DN_CONV_CH = 3072
D_FF = 5632
N_EXPERTS = 8
D_FF_EXPERT = 2816
EPS = 1e-6

F32 = jnp.float32
BF16 = jnp.bfloat16
LANES = 128
V7X_VMEM_LIMIT = 56 * 1024 * 1024

P_Q = 0
P_GATE = 1024
P_CONV = 2048
P_K = 5120
P_V = 5248
P_BA = 5376
P_WIDTH = 5632


def _cparams(*sem, row_dma=False):
    return pltpu.CompilerParams(dimension_semantics=sem, vmem_limit_bytes=V7X_VMEM_LIMIT,
                                disable_bounds_checks=row_dma)


def _silu(x):
    return x * jax.nn.sigmoid(x)


def _rms(x, gain):
    return x * lax.rsqrt(jnp.mean(x * x, -1, keepdims=True) + EPS) * gain


def _dot(a, b):
    return jnp.dot(a, b, preferred_element_type=F32)


def _dot_nt(a, b):
    return lax.dot_general(a, b, (((1,), (1,)), ((), ())), preferred_element_type=F32)


def _dot_tn(a, b):
    return lax.dot_general(a, b, (((0,), (0,)), ((), ())), preferred_element_type=F32)


def _split_bf16(a):
    hi = a.astype(BF16)
    lo = (a - hi.astype(F32)).astype(BF16)
    return hi, lo


def _dot_x3(a, b, dot=_dot, out_axis=0):
    a_hi, a_lo = _split_bf16(a)
    b_hi, b_lo = _split_bf16(b)
    n = a.shape[out_axis]
    top = dot(jnp.concatenate([a_hi, a_lo], out_axis), b_hi)
    return top[:n] + top[n:] + dot(a_hi, b_lo)


def _dot_any(a, b, precise, dot=_dot, out_axis=0):
    if precise:
        return _dot_x3(a, b, dot, out_axis)
    return dot(a.astype(BF16), b.astype(BF16))


def _mm(a, w):
    if len(w) == 1:
        return _dot(a.astype(BF16), w[0])
    a_hi, a_lo = _split_bf16(a)
    n = a.shape[0]
    top = _dot(jnp.concatenate([a_hi, a_lo], 0), w[0])
    return top[:n] + top[n:] + _dot(a_hi, w[1])


def _split_weight(w):
    hi, lo = _split_bf16(w)
    return (hi, lo)


def _mod_spec(rows, tm, d):
    if rows == 1:
        return pl.BlockSpec((1, d), lambda i, *_: (0, 0))
    return pl.BlockSpec((tm, d), lambda i, *_: (i, 0))


def _mod_kernel(c_ref, w_ref, b_ref, o_ref):
    o_ref[0] = _dot_x3(_silu(c_ref[...]), w_ref[0]) + b_ref[0]


def _modulation(c_all, w_mod, b_mod):
    rows = c_all.shape[0]
    n = w_mod.shape[2]
    tn = 1024
    return pl.pallas_call(
        _mod_kernel,
        grid=(DEPTH, n // tn),
        in_specs=[
            pl.BlockSpec((rows, D_MODEL), lambda l, j: (0, 0)),
            pl.BlockSpec((1, D_MODEL, tn), lambda l, j: (l, 0, j)),
            pl.BlockSpec((1, 1, tn), lambda l, j: (l, 0, j)),
        ],
        out_specs=pl.BlockSpec((1, rows, tn), lambda l, j: (l, 0, j)),
        out_shape=jax.ShapeDtypeStruct((DEPTH, rows, n), F32),
        compiler_params=_cparams("parallel", "parallel"),
        name="modulation",
    )(c_all, w_mod, b_mod.reshape(DEPTH, 1, n))


def _norm_proj_kernel(nw, x_ref, g_ref, sc_ref, sh_ref, *refs):
    w_refs, (o_ref, h_ref) = refs[:nw], refs[nw:]

    @pl.when(pl.program_id(1) == 0)
    def _():
        h = _rms(x_ref[...], g_ref[...]) * (1.0 + sc_ref[...]) + sh_ref[...]
        h_ref[...] = h.astype(h_ref.dtype)

    o_ref[...] = _mm(h_ref[...], tuple(r[...] for r in w_refs))


def _norm_proj(x, gain, scale, shift, w, tm, tn):
    m, d = x.shape
    n = w[0].shape[1]
    return pl.pallas_call(
        functools.partial(_norm_proj_kernel, len(w)),
        grid=(m // tm, n // tn),
        in_specs=[
            pl.BlockSpec((tm, d), lambda i, j: (i, 0)),
            pl.BlockSpec((1, d), lambda i, j: (0, 0)),
            _mod_spec(scale.shape[0], tm, d),
            _mod_spec(shift.shape[0], tm, d),
        ] + [pl.BlockSpec((d, tn), lambda i, j: (0, j))] * len(w),
        out_specs=pl.BlockSpec((tm, tn), lambda i, j: (i, j)),
        out_shape=jax.ShapeDtypeStruct((m, n), F32),
        scratch_shapes=[pltpu.VMEM((tm, d), BF16 if len(w) == 1 else F32)],
        compiler_params=_cparams("parallel", "arbitrary"),
        name="norm_proj",
    )(x, gain, scale, shift, *w)


def _rope(x, cos, sa, sb):
    return x * cos + pltpu.roll(x, LANES - 8, 1) * sa + pltpu.roll(x, 8, 1) * sb


def _kv_variants(k, v):
    lo = lax.broadcasted_iota(jnp.int32, k.shape, 1) < ATTN_HEAD_DIM
    kr = pltpu.roll(k, ATTN_HEAD_DIM, 1)
    vr = pltpu.roll(v, ATTN_HEAD_DIM, 1)
    zero = jnp.zeros_like(k)
    k_lo = (jnp.where(lo, k, zero), jnp.where(lo, kr, zero))
    k_hi = (jnp.where(lo, zero, kr), jnp.where(lo, zero, k))
    v_lo = (jnp.where(lo, v, zero), jnp.where(lo, vr, zero))
    v_hi = (jnp.where(lo, zero, vr), jnp.where(lo, zero, v))
    return k_lo, k_hi, v_lo, v_hi


def _sink_softmax(s, sink):
    m = jnp.maximum(jnp.max(s, -1, keepdims=True), sink)
    p = jnp.exp(s - m)
    den = jnp.sum(p, -1, keepdims=True) + jnp.exp(sink - m)
    return p / den


def _attn_core(qb, k_lo, k_hi, v_lo, v_hi, bias, sink_even, sink_odd, precise=False):
    p_even = _sink_softmax(_dot_any(qb, k_lo, precise, _dot_nt) + bias, sink_even)
    p_odd = _sink_softmax(_dot_any(qb, k_hi, precise, _dot_nt) + bias, sink_odd)
    return _dot_any(p_even, v_lo, precise) + _dot_any(p_odd, v_hi, precise)


def _sink_columns(sink_ref, rows_per_pair):
    n = 4 * rows_per_pair
    pair = lax.broadcasted_iota(jnp.int32, (n, 1), 0) // rows_per_pair
    out = []
    for j in range(ATTN_KV_HEADS):
        cols = []
        for par in range(2):
            col = jnp.zeros((n, 1), F32)
            for a in range(4):
                col = jnp.where(pair == a, sink_ref[8 * j + 2 * a + par], col)
            cols.append(col)
        out.append(cols)
    return out


def _attn_prompt_kernel(sink_ref, q_ref, kv_ref, cos_ref, sa_ref, sb_ref, o_ref, knew_ref,
                        qs_ref, klo_ref, khi_ref, vlo_ref, vhi_ref):
    i = pl.program_id(0)
    tb = q_ref.shape[0]
    bufs = (klo_ref, khi_ref, vlo_ref, vhi_ref)

    @pl.when(i == 0)
    def _():
        for r in bufs:
            r[:, 0:WINDOW, :] = jnp.zeros((ATTN_KV_HEADS, WINDOW, LANES), BF16)

    @pl.when(i > 0)
    def _():
        for r in bufs:
            r[:, 0:WINDOW, :] = r[:, tb:tb + WINDOW, :]

    cos, sa, sb = cos_ref[...], sa_ref[...], sb_ref[...]
    k = _rope(kv_ref[:, 0:LANES], cos, sa, sb)
    knew_ref[...] = k
    variants = _kv_variants(k, kv_ref[:, LANES:2 * LANES])
    for r, var in zip(bufs, variants):
        for j in range(ATTN_KV_HEADS):
            r[j, WINDOW:, :] = var[j].astype(BF16)
    scale = ATTN_HEAD_DIM ** -0.5
    for a in range(ATTN_WIDTH // LANES):
        cols = slice(a * LANES, (a + 1) * LANES)
        qs_ref[:, cols] = (_rope(q_ref[:, cols], cos, sa, sb) * scale).astype(BF16)

    sinks = _sink_columns(sink_ref, CHUNK)
    nk = WINDOW + CHUNK

    def chunk_body(c, carry):
        r0 = pl.multiple_of(c * CHUNK, CHUNK)
        kpos = i * tb - WINDOW + r0 + lax.broadcasted_iota(jnp.int32, (1, nk), 1)
        bias = jnp.where(kpos >= 0, 0.0, -jnp.inf).astype(F32)
        for j in range(ATTN_KV_HEADS):
            qb = jnp.concatenate(
                [qs_ref[pl.ds(r0, CHUNK), (4 * j + a) * LANES:(4 * j + a + 1) * LANES] for a in range(4)], 0)
            o = _attn_core(qb, klo_ref[j, pl.ds(r0, nk), :], khi_ref[j, pl.ds(r0, nk), :],
                           vlo_ref[j, pl.ds(r0, nk), :], vhi_ref[j, pl.ds(r0, nk), :],
                           bias, sinks[j][0], sinks[j][1])
            for a in range(4):
                o_ref[pl.ds(r0, CHUNK), (4 * j + a) * LANES:(4 * j + a + 1) * LANES] = (
                    o[a * CHUNK:(a + 1) * CHUNK].astype(BF16))
        return carry

    lax.fori_loop(0, tb // CHUNK, chunk_body, 0)


def _attn_prompt(p, sinks, cos, sa, sb, tb):
    t = p.shape[0]
    kv_blk = P_K // (2 * LANES)
    row = lambda i: (i, 0)
    return pl.pallas_call(
        _attn_prompt_kernel,
        grid=(t // tb,),
        in_specs=[
            pl.BlockSpec(memory_space=pltpu.SMEM),
            pl.BlockSpec((tb, ATTN_WIDTH), row),
            pl.BlockSpec((tb, 2 * LANES), lambda i: (i, kv_blk)),
            pl.BlockSpec((tb, LANES), row),
            pl.BlockSpec((tb, LANES), row),
            pl.BlockSpec((tb, LANES), row),
        ],
        out_specs=[pl.BlockSpec((tb, ATTN_WIDTH), row), pl.BlockSpec((tb, LANES), row)],
        out_shape=[jax.ShapeDtypeStruct((t, ATTN_WIDTH), BF16), jax.ShapeDtypeStruct((t, LANES), F32)],
        scratch_shapes=[pltpu.VMEM((tb, ATTN_WIDTH), BF16)]
        + [pltpu.VMEM((ATTN_KV_HEADS, tb + WINDOW, LANES), BF16) for _ in range(4)],
        compiler_params=_cparams("arbitrary"),
        name="attn_prompt",
    )(sinks, p, p, cos, sa, sb)


def _attn_sample_kernel(sink_ref, q_ref, kv_ref, ck_ref, cv_ref, cos_ref, sa_ref, sb_ref, bias_ref,
                        o_ref, knew_ref):
    s = q_ref.shape[0]
    cos, sa, sb = cos_ref[...], sa_ref[...], sb_ref[...]
    k = _rope(kv_ref[:, 0:LANES], cos, sa, sb)
    knew_ref[...] = k
    kk = jnp.concatenate([ck_ref[0], k], 0)
    vv = jnp.concatenate([cv_ref[0], kv_ref[:, LANES:2 * LANES]], 0)
    k_lo, k_hi, v_lo, v_hi = _kv_variants(kk, vv)
    sinks = _sink_columns(sink_ref, s)
    scale = ATTN_HEAD_DIM ** -0.5
    bias = bias_ref[...]
    for j in range(ATTN_KV_HEADS):
        qb = jnp.concatenate(
            [_rope(q_ref[:, (4 * j + a) * LANES:(4 * j + a + 1) * LANES], cos, sa, sb) * scale for a in range(4)], 0)
        o = _attn_core(qb, k_lo[j], k_hi[j], v_lo[j], v_hi[j], bias, sinks[j][0], sinks[j][1], precise=True)
        for a in range(4):
            o_ref[:, (4 * j + a) * LANES:(4 * j + a + 1) * LANES] = o[a * s:(a + 1) * s]


def _attn_sample(p, cache_k, cache_v, sinks, cos, sa, sb, bias, batch, s):
    lc = cache_k.shape[1]
    kv_blk = P_K // (2 * LANES)
    row = lambda b: (b, 0)
    const = lambda b: (0, 0)
    return pl.pallas_call(
        _attn_sample_kernel,
        grid=(batch,),
        in_specs=[
            pl.BlockSpec(memory_space=pltpu.SMEM),
            pl.BlockSpec((s, ATTN_WIDTH), row),
            pl.BlockSpec((s, 2 * LANES), lambda b: (b, kv_blk)),
            pl.BlockSpec((1, lc, LANES), lambda b: (b, 0, 0)),
            pl.BlockSpec((1, lc, LANES), lambda b: (b, 0, 0)),
            pl.BlockSpec((s, LANES), const),
            pl.BlockSpec((s, LANES), const),
            pl.BlockSpec((s, LANES), const),
            pl.BlockSpec((4 * s, lc + s), const),
        ],
        out_specs=[pl.BlockSpec((s, ATTN_WIDTH), row), pl.BlockSpec((s, LANES), row)],
        out_shape=[jax.ShapeDtypeStruct((batch * s, ATTN_WIDTH), F32),
                   jax.ShapeDtypeStruct((batch * s, LANES), F32)],
        compiler_params=_cparams("parallel"),
        name="attn_sample",
    )(sinks, p, p, cache_k, cache_v, cos, sa, sb, bias)


class _BlockDiag:
    def __init__(self, chunk, group):
        self.chunk, self.group = chunk, group
        n = chunk * group
        lane = lax.broadcasted_iota(jnp.int32, (chunk, n), 1)
        self.lane_block = lane // chunk
        self.eye = (lax.broadcasted_iota(jnp.int32, (chunk, n), 0) == lane % chunk).astype(F32)

    def wide(self, tall):
        c = self.chunk
        out = tall[0:c]
        for b in range(1, self.group):
            out = out + tall[b * c:(b + 1) * c]
        return out

    def expand(self, wide):
        if self.group == 1:
            return wide
        zero = jnp.zeros_like(wide)
        return jnp.concatenate([jnp.where(self.lane_block == b, wide, zero) for b in range(self.group)], 0)

    def rmul(self, lhs, wide):
        l_hi, l_lo = _split_bf16(lhs)
        w_hi, w_lo = _split_bf16(wide)
        m = lhs.shape[0]
        top = _dot(jnp.concatenate([l_hi, l_lo], 0), self.expand(w_hi))
        return top[:m] + top[m:] + _dot(l_hi, self.expand(w_lo))

    def lmul(self, wide, rhs):
        w_hi, w_lo = _split_bf16(wide)
        r_hi, r_lo = _split_bf16(rhs)
        n = self.chunk * self.group
        e_hi = self.expand(w_hi)
        top = _dot(jnp.concatenate([e_hi, self.expand(w_lo)], 0), r_hi)
        return top[:n] + top[n:] + _dot(e_hi, r_lo)

    def unit_lower_inverse(self, a_talls):
        c = self.chunk
        negs = [-self.wide(a) for a in a_talls]
        xs = [self.eye + neg for neg in negs]
        powers = [self.rmul(neg, neg) for neg in negs]
        iters = int(math.log2(c)) - 1
        for it in range(iters):
            last = it == iters - 1
            rs = [self.rmul(x if last else jnp.concatenate([x, p], 0), p) for x, p in zip(xs, powers)]
            xs = [x + r[:c] for x, r in zip(xs, rs)]
            if not last:
                powers = [r[c:] for r in rs]
        return xs


def _softplus(x):
    return jnp.maximum(x, 0.0) + jnp.log1p(jnp.exp(-jnp.abs(x)))


def _dn_prep_kernel(chunk, group, precise, qd_ref, kd_ref, vd_ref, ba_ref, hq_ref, hk_ref, hv_ref,
                    wq_ref, wk_ref, wv_ref, alog_ref, dtb_ref,
                    wv_out, wk_out, qdec_out, kend_out, p_out, gend_out):
    h = pl.program_id(1)
    tb = qd_ref.shape[0]
    n = chunk * group

    def conv_silu(x_ref, halo_ref, w_ref):
        xp = jnp.concatenate([halo_ref[0], x_ref[...]], 0)
        w = w_ref[...]
        y = xp[5:5 + tb] * w[0:1]
        for tap in range(1, CONV_WIDTH):
            y = y + xp[5 + tap:5 + tap + tb] * w[tap:tap + 1]
        return _silu(y)

    q = conv_silu(qd_ref, hq_ref, wq_ref)
    k = conv_silu(kd_ref, hk_ref, wk_ref)
    v = conv_silu(vd_ref, hv_ref, wv_ref)
    q = q * lax.rsqrt(jnp.sum(q * q, -1, keepdims=True) + EPS) * (DN_KEY_DIM ** -0.5)
    k = k * lax.rsqrt(jnp.sum(k * k, -1, keepdims=True) + EPS)

    ba = ba_ref[...]
    lane = lax.broadcasted_iota(jnp.int32, ba.shape, 1)
    beta_all = jax.nn.sigmoid(ba)
    g_all = -jnp.exp(alog_ref[...]) * _softplus(ba + dtb_ref[...])
    beta = jnp.sum(jnp.where(lane == h, beta_all, 0.0), -1, keepdims=True)
    g = jnp.sum(jnp.where(lane == h + DN_HEADS, g_all, 0.0), -1, keepdims=True)

    li = lax.broadcasted_iota(jnp.int32, (n, n), 0)
    mi = lax.broadcasted_iota(jnp.int32, (n, n), 1)
    same = (li // chunk) == (mi // chunk)
    upto = jnp.logical_and(same, li <= mi)
    since = jnp.logical_and(same, li >= mi)
    chunk_end = mi == (li // chunk) * chunk + (chunk - 1)
    blocks = _BlockDiag(chunk, group)
    op_dtype = wk_out.dtype
    slot = p_out.shape[1]

    row_groups = [slice(gi * n, (gi + 1) * n) for gi in range(tb // n)]
    a_mats, stash = [], []
    for rows in row_groups:
        qc, kc, bc, gc = q[rows], k[rows], beta[rows], g[rows]
        g_row = jnp.sum(jnp.where(upto, gc, 0.0), 0, keepdims=True)
        g_col = jnp.sum(jnp.where(li == mi, g_row, 0.0), 1, keepdims=True)
        g_end = jnp.sum(jnp.where(chunk_end, g_row, 0.0), 1, keepdims=True)
        decay = jnp.exp(jnp.where(since, g_col - g_row, -jnp.inf))
        qk_kk = _dot_any(jnp.concatenate([qc, kc], 0), kc, precise, _dot_nt)
        a_mats.append(jnp.where(li > mi, bc * decay * qk_kk[n:], 0.0))
        e_g = jnp.exp(g_col)
        qdec_out[rows, :] = (e_g * qc).astype(op_dtype)
        kend_out[rows, :] = (jnp.exp(g_end - g_col) * kc).astype(op_dtype)
        p_out[rows, 0:n] = (qk_kk[:n] * decay).astype(op_dtype)
        if slot > n:
            p_out[rows, n:] = jnp.zeros((n, slot - n), op_dtype)
        stash.append((e_g, g_end))
    t_invs = blocks.unit_lower_inverse(a_mats)
    for gi, (rows, t_inv, (e_g, g_end)) in enumerate(zip(row_groups, t_invs, stash)):
        kc, vc, bc = k[rows], v[rows], beta[rows]
        w = blocks.lmul(t_inv, jnp.concatenate([bc * vc, (bc * e_g) * kc], 1))
        wv_out[rows, :] = w[:, :DN_VAL_DIM]
        wk_out[rows, :] = w[:, DN_VAL_DIM:].astype(op_dtype)
        for c in range(group):
            last = c * chunk + chunk - 1
            gend_out[gi * group + c] = jnp.broadcast_to(jnp.exp(g_end[last:last + 1]), (1, LANES))


def _dn_score_slot(chunk, group):
    return max(chunk * group, LANES)


def _dn_prep(p, halo, w_conv8, alog_row, dtb_row, chunk, group, tb, precise):
    m = p.shape[0]
    op_dtype = F32 if precise else BF16
    slot = _dn_score_slot(chunk, group)
    nh = DN_HEADS
    cq, ck, cv = P_CONV // LANES, P_CONV // LANES + nh, P_CONV // LANES + 2 * nh
    col = lambda base: (lambda i, h: (i, base + h))
    halo_spec = lambda base: pl.BlockSpec((1, 8, LANES), lambda i, h: (i, 0, base + h))
    w_spec = lambda base: pl.BlockSpec((8, LANES), lambda i, h: (0, base + h))
    const = pl.BlockSpec((1, LANES), lambda i, h: (0, 0))
    head_blk = pl.BlockSpec((tb, LANES), lambda i, h: (i, h))
    out_shape = [
        jax.ShapeDtypeStruct((m, DN_WIDTH), F32),
        jax.ShapeDtypeStruct((m, DN_WIDTH), op_dtype),
        jax.ShapeDtypeStruct((m, DN_WIDTH), op_dtype),
        jax.ShapeDtypeStruct((m, DN_WIDTH), op_dtype),
        jax.ShapeDtypeStruct((m, DN_HEADS * slot), op_dtype),
        jax.ShapeDtypeStruct((m // chunk, 1, DN_WIDTH), F32),
    ]
    return pl.pallas_call(
        functools.partial(_dn_prep_kernel, chunk, group, precise),
        grid=(m // tb, nh),
        in_specs=[
            pl.BlockSpec((tb, LANES), col(cq)),
            pl.BlockSpec((tb, LANES), col(ck)),
            pl.BlockSpec((tb, LANES), col(cv)),
            pl.BlockSpec((tb, LANES), lambda i, h: (i, P_BA // LANES)),
            halo_spec(0), halo_spec(nh), halo_spec(2 * nh),
            w_spec(0), w_spec(nh), w_spec(2 * nh),
            const, const,
        ],
        out_specs=[head_blk] * 4 + [pl.BlockSpec((tb, slot), lambda i, h: (i, h)),
                                    pl.BlockSpec((tb // chunk, 1, LANES), lambda i, h: (i, 0, h))],
        out_shape=out_shape,
        compiler_params=_cparams("parallel", "parallel"),
        name="dn_prep",
    )(p, p, p, p, halo, halo, halo, w_conv8, w_conv8, w_conv8, alog_row, dtb_row)


def _dn_scan_kernel(chunk, group, n_chunks, wv_ref, wk_ref, qd_ref, ke_ref, p_ref, ge_ref, gate_ref, s0_ref,
                    onorm_ref, od_ref, sout_ref, s_scr):
    n = pl.program_id(1)
    precise = wk_ref.dtype == F32
    slot = _dn_score_slot(chunk, group)

    @pl.when(n == 0)
    def _():
        s_scr[...] = s0_ref[0]

    onorm = onorm_ref[...]
    u_group = [[] for _ in range(DN_HEADS)]
    for c in range(n_chunks):
        rows = slice(c * chunk, (c + 1) * chunk)
        local = c % group
        for h in range(DN_HEADS):
            cols = slice(h * LANES, (h + 1) * LANES)
            s = s_scr[h]
            if not precise:
                s = s.astype(BF16)
            u = wv_ref[rows, cols] - _dot_any(wk_ref[rows, cols], s, precise)
            if not precise:
                u = u.astype(BF16)
            if local == 0:
                u_group[h] = []
            u_group[h].append(u)
            u_cat = u if local == 0 else jnp.concatenate(u_group[h], 0)
            o = (_dot_any(qd_ref[rows, cols], s, precise)
                 + _dot_any(p_ref[rows, h * slot:h * slot + (local + 1) * chunk], u_cat, precise))
            s_scr[h] = ge_ref[c, :, cols] * s_scr[h] + _dot_any(ke_ref[rows, cols], u, precise, _dot_tn, 1)
            gate = gate_ref[rows, cols]
            od_ref[rows, cols] = (_rms(o, onorm) * _silu(gate)).astype(od_ref.dtype)

    @pl.when(n == pl.num_programs(1) - 1)
    def _():
        sout_ref[0] = s_scr[...]


def _dn_scan(prep, p, s0, onorm_row, chunk, group, n_chunks, batch):
    wv, wk, qdec, kend, pm, gend = prep
    m = wv.shape[0]
    assert n_chunks % group == 0
    rows = chunk * n_chunks
    steps = m // batch // rows
    blk = lambda b, n: (b * steps + n, 0)
    wide = pl.BlockSpec((rows, DN_WIDTH), blk)
    state = pl.BlockSpec((1, DN_HEADS, DN_KEY_DIM, DN_VAL_DIM), lambda b, n: (b, 0, 0, 0))
    slot = _dn_score_slot(chunk, group)
    assert pm.shape[1] == DN_HEADS * slot
    return pl.pallas_call(
        functools.partial(_dn_scan_kernel, chunk, group, n_chunks),
        grid=(batch, steps),
        in_specs=[
            wide, wide, wide, wide, pl.BlockSpec((rows, DN_HEADS * slot), blk),
            pl.BlockSpec((n_chunks, 1, DN_WIDTH), lambda b, n: (b * steps + n, 0, 0)),
            pl.BlockSpec((rows, DN_WIDTH), lambda b, n: (b * steps + n, P_GATE // DN_WIDTH)),
            state,
            pl.BlockSpec((1, LANES), lambda b, n: (0, 0)),
        ],
        out_specs=[wide, state],
        out_shape=[jax.ShapeDtypeStruct((m, DN_WIDTH), wk.dtype),
                   jax.ShapeDtypeStruct((batch, DN_HEADS, DN_KEY_DIM, DN_VAL_DIM), F32)],
        scratch_shapes=[pltpu.VMEM((DN_HEADS, DN_KEY_DIM, DN_VAL_DIM), F32)],
        compiler_params=_cparams("parallel", "arbitrary"),
        name="dn_scan",
    )(wv, wk, qdec, kend, pm, gend, p, s0, onorm_row)


def _out_proj_kernel(nw, attn_ref, od_ref, *refs):
    w_refs, (x_ref, g_ref, gate_ref, o_ref) = refs[:nw], refs[nw:]
    y = (_mm(attn_ref[...], tuple(r[0:ATTN_WIDTH, :] for r in w_refs))
         + _mm(od_ref[...], tuple(r[ATTN_WIDTH:, :] for r in w_refs)))
    o_ref[...] = x_ref[...] + gate_ref[...] * _rms(y, g_ref[...])


def _out_proj(attn, od, w, x, gain, gate, tm):
    m, d = x.shape
    row = lambda i: (i, 0)
    return pl.pallas_call(
        functools.partial(_out_proj_kernel, len(w)),
        grid=(m // tm,),
        in_specs=[
            pl.BlockSpec((tm, ATTN_WIDTH), row),
            pl.BlockSpec((tm, DN_WIDTH), row),
        ] + [pl.BlockSpec((ATTN_WIDTH + DN_WIDTH, d), lambda i: (0, 0))] * len(w) + [
            pl.BlockSpec((tm, d), row),
            pl.BlockSpec((1, d), lambda i: (0, 0)),
            _mod_spec(gate.shape[0], tm, d),
        ],
        out_specs=pl.BlockSpec((tm, d), row),
        out_shape=jax.ShapeDtypeStruct((m, d), F32),
        compiler_params=_cparams("parallel"),
        name="out_proj",
    )(attn, od, *w, x, gain, gate)


def _ffn_kernel(nw, x_ref, g_ref, sc_ref, sh_ref, *refs):
    wg_refs, wu_refs, wd_refs = refs[:nw], refs[nw:2 * nw], refs[2 * nw:3 * nw]
    g2_ref, gate_ref, o_ref, h_ref, acc_ref = refs[3 * nw:]
    j = pl.program_id(1)

    @pl.when(j == 0)
    def _():
        h = _rms(x_ref[...], g_ref[...]) * (1.0 + sc_ref[...]) + sh_ref[...]
        h_ref[...] = h.astype(h_ref.dtype)
        acc_ref[...] = jnp.zeros_like(acc_ref)

    h = h_ref[...]
    act = _silu(_mm(h, tuple(r[...] for r in wg_refs))) * _mm(h, tuple(r[...] for r in wu_refs))
    acc_ref[...] += _mm(act, tuple(r[...] for r in wd_refs))

    @pl.when(j == pl.num_programs(1) - 1)
    def _():
        o_ref[...] = x_ref[...] + gate_ref[...] * _rms(acc_ref[...], g2_ref[...])


def _ffn(x, gain, scale, shift, wg, wu, wd, gain2, gate, tm, tf):
    m, d = x.shape
    nw = len(wg)
    f = wg[0].shape[1]
    row = lambda i, j: (i, 0)
    vec = pl.BlockSpec((1, d), lambda i, j: (0, 0))
    return pl.pallas_call(
        functools.partial(_ffn_kernel, nw),
        grid=(m // tm, f // tf),
        in_specs=[
            pl.BlockSpec((tm, d), row), vec,
            _mod_spec(scale.shape[0], tm, d), _mod_spec(shift.shape[0], tm, d),
        ] + [pl.BlockSpec((d, tf), lambda i, j: (0, j))] * (2 * nw)
        + [pl.BlockSpec((tf, d), lambda i, j: (j, 0))] * nw
        + [vec, _mod_spec(gate.shape[0], tm, d)],
        out_specs=pl.BlockSpec((tm, d), row),
        out_shape=jax.ShapeDtypeStruct((m, d), F32),
        scratch_shapes=[pltpu.VMEM((tm, d), BF16 if nw == 1 else F32), pltpu.VMEM((tm, d), F32)],
        compiler_params=_cparams("parallel", "arbitrary"),
        name="ffn_dense",
    )(x, gain, scale, shift, *wg, *wu, *wd, gain2, gate)


def _router_kernel(x_ref, g_ref, sc_ref, sh_ref, wr_ref, h_ref, gates_ref, idx_ref, w12_ref):
    h = _rms(x_ref[...], g_ref[...]) * (1.0 + sc_ref[...]) + sh_ref[...]
    h_ref[...] = h
    logits = jnp.dot(h, wr_ref[...], preferred_element_type=F32, precision=lax.Precision.HIGHEST)
    lane = lax.broadcasted_iota(jnp.int32, logits.shape, 1).astype(F32)
    logits = jnp.where(lane < N_EXPERTS, logits, -jnp.inf)
    m1 = jnp.max(logits, -1, keepdims=True)
    i1 = jnp.min(jnp.where(logits == m1, lane, float(LANES)), -1, keepdims=True)
    rest = jnp.where(lane == i1, -jnp.inf, logits)
    m2 = jnp.max(rest, -1, keepdims=True)
    i2 = jnp.min(jnp.where(rest == m2, lane, float(LANES)), -1, keepdims=True)
    t = jnp.exp(m2 - m1)
    w1 = 1.0 / (1.0 + t)
    w2 = t / (1.0 + t)
    gates_ref[...] = jnp.where(lane == i1, w1, 0.0) + jnp.where(lane == i2, w2, 0.0)
    idx_ref[...] = jnp.where(lane == 0.0, i1, jnp.where(lane == 1.0, i2, 0.0)).astype(jnp.int32)
    w12_ref[...] = jnp.where(lane == 0.0, w1, jnp.where(lane == 1.0, w2, 0.0))


def _router(x, gain, scale, shift, w_router_pad, tm):
    m, d = x.shape
    row = lambda i: (i, 0)
    vec = pl.BlockSpec((1, d), lambda i: (0, 0))
    small = pl.BlockSpec((tm, LANES), row)
    return pl.pallas_call(
        _router_kernel,
        grid=(m // tm,),
        in_specs=[pl.BlockSpec((tm, d), row), vec,
                  _mod_spec(scale.shape[0], tm, d), _mod_spec(shift.shape[0], tm, d),
                  pl.BlockSpec((d, LANES), lambda i: (0, 0))],
        out_specs=[pl.BlockSpec((tm, d), row), small, small, small],
        out_shape=[jax.ShapeDtypeStruct((m, d), F32), jax.ShapeDtypeStruct((m, LANES), F32),
                   jax.ShapeDtypeStruct((m, LANES), jnp.int32), jax.ShapeDtypeStruct((m, LANES), F32)],
        compiler_params=_cparams("parallel"),
        name="moe_router",
    )(x, gain, scale, shift, w_router_pad)


def _moe_gemm_kernel(nj, te_ref, tot_ref, rt_ref, h_hbm, wg_ref, wu_ref, wd_ref, ys_ref, xs_ref, xb_ref, acc_ref,
                     sems):
    r = pl.program_id(0)
    j = pl.program_id(1)
    tm = xb_ref.shape[0]
    total = tot_ref[0]
    active = r < total
    slot = r % 2
    share = -(-tm // nj)

    def row_copy(tile, t, s):
        return pltpu.make_async_copy(h_hbm.at[pl.ds(rt_ref[tile * tm + t], 1), :],
                                     xs_ref.at[s, pl.ds(t, 1), :], sems.at[s])

    def start_rows(tile, s, lo, hi):
        def body(t, carry):
            row_copy(tile, t, s).start()
            return carry

        lax.fori_loop(lo, hi, body, 0)

    @pl.when(jnp.logical_and(r == 0, j == 0))
    def _():
        start_rows(0, 0, 0, tm)

    @pl.when(jnp.logical_and(active, j == 0))
    def _():
        pltpu.make_async_copy(h_hbm.at[pl.ds(0, tm), :], xs_ref.at[slot], sems.at[slot]).wait()
        xb_ref[...] = xs_ref[slot].astype(BF16)
        acc_ref[...] = jnp.zeros_like(acc_ref)

    @pl.when(active)
    def _():
        more = r + 1 < total
        for u in range(share):
            t = j * share + u

            @pl.when(jnp.logical_and(more, t < tm))
            def _():
                row_copy(jnp.minimum(r + 1, pl.num_programs(0) - 1), jnp.minimum(t, tm - 1), 1 - slot).start()

        xb = xb_ref[...]
        act = (_silu(_dot(xb, wg_ref[0])) * _dot(xb, wu_ref[0])).astype(BF16)
        acc_ref[...] += _dot(act, wd_ref[0])

    @pl.when(j == pl.num_programs(1) - 1)
    def _():
        ys_ref[...] = jnp.where(active, acc_ref[...], 0.0)


def _moe_gemm(tile_expert, total_tiles, row_token, h, wg, wu, wd, tm, tf):
    n_tiles = tile_expert.shape[0]
    d = h.shape[1]
    f = wg.shape[2]
    nj = f // tf

    def w_col(r, j, te, tot, rt):
        return (te[r], 0, jnp.where(r < tot[0], j, nj - 1))

    def w_row(r, j, te, tot, rt):
        return (te[r], jnp.where(r < tot[0], j, nj - 1), 0)

    grid_spec = pltpu.PrefetchScalarGridSpec(
        num_scalar_prefetch=3,
        grid=(n_tiles, nj),
        in_specs=[
            pl.BlockSpec(memory_space=pl.ANY),
            pl.BlockSpec((1, d, tf), w_col),
            pl.BlockSpec((1, d, tf), w_col),
            pl.BlockSpec((1, tf, d), w_row),
        ],
        out_specs=pl.BlockSpec((tm, d), lambda r, j, te, tot, rt: (r, 0)),
        scratch_shapes=[pltpu.VMEM((2, tm, d), F32), pltpu.VMEM((tm, d), BF16), pltpu.VMEM((tm, d), F32),
                        pltpu.SemaphoreType.DMA((2,))],
    )
    return pl.pallas_call(
        functools.partial(_moe_gemm_kernel, nj),
        grid_spec=grid_spec,
        out_shape=jax.ShapeDtypeStruct((n_tiles * tm, d), F32),
        compiler_params=_cparams("arbitrary", "arbitrary", row_dma=True),
        name="moe_gemm",
    )(tile_expert, total_tiles, row_token, h, wg, wu, wd)


def _moe_combine_kernel(dest_ref, x_ref, w12_ref, g_ref, gate_ref, ys_hbm, o_ref, buf_ref, sems):
    i = pl.program_id(0)
    tb = x_ref.shape[0]
    slot = i % 2

    def row_copy(blk, t, k, s):
        src = dest_ref[2 * (blk * tb + t) + k]
        return pltpu.make_async_copy(ys_hbm.at[pl.ds(src, 1), :], buf_ref.at[s, k, pl.ds(t, 1), :], sems.at[s])

    def start_block(blk, s):
        def body(t, carry):
            row_copy(blk, t, 0, s).start()
            row_copy(blk, t, 1, s).start()
            return carry

        lax.fori_loop(0, tb, body, 0, unroll=8)

    @pl.when(i == 0)
    def _():
        start_block(0, 0)

    @pl.when(i + 1 < pl.num_programs(0))
    def _():
        start_block(i + 1, 1 - slot)

    for k in range(2):
        pltpu.make_async_copy(ys_hbm.at[pl.ds(0, tb), :], buf_ref.at[slot, k], sems.at[slot]).wait()
    w12 = w12_ref[...]
    y = w12[:, 0:1] * buf_ref[slot, 0] + w12[:, 1:2] * buf_ref[slot, 1]
    o_ref[...] = x_ref[...] + gate_ref[...] * _rms(y, g_ref[...])


def _moe_combine(dest, x, w12, gain, gate, ys, tb):
    m, d = x.shape
    row = lambda i, dst: (i, 0)
    grid_spec = pltpu.PrefetchScalarGridSpec(
        num_scalar_prefetch=1,
        grid=(m // tb,),
        in_specs=[
            pl.BlockSpec((tb, d), row),
            pl.BlockSpec((tb, LANES), row),
            pl.BlockSpec((1, d), lambda i, dst: (0, 0)),
            pl.BlockSpec((1, d), lambda i, dst: (0, 0)),
            pl.BlockSpec(memory_space=pl.ANY),
        ],
        out_specs=pl.BlockSpec((tb, d), row),
        scratch_shapes=[pltpu.VMEM((2, 2, tb, d), F32), pltpu.SemaphoreType.DMA((2,))],
    )
    return pl.pallas_call(
        _moe_combine_kernel,
        grid_spec=grid_spec,
        out_shape=jax.ShapeDtypeStruct((m, d), F32),
        compiler_params=_cparams("arbitrary", row_dma=True),
        name="moe_combine",
    )(dest, x, w12, gain, gate, ys)


def _route_tables(idx2, tm, n_tiles):
    m = idx2.shape[0]
    n_assign = 2 * m
    assert n_tiles * tm == n_assign + N_EXPERTS * tm
    experts = jnp.arange(N_EXPERTS, dtype=jnp.int32)
    e_flat = idx2.reshape(n_assign)
    onehot = (e_flat[:, None] == experts[None, :]).astype(jnp.int32)
    csum = jnp.cumsum(onehot, 0)
    counts = csum[-1]
    padded = ((counts + tm - 1) // tm) * tm
    pend = jnp.cumsum(padded)
    pstart = pend - padded
    dest = jnp.sum(onehot * (pstart[None, :] + csum - 1), -1).astype(jnp.int32)
    total_tiles = (pend[-1] // tm).astype(jnp.int32).reshape(1)
    tile_expert = jnp.minimum(
        jnp.searchsorted(pend // tm, jnp.arange(n_tiles, dtype=jnp.int32), side="right"), N_EXPERTS - 1
    ).astype(jnp.int32)
    filler_key = jnp.where(jnp.arange(tm, dtype=jnp.int32)[None, :] < (padded - counts)[:, None],
                           experts[:, None], N_EXPERTS).reshape(-1)
    keys = jnp.concatenate([e_flat, filler_key])
    tokens = jnp.concatenate([jnp.arange(n_assign, dtype=jnp.int32) // 2,
                              jnp.zeros((N_EXPERTS * tm,), jnp.int32)])
    _, row_token = lax.sort((keys, tokens), num_keys=1, is_stable=True)
    return tile_expert, total_tiles, row_token, dest


def _moe_dense_kernel(h_ref, gates_ref, wg_ref, wu_ref, wd_ref, x_ref, g_ref, gate_ref, o_ref, acc_ref, tot_ref):
    e = pl.program_id(0)
    j = pl.program_id(1)
    nj = pl.num_programs(1)

    @pl.when(jnp.logical_and(e == 0, j == 0))
    def _():
        tot_ref[...] = jnp.zeros_like(tot_ref)

    @pl.when(j == 0)
    def _():
        acc_ref[...] = jnp.zeros_like(acc_ref)

    h = h_ref[...].astype(BF16)
    act = (_silu(_dot(h, wg_ref[0])) * _dot(h, wu_ref[0])).astype(BF16)
    acc_ref[...] += _dot(act, wd_ref[0])

    @pl.when(j == nj - 1)
    def _():
        gates = gates_ref[...]
        lane = lax.broadcasted_iota(jnp.int32, gates.shape, 1)
        ge = jnp.sum(jnp.where(lane == e, gates, 0.0), -1, keepdims=True)
        tot_ref[...] += ge * acc_ref[...]

    @pl.when(jnp.logical_and(e == pl.num_programs(0) - 1, j == nj - 1))
    def _():
        o_ref[...] = x_ref[...] + gate_ref[...] * _rms(tot_ref[...], g_ref[...])


def _moe_dense(h, gates, wg, wu, wd, x, gain, gate, tf):
    m, d = x.shape
    f = wg.shape[2]
    full = pl.BlockSpec((m, d), lambda e, j: (0, 0))
    return pl.pallas_call(
        _moe_dense_kernel,
        grid=(N_EXPERTS, f // tf),
        in_specs=[
            full,
            pl.BlockSpec((m, LANES), lambda e, j: (0, 0)),
            pl.BlockSpec((1, d, tf), lambda e, j: (e, 0, j)),
            pl.BlockSpec((1, d, tf), lambda e, j: (e, 0, j)),
            pl.BlockSpec((1, tf, d), lambda e, j: (e, j, 0)),
            full,
            pl.BlockSpec((1, d), lambda e, j: (0, 0)),
            full,
        ],
        out_specs=full,
        out_shape=jax.ShapeDtypeStruct((m, d), F32),
        scratch_shapes=[pltpu.VMEM((m, d), F32), pltpu.VMEM((m, d), F32)],
        compiler_params=_cparams("arbitrary", "arbitrary"),
        name="moe_dense",
    )(h, gates, wg, wu, wd, x, gain, gate)


def _rope_tables(pos):
    half = ROPE_DIM // 2
    inv_freq = jnp.power(ROPE_THETA, -2.0 * jnp.arange(half, dtype=F32) / ROPE_DIM)
    ang = pos.astype(F32)[:, None] * inv_freq[None, :]
    cos, sin = jnp.cos(ang), jnp.sin(ang)
    t = pos.shape[0]
    rest = ATTN_HEAD_DIM - ROPE_DIM
    cos_h = jnp.concatenate([cos, cos, jnp.ones((t, rest), F32)], 1)
    sa_h = jnp.concatenate([-sin, jnp.zeros((t, half + rest), F32)], 1)
    sb_h = jnp.concatenate([jnp.zeros((t, half), F32), sin, jnp.zeros((t, rest), F32)], 1)
    rep = LANES // ATTN_HEAD_DIM
    return tuple(jnp.tile(a, (1, rep)) for a in (cos_h, sa_h, sb_h))


def _permute_w_in(w):
    o1 = ATTN_WIDTH
    o2 = o1 + KV_WIDTH
    o3 = o2 + KV_WIDTH
    o4 = o3 + DN_CONV_CH
    o5 = o4 + DN_WIDTH
    parts = [w[:, :o1], w[:, o4:o5], w[:, o3:o4], w[:, o1:o2], w[:, o2:o3], w[:, o5:]]
    used = sum(a.shape[1] for a in parts)
    parts.append(jnp.zeros((w.shape[0], P_WIDTH - used), w.dtype))
    return jnp.concatenate(parts, 1)


def _sample_mask_bias(s, lc):
    q_pos = PAST_LEN + np.arange(s)
    k_pos = np.concatenate([PAST_LEN - lc + np.arange(lc), q_pos])
    q_chunk = q_pos[:, None] // CHUNK
    k_chunk = k_pos[None, :] // CHUNK
    mask = (k_pos[None, :] >= 0) & (k_chunk <= q_chunk) & (k_pos[None, :] >= q_chunk * CHUNK - WINDOW)
    bias = np.where(mask, 0.0, -np.inf).astype(np.float32)
    return jnp.asarray(np.tile(bias, (4, 1)))


def _conv_halo(p, init, tb, seq):
    m = p.shape[0]
    batch = m // seq
    nb = seq // tb
    tails = p.reshape(batch, nb, tb, P_WIDTH)[:, :nb - 1, tb - (CONV_WIDTH - 1):, P_CONV:P_CONV + DN_CONV_CH]
    prev = jnp.concatenate([init[:, None], tails], 1)
    prev = prev.reshape(batch * nb, CONV_WIDTH - 1, DN_CONV_CH)
    return jnp.pad(prev, ((0, 0), (8 - (CONV_WIDTH - 1), 0), (0, 0)))


def _trunk(x, mods, layer_w, rope, past, cfg):
    m = x.shape[0]
    batch, seq = cfg["batch"], cfg["seq"]
    precise = cfg["precise"]
    nw = 2 if precise else 1
    ks, vs, ss, bufs = [], [], [], []
    cos, sa, sb = rope
    for l in range(DEPTH):
        w = layer_w[l]
        sh_a, sc_a, g_a, sh_f, sc_f, g_f = mods[l]
        p = _norm_proj(x, w["gain"][0], sc_a, sh_a, w["w_in"][:nw], cfg["tm_proj"], cfg["tn_proj"])
        if past is None:
            attn, k_new = _attn_prompt(p, w["sinks"], cos, sa, sb, cfg["tb_attn"])
            s0 = jnp.zeros((batch, DN_HEADS, DN_KEY_DIM, DN_VAL_DIM), F32)
            conv_init = jnp.zeros((batch, CONV_WIDTH - 1, DN_CONV_CH), F32)
        else:
            ck = past[0][l].reshape(batch, -1, KV_WIDTH)
            cv = past[1][l].reshape(batch, -1, KV_WIDTH)
            attn, k_new = _attn_sample(p, ck, cv, w["sinks"], cos, sa, sb, cfg["bias"], batch, seq)
            s0 = past[2][l]
            conv_init = past[3][l]
        halo = _conv_halo(p, conv_init, cfg["tb_dn"], seq)
        prep = _dn_prep(p, halo, w["w_conv"], w["alog"], w["dtb"], cfg["chunk"], cfg["group"], cfg["tb_dn"],
                        precise)
        od, s_new = _dn_scan(prep, p, s0, w["onorm"], cfg["chunk"], cfg["group"], cfg["scan_chunks"], batch)
        x = _out_proj(attn, od, w["w_out"][:nw], x, w["gain"][1], g_a, cfg["tm_out"])
        if l % 2 == 0:
            x = _ffn(x, w["gain"][2], sc_f, sh_f, w["ffn_gate"][:nw], w["ffn_up"][:nw], w["ffn_down"][:nw],
                     w["gain"][3], g_f, cfg["tm_ffn"], cfg["tf_ffn"])
        else:
            h, gates, idx, w12 = _router(x, w["gain"][2], sc_f, sh_f, w["router"], cfg["tm_router"])
            if cfg["routed"]:
                tm = cfg["tm_moe"]
                n_tiles = 2 * m // tm + N_EXPERTS
                tile_expert, total_tiles, row_token, dest = _route_tables(idx[:, :2], tm, n_tiles)
                ys = _moe_gemm(tile_expert, total_tiles, row_token, h, w["moe_gate"], w["moe_up"], w["moe_down"],
                               tm, cfg["tf_moe"])
                x = _moe_combine(dest, x, w12, w["gain"][3], g_f, ys, cfg["tb_combine"])
            else:
                x = _moe_dense(h, gates, w["moe_gate"], w["moe_up"], w["moe_down"], x, w["gain"][3], g_f,
                               cfg["tf_moe"])
        pb = p.reshape(batch, seq, P_WIDTH)
        keep = min(WINDOW, seq) if past is None else seq
        ks.append(k_new.reshape(batch, seq, ATTN_KV_HEADS, ATTN_HEAD_DIM)[:, seq - keep:])
        vs.append(pb[:, seq - keep:, P_V:P_V + KV_WIDTH].reshape(batch, keep, ATTN_KV_HEADS, ATTN_HEAD_DIM))
        ss.append(s_new)
        assert seq >= CONV_WIDTH - 1
        bufs.append(pb[:, seq - (CONV_WIDTH - 1):, P_CONV:P_CONV + DN_CONV_CH])
    return x, jnp.stack(ks), jnp.stack(vs), jnp.stack(ss), jnp.stack(bufs)


def kernel(x_prompt, x_sample, cache_attn_k, cache_attn_v, state_delta, state_conv, c_prompt, c_sample, w_in, w_conv, attn_sinks, dn_a_log, dn_dt_bias, dn_norm, w_out, w_mod, b_mod, norm_gains, ffn_gate, ffn_up, ffn_down, moe_router, moe_gate, moe_up, moe_down):
    bp, tp, d = x_prompt.shape
    bs, ts, _ = x_sample.shape
    assert bp == 1 and d == D_MODEL

    c_all = jnp.concatenate([c_prompt, c_sample, jnp.zeros((16 - bp - bs, d), F32)], 0)
    mod = _modulation(c_all, w_mod, b_mod)
    mods_p, mods_s = [], []
    for l in range(DEPTH):
        six = jnp.split(mod[l], 6, -1)
        mods_p.append([a[0:bp] for a in six])
        mods_s.append([jnp.repeat(a[bp:bp + bs], ts, axis=0) for a in six])

    def pad_lanes(v, at):
        return jnp.zeros((1, LANES), F32).at[0, at:at + v.shape[0]].set(v)

    layer_w = []
    for l in range(DEPTH):
        w = {
            "gain": [norm_gains[l, i].reshape(1, d) for i in range(4)],
            "w_in": _split_weight(_permute_w_in(w_in[l])),
            "sinks": attn_sinks[l],
            "w_conv": jnp.pad(w_conv[l], ((0, 8 - CONV_WIDTH), (0, 0))),
            "alog": pad_lanes(dn_a_log[l], DN_HEADS),
            "dtb": pad_lanes(dn_dt_bias[l], DN_HEADS),
            "onorm": dn_norm[l].reshape(1, DN_VAL_DIM),
            "w_out": _split_weight(w_out[l]),
        }
        if l % 2 == 0:
            w["ffn_gate"] = _split_weight(ffn_gate[l // 2])
            w["ffn_up"] = _split_weight(ffn_up[l // 2])
            w["ffn_down"] = _split_weight(ffn_down[l // 2])
        else:
            w["router"] = jnp.pad(moe_router[l // 2], ((0, 0), (0, LANES - N_EXPERTS)))
            w["moe_gate"] = moe_gate[l // 2].astype(BF16)
            w["moe_up"] = moe_up[l // 2].astype(BF16)
            w["moe_down"] = moe_down[l // 2].astype(BF16)
        layer_w.append(w)

    cfg_p = dict(batch=bp, seq=tp, precise=False, chunk=CHUNK, group=2, scan_chunks=4, tm_proj=1024, tn_proj=512, tb_attn=512,
                 tb_dn=512, tm_out=512, tm_ffn=512, tf_ffn=512, tm_router=512, routed=True, tm_moe=512,
                 tf_moe=256, tb_combine=256)
    rope_p = _rope_tables(jnp.arange(tp, dtype=jnp.int32))
    y_p, k_p, v_p, s_p, conv_p = _trunk(x_prompt.reshape(bp * tp, d), mods_p, layer_w, rope_p, None, cfg_p)

    ms = bs * ts
    cfg_s = dict(batch=bs, seq=ts, precise=True, chunk=ts, group=1, scan_chunks=1, tm_proj=ms, tn_proj=512, tb_dn=ts, tm_out=ms,
                 tm_ffn=ms, tf_ffn=512, tm_router=ms, routed=False, tf_moe=1408,
                 bias=_sample_mask_bias(ts, cache_attn_k.shape[2]))
    rope_s = _rope_tables(PAST_LEN + jnp.arange(ts, dtype=jnp.int32))
    past = (cache_attn_k, cache_attn_v, state_delta, state_conv)
    y_s, k_s, v_s, s_s, conv_s = _trunk(x_sample.reshape(ms, d), mods_s, layer_w, rope_s, past, cfg_s)

    return (y_p.reshape(bp, tp, d), y_s.reshape(bs, ts, d), k_p, v_p, s_p, conv_p, k_s, v_s, s_s, conv_s)
```

```python
import functools
import math

import numpy as np
import jax
import jax.numpy as jnp
from jax import lax
from jax.experimental import pallas as pl
from jax.experimental.pallas import tpu as pltpu

D_MODEL = 2048
DEPTH = 2
PAST_LEN = 1024
CHUNK = 64
ATTN_HEADS = 16
ATTN_KV_HEADS = 2
ATTN_HEAD_DIM = 64
ATTN_WIDTH = 1024
KV_WIDTH = 128
WINDOW = 128
ROPE_THETA = 500000.0
ROPE_DIM = 16
DN_HEADS = 8
DN_KEY_DIM = 128
DN_VAL_DIM = 128
DN_WIDTH = 1024
CONV_WIDTH = 4
DN_CONV_CH = 3072
D_FF = 5632
N_EXPERTS = 8
D_FF_EXPERT = 2816
EPS = 1e-6

F32 = jnp.float32
BF16 = jnp.bfloat16
LANES = 128
V7X_VMEM_LIMIT = 56 * 1024 * 1024

P_Q = 0
P_GATE = 1024
P_CONV = 2048
P_K = 5120
P_V = 5248
P_BA = 5376
P_WIDTH = 5632


def _cparams(*sem, row_dma=False):
    return pltpu.CompilerParams(dimension_semantics=sem, vmem_limit_bytes=V7X_VMEM_LIMIT,
                                disable_bounds_checks=row_dma)


def _silu(x):
    return x * jax.nn.sigmoid(x)


def _rms(x, gain):
    return x * lax.rsqrt(jnp.mean(x * x, -1, keepdims=True) + EPS) * gain


def _dot(a, b):
    return jnp.dot(a, b, preferred_element_type=F32)


def _dot_nt(a, b):
    return lax.dot_general(a, b, (((1,), (1,)), ((), ())), preferred_element_type=F32)


def _dot_tn(a, b):
    return lax.dot_general(a, b, (((0,), (0,)), ((), ())), preferred_element_type=F32)


def _split_bf16(a):
    hi = a.astype(BF16)
    lo = (a - hi.astype(F32)).astype(BF16)
    return hi, lo


def _dot_x3(a, b, dot=_dot, out_axis=0):
    a_hi, a_lo = _split_bf16(a)
    b_hi, b_lo = _split_bf16(b)
    n = a.shape[out_axis]
    top = dot(jnp.concatenate([a_hi, a_lo], out_axis), b_hi)
    return top[:n] + top[n:] + dot(a_hi, b_lo)


def _dot_any(a, b, precise, dot=_dot, out_axis=0):
    if precise:
        return _dot_x3(a, b, dot, out_axis)
    return dot(a.astype(BF16), b.astype(BF16))


def _mm(a, w):
    if len(w) == 1:
        return _dot(a.astype(BF16), w[0])
    a_hi, a_lo = _split_bf16(a)
    n = a.shape[0]
    top = _dot(jnp.concatenate([a_hi, a_lo], 0), w[0])
    return top[:n] + top[n:] + _dot(a_hi, w[1])


def _split_weight(w):
    hi, lo = _split_bf16(w)
    return (hi, lo)


def _mod_spec(rows, tm, d):
    if rows == 1:
        return pl.BlockSpec((1, d), lambda i, *_: (0, 0))
    return pl.BlockSpec((tm, d), lambda i, *_: (i, 0))


def _mod_kernel(c_ref, w_ref, b_ref, o_ref):
    o_ref[0] = _dot_x3(_silu(c_ref[...]), w_ref[0]) + b_ref[0]


def _modulation(c_all, w_mod, b_mod):
    rows = c_all.shape[0]
    n = w_mod.shape[2]
    tn = 1024
    return pl.pallas_call(
        _mod_kernel,
        grid=(DEPTH, n // tn),
        in_specs=[
            pl.BlockSpec((rows, D_MODEL), lambda l, j: (0, 0)),
            pl.BlockSpec((1, D_MODEL, tn), lambda l, j: (l, 0, j)),
            pl.BlockSpec((1, 1, tn), lambda l, j: (l, 0, j)),
        ],
        out_specs=pl.BlockSpec((1, rows, tn), lambda l, j: (l, 0, j)),
        out_shape=jax.ShapeDtypeStruct((DEPTH, rows, n), F32),
        compiler_params=_cparams("parallel", "parallel"),
        name="modulation",
    )(c_all, w_mod, b_mod.reshape(DEPTH, 1, n))


def _norm_proj_kernel(nw, x_ref, g_ref, sc_ref, sh_ref, *refs):
    w_refs, (o_ref, h_ref) = refs[:nw], refs[nw:]

    @pl.when(pl.program_id(1) == 0)
    def _():
        h = _rms(x_ref[...], g_ref[...]) * (1.0 + sc_ref[...]) + sh_ref[...]
        h_ref[...] = h.astype(h_ref.dtype)

    o_ref[...] = _mm(h_ref[...], tuple(r[...] for r in w_refs))


def _norm_proj(x, gain, scale, shift, w, tm, tn):
    m, d = x.shape
    n = w[0].shape[1]
    return pl.pallas_call(
        functools.partial(_norm_proj_kernel, len(w)),
        grid=(m // tm, n // tn),
        in_specs=[
            pl.BlockSpec((tm, d), lambda i, j: (i, 0)),
            pl.BlockSpec((1, d), lambda i, j: (0, 0)),
            _mod_spec(scale.shape[0], tm, d),
            _mod_spec(shift.shape[0], tm, d),
        ] + [pl.BlockSpec((d, tn), lambda i, j: (0, j))] * len(w),
        out_specs=pl.BlockSpec((tm, tn), lambda i, j: (i, j)),
        out_shape=jax.ShapeDtypeStruct((m, n), F32),
        scratch_shapes=[pltpu.VMEM((tm, d), BF16 if len(w) == 1 else F32)],
        compiler_params=_cparams("parallel", "arbitrary"),
        name="norm_proj",
    )(x, gain, scale, shift, *w)


def _rope(x, cos, sa, sb):
    return x * cos + pltpu.roll(x, LANES - 8, 1) * sa + pltpu.roll(x, 8, 1) * sb


def _kv_variants(k, v):
    lo = lax.broadcasted_iota(jnp.int32, k.shape, 1) < ATTN_HEAD_DIM
    kr = pltpu.roll(k, ATTN_HEAD_DIM, 1)
    vr = pltpu.roll(v, ATTN_HEAD_DIM, 1)
    zero = jnp.zeros_like(k)
    k_lo = (jnp.where(lo, k, zero), jnp.where(lo, kr, zero))
    k_hi = (jnp.where(lo, zero, kr), jnp.where(lo, zero, k))
    v_lo = (jnp.where(lo, v, zero), jnp.where(lo, vr, zero))
    v_hi = (jnp.where(lo, zero, vr), jnp.where(lo, zero, v))
    return k_lo, k_hi, v_lo, v_hi


def _sink_softmax(s, sink):
    m = jnp.maximum(jnp.max(s, -1, keepdims=True), sink)
    p = jnp.exp(s - m)
    den = jnp.sum(p, -1, keepdims=True) + jnp.exp(sink - m)
    return p / den


def _attn_core(qbs, k_los, k_his, v_los, v_his, bias, sinks, precise=False):
    scores = [(_dot_any(qb, k_lo, precise, _dot_nt) + bias, _dot_any(qb, k_hi, precise, _dot_nt) + bias)
              for qb, k_lo, k_hi in zip(qbs, k_los, k_his)]
    probs = [(_sink_softmax(s_even, sink[0]), _sink_softmax(s_odd, sink[1]))
             for (s_even, s_odd), sink in zip(scores, sinks)]
    return [_dot_any(p_even, v_lo, precise) + _dot_any(p_odd, v_hi, precise)
            for (p_even, p_odd), v_lo, v_hi in zip(probs, v_los, v_his)]


def _sink_columns(sink_ref, rows_per_pair):
    n = 4 * rows_per_pair
    pair = lax.broadcasted_iota(jnp.int32, (n, 1), 0) // rows_per_pair
    out = []
    for j in range(ATTN_KV_HEADS):
        cols = []
        for par in range(2):
            col = jnp.zeros((n, 1), F32)
            for a in range(4):
                col = jnp.where(pair == a, sink_ref[8 * j + 2 * a + par], col)
            cols.append(col)
        out.append(cols)
    return out


def _attn_prompt_kernel(sink_ref, q_ref, kv_ref, cos_ref, sa_ref, sb_ref, o_ref, knew_ref,
                        qs_ref, klo_ref, khi_ref, vlo_ref, vhi_ref):
    i = pl.program_id(0)
    tb = q_ref.shape[0]
    bufs = (klo_ref, khi_ref, vlo_ref, vhi_ref)

    @pl.when(i == 0)
    def _():
        for r in bufs:
            r[:, 0:WINDOW, :] = jnp.zeros((ATTN_KV_HEADS, WINDOW, LANES), BF16)

    @pl.when(i > 0)
    def _():
        for r in bufs:
            r[:, 0:WINDOW, :] = r[:, tb:tb + WINDOW, :]

    cos, sa, sb = cos_ref[...], sa_ref[...], sb_ref[...]
    k = _rope(kv_ref[:, 0:LANES], cos, sa, sb)
    knew_ref[...] = k
    variants = _kv_variants(k, kv_ref[:, LANES:2 * LANES])
    for r, var in zip(bufs, variants):
        for j in range(ATTN_KV_HEADS):
            r[j, WINDOW:, :] = var[j].astype(BF16)
    scale = ATTN_HEAD_DIM ** -0.5
    for a in range(ATTN_WIDTH // LANES):
        cols = slice(a * LANES, (a + 1) * LANES)
        qs_ref[:, cols] = (_rope(q_ref[:, cols], cos, sa, sb) * scale).astype(BF16)

    sinks = _sink_columns(sink_ref, CHUNK)
    nk = WINDOW + CHUNK

    def chunk_body(c, carry):
        r0 = pl.multiple_of(c * CHUNK, CHUNK)
        kpos = i * tb - WINDOW + r0 + lax.broadcasted_iota(jnp.int32, (1, nk), 1)
        bias = jnp.where(kpos >= 0, 0.0, -jnp.inf).astype(F32)
        heads = range(ATTN_KV_HEADS)
        qbs = [jnp.concatenate(
            [qs_ref[pl.ds(r0, CHUNK), (4 * j + a) * LANES:(4 * j + a + 1) * LANES] for a in range(4)], 0)
            for j in heads]
        keys = pl.ds(r0, nk)
        outs = _attn_core(qbs, [klo_ref[j, keys, :] for j in heads], [khi_ref[j, keys, :] for j in heads],
                          [vlo_ref[j, keys, :] for j in heads], [vhi_ref[j, keys, :] for j in heads], bias, sinks)
        for j, o in zip(heads, outs):
            for a in range(4):
                o_ref[pl.ds(r0, CHUNK), (4 * j + a) * LANES:(4 * j + a + 1) * LANES] = (
                    o[a * CHUNK:(a + 1) * CHUNK].astype(BF16))
        return carry

    lax.fori_loop(0, tb // CHUNK, chunk_body, 0, unroll=2)


def _attn_prompt(p, sinks, cos, sa, sb, tb):
    t = p.shape[0]
    kv_blk = P_K // (2 * LANES)
    row = lambda i: (i, 0)
    return pl.pallas_call(
        _attn_prompt_kernel,
        grid=(t // tb,),
        in_specs=[
            pl.BlockSpec(memory_space=pltpu.SMEM),
            pl.BlockSpec((tb, ATTN_WIDTH), row),
            pl.BlockSpec((tb, 2 * LANES), lambda i: (i, kv_blk)),
            pl.BlockSpec((tb, LANES), row),
            pl.BlockSpec((tb, LANES), row),
            pl.BlockSpec((tb, LANES), row),
        ],
        out_specs=[pl.BlockSpec((tb, ATTN_WIDTH), row), pl.BlockSpec((tb, LANES), row)],
        out_shape=[jax.ShapeDtypeStruct((t, ATTN_WIDTH), BF16), jax.ShapeDtypeStruct((t, LANES), F32)],
        scratch_shapes=[pltpu.VMEM((tb, ATTN_WIDTH), BF16)]
        + [pltpu.VMEM((ATTN_KV_HEADS, tb + WINDOW, LANES), BF16) for _ in range(4)],
        compiler_params=_cparams("arbitrary"),
        name="attn_prompt",
    )(sinks, p, p, cos, sa, sb)


def _attn_sample_kernel(sink_ref, q_ref, kv_ref, ck_ref, cv_ref, cos_ref, sa_ref, sb_ref, bias_ref,
                        o_ref, knew_ref):
    s = q_ref.shape[0]
    cos, sa, sb = cos_ref[...], sa_ref[...], sb_ref[...]
    k = _rope(kv_ref[:, 0:LANES], cos, sa, sb)
    knew_ref[...] = k
    kk = jnp.concatenate([ck_ref[0], k], 0)
    vv = jnp.concatenate([cv_ref[0], kv_ref[:, LANES:2 * LANES]], 0)
    k_lo, k_hi, v_lo, v_hi = _kv_variants(kk, vv)
    sinks = _sink_columns(sink_ref, s)
    scale = ATTN_HEAD_DIM ** -0.5
    bias = bias_ref[...]
    qbs = [jnp.concatenate(
        [_rope(q_ref[:, (4 * j + a) * LANES:(4 * j + a + 1) * LANES], cos, sa, sb) * scale for a in range(4)], 0)
        for j in range(ATTN_KV_HEADS)]
    outs = _attn_core(qbs, k_lo, k_hi, v_lo, v_hi, bias, sinks, precise=True)
    for j, o in enumerate(outs):
        for a in range(4):
            o_ref[:, (4 * j + a) * LANES:(4 * j + a + 1) * LANES] = o[a * s:(a + 1) * s]


def _attn_sample(p, cache_k, cache_v, sinks, cos, sa, sb, bias, batch, s):
    lc = cache_k.shape[1]
    kv_blk = P_K // (2 * LANES)
    row = lambda b: (b, 0)
    const = lambda b: (0, 0)
    return pl.pallas_call(
        _attn_sample_kernel,
        grid=(batch,),
        in_specs=[
            pl.BlockSpec(memory_space=pltpu.SMEM),
            pl.BlockSpec((s, ATTN_WIDTH), row),
            pl.BlockSpec((s, 2 * LANES), lambda b: (b, kv_blk)),
            pl.BlockSpec((1, lc, LANES), lambda b: (b, 0, 0)),
            pl.BlockSpec((1, lc, LANES), lambda b: (b, 0, 0)),
            pl.BlockSpec((s, LANES), const),
            pl.BlockSpec((s, LANES), const),
            pl.BlockSpec((s, LANES), const),
            pl.BlockSpec((4 * s, lc + s), const),
        ],
        out_specs=[pl.BlockSpec((s, ATTN_WIDTH), row), pl.BlockSpec((s, LANES), row)],
        out_shape=[jax.ShapeDtypeStruct((batch * s, ATTN_WIDTH), F32),
                   jax.ShapeDtypeStruct((batch * s, LANES), F32)],
        compiler_params=_cparams("parallel"),
        name="attn_sample",
    )(sinks, p, p, cache_k, cache_v, cos, sa, sb, bias)


class _BlockDiag:
    def __init__(self, chunk, group):
        self.chunk, self.group = chunk, group
        n = chunk * group
        lane = lax.broadcasted_iota(jnp.int32, (chunk, n), 1)
        self.lane_block = lane // chunk
        self.eye = (lax.broadcasted_iota(jnp.int32, (chunk, n), 0) == lane % chunk).astype(F32)

    def wide(self, tall):
        c = self.chunk
        out = tall[0:c]
        for b in range(1, self.group):
            out = out + tall[b * c:(b + 1) * c]
        return out

    def expand(self, wide):
        if self.group == 1:
            return wide
        zero = jnp.zeros_like(wide)
        return jnp.concatenate([jnp.where(self.lane_block == b, wide, zero) for b in range(self.group)], 0)

    def rmul(self, lhs, wide):
        l_hi, l_lo = _split_bf16(lhs)
        w_hi, w_lo = _split_bf16(wide)
        m = lhs.shape[0]
        top = _dot(jnp.concatenate([l_hi, l_lo], 0), self.expand(w_hi))
        return top[:m] + top[m:] + _dot(l_hi, self.expand(w_lo))

    def lmul(self, wide, rhs):
        w_hi, w_lo = _split_bf16(wide)
        r_hi, r_lo = _split_bf16(rhs)
        n = self.chunk * self.group
        e_hi = self.expand(w_hi)
        top = _dot(jnp.concatenate([e_hi, self.expand(w_lo)], 0), r_hi)
        return top[:n] + top[n:] + _dot(e_hi, r_lo)

    def unit_lower_inverse(self, a_talls):
        c = self.chunk
        negs = [-self.wide(a) for a in a_talls]
        xs = [self.eye + neg for neg in negs]
        powers = [self.rmul(neg, neg) for neg in negs]
        iters = int(math.log2(c)) - 1
        for it in range(iters):
            last = it == iters - 1
            rs = [self.rmul(x if last else jnp.concatenate([x, p], 0), p) for x, p in zip(xs, powers)]
            xs = [x + r[:c] for x, r in zip(xs, rs)]
            if not last:
                powers = [r[c:] for r in rs]
        return xs


def _softplus(x):
    return jnp.maximum(x, 0.0) + jnp.log1p(jnp.exp(-jnp.abs(x)))


def _dn_prep_kernel(chunk, group, precise, qd_ref, kd_ref, vd_ref, ba_ref, hq_ref, hk_ref, hv_ref,
                    wq_ref, wk_ref, wv_ref, alog_ref, dtb_ref,
                    wv_out, wk_out, qdec_out, kend_out, p_out, gend_out):
    h = pl.program_id(1)
    tb = qd_ref.shape[0]
    n = chunk * group

    def conv_silu(x_ref, halo_ref, w_ref):
        xp = jnp.concatenate([halo_ref[0], x_ref[...]], 0)
        w = w_ref[...]
        y = xp[5:5 + tb] * w[0:1]
        for tap in range(1, CONV_WIDTH):
            y = y + xp[5 + tap:5 + tap + tb] * w[tap:tap + 1]
        return _silu(y)

    q = conv_silu(qd_ref, hq_ref, wq_ref)
    k = conv_silu(kd_ref, hk_ref, wk_ref)
    v = conv_silu(vd_ref, hv_ref, wv_ref)
    q = q * lax.rsqrt(jnp.sum(q * q, -1, keepdims=True) + EPS) * (DN_KEY_DIM ** -0.5)
    k = k * lax.rsqrt(jnp.sum(k * k, -1, keepdims=True) + EPS)

    ba = ba_ref[...]
    lane = lax.broadcasted_iota(jnp.int32, ba.shape, 1)
    beta_all = jax.nn.sigmoid(ba)
    g_all = -jnp.exp(alog_ref[...]) * _softplus(ba + dtb_ref[...])
    beta = jnp.sum(jnp.where(lane == h, beta_all, 0.0), -1, keepdims=True)
    g = jnp.sum(jnp.where(lane == h + DN_HEADS, g_all, 0.0), -1, keepdims=True)

    li = lax.broadcasted_iota(jnp.int32, (n, n), 0)
    mi = lax.broadcasted_iota(jnp.int32, (n, n), 1)
    same = (li // chunk) == (mi // chunk)
    upto = jnp.logical_and(same, li <= mi)
    since = jnp.logical_and(same, li >= mi)
    chunk_end = mi == (li // chunk) * chunk + (chunk - 1)
    blocks = _BlockDiag(chunk, group)
    op_dtype = wk_out.dtype
    slot = p_out.shape[1]

    row_groups = [slice(gi * n, (gi + 1) * n) for gi in range(tb // n)]
    a_mats, stash = [], []
    for rows in row_groups:
        qc, kc, bc, gc = q[rows], k[rows], beta[rows], g[rows]
        g_row = jnp.sum(jnp.where(upto, gc, 0.0), 0, keepdims=True)
        g_col = jnp.sum(jnp.where(li == mi, g_row, 0.0), 1, keepdims=True)
        g_end = jnp.sum(jnp.where(chunk_end, g_row, 0.0), 1, keepdims=True)
        decay = jnp.exp(jnp.where(since, g_col - g_row, -jnp.inf))
        qk_kk = _dot_any(jnp.concatenate([qc, kc], 0), kc, precise, _dot_nt)
        a_mats.append(jnp.where(li > mi, bc * decay * qk_kk[n:], 0.0))
        e_g = jnp.exp(g_col)
        qdec_out[rows, :] = (e_g * qc).astype(op_dtype)
        kend_out[rows, :] = (jnp.exp(g_end - g_col) * kc).astype(op_dtype)
        p_out[rows, 0:n] = (qk_kk[:n] * decay).astype(op_dtype)
        if slot > n:
            p_out[rows, n:] = jnp.zeros((n, slot - n), op_dtype)
        stash.append((e_g, g_end))
    t_invs = blocks.unit_lower_inverse(a_mats)
    for gi, (rows, t_inv, (e_g, g_end)) in enumerate(zip(row_groups, t_invs, stash)):
        kc, vc, bc = k[rows], v[rows], beta[rows]
        w = blocks.lmul(t_inv, jnp.concatenate([bc * vc, (bc * e_g) * kc], 1))
        wv_out[rows, :] = w[:, :DN_VAL_DIM]
        wk_out[rows, :] = w[:, DN_VAL_DIM:].astype(op_dtype)
        for c in range(group):
            last = c * chunk + chunk - 1
            gend_out[gi * group + c] = jnp.broadcast_to(jnp.exp(g_end[last:last + 1]), (1, LANES))


def _dn_score_slot(chunk, group):
    return max(chunk * group, LANES)


def _dn_prep(p, halo, w_conv8, alog_row, dtb_row, chunk, group, tb, precise):
    m = p.shape[0]
    op_dtype = F32 if precise else BF16
    slot = _dn_score_slot(chunk, group)
    nh = DN_HEADS
    cq, ck, cv = P_CONV // LANES, P_CONV // LANES + nh, P_CONV // LANES + 2 * nh
    col = lambda base: (lambda i, h: (i, base + h))
    halo_spec = lambda base: pl.BlockSpec((1, 8, LANES), lambda i, h: (i, 0, base + h))
    w_spec = lambda base: pl.BlockSpec((8, LANES), lambda i, h: (0, base + h))
    const = pl.BlockSpec((1, LANES), lambda i, h: (0, 0))
    head_blk = pl.BlockSpec((tb, LANES), lambda i, h: (i, h))
    out_shape = [
        jax.ShapeDtypeStruct((m, DN_WIDTH), F32),
        jax.ShapeDtypeStruct((m, DN_WIDTH), op_dtype),
        jax.ShapeDtypeStruct((m, DN_WIDTH), op_dtype),
        jax.ShapeDtypeStruct((m, DN_WIDTH), op_dtype),
        jax.ShapeDtypeStruct((m, DN_HEADS * slot), op_dtype),
        jax.ShapeDtypeStruct((m // chunk, 1, DN_WIDTH), F32),
    ]
    return pl.pallas_call(
        functools.partial(_dn_prep_kernel, chunk, group, precise),
        grid=(m // tb, nh),
        in_specs=[
            pl.BlockSpec((tb, LANES), col(cq)),
            pl.BlockSpec((tb, LANES), col(ck)),
            pl.BlockSpec((tb, LANES), col(cv)),
            pl.BlockSpec((tb, LANES), lambda i, h: (i, P_BA // LANES)),
            halo_spec(0), halo_spec(nh), halo_spec(2 * nh),
            w_spec(0), w_spec(nh), w_spec(2 * nh),
            const, const,
        ],
        out_specs=[head_blk] * 4 + [pl.BlockSpec((tb, slot), lambda i, h: (i, h)),
                                    pl.BlockSpec((tb // chunk, 1, LANES), lambda i, h: (i, 0, h))],
        out_shape=out_shape,
        compiler_params=_cparams("parallel", "parallel"),
        name="dn_prep",
    )(p, p, p, p, halo, halo, halo, w_conv8, w_conv8, w_conv8, alog_row, dtb_row)


def _dn_scan_kernel(chunk, group, n_chunks, wv_ref, wk_ref, qd_ref, ke_ref, p_ref, ge_ref, gate_ref, s0_ref,
                    onorm_ref, od_ref, sout_ref, s_scr):
    n = pl.program_id(1)
    precise = wk_ref.dtype == F32
    slot = _dn_score_slot(chunk, group)

    @pl.when(n == 0)
    def _():
        s_scr[...] = s0_ref[0]

    onorm = onorm_ref[...]
    u_group = [[] for _ in range(DN_HEADS)]
    for c in range(n_chunks):
        rows = slice(c * chunk, (c + 1) * chunk)
        local = c % group
        for h in range(DN_HEADS):
            cols = slice(h * LANES, (h + 1) * LANES)
            s = s_scr[h]
            if not precise:
                s = s.astype(BF16)
            ws_qs = _dot_any(jnp.concatenate([wk_ref[rows, cols], qd_ref[rows, cols]], 0), s, precise)
            u = wv_ref[rows, cols] - ws_qs[:chunk]
            if not precise:
                u = u.astype(BF16)
            if local == 0:
                u_group[h] = []
            u_group[h].append(u)
            u_cat = u if local == 0 else jnp.concatenate(u_group[h], 0)
            o = ws_qs[chunk:] + _dot_any(p_ref[rows, h * slot:h * slot + (local + 1) * chunk], u_cat, precise)
            s_scr[h] = ge_ref[c, :, cols] * s_scr[h] + _dot_any(ke_ref[rows, cols], u, precise, _dot_tn, 1)
            gate = gate_ref[rows, cols]
            od_ref[rows, cols] = (_rms(o, onorm) * _silu(gate)).astype(od_ref.dtype)

    @pl.when(n == pl.num_programs(1) - 1)
    def _():
        sout_ref[0] = s_scr[...]


def _dn_scan(prep, p, s0, onorm_row, chunk, group, n_chunks, batch):
    wv, wk, qdec, kend, pm, gend = prep
    m = wv.shape[0]
    assert n_chunks % group == 0
    rows = chunk * n_chunks
    steps = m // batch // rows
    blk = lambda b, n: (b * steps + n, 0)
    wide = pl.BlockSpec((rows, DN_WIDTH), blk)
    state = pl.BlockSpec((1, DN_HEADS, DN_KEY_DIM, DN_VAL_DIM), lambda b, n: (b, 0, 0, 0))
    slot = _dn_score_slot(chunk, group)
    assert pm.shape[1] == DN_HEADS * slot
    return pl.pallas_call(
        functools.partial(_dn_scan_kernel, chunk, group, n_chunks),
        grid=(batch, steps),
        in_specs=[
            wide, wide, wide, wide, pl.BlockSpec((rows, DN_HEADS * slot), blk),
            pl.BlockSpec((n_chunks, 1, DN_WIDTH), lambda b, n: (b * steps + n, 0, 0)),
            pl.BlockSpec((rows, DN_WIDTH), lambda b, n: (b * steps + n, P_GATE // DN_WIDTH)),
            state,
            pl.BlockSpec((1, LANES), lambda b, n: (0, 0)),
        ],
        out_specs=[wide, state],
        out_shape=[jax.ShapeDtypeStruct((m, DN_WIDTH), wk.dtype),
                   jax.ShapeDtypeStruct((batch, DN_HEADS, DN_KEY_DIM, DN_VAL_DIM), F32)],
        scratch_shapes=[pltpu.VMEM((DN_HEADS, DN_KEY_DIM, DN_VAL_DIM), F32)],
        compiler_params=_cparams("parallel", "arbitrary"),
        name="dn_scan",
    )(wv, wk, qdec, kend, pm, gend, p, s0, onorm_row)


def _out_proj_kernel(nw, attn_ref, od_ref, *refs):
    w_refs, (x_ref, g_ref, gate_ref, o_ref) = refs[:nw], refs[nw:]
    y = (_mm(attn_ref[...], tuple(r[0:ATTN_WIDTH, :] for r in w_refs))
         + _mm(od_ref[...], tuple(r[ATTN_WIDTH:, :] for r in w_refs)))
    o_ref[...] = x_ref[...] + gate_ref[...] * _rms(y, g_ref[...])


def _out_proj(attn, od, w, x, gain, gate, tm):
    m, d = x.shape
    row = lambda i: (i, 0)
    return pl.pallas_call(
        functools.partial(_out_proj_kernel, len(w)),
        grid=(m // tm,),
        in_specs=[
            pl.BlockSpec((tm, ATTN_WIDTH), row),
            pl.BlockSpec((tm, DN_WIDTH), row),
        ] + [pl.BlockSpec((ATTN_WIDTH + DN_WIDTH, d), lambda i: (0, 0))] * len(w) + [
            pl.BlockSpec((tm, d), row),
            pl.BlockSpec((1, d), lambda i: (0, 0)),
            _mod_spec(gate.shape[0], tm, d),
        ],
        out_specs=pl.BlockSpec((tm, d), row),
        out_shape=jax.ShapeDtypeStruct((m, d), F32),
        compiler_params=_cparams("parallel"),
        name="out_proj",
    )(attn, od, *w, x, gain, gate)


def _ffn_kernel(nw, x_ref, g_ref, sc_ref, sh_ref, *refs):
    wg_refs, wu_refs, wd_refs = refs[:nw], refs[nw:2 * nw], refs[2 * nw:3 * nw]
    g2_ref, gate_ref, o_ref, h_ref, acc_ref = refs[3 * nw:]
    j = pl.program_id(1)

    @pl.when(j == 0)
    def _():
        h = _rms(x_ref[...], g_ref[...]) * (1.0 + sc_ref[...]) + sh_ref[...]
        h_ref[...] = h.astype(h_ref.dtype)
        acc_ref[...] = jnp.zeros_like(acc_ref)

    h = h_ref[...]
    act = _silu(_mm(h, tuple(r[...] for r in wg_refs))) * _mm(h, tuple(r[...] for r in wu_refs))
    acc_ref[...] += _mm(act, tuple(r[...] for r in wd_refs))

    @pl.when(j == pl.num_programs(1) - 1)
    def _():
        o_ref[...] = x_ref[...] + gate_ref[...] * _rms(acc_ref[...], g2_ref[...])


def _ffn(x, gain, scale, shift, wg, wu, wd, gain2, gate, tm, tf):
    m, d = x.shape
    nw = len(wg)
    f = wg[0].shape[1]
    row = lambda i, j: (i, 0)
    vec = pl.BlockSpec((1, d), lambda i, j: (0, 0))
    return pl.pallas_call(
        functools.partial(_ffn_kernel, nw),
        grid=(m // tm, f // tf),
        in_specs=[
            pl.BlockSpec((tm, d), row), vec,
            _mod_spec(scale.shape[0], tm, d), _mod_spec(shift.shape[0], tm, d),
        ] + [pl.BlockSpec((d, tf), lambda i, j: (0, j))] * (2 * nw)
        + [pl.BlockSpec((tf, d), lambda i, j: (j, 0))] * nw
        + [vec, _mod_spec(gate.shape[0], tm, d)],
        out_specs=pl.BlockSpec((tm, d), row),
        out_shape=jax.ShapeDtypeStruct((m, d), F32),
        scratch_shapes=[pltpu.VMEM((tm, d), BF16 if nw == 1 else F32), pltpu.VMEM((tm, d), F32)],
        compiler_params=_cparams("parallel", "arbitrary"),
        name="ffn_dense",
    )(x, gain, scale, shift, *wg, *wu, *wd, gain2, gate)


def _router_kernel(x_ref, g_ref, sc_ref, sh_ref, wr_ref, h_ref, gates_ref, idx_ref, w12_ref):
    h = _rms(x_ref[...], g_ref[...]) * (1.0 + sc_ref[...]) + sh_ref[...]
    h_ref[...] = h
    logits = jnp.dot(h, wr_ref[...], preferred_element_type=F32, precision=lax.Precision.HIGHEST)
    lane = lax.broadcasted_iota(jnp.int32, logits.shape, 1).astype(F32)
    logits = jnp.where(lane < N_EXPERTS, logits, -jnp.inf)
    m1 = jnp.max(logits, -1, keepdims=True)
    i1 = jnp.min(jnp.where(logits == m1, lane, float(LANES)), -1, keepdims=True)
    rest = jnp.where(lane == i1, -jnp.inf, logits)
    m2 = jnp.max(rest, -1, keepdims=True)
    i2 = jnp.min(jnp.where(rest == m2, lane, float(LANES)), -1, keepdims=True)
    t = jnp.exp(m2 - m1)
    w1 = 1.0 / (1.0 + t)
    w2 = t / (1.0 + t)
    gates_ref[...] = jnp.where(lane == i1, w1, 0.0) + jnp.where(lane == i2, w2, 0.0)
    idx_ref[...] = jnp.where(lane == 0.0, i1, jnp.where(lane == 1.0, i2, 0.0)).astype(jnp.int32)
    w12_ref[...] = jnp.where(lane == 0.0, w1, jnp.where(lane == 1.0, w2, 0.0))


def _router(x, gain, scale, shift, w_router_pad, tm):
    m, d = x.shape
    row = lambda i: (i, 0)
    vec = pl.BlockSpec((1, d), lambda i: (0, 0))
    small = pl.BlockSpec((tm, LANES), row)
    return pl.pallas_call(
        _router_kernel,
        grid=(m // tm,),
        in_specs=[pl.BlockSpec((tm, d), row), vec,
                  _mod_spec(scale.shape[0], tm, d), _mod_spec(shift.shape[0], tm, d),
                  pl.BlockSpec((d, LANES), lambda i: (0, 0))],
        out_specs=[pl.BlockSpec((tm, d), row), small, small, small],
        out_shape=[jax.ShapeDtypeStruct((m, d), F32), jax.ShapeDtypeStruct((m, LANES), F32),
                   jax.ShapeDtypeStruct((m, LANES), jnp.int32), jax.ShapeDtypeStruct((m, LANES), F32)],
        compiler_params=_cparams("parallel"),
        name="moe_router",
    )(x, gain, scale, shift, w_router_pad)


def _moe_gemm_kernel(nj, te_ref, tot_ref, rt_ref, h_hbm, wg_ref, wu_ref, wd_ref, ys_ref, xs_ref, xb_ref, acc_ref,
                     sems):
    r = pl.program_id(0)
    j = pl.program_id(1)
    tm = xb_ref.shape[0]
    total = tot_ref[0]
    active = r < total
    slot = r % 2
    share = -(-tm // nj)

    def row_copy(tile, t, s):
        return pltpu.make_async_copy(h_hbm.at[pl.ds(rt_ref[tile * tm + t], 1), :],
                                     xs_ref.at[s, pl.ds(t, 1), :], sems.at[s])

    def start_rows(tile, s, lo, hi):
        def body(t, carry):
            row_copy(tile, t, s).start()
            return carry

        lax.fori_loop(lo, hi, body, 0)

    @pl.when(jnp.logical_and(r == 0, j == 0))
    def _():
        start_rows(0, 0, 0, tm)

    @pl.when(jnp.logical_and(active, j == 0))
    def _():
        pltpu.make_async_copy(h_hbm.at[pl.ds(0, tm), :], xs_ref.at[slot], sems.at[slot]).wait()
        xb_ref[...] = xs_ref[slot].astype(BF16)
        acc_ref[...] = jnp.zeros_like(acc_ref)

    @pl.when(active)
    def _():
        more = r + 1 < total
        for u in range(share):
            t = j * share + u

            @pl.when(jnp.logical_and(more, t < tm))
            def _():
                row_copy(jnp.minimum(r + 1, pl.num_programs(0) - 1), jnp.minimum(t, tm - 1), 1 - slot).start()

        xb = xb_ref[...]
        act = (_silu(_dot(xb, wg_ref[0])) * _dot(xb, wu_ref[0])).astype(BF16)
        acc_ref[...] += _dot(act, wd_ref[0])

    @pl.when(j == pl.num_programs(1) - 1)
    def _():
        ys_ref[...] = jnp.where(active, acc_ref[...], 0.0)


def _moe_gemm(tile_expert, total_tiles, row_token, h, wg, wu, wd, tm, tf):
    n_tiles = tile_expert.shape[0]
    d = h.shape[1]
    f = wg.shape[2]
    nj = f // tf

    def w_col(r, j, te, tot, rt):
        return (te[r], 0, jnp.where(r < tot[0], j, nj - 1))

    def w_row(r, j, te, tot, rt):
        return (te[r], jnp.where(r < tot[0], j, nj - 1), 0)

    grid_spec = pltpu.PrefetchScalarGridSpec(
        num_scalar_prefetch=3,
        grid=(n_tiles, nj),
        in_specs=[
            pl.BlockSpec(memory_space=pl.ANY),
            pl.BlockSpec((1, d, tf), w_col),
            pl.BlockSpec((1, d, tf), w_col),
            pl.BlockSpec((1, tf, d), w_row),
        ],
        out_specs=pl.BlockSpec((tm, d), lambda r, j, te, tot, rt: (r, 0)),
        scratch_shapes=[pltpu.VMEM((2, tm, d), F32), pltpu.VMEM((tm, d), BF16), pltpu.VMEM((tm, d), F32),
                        pltpu.SemaphoreType.DMA((2,))],
    )
    return pl.pallas_call(
        functools.partial(_moe_gemm_kernel, nj),
        grid_spec=grid_spec,
        out_shape=jax.ShapeDtypeStruct((n_tiles * tm, d), F32),
        compiler_params=_cparams("arbitrary", "arbitrary", row_dma=True),
        name="moe_gemm",
    )(tile_expert, total_tiles, row_token, h, wg, wu, wd)


def _moe_combine_kernel(dest_ref, x_ref, w12_ref, g_ref, gate_ref, ys_hbm, o_ref, buf_ref, sems):
    i = pl.program_id(0)
    tb = x_ref.shape[0]
    slot = i % 2

    def row_copy(blk, t, k, s):
        src = dest_ref[2 * (blk * tb + t) + k]
        return pltpu.make_async_copy(ys_hbm.at[pl.ds(src, 1), :], buf_ref.at[s, k, pl.ds(t, 1), :], sems.at[s])

    def start_block(blk, s):
        def body(t, carry):
            row_copy(blk, t, 0, s).start()
            row_copy(blk, t, 1, s).start()
            return carry

        lax.fori_loop(0, tb, body, 0, unroll=8)

    @pl.when(i == 0)
    def _():
        start_block(0, 0)

    @pl.when(i + 1 < pl.num_programs(0))
    def _():
        start_block(i + 1, 1 - slot)

    for k in range(2):
        pltpu.make_async_copy(ys_hbm.at[pl.ds(0, tb), :], buf_ref.at[slot, k], sems.at[slot]).wait()
    w12 = w12_ref[...]
    y = w12[:, 0:1] * buf_ref[slot, 0] + w12[:, 1:2] * buf_ref[slot, 1]
    o_ref[...] = x_ref[...] + gate_ref[...] * _rms(y, g_ref[...])


def _moe_combine(dest, x, w12, gain, gate, ys, tb):
    m, d = x.shape
    row = lambda i, dst: (i, 0)
    grid_spec = pltpu.PrefetchScalarGridSpec(
        num_scalar_prefetch=1,
        grid=(m // tb,),
        in_specs=[
            pl.BlockSpec((tb, d), row),
            pl.BlockSpec((tb, LANES), row),
            pl.BlockSpec((1, d), lambda i, dst: (0, 0)),
            pl.BlockSpec((1, d), lambda i, dst: (0, 0)),
            pl.BlockSpec(memory_space=pl.ANY),
        ],
        out_specs=pl.BlockSpec((tb, d), row),
        scratch_shapes=[pltpu.VMEM((2, 2, tb, d), F32), pltpu.SemaphoreType.DMA((2,))],
    )
    return pl.pallas_call(
        _moe_combine_kernel,
        grid_spec=grid_spec,
        out_shape=jax.ShapeDtypeStruct((m, d), F32),
        compiler_params=_cparams("arbitrary", row_dma=True),
        name="moe_combine",
    )(dest, x, w12, gain, gate, ys)


def _route_tables(idx2, tm, n_tiles):
    m = idx2.shape[0]
    n_assign = 2 * m
    assert n_tiles * tm == n_assign + N_EXPERTS * tm
    experts = jnp.arange(N_EXPERTS, dtype=jnp.int32)
    e_flat = idx2.reshape(n_assign)
    onehot = (e_flat[:, None] == experts[None, :]).astype(jnp.int32)
    csum = jnp.cumsum(onehot, 0)
    counts = csum[-1]
    padded = ((counts + tm - 1) // tm) * tm
    pend = jnp.cumsum(padded)
    pstart = pend - padded
    dest = jnp.sum(onehot * (pstart[None, :] + csum - 1), -1).astype(jnp.int32)
    total_tiles = (pend[-1] // tm).astype(jnp.int32).reshape(1)
    tile_expert = jnp.minimum(
        jnp.searchsorted(pend // tm, jnp.arange(n_tiles, dtype=jnp.int32), side="right"), N_EXPERTS - 1
    ).astype(jnp.int32)
    filler_key = jnp.where(jnp.arange(tm, dtype=jnp.int32)[None, :] < (padded - counts)[:, None],
                           experts[:, None], N_EXPERTS).reshape(-1)
    keys = jnp.concatenate([e_flat, filler_key])
    tokens = jnp.concatenate([jnp.arange(n_assign, dtype=jnp.int32) // 2,
                              jnp.zeros((N_EXPERTS * tm,), jnp.int32)])
    _, row_token = lax.sort((keys, tokens), num_keys=1, is_stable=True)
    return tile_expert, total_tiles, row_token, dest


def _moe_dense_kernel(h_ref, gates_ref, wg_ref, wu_ref, wd_ref, x_ref, g_ref, gate_ref, o_ref, acc_ref, tot_ref):
    e = pl.program_id(0)
    j = pl.program_id(1)
    nj = pl.num_programs(1)

    @pl.when(jnp.logical_and(e == 0, j == 0))
    def _():
        tot_ref[...] = jnp.zeros_like(tot_ref)

    @pl.when(j == 0)
    def _():
        acc_ref[...] = jnp.zeros_like(acc_ref)

    h = h_ref[...].astype(BF16)
    act = (_silu(_dot(h, wg_ref[0])) * _dot(h, wu_ref[0])).astype(BF16)
    acc_ref[...] += _dot(act, wd_ref[0])

    @pl.when(j == nj - 1)
    def _():
        gates = gates_ref[...]
        lane = lax.broadcasted_iota(jnp.int32, gates.shape, 1)
        ge = jnp.sum(jnp.where(lane == e, gates, 0.0), -1, keepdims=True)
        tot_ref[...] += ge * acc_ref[...]

    @pl.when(jnp.logical_and(e == pl.num_programs(0) - 1, j == nj - 1))
    def _():
        o_ref[...] = x_ref[...] + gate_ref[...] * _rms(tot_ref[...], g_ref[...])


def _moe_dense(h, gates, wg, wu, wd, x, gain, gate, tf):
    m, d = x.shape
    f = wg.shape[2]
    full = pl.BlockSpec((m, d), lambda e, j: (0, 0))
    return pl.pallas_call(
        _moe_dense_kernel,
        grid=(N_EXPERTS, f // tf),
        in_specs=[
            full,
            pl.BlockSpec((m, LANES), lambda e, j: (0, 0)),
            pl.BlockSpec((1, d, tf), lambda e, j: (e, 0, j)),
            pl.BlockSpec((1, d, tf), lambda e, j: (e, 0, j)),
            pl.BlockSpec((1, tf, d), lambda e, j: (e, j, 0)),
            full,
            pl.BlockSpec((1, d), lambda e, j: (0, 0)),
            full,
        ],
        out_specs=full,
        out_shape=jax.ShapeDtypeStruct((m, d), F32),
        scratch_shapes=[pltpu.VMEM((m, d), F32), pltpu.VMEM((m, d), F32)],
        compiler_params=_cparams("arbitrary", "arbitrary"),
        name="moe_dense",
    )(h, gates, wg, wu, wd, x, gain, gate)


def _rope_tables(pos):
    half = ROPE_DIM // 2
    inv_freq = jnp.power(ROPE_THETA, -2.0 * jnp.arange(half, dtype=F32) / ROPE_DIM)
    ang = pos.astype(F32)[:, None] * inv_freq[None, :]
    cos, sin = jnp.cos(ang), jnp.sin(ang)
    t = pos.shape[0]
    rest = ATTN_HEAD_DIM - ROPE_DIM
    cos_h = jnp.concatenate([cos, cos, jnp.ones((t, rest), F32)], 1)
    sa_h = jnp.concatenate([-sin, jnp.zeros((t, half + rest), F32)], 1)
    sb_h = jnp.concatenate([jnp.zeros((t, half), F32), sin, jnp.zeros((t, rest), F32)], 1)
    rep = LANES // ATTN_HEAD_DIM
    return tuple(jnp.tile(a, (1, rep)) for a in (cos_h, sa_h, sb_h))


def _permute_w_in(w):
    o1 = ATTN_WIDTH
    o2 = o1 + KV_WIDTH
    o3 = o2 + KV_WIDTH
    o4 = o3 + DN_CONV_CH
    o5 = o4 + DN_WIDTH
    parts = [w[:, :o1], w[:, o4:o5], w[:, o3:o4], w[:, o1:o2], w[:, o2:o3], w[:, o5:]]
    used = sum(a.shape[1] for a in parts)
    parts.append(jnp.zeros((w.shape[0], P_WIDTH - used), w.dtype))
    return jnp.concatenate(parts, 1)


def _sample_mask_bias(s, lc):
    q_pos = PAST_LEN + np.arange(s)
    k_pos = np.concatenate([PAST_LEN - lc + np.arange(lc), q_pos])
    q_chunk = q_pos[:, None] // CHUNK
    k_chunk = k_pos[None, :] // CHUNK
    mask = (k_pos[None, :] >= 0) & (k_chunk <= q_chunk) & (k_pos[None, :] >= q_chunk * CHUNK - WINDOW)
    bias = np.where(mask, 0.0, -np.inf).astype(np.float32)
    return jnp.asarray(np.tile(bias, (4, 1)))


def _conv_halo(p, init, tb, seq):
    m = p.shape[0]
    batch = m // seq
    nb = seq // tb
    tails = p.reshape(batch, nb, tb, P_WIDTH)[:, :nb - 1, tb - (CONV_WIDTH - 1):, P_CONV:P_CONV + DN_CONV_CH]
    prev = jnp.concatenate([init[:, None], tails], 1)
    prev = prev.reshape(batch * nb, CONV_WIDTH - 1, DN_CONV_CH)
    return jnp.pad(prev, ((0, 0), (8 - (CONV_WIDTH - 1), 0), (0, 0)))


def _trunk(x, mods, layer_w, rope, past, cfg):
    m = x.shape[0]
    batch, seq = cfg["batch"], cfg["seq"]
    precise = cfg["precise"]
    nw = 2 if precise else 1
    ks, vs, ss, bufs = [], [], [], []
    cos, sa, sb = rope
    for l in range(DEPTH):
        w = layer_w[l]
        sh_a, sc_a, g_a, sh_f, sc_f, g_f = mods[l]
        p = _norm_proj(x, w["gain"][0], sc_a, sh_a, w["w_in"][:nw], cfg["tm_proj"], cfg["tn_proj"])
        if past is None:
            attn, k_new = _attn_prompt(p, w["sinks"], cos, sa, sb, cfg["tb_attn"])
            s0 = jnp.zeros((batch, DN_HEADS, DN_KEY_DIM, DN_VAL_DIM), F32)
            conv_init = jnp.zeros((batch, CONV_WIDTH - 1, DN_CONV_CH), F32)
        else:
            ck = past[0][l].reshape(batch, -1, KV_WIDTH)
            cv = past[1][l].reshape(batch, -1, KV_WIDTH)
            attn, k_new = _attn_sample(p, ck, cv, w["sinks"], cos, sa, sb, cfg["bias"], batch, seq)
            s0 = past[2][l]
            conv_init = past[3][l]
        halo = _conv_halo(p, conv_init, cfg["tb_dn"], seq)
        prep = _dn_prep(p, halo, w["w_conv"], w["alog"], w["dtb"], cfg["chunk"], cfg["group"], cfg["tb_dn"],
                        precise)
        od, s_new = _dn_scan(prep, p, s0, w["onorm"], cfg["chunk"], cfg["group"], cfg["scan_chunks"], batch)
        x = _out_proj(attn, od, w["w_out"][:nw], x, w["gain"][1], g_a, cfg["tm_out"])
        if l % 2 == 0:
            x = _ffn(x, w["gain"][2], sc_f, sh_f, w["ffn_gate"][:nw], w["ffn_up"][:nw], w["ffn_down"][:nw],
                     w["gain"][3], g_f, cfg["tm_ffn"], cfg["tf_ffn"])
        else:
            h, gates, idx, w12 = _router(x, w["gain"][2], sc_f, sh_f, w["router"], cfg["tm_router"])
            if cfg["routed"]:
                tm = cfg["tm_moe"]
                n_tiles = 2 * m // tm + N_EXPERTS
                tile_expert, total_tiles, row_token, dest = _route_tables(idx[:, :2], tm, n_tiles)
                ys = _moe_gemm(tile_expert, total_tiles, row_token, h, w["moe_gate"], w["moe_up"], w["moe_down"],
                               tm, cfg["tf_moe"])
                x = _moe_combine(dest, x, w12, w["gain"][3], g_f, ys, cfg["tb_combine"])
            else:
                x = _moe_dense(h, gates, w["moe_gate"], w["moe_up"], w["moe_down"], x, w["gain"][3], g_f,
                               cfg["tf_moe"])
        pb = p.reshape(batch, seq, P_WIDTH)
        keep = min(WINDOW, seq) if past is None else seq
        ks.append(k_new.reshape(batch, seq, ATTN_KV_HEADS, ATTN_HEAD_DIM)[:, seq - keep:])
        vs.append(pb[:, seq - keep:, P_V:P_V + KV_WIDTH].reshape(batch, keep, ATTN_KV_HEADS, ATTN_HEAD_DIM))
        ss.append(s_new)
        assert seq >= CONV_WIDTH - 1
        bufs.append(pb[:, seq - (CONV_WIDTH - 1):, P_CONV:P_CONV + DN_CONV_CH])
    return x, jnp.stack(ks), jnp.stack(vs), jnp.stack(ss), jnp.stack(bufs)


def kernel(x_prompt, x_sample, cache_attn_k, cache_attn_v, state_delta, state_conv, c_prompt, c_sample, w_in, w_conv, attn_sinks, dn_a_log, dn_dt_bias, dn_norm, w_out, w_mod, b_mod, norm_gains, ffn_gate, ffn_up, ffn_down, moe_router, moe_gate, moe_up, moe_down):
    bp, tp, d = x_prompt.shape
    bs, ts, _ = x_sample.shape
    assert bp == 1 and d == D_MODEL

    c_all = jnp.concatenate([c_prompt, c_sample, jnp.zeros((16 - bp - bs, d), F32)], 0)
    mod = _modulation(c_all, w_mod, b_mod)
    mods_p, mods_s = [], []
    for l in range(DEPTH):
        six = jnp.split(mod[l], 6, -1)
        mods_p.append([a[0:bp] for a in six])
        mods_s.append([jnp.repeat(a[bp:bp + bs], ts, axis=0) for a in six])

    def pad_lanes(v, at):
        return jnp.zeros((1, LANES), F32).at[0, at:at + v.shape[0]].set(v)

    layer_w = []
    for l in range(DEPTH):
        w = {
            "gain": [norm_gains[l, i].reshape(1, d) for i in range(4)],
            "w_in": _split_weight(_permute_w_in(w_in[l])),
            "sinks": attn_sinks[l],
            "w_conv": jnp.pad(w_conv[l], ((0, 8 - CONV_WIDTH), (0, 0))),
            "alog": pad_lanes(dn_a_log[l], DN_HEADS),
            "dtb": pad_lanes(dn_dt_bias[l], DN_HEADS),
            "onorm": dn_norm[l].reshape(1, DN_VAL_DIM),
            "w_out": _split_weight(w_out[l]),
        }
        if l % 2 == 0:
            w["ffn_gate"] = _split_weight(ffn_gate[l // 2])
            w["ffn_up"] = _split_weight(ffn_up[l // 2])
            w["ffn_down"] = _split_weight(ffn_down[l // 2])
        else:
            w["router"] = jnp.pad(moe_router[l // 2], ((0, 0), (0, LANES - N_EXPERTS)))
            w["moe_gate"] = moe_gate[l // 2].astype(BF16)
            w["moe_up"] = moe_up[l // 2].astype(BF16)
            w["moe_down"] = moe_down[l // 2].astype(BF16)
        layer_w.append(w)

    cfg_p = dict(batch=bp, seq=tp, precise=False, chunk=CHUNK, group=2, scan_chunks=4, tm_proj=1024, tn_proj=512, tb_attn=512,
                 tb_dn=512, tm_out=512, tm_ffn=512, tf_ffn=512, tm_router=512, routed=True, tm_moe=512,
                 tf_moe=256, tb_combine=256)
    rope_p = _rope_tables(jnp.arange(tp, dtype=jnp.int32))
    y_p, k_p, v_p, s_p, conv_p = _trunk(x_prompt.reshape(bp * tp, d), mods_p, layer_w, rope_p, None, cfg_p)

    ms = bs * ts
    cfg_s = dict(batch=bs, seq=ts, precise=True, chunk=ts, group=1, scan_chunks=1, tm_proj=ms, tn_proj=512, tb_dn=ts, tm_out=ms,
                 tm_ffn=ms, tf_ffn=512, tm_router=ms, routed=False, tf_moe=1408,
                 bias=_sample_mask_bias(ts, cache_attn_k.shape[2]))
    rope_s = _rope_tables(PAST_LEN + jnp.arange(ts, dtype=jnp.int32))
    past = (cache_attn_k, cache_attn_v, state_delta, state_conv)
    y_s, k_s, v_s, s_s, conv_s = _trunk(x_sample.reshape(ms, d), mods_s, layer_w, rope_s, past, cfg_s)

    return (y_p.reshape(bp, tp, d), y_s.reshape(bs, ts, d), k_p, v_p, s_p, conv_p, k_s, v_s, s_s, conv_s)
```

```python
import functools
import math

import numpy as np
import jax
import jax.numpy as jnp
from jax import lax
from jax.experimental import pallas as pl
from jax.experimental.pallas import tpu as pltpu

D_MODEL = 2048
DEPTH = 2
PAST_LEN = 1024
CHUNK = 64
ATTN_HEADS = 16
ATTN_KV_HEADS = 2
ATTN_HEAD_DIM = 64
ATTN_WIDTH = 1024
KV_WIDTH = 128
WINDOW = 128
ROPE_THETA = 500000.0
ROPE_DIM = 16
DN_HEADS = 8
DN_KEY_DIM = 128
DN_VAL_DIM = 128
DN_WIDTH = 1024
CONV_WIDTH = 4
DN_CONV_CH = 3072
D_FF = 5632
N_EXPERTS = 8
D_FF_EXPERT = 2816
EPS = 1e-6

F32 = jnp.float32
BF16 = jnp.bfloat16
LANES = 128
V7X_MXU_DEPTH = 256
V7X_VMEM_LIMIT = 56 * 1024 * 1024

P_Q = 0
P_GATE = 1024
P_CONV = 2048
P_K = 5120
P_V = 5248
P_BA = 5376
P_WIDTH = 5632


def _cparams(*sem, row_dma=False):
    return pltpu.CompilerParams(dimension_semantics=sem, vmem_limit_bytes=V7X_VMEM_LIMIT,
                                disable_bounds_checks=row_dma)


def _silu(x):
    return x * jax.nn.sigmoid(x)


def _rms(x, gain):
    return x * lax.rsqrt(jnp.mean(x * x, -1, keepdims=True) + EPS) * gain


def _dot(a, b):
    return jnp.dot(a, b, preferred_element_type=F32)


def _dot_nt(a, b):
    return lax.dot_general(a, b, (((1,), (1,)), ((), ())), preferred_element_type=F32)


def _dot_tn(a, b):
    return lax.dot_general(a, b, (((0,), (0,)), ((), ())), preferred_element_type=F32)


def _split_bf16(a):
    hi = a.astype(BF16)
    lo = (a - hi.astype(F32)).astype(BF16)
    return hi, lo


def _dot_x3(a, b, dot=_dot, out_axis=0):
    a_hi, a_lo = _split_bf16(a)
    b_hi, b_lo = _split_bf16(b)
    n = a.shape[out_axis]
    top = dot(jnp.concatenate([a_hi, a_lo], out_axis), b_hi)
    return top[:n] + top[n:] + dot(a_hi, b_lo)


def _dot_any(a, b, precise, dot=_dot, out_axis=0):
    if precise:
        return _dot_x3(a, b, dot, out_axis)
    return dot(a.astype(BF16), b.astype(BF16))


def _mm(a, w):
    if len(w) == 1:
        return _dot(a.astype(BF16), w[0])
    a_hi, a_lo = _split_bf16(a)
    n = a.shape[0]
    top = _dot(jnp.concatenate([a_hi, a_lo], 0), w[0])
    return top[:n] + top[n:] + _dot(a_hi, w[1])


def _split_weight(w):
    hi, lo = _split_bf16(w)
    return (hi, lo)


def _mod_spec(rows, tm, d):
    if rows == 1:
        return pl.BlockSpec((1, d), lambda i, *_: (0, 0))
    return pl.BlockSpec((tm, d), lambda i, *_: (i, 0))


def _mod_kernel(c_ref, w_ref, b_ref, o_ref):
    o_ref[0] = _dot_x3(_silu(c_ref[...]), w_ref[0]) + b_ref[0]


def _modulation(c_all, w_mod, b_mod):
    rows = c_all.shape[0]
    n = w_mod.shape[2]
    tn = 1024
    return pl.pallas_call(
        _mod_kernel,
        grid=(DEPTH, n // tn),
        in_specs=[
            pl.BlockSpec((rows, D_MODEL), lambda l, j: (0, 0)),
            pl.BlockSpec((1, D_MODEL, tn), lambda l, j: (l, 0, j)),
            pl.BlockSpec((1, 1, tn), lambda l, j: (l, 0, j)),
        ],
        out_specs=pl.BlockSpec((1, rows, tn), lambda l, j: (l, 0, j)),
        out_shape=jax.ShapeDtypeStruct((DEPTH, rows, n), F32),
        compiler_params=_cparams("parallel", "parallel"),
        name="modulation",
    )(c_all, w_mod, b_mod.reshape(DEPTH, 1, n))


def _norm_proj_kernel(nw, x_ref, g_ref, sc_ref, sh_ref, *refs):
    w_refs, (o_ref, h_ref) = refs[:nw], refs[nw:]

    @pl.when(pl.program_id(1) == 0)
    def _():
        h = _rms(x_ref[...], g_ref[...]) * (1.0 + sc_ref[...]) + sh_ref[...]
        h_ref[...] = h.astype(h_ref.dtype)

    o_ref[...] = _mm(h_ref[...], tuple(r[...] for r in w_refs))


def _norm_proj(x, gain, scale, shift, w, tm, tn):
    m, d = x.shape
    n = w[0].shape[1]
    return pl.pallas_call(
        functools.partial(_norm_proj_kernel, len(w)),
        grid=(m // tm, n // tn),
        in_specs=[
            pl.BlockSpec((tm, d), lambda i, j: (i, 0)),
            pl.BlockSpec((1, d), lambda i, j: (0, 0)),
            _mod_spec(scale.shape[0], tm, d),
            _mod_spec(shift.shape[0], tm, d),
        ] + [pl.BlockSpec((d, tn), lambda i, j: (0, j))] * len(w),
        out_specs=pl.BlockSpec((tm, tn), lambda i, j: (i, j)),
        out_shape=jax.ShapeDtypeStruct((m, n), F32),
        scratch_shapes=[pltpu.VMEM((tm, d), BF16 if len(w) == 1 else F32)],
        compiler_params=_cparams("parallel", "arbitrary"),
        name="norm_proj",
    )(x, gain, scale, shift, *w)


def _rope(x, cos, sa, sb):
    return x * cos + pltpu.roll(x, LANES - 8, 1) * sa + pltpu.roll(x, 8, 1) * sb


def _kv_variants(k, v):
    lo = lax.broadcasted_iota(jnp.int32, k.shape, 1) < ATTN_HEAD_DIM
    kr = pltpu.roll(k, ATTN_HEAD_DIM, 1)
    vr = pltpu.roll(v, ATTN_HEAD_DIM, 1)
    zero = jnp.zeros_like(k)
    k_lo = (jnp.where(lo, k, zero), jnp.where(lo, kr, zero))
    k_hi = (jnp.where(lo, zero, kr), jnp.where(lo, zero, k))
    v_lo = (jnp.where(lo, v, zero), jnp.where(lo, vr, zero))
    v_hi = (jnp.where(lo, zero, vr), jnp.where(lo, zero, v))
    return k_lo, k_hi, v_lo, v_hi


def _sink_softmax(s, sink):
    m = jnp.maximum(jnp.max(s, -1, keepdims=True), sink)
    p = jnp.exp(s - m)
    den = jnp.sum(p, -1, keepdims=True) + jnp.exp(sink - m)
    return p / den


def _attn_core(qbs, k_los, k_his, v_los, v_his, bias, sinks, precise=False):
    scores = [(_dot_any(qb, k_lo, precise, _dot_nt) + bias, _dot_any(qb, k_hi, precise, _dot_nt) + bias)
              for qb, k_lo, k_hi in zip(qbs, k_los, k_his)]
    probs = [(_sink_softmax(s_even, sink[0]), _sink_softmax(s_odd, sink[1]))
             for (s_even, s_odd), sink in zip(scores, sinks)]
    return [_dot_any(p_even, v_lo, precise) + _dot_any(p_odd, v_hi, precise)
            for (p_even, p_odd), v_lo, v_hi in zip(probs, v_los, v_his)]


def _sink_columns(sink_ref, rows_per_pair):
    n = 4 * rows_per_pair
    pair = lax.broadcasted_iota(jnp.int32, (n, 1), 0) // rows_per_pair
    out = []
    for j in range(ATTN_KV_HEADS):
        cols = []
        for par in range(2):
            col = jnp.zeros((n, 1), F32)
            for a in range(4):
                col = jnp.where(pair == a, sink_ref[8 * j + 2 * a + par], col)
            cols.append(col)
        out.append(cols)
    return out


def _attn_prompt_kernel(sink_ref, q_ref, kv_ref, cos_ref, sa_ref, sb_ref, o_ref, knew_ref,
                        qs_ref, klo_ref, khi_ref, vlo_ref, vhi_ref):
    i = pl.program_id(0)
    tb = q_ref.shape[0]
    bufs = (klo_ref, khi_ref, vlo_ref, vhi_ref)

    @pl.when(i == 0)
    def _():
        for r in bufs:
            r[:, 0:WINDOW, :] = jnp.zeros((ATTN_KV_HEADS, WINDOW, LANES), BF16)

    @pl.when(i > 0)
    def _():
        for r in bufs:
            r[:, 0:WINDOW, :] = r[:, tb:tb + WINDOW, :]

    cos, sa, sb = cos_ref[...], sa_ref[...], sb_ref[...]
    k = _rope(kv_ref[:, 0:LANES], cos, sa, sb)
    knew_ref[...] = k
    variants = _kv_variants(k, kv_ref[:, LANES:2 * LANES])
    for r, var in zip(bufs, variants):
        for j in range(ATTN_KV_HEADS):
            r[j, WINDOW:, :] = var[j].astype(BF16)
    scale = ATTN_HEAD_DIM ** -0.5
    for a in range(ATTN_WIDTH // LANES):
        cols = slice(a * LANES, (a + 1) * LANES)
        qs_ref[:, cols] = (_rope(q_ref[:, cols], cos, sa, sb) * scale).astype(BF16)

    sinks = _sink_columns(sink_ref, CHUNK)
    nk = WINDOW + CHUNK

    def chunk_body(c, carry):
        r0 = pl.multiple_of(c * CHUNK, CHUNK)
        kpos = i * tb - WINDOW + r0 + lax.broadcasted_iota(jnp.int32, (1, nk), 1)
        bias = jnp.where(kpos >= 0, 0.0, -jnp.inf).astype(F32)
        heads = range(ATTN_KV_HEADS)
        qbs = [jnp.concatenate(
            [qs_ref[pl.ds(r0, CHUNK), (4 * j + a) * LANES:(4 * j + a + 1) * LANES] for a in range(4)], 0)
            for j in heads]
        keys = pl.ds(r0, nk)
        outs = _attn_core(qbs, [klo_ref[j, keys, :] for j in heads], [khi_ref[j, keys, :] for j in heads],
                          [vlo_ref[j, keys, :] for j in heads], [vhi_ref[j, keys, :] for j in heads], bias, sinks)
        for j, o in zip(heads, outs):
            for a in range(4):
                o_ref[pl.ds(r0, CHUNK), (4 * j + a) * LANES:(4 * j + a + 1) * LANES] = (
                    o[a * CHUNK:(a + 1) * CHUNK].astype(BF16))
        return carry

    lax.fori_loop(0, tb // CHUNK, chunk_body, 0, unroll=2)


def _attn_prompt(p, sinks, cos, sa, sb, tb):
    t = p.shape[0]
    kv_blk = P_K // (2 * LANES)
    row = lambda i: (i, 0)
    return pl.pallas_call(
        _attn_prompt_kernel,
        grid=(t // tb,),
        in_specs=[
            pl.BlockSpec(memory_space=pltpu.SMEM),
            pl.BlockSpec((tb, ATTN_WIDTH), row),
            pl.BlockSpec((tb, 2 * LANES), lambda i: (i, kv_blk)),
            pl.BlockSpec((tb, LANES), row),
            pl.BlockSpec((tb, LANES), row),
            pl.BlockSpec((tb, LANES), row),
        ],
        out_specs=[pl.BlockSpec((tb, ATTN_WIDTH), row), pl.BlockSpec((tb, LANES), row)],
        out_shape=[jax.ShapeDtypeStruct((t, ATTN_WIDTH), BF16), jax.ShapeDtypeStruct((t, LANES), F32)],
        scratch_shapes=[pltpu.VMEM((tb, ATTN_WIDTH), BF16)]
        + [pltpu.VMEM((ATTN_KV_HEADS, tb + WINDOW, LANES), BF16) for _ in range(4)],
        compiler_params=_cparams("arbitrary"),
        name="attn_prompt",
    )(sinks, p, p, cos, sa, sb)


def _attn_sample_kernel(sink_ref, q_ref, kv_ref, ck_ref, cv_ref, cos_ref, sa_ref, sb_ref, bias_ref,
                        o_ref, knew_ref):
    s = q_ref.shape[0]
    cos, sa, sb = cos_ref[...], sa_ref[...], sb_ref[...]
    k = _rope(kv_ref[:, 0:LANES], cos, sa, sb)
    knew_ref[...] = k
    kk = jnp.concatenate([ck_ref[0], k], 0)
    vv = jnp.concatenate([cv_ref[0], kv_ref[:, LANES:2 * LANES]], 0)
    k_lo, k_hi, v_lo, v_hi = _kv_variants(kk, vv)
    sinks = _sink_columns(sink_ref, s)
    scale = ATTN_HEAD_DIM ** -0.5
    bias = bias_ref[...]
    qbs = [jnp.concatenate(
        [_rope(q_ref[:, (4 * j + a) * LANES:(4 * j + a + 1) * LANES], cos, sa, sb) * scale for a in range(4)], 0)
        for j in range(ATTN_KV_HEADS)]
    outs = _attn_core(qbs, k_lo, k_hi, v_lo, v_hi, bias, sinks, precise=True)
    for j, o in enumerate(outs):
        for a in range(4):
            o_ref[:, (4 * j + a) * LANES:(4 * j + a + 1) * LANES] = o[a * s:(a + 1) * s]


def _attn_sample(p, cache_k, cache_v, sinks, cos, sa, sb, bias, batch, s):
    lc = cache_k.shape[1]
    kv_blk = P_K // (2 * LANES)
    row = lambda b: (b, 0)
    const = lambda b: (0, 0)
    return pl.pallas_call(
        _attn_sample_kernel,
        grid=(batch,),
        in_specs=[
            pl.BlockSpec(memory_space=pltpu.SMEM),
            pl.BlockSpec((s, ATTN_WIDTH), row),
            pl.BlockSpec((s, 2 * LANES), lambda b: (b, kv_blk)),
            pl.BlockSpec((1, lc, LANES), lambda b: (b, 0, 0)),
            pl.BlockSpec((1, lc, LANES), lambda b: (b, 0, 0)),
            pl.BlockSpec((s, LANES), const),
            pl.BlockSpec((s, LANES), const),
            pl.BlockSpec((s, LANES), const),
            pl.BlockSpec((4 * s, lc + s), const),
        ],
        out_specs=[pl.BlockSpec((s, ATTN_WIDTH), row), pl.BlockSpec((s, LANES), row)],
        out_shape=[jax.ShapeDtypeStruct((batch * s, ATTN_WIDTH), F32),
                   jax.ShapeDtypeStruct((batch * s, LANES), F32)],
        compiler_params=_cparams("parallel"),
        name="attn_sample",
    )(sinks, p, p, cache_k, cache_v, cos, sa, sb, bias)


class _BlockDiag:
    def __init__(self, chunk, group):
        self.chunk, self.group = chunk, group
        n = chunk * group
        lane = lax.broadcasted_iota(jnp.int32, (chunk, n), 1)
        self.lane_block = lane // chunk
        self.eye = (lax.broadcasted_iota(jnp.int32, (chunk, n), 0) == lane % chunk).astype(F32)

    def wide(self, tall):
        c = self.chunk
        out = tall[0:c]
        for b in range(1, self.group):
            out = out + tall[b * c:(b + 1) * c]
        return out

    def expand(self, wide):
        if self.group == 1:
            return wide
        zero = jnp.zeros_like(wide)
        return jnp.concatenate([jnp.where(self.lane_block == b, wide, zero) for b in range(self.group)], 0)

    def rmul(self, lhs, wide):
        l_hi, l_lo = _split_bf16(lhs)
        w_hi, w_lo = _split_bf16(wide)
        m = lhs.shape[0]
        top = _dot(jnp.concatenate([l_hi, l_lo], 0), self.expand(w_hi))
        return top[:m] + top[m:] + _dot(l_hi, self.expand(w_lo))

    def lmul(self, wide, rhs):
        w_hi, w_lo = _split_bf16(wide)
        r_hi, r_lo = _split_bf16(rhs)
        n = self.chunk * self.group
        e_hi = self.expand(w_hi)
        top = _dot(jnp.concatenate([e_hi, self.expand(w_lo)], 0), r_hi)
        return top[:n] + top[n:] + _dot(e_hi, r_lo)

    def unit_lower_inverse(self, a_talls):
        c = self.chunk
        negs = [-self.wide(a) for a in a_talls]
        xs = [self.eye + neg for neg in negs]
        powers = [self.rmul(neg, neg) for neg in negs]
        iters = int(math.log2(c)) - 1
        for it in range(iters):
            last = it == iters - 1
            rs = [self.rmul(x if last else jnp.concatenate([x, p], 0), p) for x, p in zip(xs, powers)]
            xs = [x + r[:c] for x, r in zip(xs, rs)]
            if not last:
                powers = [r[c:] for r in rs]
        return xs


def _softplus(x):
    return jnp.maximum(x, 0.0) + jnp.log1p(jnp.exp(-jnp.abs(x)))


def _dn_prep_kernel(chunk, group, precise, qd_ref, kd_ref, vd_ref, ba_ref, hq_ref, hk_ref, hv_ref,
                    wq_ref, wk_ref, wv_ref, alog_ref, dtb_ref,
                    wv_out, wk_out, qdec_out, kend_out, p_out, gend_out):
    h = pl.program_id(1)
    tb = qd_ref.shape[0]
    n = chunk * group

    def conv_silu(x_ref, halo_ref, w_ref):
        xp = jnp.concatenate([halo_ref[0], x_ref[...]], 0)
        w = w_ref[...]
        y = xp[5:5 + tb] * w[0:1]
        for tap in range(1, CONV_WIDTH):
            y = y + xp[5 + tap:5 + tap + tb] * w[tap:tap + 1]
        return _silu(y)

    q = conv_silu(qd_ref, hq_ref, wq_ref)
    k = conv_silu(kd_ref, hk_ref, wk_ref)
    v = conv_silu(vd_ref, hv_ref, wv_ref)
    q = q * lax.rsqrt(jnp.sum(q * q, -1, keepdims=True) + EPS) * (DN_KEY_DIM ** -0.5)
    k = k * lax.rsqrt(jnp.sum(k * k, -1, keepdims=True) + EPS)

    ba = ba_ref[...]
    lane = lax.broadcasted_iota(jnp.int32, ba.shape, 1)
    beta_all = jax.nn.sigmoid(ba)
    g_all = -jnp.exp(alog_ref[...]) * _softplus(ba + dtb_ref[...])
    beta = jnp.sum(jnp.where(lane == h, beta_all, 0.0), -1, keepdims=True)
    g = jnp.sum(jnp.where(lane == h + DN_HEADS, g_all, 0.0), -1, keepdims=True)

    li = lax.broadcasted_iota(jnp.int32, (n, n), 0)
    mi = lax.broadcasted_iota(jnp.int32, (n, n), 1)
    same = (li // chunk) == (mi // chunk)
    upto = jnp.logical_and(same, li <= mi)
    since = jnp.logical_and(same, li >= mi)
    chunk_end = mi == (li // chunk) * chunk + (chunk - 1)
    blocks = _BlockDiag(chunk, group)
    op_dtype = wk_out.dtype
    slot = p_out.shape[1]

    row_groups = [slice(gi * n, (gi + 1) * n) for gi in range(tb // n)]
    a_mats, stash = [], []
    for rows in row_groups:
        qc, kc, bc, gc = q[rows], k[rows], beta[rows], g[rows]
        g_row = jnp.sum(jnp.where(upto, gc, 0.0), 0, keepdims=True)
        g_col = jnp.sum(jnp.where(li == mi, g_row, 0.0), 1, keepdims=True)
        g_end = jnp.sum(jnp.where(chunk_end, g_row, 0.0), 1, keepdims=True)
        decay = jnp.exp(jnp.where(since, g_col - g_row, -jnp.inf))
        qk_kk = _dot_any(jnp.concatenate([qc, kc], 0), kc, precise, _dot_nt)
        a_mats.append(jnp.where(li > mi, bc * decay * qk_kk[n:], 0.0))
        e_g = jnp.exp(g_col)
        qdec_out[rows, :] = (e_g * qc).astype(op_dtype)
        kend_out[rows, :] = (jnp.exp(g_end - g_col) * kc).astype(op_dtype)
        p_out[rows, 0:n] = (qk_kk[:n] * decay).astype(op_dtype)
        if slot > n:
            p_out[rows, n:] = jnp.zeros((n, slot - n), op_dtype)
        stash.append((e_g, g_end))
    t_invs = blocks.unit_lower_inverse(a_mats)
    for gi, (rows, t_inv, (e_g, g_end)) in enumerate(zip(row_groups, t_invs, stash)):
        kc, vc, bc = k[rows], v[rows], beta[rows]
        w = blocks.lmul(t_inv, jnp.concatenate([bc * vc, (bc * e_g) * kc], 1))
        wv_out[rows, :] = w[:, :DN_VAL_DIM]
        wk_out[rows, :] = w[:, DN_VAL_DIM:].astype(op_dtype)
        for c in range(group):
            last = c * chunk + chunk - 1
            gend_out[gi * group + c] = jnp.broadcast_to(jnp.exp(g_end[last:last + 1]), (1, LANES))


def _dn_score_slot(chunk, group):
    return LANES if chunk * group < LANES else max(chunk * group, V7X_MXU_DEPTH)


def _dn_prep(p, halo, w_conv8, alog_row, dtb_row, chunk, group, tb, precise):
    m = p.shape[0]
    op_dtype = F32 if precise else BF16
    slot = _dn_score_slot(chunk, group)
    nh = DN_HEADS
    cq, ck, cv = P_CONV // LANES, P_CONV // LANES + nh, P_CONV // LANES + 2 * nh
    col = lambda base: (lambda i, h: (i, base + h))
    halo_spec = lambda base: pl.BlockSpec((1, 8, LANES), lambda i, h: (i, 0, base + h))
    w_spec = lambda base: pl.BlockSpec((8, LANES), lambda i, h: (0, base + h))
    const = pl.BlockSpec((1, LANES), lambda i, h: (0, 0))
    head_blk = pl.BlockSpec((tb, LANES), lambda i, h: (i, h))
    out_shape = [
        jax.ShapeDtypeStruct((m, DN_WIDTH), F32),
        jax.ShapeDtypeStruct((m, DN_WIDTH), op_dtype),
        jax.ShapeDtypeStruct((m, DN_WIDTH), op_dtype),
        jax.ShapeDtypeStruct((m, DN_WIDTH), op_dtype),
        jax.ShapeDtypeStruct((m, DN_HEADS * slot), op_dtype),
        jax.ShapeDtypeStruct((m // chunk, 1, DN_WIDTH), F32),
    ]
    return pl.pallas_call(
        functools.partial(_dn_prep_kernel, chunk, group, precise),
        grid=(m // tb, nh),
        in_specs=[
            pl.BlockSpec((tb, LANES), col(cq)),
            pl.BlockSpec((tb, LANES), col(ck)),
            pl.BlockSpec((tb, LANES), col(cv)),
            pl.BlockSpec((tb, LANES), lambda i, h: (i, P_BA // LANES)),
            halo_spec(0), halo_spec(nh), halo_spec(2 * nh),
            w_spec(0), w_spec(nh), w_spec(2 * nh),
            const, const,
        ],
        out_specs=[head_blk] * 4 + [pl.BlockSpec((tb, slot), lambda i, h: (i, h)),
                                    pl.BlockSpec((tb // chunk, 1, LANES), lambda i, h: (i, 0, h))],
        out_shape=out_shape,
        compiler_params=_cparams("parallel", "parallel"),
        name="dn_prep",
    )(p, p, p, p, halo, halo, halo, w_conv8, w_conv8, w_conv8, alog_row, dtb_row)


def _dn_scan_kernel(chunk, group, n_chunks, wv_ref, wk_ref, qd_ref, ke_ref, p_ref, ge_ref, gate_ref, s0_ref,
                    onorm_ref, od_ref, sout_ref, s_scr):
    n = pl.program_id(1)
    precise = wk_ref.dtype == F32
    slot = _dn_score_slot(chunk, group)

    @pl.when(n == 0)
    def _():
        s_scr[...] = s0_ref[0]

    onorm = onorm_ref[...]
    u_group = [[] for _ in range(DN_HEADS)]
    for c in range(n_chunks):
        rows = slice(c * chunk, (c + 1) * chunk)
        local = c % group
        for h in range(DN_HEADS):
            cols = slice(h * LANES, (h + 1) * LANES)
            s = s_scr[h]
            if not precise:
                s = s.astype(BF16)
            ws_qs = _dot_any(jnp.concatenate([wk_ref[rows, cols], qd_ref[rows, cols]], 0), s, precise)
            u = wv_ref[rows, cols] - ws_qs[:chunk]
            if not precise:
                u = u.astype(BF16)
            if local == 0:
                u_group[h] = []
            u_group[h].append(u)
            seen = (local + 1) * chunk
            u_cat = jnp.concatenate(u_group[h] + [jnp.zeros((slot - seen, DN_VAL_DIM), u.dtype)], 0)
            o = ws_qs[chunk:] + _dot_any(p_ref[rows, h * slot:(h + 1) * slot], u_cat, precise)
            s_scr[h] = ge_ref[c, :, cols] * s_scr[h] + _dot_any(ke_ref[rows, cols], u, precise, _dot_tn, 1)
            gate = gate_ref[rows, cols]
            od_ref[rows, cols] = (_rms(o, onorm) * _silu(gate)).astype(od_ref.dtype)

    @pl.when(n == pl.num_programs(1) - 1)
    def _():
        sout_ref[0] = s_scr[...]


def _dn_scan(prep, p, s0, onorm_row, chunk, group, n_chunks, batch):
    wv, wk, qdec, kend, pm, gend = prep
    m = wv.shape[0]
    assert n_chunks % group == 0
    rows = chunk * n_chunks
    steps = m // batch // rows
    blk = lambda b, n: (b * steps + n, 0)
    wide = pl.BlockSpec((rows, DN_WIDTH), blk)
    state = pl.BlockSpec((1, DN_HEADS, DN_KEY_DIM, DN_VAL_DIM), lambda b, n: (b, 0, 0, 0))
    slot = _dn_score_slot(chunk, group)
    assert pm.shape[1] == DN_HEADS * slot
    return pl.pallas_call(
        functools.partial(_dn_scan_kernel, chunk, group, n_chunks),
        grid=(batch, steps),
        in_specs=[
            wide, wide, wide, wide, pl.BlockSpec((rows, DN_HEADS * slot), blk),
            pl.BlockSpec((n_chunks, 1, DN_WIDTH), lambda b, n: (b * steps + n, 0, 0)),
            pl.BlockSpec((rows, DN_WIDTH), lambda b, n: (b * steps + n, P_GATE // DN_WIDTH)),
            state,
            pl.BlockSpec((1, LANES), lambda b, n: (0, 0)),
        ],
        out_specs=[wide, state],
        out_shape=[jax.ShapeDtypeStruct((m, DN_WIDTH), wk.dtype),
                   jax.ShapeDtypeStruct((batch, DN_HEADS, DN_KEY_DIM, DN_VAL_DIM), F32)],
        scratch_shapes=[pltpu.VMEM((DN_HEADS, DN_KEY_DIM, DN_VAL_DIM), F32)],
        compiler_params=_cparams("parallel", "arbitrary"),
        name="dn_scan",
    )(wv, wk, qdec, kend, pm, gend, p, s0, onorm_row)


def _out_proj_kernel(nw, attn_ref, od_ref, *refs):
    w_refs, (x_ref, g_ref, gate_ref, o_ref) = refs[:nw], refs[nw:]
    y = (_mm(attn_ref[...], tuple(r[0:ATTN_WIDTH, :] for r in w_refs))
         + _mm(od_ref[...], tuple(r[ATTN_WIDTH:, :] for r in w_refs)))
    o_ref[...] = x_ref[...] + gate_ref[...] * _rms(y, g_ref[...])


def _out_proj(attn, od, w, x, gain, gate, tm):
    m, d = x.shape
    row = lambda i: (i, 0)
    return pl.pallas_call(
        functools.partial(_out_proj_kernel, len(w)),
        grid=(m // tm,),
        in_specs=[
            pl.BlockSpec((tm, ATTN_WIDTH), row),
            pl.BlockSpec((tm, DN_WIDTH), row),
        ] + [pl.BlockSpec((ATTN_WIDTH + DN_WIDTH, d), lambda i: (0, 0))] * len(w) + [
            pl.BlockSpec((tm, d), row),
            pl.BlockSpec((1, d), lambda i: (0, 0)),
            _mod_spec(gate.shape[0], tm, d),
        ],
        out_specs=pl.BlockSpec((tm, d), row),
        out_shape=jax.ShapeDtypeStruct((m, d), F32),
        compiler_params=_cparams("parallel"),
        name="out_proj",
    )(attn, od, *w, x, gain, gate)


def _ffn_kernel(nw, x_ref, g_ref, sc_ref, sh_ref, *refs):
    wg_refs, wu_refs, wd_refs = refs[:nw], refs[nw:2 * nw], refs[2 * nw:3 * nw]
    g2_ref, gate_ref, o_ref, h_ref, acc_ref = refs[3 * nw:]
    j = pl.program_id(1)

    @pl.when(j == 0)
    def _():
        h = _rms(x_ref[...], g_ref[...]) * (1.0 + sc_ref[...]) + sh_ref[...]
        h_ref[...] = h.astype(h_ref.dtype)
        acc_ref[...] = jnp.zeros_like(acc_ref)

    h = h_ref[...]
    act = _silu(_mm(h, tuple(r[...] for r in wg_refs))) * _mm(h, tuple(r[...] for r in wu_refs))
    acc_ref[...] += _mm(act, tuple(r[...] for r in wd_refs))

    @pl.when(j == pl.num_programs(1) - 1)
    def _():
        o_ref[...] = x_ref[...] + gate_ref[...] * _rms(acc_ref[...], g2_ref[...])


def _ffn(x, gain, scale, shift, wg, wu, wd, gain2, gate, tm, tf):
    m, d = x.shape
    nw = len(wg)
    f = wg[0].shape[1]
    row = lambda i, j: (i, 0)
    vec = pl.BlockSpec((1, d), lambda i, j: (0, 0))
    return pl.pallas_call(
        functools.partial(_ffn_kernel, nw),
        grid=(m // tm, f // tf),
        in_specs=[
            pl.BlockSpec((tm, d), row), vec,
            _mod_spec(scale.shape[0], tm, d), _mod_spec(shift.shape[0], tm, d),
        ] + [pl.BlockSpec((d, tf), lambda i, j: (0, j))] * (2 * nw)
        + [pl.BlockSpec((tf, d), lambda i, j: (j, 0))] * nw
        + [vec, _mod_spec(gate.shape[0], tm, d)],
        out_specs=pl.BlockSpec((tm, d), row),
        out_shape=jax.ShapeDtypeStruct((m, d), F32),
        scratch_shapes=[pltpu.VMEM((tm, d), BF16 if nw == 1 else F32), pltpu.VMEM((tm, d), F32)],
        compiler_params=_cparams("parallel", "arbitrary"),
        name="ffn_dense",
    )(x, gain, scale, shift, *wg, *wu, *wd, gain2, gate)


def _router_kernel(x_ref, g_ref, sc_ref, sh_ref, wr_ref, h_ref, gates_ref, idx_ref, w12_ref):
    h = _rms(x_ref[...], g_ref[...]) * (1.0 + sc_ref[...]) + sh_ref[...]
    h_ref[...] = h
    logits = _dot_x3(h, wr_ref[...])
    lane = lax.broadcasted_iota(jnp.int32, logits.shape, 1).astype(F32)
    logits = jnp.where(lane < N_EXPERTS, logits, -jnp.inf)
    m1 = jnp.max(logits, -1, keepdims=True)
    i1 = jnp.min(jnp.where(logits == m1, lane, float(LANES)), -1, keepdims=True)
    rest = jnp.where(lane == i1, -jnp.inf, logits)
    m2 = jnp.max(rest, -1, keepdims=True)
    i2 = jnp.min(jnp.where(rest == m2, lane, float(LANES)), -1, keepdims=True)
    t = jnp.exp(m2 - m1)
    w1 = 1.0 / (1.0 + t)
    w2 = t / (1.0 + t)
    gates_ref[...] = jnp.where(lane == i1, w1, 0.0) + jnp.where(lane == i2, w2, 0.0)
    idx_ref[...] = jnp.where(lane == 0.0, i1, jnp.where(lane == 1.0, i2, 0.0)).astype(jnp.int32)
    w12_ref[...] = jnp.where(lane == 0.0, w1, jnp.where(lane == 1.0, w2, 0.0))


def _router(x, gain, scale, shift, w_router_pad, tm):
    m, d = x.shape
    row = lambda i: (i, 0)
    vec = pl.BlockSpec((1, d), lambda i: (0, 0))
    small = pl.BlockSpec((tm, LANES), row)
    return pl.pallas_call(
        _router_kernel,
        grid=(m // tm,),
        in_specs=[pl.BlockSpec((tm, d), row), vec,
                  _mod_spec(scale.shape[0], tm, d), _mod_spec(shift.shape[0], tm, d),
                  pl.BlockSpec((d, LANES), lambda i: (0, 0))],
        out_specs=[pl.BlockSpec((tm, d), row), small, small, small],
        out_shape=[jax.ShapeDtypeStruct((m, d), F32), jax.ShapeDtypeStruct((m, LANES), F32),
                   jax.ShapeDtypeStruct((m, LANES), jnp.int32), jax.ShapeDtypeStruct((m, LANES), F32)],
        compiler_params=_cparams("parallel"),
        name="moe_router",
    )(x, gain, scale, shift, w_router_pad)


def _moe_gemm_kernel(nj, te_ref, tot_ref, rt_ref, h_hbm, wg_ref, wu_ref, wd_ref, ys_ref, xs_ref, xb_ref, acc_ref,
                     sems):
    r = pl.program_id(0)
    j = pl.program_id(1)
    tm = xb_ref.shape[0]
    total = tot_ref[0]
    active = r < total
    slot = r % 2
    share = -(-tm // nj)

    def row_copy(tile, t, s):
        return pltpu.make_async_copy(h_hbm.at[pl.ds(rt_ref[tile * tm + t], 1), :],
                                     xs_ref.at[s, pl.ds(t, 1), :], sems.at[s])

    def start_rows(tile, s, lo, hi):
        def body(t, carry):
            row_copy(tile, t, s).start()
            return carry

        lax.fori_loop(lo, hi, body, 0)

    @pl.when(jnp.logical_and(r == 0, j == 0))
    def _():
        start_rows(0, 0, 0, tm)

    @pl.when(jnp.logical_and(active, j == 0))
    def _():
        pltpu.make_async_copy(h_hbm.at[pl.ds(0, tm), :], xs_ref.at[slot], sems.at[slot]).wait()
        xb_ref[...] = xs_ref[slot].astype(BF16)
        acc_ref[...] = jnp.zeros_like(acc_ref)

    @pl.when(active)
    def _():
        more = r + 1 < total
        for u in range(share):
            t = j * share + u

            @pl.when(jnp.logical_and(more, t < tm))
            def _():
                row_copy(jnp.minimum(r + 1, pl.num_programs(0) - 1), jnp.minimum(t, tm - 1), 1 - slot).start()

        xb = xb_ref[...]
        act = (_silu(_dot(xb, wg_ref[0])) * _dot(xb, wu_ref[0])).astype(BF16)
        acc_ref[...] += _dot(act, wd_ref[0])

    @pl.when(j == pl.num_programs(1) - 1)
    def _():
        ys_ref[...] = jnp.where(active, acc_ref[...], 0.0)


def _moe_gemm(tile_expert, total_tiles, row_token, h, wg, wu, wd, tm, tf):
    n_tiles = tile_expert.shape[0]
    d = h.shape[1]
    f = wg.shape[2]
    nj = f // tf

    def w_col(r, j, te, tot, rt):
        return (te[r], 0, jnp.where(r < tot[0], j, nj - 1))

    def w_row(r, j, te, tot, rt):
        return (te[r], jnp.where(r < tot[0], j, nj - 1), 0)

    grid_spec = pltpu.PrefetchScalarGridSpec(
        num_scalar_prefetch=3,
        grid=(n_tiles, nj),
        in_specs=[
            pl.BlockSpec(memory_space=pl.ANY),
            pl.BlockSpec((1, d, tf), w_col),
            pl.BlockSpec((1, d, tf), w_col),
            pl.BlockSpec((1, tf, d), w_row),
        ],
        out_specs=pl.BlockSpec((tm, d), lambda r, j, te, tot, rt: (r, 0)),
        scratch_shapes=[pltpu.VMEM((2, tm, d), F32), pltpu.VMEM((tm, d), BF16), pltpu.VMEM((tm, d), F32),
                        pltpu.SemaphoreType.DMA((2,))],
    )
    return pl.pallas_call(
        functools.partial(_moe_gemm_kernel, nj),
        grid_spec=grid_spec,
        out_shape=jax.ShapeDtypeStruct((n_tiles * tm, d), F32),
        compiler_params=_cparams("arbitrary", "arbitrary", row_dma=True),
        name="moe_gemm",
    )(tile_expert, total_tiles, row_token, h, wg, wu, wd)


def _moe_combine_kernel(dest_ref, x_ref, w12_ref, g_ref, gate_ref, ys_hbm, o_ref, buf_ref, sems):
    i = pl.program_id(0)
    tb = x_ref.shape[0]
    slot = i % 2

    def row_copy(blk, t, k, s):
        src = dest_ref[2 * (blk * tb + t) + k]
        return pltpu.make_async_copy(ys_hbm.at[pl.ds(src, 1), :], buf_ref.at[s, k, pl.ds(t, 1), :], sems.at[s])

    def start_block(blk, s):
        def body(t, carry):
            row_copy(blk, t, 0, s).start()
            row_copy(blk, t, 1, s).start()
            return carry

        lax.fori_loop(0, tb, body, 0, unroll=8)

    @pl.when(i == 0)
    def _():
        start_block(0, 0)

    @pl.when(i + 1 < pl.num_programs(0))
    def _():
        start_block(i + 1, 1 - slot)

    for k in range(2):
        pltpu.make_async_copy(ys_hbm.at[pl.ds(0, tb), :], buf_ref.at[slot, k], sems.at[slot]).wait()
    w12 = w12_ref[...]
    y = w12[:, 0:1] * buf_ref[slot, 0] + w12[:, 1:2] * buf_ref[slot, 1]
    o_ref[...] = x_ref[...] + gate_ref[...] * _rms(y, g_ref[...])


def _moe_combine(dest, x, w12, gain, gate, ys, tb):
    m, d = x.shape
    row = lambda i, dst: (i, 0)
    grid_spec = pltpu.PrefetchScalarGridSpec(
        num_scalar_prefetch=1,
        grid=(m // tb,),
        in_specs=[
            pl.BlockSpec((tb, d), row),
            pl.BlockSpec((tb, LANES), row),
            pl.BlockSpec((1, d), lambda i, dst: (0, 0)),
            pl.BlockSpec((1, d), lambda i, dst: (0, 0)),
            pl.BlockSpec(memory_space=pl.ANY),
        ],
        out_specs=pl.BlockSpec((tb, d), row),
        scratch_shapes=[pltpu.VMEM((2, 2, tb, d), F32), pltpu.SemaphoreType.DMA((2,))],
    )
    return pl.pallas_call(
        _moe_combine_kernel,
        grid_spec=grid_spec,
        out_shape=jax.ShapeDtypeStruct((m, d), F32),
        compiler_params=_cparams("arbitrary", row_dma=True),
        name="moe_combine",
    )(dest, x, w12, gain, gate, ys)


def _route_tables(idx2, tm, n_tiles):
    m = idx2.shape[0]
    n_assign = 2 * m
    assert n_tiles * tm == n_assign + N_EXPERTS * tm
    experts = jnp.arange(N_EXPERTS, dtype=jnp.int32)
    e_flat = idx2.reshape(n_assign)
    onehot = (e_flat[:, None] == experts[None, :]).astype(jnp.int32)
    csum = jnp.cumsum(onehot, 0)
    counts = csum[-1]
    padded = ((counts + tm - 1) // tm) * tm
    pend = jnp.cumsum(padded)
    pstart = pend - padded
    dest = jnp.sum(onehot * (pstart[None, :] + csum - 1), -1).astype(jnp.int32)
    total_tiles = (pend[-1] // tm).astype(jnp.int32).reshape(1)
    tile_expert = jnp.minimum(
        jnp.searchsorted(pend // tm, jnp.arange(n_tiles, dtype=jnp.int32), side="right"), N_EXPERTS - 1
    ).astype(jnp.int32)
    filler_key = jnp.where(jnp.arange(tm, dtype=jnp.int32)[None, :] < (padded - counts)[:, None],
                           experts[:, None], N_EXPERTS).reshape(-1)
    keys = jnp.concatenate([e_flat, filler_key])
    tokens = jnp.concatenate([jnp.arange(n_assign, dtype=jnp.int32) // 2,
                              jnp.zeros((N_EXPERTS * tm,), jnp.int32)])
    _, row_token = lax.sort((keys, tokens), num_keys=1, is_stable=True)
    return tile_expert, total_tiles, row_token, dest


def _moe_dense_kernel(h_ref, gates_ref, wg_ref, wu_ref, wd_ref, x_ref, g_ref, gate_ref, o_ref, acc_ref, tot_ref):
    e = pl.program_id(0)
    j = pl.program_id(1)
    nj = pl.num_programs(1)

    @pl.when(jnp.logical_and(e == 0, j == 0))
    def _():
        tot_ref[...] = jnp.zeros_like(tot_ref)

    @pl.when(j == 0)
    def _():
        acc_ref[...] = jnp.zeros_like(acc_ref)

    h = h_ref[...].astype(BF16)
    act = (_silu(_dot(h, wg_ref[0])) * _dot(h, wu_ref[0])).astype(BF16)
    acc_ref[...] += _dot(act, wd_ref[0])

    @pl.when(j == nj - 1)
    def _():
        gates = gates_ref[...]
        lane = lax.broadcasted_iota(jnp.int32, gates.shape, 1)
        ge = jnp.sum(jnp.where(lane == e, gates, 0.0), -1, keepdims=True)
        tot_ref[...] += ge * acc_ref[...]

    @pl.when(jnp.logical_and(e == pl.num_programs(0) - 1, j == nj - 1))
    def _():
        o_ref[...] = x_ref[...] + gate_ref[...] * _rms(tot_ref[...], g_ref[...])


def _moe_dense(h, gates, wg, wu, wd, x, gain, gate, tf):
    m, d = x.shape
    f = wg.shape[2]
    full = pl.BlockSpec((m, d), lambda e, j: (0, 0))
    return pl.pallas_call(
        _moe_dense_kernel,
        grid=(N_EXPERTS, f // tf),
        in_specs=[
            full,
            pl.BlockSpec((m, LANES), lambda e, j: (0, 0)),
            pl.BlockSpec((1, d, tf), lambda e, j: (e, 0, j)),
            pl.BlockSpec((1, d, tf), lambda e, j: (e, 0, j)),
            pl.BlockSpec((1, tf, d), lambda e, j: (e, j, 0)),
            full,
            pl.BlockSpec((1, d), lambda e, j: (0, 0)),
            full,
        ],
        out_specs=full,
        out_shape=jax.ShapeDtypeStruct((m, d), F32),
        scratch_shapes=[pltpu.VMEM((m, d), F32), pltpu.VMEM((m, d), F32)],
        compiler_params=_cparams("arbitrary", "arbitrary"),
        name="moe_dense",
    )(h, gates, wg, wu, wd, x, gain, gate)


def _rope_tables(pos):
    half = ROPE_DIM // 2
    inv_freq = jnp.power(ROPE_THETA, -2.0 * jnp.arange(half, dtype=F32) / ROPE_DIM)
    ang = pos.astype(F32)[:, None] * inv_freq[None, :]
    cos, sin = jnp.cos(ang), jnp.sin(ang)
    t = pos.shape[0]
    rest = ATTN_HEAD_DIM - ROPE_DIM
    cos_h = jnp.concatenate([cos, cos, jnp.ones((t, rest), F32)], 1)
    sa_h = jnp.concatenate([-sin, jnp.zeros((t, half + rest), F32)], 1)
    sb_h = jnp.concatenate([jnp.zeros((t, half), F32), sin, jnp.zeros((t, rest), F32)], 1)
    rep = LANES // ATTN_HEAD_DIM
    return tuple(jnp.tile(a, (1, rep)) for a in (cos_h, sa_h, sb_h))


def _permute_w_in(w):
    o1 = ATTN_WIDTH
    o2 = o1 + KV_WIDTH
    o3 = o2 + KV_WIDTH
    o4 = o3 + DN_CONV_CH
    o5 = o4 + DN_WIDTH
    parts = [w[:, :o1], w[:, o4:o5], w[:, o3:o4], w[:, o1:o2], w[:, o2:o3], w[:, o5:]]
    used = sum(a.shape[1] for a in parts)
    parts.append(jnp.zeros((w.shape[0], P_WIDTH - used), w.dtype))
    return jnp.concatenate(parts, 1)


def _sample_mask_bias(s, lc):
    q_pos = PAST_LEN + np.arange(s)
    k_pos = np.concatenate([PAST_LEN - lc + np.arange(lc), q_pos])
    q_chunk = q_pos[:, None] // CHUNK
    k_chunk = k_pos[None, :] // CHUNK
    mask = (k_pos[None, :] >= 0) & (k_chunk <= q_chunk) & (k_pos[None, :] >= q_chunk * CHUNK - WINDOW)
    bias = np.where(mask, 0.0, -np.inf).astype(np.float32)
    return jnp.asarray(np.tile(bias, (4, 1)))


def _conv_halo(p, init, tb, seq):
    m = p.shape[0]
    batch = m // seq
    nb = seq // tb
    tails = p.reshape(batch, nb, tb, P_WIDTH)[:, :nb - 1, tb - (CONV_WIDTH - 1):, P_CONV:P_CONV + DN_CONV_CH]
    prev = jnp.concatenate([init[:, None], tails], 1)
    prev = prev.reshape(batch * nb, CONV_WIDTH - 1, DN_CONV_CH)
    return jnp.pad(prev, ((0, 0), (8 - (CONV_WIDTH - 1), 0), (0, 0)))


def _trunk(x, mods, layer_w, rope, past, cfg):
    m = x.shape[0]
    batch, seq = cfg["batch"], cfg["seq"]
    precise = cfg["precise"]
    nw = 2 if precise else 1
    ks, vs, ss, bufs = [], [], [], []
    cos, sa, sb = rope
    for l in range(DEPTH):
        w = layer_w[l]
        sh_a, sc_a, g_a, sh_f, sc_f, g_f = mods[l]
        p = _norm_proj(x, w["gain"][0], sc_a, sh_a, w["w_in"][:nw], cfg["tm_proj"], cfg["tn_proj"])
        if past is None:
            attn, k_new = _attn_prompt(p, w["sinks"], cos, sa, sb, cfg["tb_attn"])
            s0 = jnp.zeros((batch, DN_HEADS, DN_KEY_DIM, DN_VAL_DIM), F32)
            conv_init = jnp.zeros((batch, CONV_WIDTH - 1, DN_CONV_CH), F32)
        else:
            ck = past[0][l].reshape(batch, -1, KV_WIDTH)
            cv = past[1][l].reshape(batch, -1, KV_WIDTH)
            attn, k_new = _attn_sample(p, ck, cv, w["sinks"], cos, sa, sb, cfg["bias"], batch, seq)
            s0 = past[2][l]
            conv_init = past[3][l]
        halo = _conv_halo(p, conv_init, cfg["tb_dn"], seq)
        prep = _dn_prep(p, halo, w["w_conv"], w["alog"], w["dtb"], cfg["chunk"], cfg["group"], cfg["tb_dn"],
                        precise)
        od, s_new = _dn_scan(prep, p, s0, w["onorm"], cfg["chunk"], cfg["group"], cfg["scan_chunks"], batch)
        x = _out_proj(attn, od, w["w_out"][:nw], x, w["gain"][1], g_a, cfg["tm_out"])
        if l % 2 == 0:
            x = _ffn(x, w["gain"][2], sc_f, sh_f, w["ffn_gate"][:nw], w["ffn_up"][:nw], w["ffn_down"][:nw],
                     w["gain"][3], g_f, cfg["tm_ffn"], cfg["tf_ffn"])
        else:
            h, gates, idx, w12 = _router(x, w["gain"][2], sc_f, sh_f, w["router"], cfg["tm_router"])
            if cfg["routed"]:
                tm = cfg["tm_moe"]
                n_tiles = 2 * m // tm + N_EXPERTS
                tile_expert, total_tiles, row_token, dest = _route_tables(idx[:, :2], tm, n_tiles)
                ys = _moe_gemm(tile_expert, total_tiles, row_token, h, w["moe_gate"], w["moe_up"], w["moe_down"],
                               tm, cfg["tf_moe"])
                x = _moe_combine(dest, x, w12, w["gain"][3], g_f, ys, cfg["tb_combine"])
            else:
                x = _moe_dense(h, gates, w["moe_gate"], w["moe_up"], w["moe_down"], x, w["gain"][3], g_f,
                               cfg["tf_moe"])
        pb = p.reshape(batch, seq, P_WIDTH)
        keep = min(WINDOW, seq) if past is None else seq
        ks.append(k_new.reshape(batch, seq, ATTN_KV_HEADS, ATTN_HEAD_DIM)[:, seq - keep:])
        vs.append(pb[:, seq - keep:, P_V:P_V + KV_WIDTH].reshape(batch, keep, ATTN_KV_HEADS, ATTN_HEAD_DIM))
        ss.append(s_new)
        assert seq >= CONV_WIDTH - 1
        bufs.append(pb[:, seq - (CONV_WIDTH - 1):, P_CONV:P_CONV + DN_CONV_CH])
    return x, jnp.stack(ks), jnp.stack(vs), jnp.stack(ss), jnp.stack(bufs)


def kernel(x_prompt, x_sample, cache_attn_k, cache_attn_v, state_delta, state_conv, c_prompt, c_sample, w_in, w_conv, attn_sinks, dn_a_log, dn_dt_bias, dn_norm, w_out, w_mod, b_mod, norm_gains, ffn_gate, ffn_up, ffn_down, moe_router, moe_gate, moe_up, moe_down):
    bp, tp, d = x_prompt.shape
    bs, ts, _ = x_sample.shape
    assert bp == 1 and d == D_MODEL

    c_all = jnp.concatenate([c_prompt, c_sample, jnp.zeros((16 - bp - bs, d), F32)], 0)
    mod = _modulation(c_all, w_mod, b_mod)
    mods_p, mods_s = [], []
    for l in range(DEPTH):
        six = jnp.split(mod[l], 6, -1)
        mods_p.append([a[0:bp] for a in six])
        mods_s.append([jnp.repeat(a[bp:bp + bs], ts, axis=0) for a in six])

    def pad_lanes(v, at):
        return jnp.zeros((1, LANES), F32).at[0, at:at + v.shape[0]].set(v)

    layer_w = []
    for l in range(DEPTH):
        w = {
            "gain": [norm_gains[l, i].reshape(1, d) for i in range(4)],
            "w_in": _split_weight(_permute_w_in(w_in[l])),
            "sinks": attn_sinks[l],
            "w_conv": jnp.pad(w_conv[l], ((0, 8 - CONV_WIDTH), (0, 0))),
            "alog": pad_lanes(dn_a_log[l], DN_HEADS),
            "dtb": pad_lanes(dn_dt_bias[l], DN_HEADS),
            "onorm": dn_norm[l].reshape(1, DN_VAL_DIM),
            "w_out": _split_weight(w_out[l]),
        }
        if l % 2 == 0:
            w["ffn_gate"] = _split_weight(ffn_gate[l // 2])
            w["ffn_up"] = _split_weight(ffn_up[l // 2])
            w["ffn_down"] = _split_weight(ffn_down[l // 2])
        else:
            w["router"] = jnp.pad(moe_router[l // 2], ((0, 0), (0, LANES - N_EXPERTS)))
            w["moe_gate"] = moe_gate[l // 2].astype(BF16)
            w["moe_up"] = moe_up[l // 2].astype(BF16)
            w["moe_down"] = moe_down[l // 2].astype(BF16)
        layer_w.append(w)

    cfg_p = dict(batch=bp, seq=tp, precise=False, chunk=CHUNK, group=2, scan_chunks=4, tm_proj=1024, tn_proj=512, tb_attn=512,
                 tb_dn=512, tm_out=512, tm_ffn=512, tf_ffn=512, tm_router=512, routed=True, tm_moe=512,
                 tf_moe=256, tb_combine=256)
    rope_p = _rope_tables(jnp.arange(tp, dtype=jnp.int32))
    y_p, k_p, v_p, s_p, conv_p = _trunk(x_prompt.reshape(bp * tp, d), mods_p, layer_w, rope_p, None, cfg_p)

    ms = bs * ts
    cfg_s = dict(batch=bs, seq=ts, precise=True, chunk=ts, group=1, scan_chunks=1, tm_proj=ms, tn_proj=512, tb_dn=ts, tm_out=ms,
                 tm_ffn=ms, tf_ffn=512, tm_router=ms, routed=False, tf_moe=1408,
                 bias=_sample_mask_bias(ts, cache_attn_k.shape[2]))
    rope_s = _rope_tables(PAST_LEN + jnp.arange(ts, dtype=jnp.int32))
    past = (cache_attn_k, cache_attn_v, state_delta, state_conv)
    y_s, k_s, v_s, s_s, conv_s = _trunk(x_sample.reshape(ms, d), mods_s, layer_w, rope_s, past, cfg_s)

    return (y_p.reshape(bp, tp, d), y_s.reshape(bs, ts, d), k_p, v_p, s_p, conv_p, k_s, v_s, s_s, conv_s)
```

```python
import functools
import math

import numpy as np
import jax
import jax.numpy as jnp
from jax import lax
from jax.experimental import pallas as pl
from jax.experimental.pallas import tpu as pltpu

D_MODEL = 2048
DEPTH = 2
PAST_LEN = 1024
CHUNK = 64
ATTN_HEADS = 16
ATTN_KV_HEADS = 2
ATTN_HEAD_DIM = 64
ATTN_WIDTH = 1024
KV_WIDTH = 128
WINDOW = 128
ROPE_THETA = 500000.0
ROPE_DIM = 16
DN_HEADS = 8
DN_KEY_DIM = 128
DN_VAL_DIM = 128
DN_WIDTH = 1024
CONV_WIDTH = 4
DN_CONV_CH = 3072
D_FF = 5632
N_EXPERTS = 8
D_FF_EXPERT = 2816
EPS = 1e-6

F32 = jnp.float32
BF16 = jnp.bfloat16
LANES = 128
V7X_MXU_DEPTH = 256
V7X_VMEM_LIMIT = 56 * 1024 * 1024

P_Q = 0
P_GATE = 1024
P_CONV = 2048
P_K = 5120
P_V = 5248
P_BA = 5376
P_WIDTH = 5632


def _cparams(*sem, row_dma=False):
    return pltpu.CompilerParams(dimension_semantics=sem, vmem_limit_bytes=V7X_VMEM_LIMIT,
                                disable_bounds_checks=row_dma)


def _silu(x):
    return x * jax.nn.sigmoid(x)


def _rms(x, gain):
    return x * lax.rsqrt(jnp.mean(x * x, -1, keepdims=True) + EPS) * gain


def _dot(a, b):
    return jnp.dot(a, b, preferred_element_type=F32)


def _dot_nt(a, b):
    return lax.dot_general(a, b, (((1,), (1,)), ((), ())), preferred_element_type=F32)


def _dot_tn(a, b):
    return lax.dot_general(a, b, (((0,), (0,)), ((), ())), preferred_element_type=F32)


def _split_bf16(a):
    hi = a.astype(BF16)
    lo = (a - hi.astype(F32)).astype(BF16)
    return hi, lo


def _dot_x3(a, b, dot=_dot, out_axis=0):
    a_hi, a_lo = _split_bf16(a)
    b_hi, b_lo = _split_bf16(b)
    n = a.shape[out_axis]
    top = dot(jnp.concatenate([a_hi, a_lo], out_axis), b_hi)
    return top[:n] + top[n:] + dot(a_hi, b_lo)


def _dot_any(a, b, precise, dot=_dot, out_axis=0):
    if precise:
        return _dot_x3(a, b, dot, out_axis)
    return dot(a.astype(BF16), b.astype(BF16))


def _mm(a, w):
    if len(w) == 1 and w[0].dtype == F32:
        return _dot_x3(a, w[0])
    if len(w) == 1:
        return _dot(a.astype(BF16), w[0])
    a_hi, a_lo = _split_bf16(a)
    n = a.shape[0]
    top = _dot(jnp.concatenate([a_hi, a_lo], 0), w[0])
    return top[:n] + top[n:] + _dot(a_hi, w[1])


def _split_weight(w):
    hi, lo = _split_bf16(w)
    return (hi, lo)


def _mod_spec(rows, tm, d):
    if rows == 1:
        return pl.BlockSpec((1, d), lambda i, *_: (0, 0))
    return pl.BlockSpec((tm, d), lambda i, *_: (i, 0))


def _mod_kernel(c_ref, w_ref, b_ref, o_ref):
    o_ref[0] = _dot_x3(_silu(c_ref[...]), w_ref[0]) + b_ref[0]


def _modulation(c_all, w_mod, b_mod):
    rows = c_all.shape[0]
    n = w_mod.shape[2]
    tn = 1024
    return pl.pallas_call(
        _mod_kernel,
        grid=(DEPTH, n // tn),
        in_specs=[
            pl.BlockSpec((rows, D_MODEL), lambda l, j: (0, 0)),
            pl.BlockSpec((1, D_MODEL, tn), lambda l, j: (l, 0, j)),
            pl.BlockSpec((1, 1, tn), lambda l, j: (l, 0, j)),
        ],
        out_specs=pl.BlockSpec((1, rows, tn), lambda l, j: (l, 0, j)),
        out_shape=jax.ShapeDtypeStruct((DEPTH, rows, n), F32),
        compiler_params=_cparams("parallel", "parallel"),
        name="modulation",
    )(c_all, w_mod, b_mod.reshape(DEPTH, 1, n))


def _norm_proj_kernel(nw, x_ref, g_ref, sc_ref, sh_ref, *refs):
    w_refs, (o_ref, h_ref) = refs[:nw], refs[nw:]

    @pl.when(pl.program_id(1) == 0)
    def _():
        h = _rms(x_ref[...], g_ref[...]) * (1.0 + sc_ref[...]) + sh_ref[...]
        h_ref[...] = h.astype(h_ref.dtype)

    o_ref[...] = _mm(h_ref[...], tuple(r[...] for r in w_refs))


def _norm_proj(x, gain, scale, shift, w, tm, tn):
    m, d = x.shape
    n = w[0].shape[1]
    return pl.pallas_call(
        functools.partial(_norm_proj_kernel, len(w)),
        grid=(m // tm, n // tn),
        in_specs=[
            pl.BlockSpec((tm, d), lambda i, j: (i, 0)),
            pl.BlockSpec((1, d), lambda i, j: (0, 0)),
            _mod_spec(scale.shape[0], tm, d),
            _mod_spec(shift.shape[0], tm, d),
        ] + [pl.BlockSpec((d, tn), lambda i, j: (0, j))] * len(w),
        out_specs=pl.BlockSpec((tm, tn), lambda i, j: (i, j)),
        out_shape=jax.ShapeDtypeStruct((m, n), F32),
        scratch_shapes=[pltpu.VMEM((tm, d), BF16 if len(w) == 1 else F32)],
        compiler_params=_cparams("parallel", "arbitrary"),
        name="norm_proj",
    )(x, gain, scale, shift, *w)


def _rope(x, cos, sa, sb):
    return x * cos + pltpu.roll(x, LANES - 8, 1) * sa + pltpu.roll(x, 8, 1) * sb


def _kv_variants(k, v):
    lo = lax.broadcasted_iota(jnp.int32, k.shape, 1) < ATTN_HEAD_DIM
    kr = pltpu.roll(k, ATTN_HEAD_DIM, 1)
    vr = pltpu.roll(v, ATTN_HEAD_DIM, 1)
    zero = jnp.zeros_like(k)
    k_lo = (jnp.where(lo, k, zero), jnp.where(lo, kr, zero))
    k_hi = (jnp.where(lo, zero, kr), jnp.where(lo, zero, k))
    v_lo = (jnp.where(lo, v, zero), jnp.where(lo, vr, zero))
    v_hi = (jnp.where(lo, zero, vr), jnp.where(lo, zero, v))
    return k_lo, k_hi, v_lo, v_hi


def _sink_softmax(s, sink):
    m = jnp.maximum(jnp.max(s, -1, keepdims=True), sink)
    p = jnp.exp(s - m)
    den = jnp.sum(p, -1, keepdims=True) + jnp.exp(sink - m)
    return p / den


def _attn_core(qbs, k_los, k_his, v_los, v_his, bias, sinks, precise=False):
    scores = [(_dot_any(qb, k_lo, precise, _dot_nt) + bias, _dot_any(qb, k_hi, precise, _dot_nt) + bias)
              for qb, k_lo, k_hi in zip(qbs, k_los, k_his)]
    probs = [(_sink_softmax(s_even, sink[0]), _sink_softmax(s_odd, sink[1]))
             for (s_even, s_odd), sink in zip(scores, sinks)]
    return [_dot_any(p_even, v_lo, precise) + _dot_any(p_odd, v_hi, precise)
            for (p_even, p_odd), v_lo, v_hi in zip(probs, v_los, v_his)]


def _sink_columns(sink_ref, rows_per_pair):
    n = 4 * rows_per_pair
    pair = lax.broadcasted_iota(jnp.int32, (n, 1), 0) // rows_per_pair
    out = []
    for j in range(ATTN_KV_HEADS):
        cols = []
        for par in range(2):
            col = jnp.zeros((n, 1), F32)
            for a in range(4):
                col = jnp.where(pair == a, sink_ref[8 * j + 2 * a + par], col)
            cols.append(col)
        out.append(cols)
    return out


def _attn_prompt_kernel(sink_ref, q_ref, kv_ref, cos_ref, sa_ref, sb_ref, o_ref, knew_ref,
                        qs_ref, klo_ref, khi_ref, vlo_ref, vhi_ref):
    i = pl.program_id(0)
    tb = q_ref.shape[0]
    bufs = (klo_ref, khi_ref, vlo_ref, vhi_ref)

    @pl.when(i == 0)
    def _():
        for r in bufs:
            r[:, 0:WINDOW, :] = jnp.zeros((ATTN_KV_HEADS, WINDOW, LANES), BF16)

    @pl.when(i > 0)
    def _():
        for r in bufs:
            r[:, 0:WINDOW, :] = r[:, tb:tb + WINDOW, :]

    cos, sa, sb = cos_ref[...], sa_ref[...], sb_ref[...]
    k = _rope(kv_ref[:, 0:LANES], cos, sa, sb)
    knew_ref[...] = k
    variants = _kv_variants(k, kv_ref[:, LANES:2 * LANES])
    for r, var in zip(bufs, variants):
        for j in range(ATTN_KV_HEADS):
            r[j, WINDOW:, :] = var[j].astype(BF16)
    scale = ATTN_HEAD_DIM ** -0.5
    for a in range(ATTN_WIDTH // LANES):
        cols = slice(a * LANES, (a + 1) * LANES)
        qs_ref[:, cols] = (_rope(q_ref[:, cols], cos, sa, sb) * scale).astype(BF16)

    sinks = _sink_columns(sink_ref, CHUNK)
    nk = WINDOW + CHUNK

    def chunk_body(c, carry):
        r0 = pl.multiple_of(c * CHUNK, CHUNK)
        kpos = i * tb - WINDOW + r0 + lax.broadcasted_iota(jnp.int32, (1, nk), 1)
        bias = jnp.where(kpos >= 0, 0.0, -jnp.inf).astype(F32)
        heads = range(ATTN_KV_HEADS)
        qbs = [jnp.concatenate(
            [qs_ref[pl.ds(r0, CHUNK), (4 * j + a) * LANES:(4 * j + a + 1) * LANES] for a in range(4)], 0)
            for j in heads]
        keys = pl.ds(r0, nk)
        outs = _attn_core(qbs, [klo_ref[j, keys, :] for j in heads], [khi_ref[j, keys, :] for j in heads],
                          [vlo_ref[j, keys, :] for j in heads], [vhi_ref[j, keys, :] for j in heads], bias, sinks)
        for j, o in zip(heads, outs):
            for a in range(4):
                o_ref[pl.ds(r0, CHUNK), (4 * j + a) * LANES:(4 * j + a + 1) * LANES] = (
                    o[a * CHUNK:(a + 1) * CHUNK].astype(BF16))
        return carry

    lax.fori_loop(0, tb // CHUNK, chunk_body, 0, unroll=2)


def _attn_prompt(p, sinks, cos, sa, sb, tb):
    t = p.shape[0]
    kv_blk = P_K // (2 * LANES)
    row = lambda i: (i, 0)
    return pl.pallas_call(
        _attn_prompt_kernel,
        grid=(t // tb,),
        in_specs=[
            pl.BlockSpec(memory_space=pltpu.SMEM),
            pl.BlockSpec((tb, ATTN_WIDTH), row),
            pl.BlockSpec((tb, 2 * LANES), lambda i: (i, kv_blk)),
            pl.BlockSpec((tb, LANES), row),
            pl.BlockSpec((tb, LANES), row),
            pl.BlockSpec((tb, LANES), row),
        ],
        out_specs=[pl.BlockSpec((tb, ATTN_WIDTH), row), pl.BlockSpec((tb, LANES), row)],
        out_shape=[jax.ShapeDtypeStruct((t, ATTN_WIDTH), BF16), jax.ShapeDtypeStruct((t, LANES), F32)],
        scratch_shapes=[pltpu.VMEM((tb, ATTN_WIDTH), BF16)]
        + [pltpu.VMEM((ATTN_KV_HEADS, tb + WINDOW, LANES), BF16) for _ in range(4)],
        compiler_params=_cparams("arbitrary"),
        name="attn_prompt",
    )(sinks, p, p, cos, sa, sb)


def _attn_sample_kernel(sink_ref, q_ref, kv_ref, ck_ref, cv_ref, cos_ref, sa_ref, sb_ref, bias_ref,
                        o_ref, knew_ref):
    s = q_ref.shape[0]
    cos, sa, sb = cos_ref[...], sa_ref[...], sb_ref[...]
    k = _rope(kv_ref[:, 0:LANES], cos, sa, sb)
    knew_ref[...] = k
    kk = jnp.concatenate([ck_ref[0], k], 0)
    vv = jnp.concatenate([cv_ref[0], kv_ref[:, LANES:2 * LANES]], 0)
    k_lo, k_hi, v_lo, v_hi = _kv_variants(kk, vv)
    sinks = _sink_columns(sink_ref, s)
    scale = ATTN_HEAD_DIM ** -0.5
    bias = bias_ref[...]
    qbs = [jnp.concatenate(
        [_rope(q_ref[:, (4 * j + a) * LANES:(4 * j + a + 1) * LANES], cos, sa, sb) * scale for a in range(4)], 0)
        for j in range(ATTN_KV_HEADS)]
    outs = _attn_core(qbs, k_lo, k_hi, v_lo, v_hi, bias, sinks, precise=True)
    for j, o in enumerate(outs):
        for a in range(4):
            o_ref[:, (4 * j + a) * LANES:(4 * j + a + 1) * LANES] = o[a * s:(a + 1) * s]


def _attn_sample(p, cache_k, cache_v, sinks, cos, sa, sb, bias, batch, s):
    lc = cache_k.shape[1]
    kv_blk = P_K // (2 * LANES)
    row = lambda b: (b, 0)
    const = lambda b: (0, 0)
    return pl.pallas_call(
        _attn_sample_kernel,
        grid=(batch,),
        in_specs=[
            pl.BlockSpec(memory_space=pltpu.SMEM),
            pl.BlockSpec((s, ATTN_WIDTH), row),
            pl.BlockSpec((s, 2 * LANES), lambda b: (b, kv_blk)),
            pl.BlockSpec((1, lc, LANES), lambda b: (b, 0, 0)),
            pl.BlockSpec((1, lc, LANES), lambda b: (b, 0, 0)),
            pl.BlockSpec((s, LANES), const),
            pl.BlockSpec((s, LANES), const),
            pl.BlockSpec((s, LANES), const),
            pl.BlockSpec((4 * s, lc + s), const),
        ],
        out_specs=[pl.BlockSpec((s, ATTN_WIDTH), row), pl.BlockSpec((s, LANES), row)],
        out_shape=[jax.ShapeDtypeStruct((batch * s, ATTN_WIDTH), F32),
                   jax.ShapeDtypeStruct((batch * s, LANES), F32)],
        compiler_params=_cparams("parallel"),
        name="attn_sample",
    )(sinks, p, p, cache_k, cache_v, cos, sa, sb, bias)


class _BlockDiag:
    def __init__(self, chunk, group):
        self.chunk, self.group = chunk, group
        n = chunk * group
        lane = lax.broadcasted_iota(jnp.int32, (chunk, n), 1)
        self.lane_block = lane // chunk
        self.eye = (lax.broadcasted_iota(jnp.int32, (chunk, n), 0) == lane % chunk).astype(F32)

    def wide(self, tall):
        c = self.chunk
        out = tall[0:c]
        for b in range(1, self.group):
            out = out + tall[b * c:(b + 1) * c]
        return out

    def expand(self, wide):
        if self.group == 1:
            return wide
        zero = jnp.zeros_like(wide)
        return jnp.concatenate([jnp.where(self.lane_block == b, wide, zero) for b in range(self.group)], 0)

    def rmul(self, lhs, wide):
        l_hi, l_lo = _split_bf16(lhs)
        w_hi, w_lo = _split_bf16(wide)
        m = lhs.shape[0]
        top = _dot(jnp.concatenate([l_hi, l_lo], 0), self.expand(w_hi))
        return top[:m] + top[m:] + _dot(l_hi, self.expand(w_lo))

    def lmul(self, wide, rhs):
        w_hi, w_lo = _split_bf16(wide)
        r_hi, r_lo = _split_bf16(rhs)
        n = self.chunk * self.group
        e_hi = self.expand(w_hi)
        top = _dot(jnp.concatenate([e_hi, self.expand(w_lo)], 0), r_hi)
        return top[:n] + top[n:] + _dot(e_hi, r_lo)

    def unit_lower_inverse(self, a_talls):
        c = self.chunk
        negs = [-self.wide(a) for a in a_talls]
        xs = [self.eye + neg for neg in negs]
        powers = [self.rmul(neg, neg) for neg in negs]
        iters = int(math.log2(c)) - 1
        for it in range(iters):
            last = it == iters - 1
            rs = [self.rmul(x if last else jnp.concatenate([x, p], 0), p) for x, p in zip(xs, powers)]
            xs = [x + r[:c] for x, r in zip(xs, rs)]
            if not last:
                powers = [r[c:] for r in rs]
        return xs


def _softplus(x):
    return jnp.maximum(x, 0.0) + jnp.log1p(jnp.exp(-jnp.abs(x)))


def _dn_prep_kernel(chunk, group, precise, qd_ref, kd_ref, vd_ref, ba_ref, hq_ref, hk_ref, hv_ref,
                    wq_ref, wk_ref, wv_ref, alog_ref, dtb_ref,
                    wv_out, wk_out, qdec_out, kend_out, p_out, gend_out):
    h = pl.program_id(1)
    tb = qd_ref.shape[0]
    n = chunk * group

    def conv_silu(x_ref, halo_ref, w_ref):
        xp = jnp.concatenate([halo_ref[0], x_ref[...]], 0)
        w = w_ref[...]
        y = xp[5:5 + tb] * w[0:1]
        for tap in range(1, CONV_WIDTH):
            y = y + xp[5 + tap:5 + tap + tb] * w[tap:tap + 1]
        return _silu(y)

    q = conv_silu(qd_ref, hq_ref, wq_ref)
    k = conv_silu(kd_ref, hk_ref, wk_ref)
    v = conv_silu(vd_ref, hv_ref, wv_ref)
    q = q * lax.rsqrt(jnp.sum(q * q, -1, keepdims=True) + EPS) * (DN_KEY_DIM ** -0.5)
    k = k * lax.rsqrt(jnp.sum(k * k, -1, keepdims=True) + EPS)

    ba = ba_ref[...]
    lane = lax.broadcasted_iota(jnp.int32, ba.shape, 1)
    beta_all = jax.nn.sigmoid(ba)
    g_all = -jnp.exp(alog_ref[...]) * _softplus(ba + dtb_ref[...])
    beta = jnp.sum(jnp.where(lane == h, beta_all, 0.0), -1, keepdims=True)
    g = jnp.sum(jnp.where(lane == h + DN_HEADS, g_all, 0.0), -1, keepdims=True)

    li = lax.broadcasted_iota(jnp.int32, (n, n), 0)
    mi = lax.broadcasted_iota(jnp.int32, (n, n), 1)
    same = (li // chunk) == (mi // chunk)
    upto = jnp.logical_and(same, li <= mi)
    since = jnp.logical_and(same, li >= mi)
    chunk_end = mi == (li // chunk) * chunk + (chunk - 1)
    blocks = _BlockDiag(chunk, group)
    op_dtype = wk_out.dtype
    slot = p_out.shape[1]

    row_groups = [slice(gi * n, (gi + 1) * n) for gi in range(tb // n)]
    a_mats, stash = [], []
    for rows in row_groups:
        qc, kc, bc, gc = q[rows], k[rows], beta[rows], g[rows]
        g_row = jnp.sum(jnp.where(upto, gc, 0.0), 0, keepdims=True)
        g_col = jnp.sum(jnp.where(li == mi, g_row, 0.0), 1, keepdims=True)
        g_end = jnp.sum(jnp.where(chunk_end, g_row, 0.0), 1, keepdims=True)
        decay = jnp.exp(jnp.where(since, g_col - g_row, -jnp.inf))
        qk_kk = _dot_any(jnp.concatenate([qc, kc], 0), kc, precise, _dot_nt)
        a_mats.append(jnp.where(li > mi, bc * decay * qk_kk[n:], 0.0))
        e_g = jnp.exp(g_col)
        qdec_out[rows, :] = (e_g * qc).astype(op_dtype)
        kend_out[rows, :] = (jnp.exp(g_end - g_col) * kc).astype(op_dtype)
        p_out[rows, 0:n] = (qk_kk[:n] * decay).astype(op_dtype)
        if slot > n:
            p_out[rows, n:] = jnp.zeros((n, slot - n), op_dtype)
        stash.append((e_g, g_end))
    t_invs = blocks.unit_lower_inverse(a_mats)
    for gi, (rows, t_inv, (e_g, g_end)) in enumerate(zip(row_groups, t_invs, stash)):
        kc, vc, bc = k[rows], v[rows], beta[rows]
        w = blocks.lmul(t_inv, jnp.concatenate([bc * vc, (bc * e_g) * kc], 1))
        wv_out[rows, :] = w[:, :DN_VAL_DIM]
        wk_out[rows, :] = w[:, DN_VAL_DIM:].astype(op_dtype)
        for c in range(group):
            last = c * chunk + chunk - 1
            gend_out[gi * group + c] = jnp.broadcast_to(jnp.exp(g_end[last:last + 1]), (1, LANES))


def _dn_score_slot(chunk, group):
    return LANES if chunk * group < LANES else max(chunk * group, V7X_MXU_DEPTH)


def _dn_prep(p, halo, w_conv8, alog_row, dtb_row, chunk, group, tb, precise):
    m = p.shape[0]
    op_dtype = F32 if precise else BF16
    slot = _dn_score_slot(chunk, group)
    nh = DN_HEADS
    cq, ck, cv = P_CONV // LANES, P_CONV // LANES + nh, P_CONV // LANES + 2 * nh
    col = lambda base: (lambda i, h: (i, base + h))
    halo_spec = lambda base: pl.BlockSpec((1, 8, LANES), lambda i, h: (i, 0, base + h))
    w_spec = lambda base: pl.BlockSpec((8, LANES), lambda i, h: (0, base + h))
    const = pl.BlockSpec((1, LANES), lambda i, h: (0, 0))
    head_blk = pl.BlockSpec((tb, LANES), lambda i, h: (i, h))
    out_shape = [
        jax.ShapeDtypeStruct((m, DN_WIDTH), F32),
        jax.ShapeDtypeStruct((m, DN_WIDTH), op_dtype),
        jax.ShapeDtypeStruct((m, DN_WIDTH), op_dtype),
        jax.ShapeDtypeStruct((m, DN_WIDTH), op_dtype),
        jax.ShapeDtypeStruct((m, DN_HEADS * slot), op_dtype),
        jax.ShapeDtypeStruct((m // chunk, 1, DN_WIDTH), F32),
    ]
    return pl.pallas_call(
        functools.partial(_dn_prep_kernel, chunk, group, precise),
        grid=(m // tb, nh),
        in_specs=[
            pl.BlockSpec((tb, LANES), col(cq)),
            pl.BlockSpec((tb, LANES), col(ck)),
            pl.BlockSpec((tb, LANES), col(cv)),
            pl.BlockSpec((tb, LANES), lambda i, h: (i, P_BA // LANES)),
            halo_spec(0), halo_spec(nh), halo_spec(2 * nh),
            w_spec(0), w_spec(nh), w_spec(2 * nh),
            const, const,
        ],
        out_specs=[head_blk] * 4 + [pl.BlockSpec((tb, slot), lambda i, h: (i, h)),
                                    pl.BlockSpec((tb // chunk, 1, LANES), lambda i, h: (i, 0, h))],
        out_shape=out_shape,
        compiler_params=_cparams("parallel", "parallel"),
        name="dn_prep",
    )(p, p, p, p, halo, halo, halo, w_conv8, w_conv8, w_conv8, alog_row, dtb_row)


def _dn_scan_kernel(chunk, group, n_chunks, wv_ref, wk_ref, qd_ref, ke_ref, p_ref, ge_ref, gate_ref, s0_ref,
                    onorm_ref, od_ref, sout_ref, s_scr, u_scr):
    n = pl.program_id(1)
    precise = wk_ref.dtype == F32
    slot = _dn_score_slot(chunk, group)

    @pl.when(n == 0)
    def _():
        s_scr[...] = s0_ref[0]
        u_scr[...] = jnp.zeros_like(u_scr)

    onorm = onorm_ref[...]
    for c in range(n_chunks):
        rows = slice(c * chunk, (c + 1) * chunk)
        group_rows = slice((c % group) * chunk, (c % group + 1) * chunk)
        for h in range(DN_HEADS):
            cols = slice(h * LANES, (h + 1) * LANES)
            s = s_scr[h]
            if not precise:
                s = s.astype(BF16)
            u = wv_ref[rows, cols] - _dot_any(wk_ref[rows, cols], s, precise)
            if not precise:
                u = u.astype(BF16)
            u_scr[h, group_rows, :] = u
            o = (_dot_any(qd_ref[rows, cols], s, precise)
                 + _dot_any(p_ref[rows, h * slot:(h + 1) * slot], u_scr[h], precise))
            s_scr[h] = ge_ref[c, :, cols] * s_scr[h] + _dot_any(ke_ref[rows, cols], u, precise, _dot_tn, 1)
            gate = gate_ref[rows, cols]
            od_ref[rows, cols] = (_rms(o, onorm) * _silu(gate)).astype(od_ref.dtype)

    @pl.when(n == pl.num_programs(1) - 1)
    def _():
        sout_ref[0] = s_scr[...]


def _dn_scan(prep, p, s0, onorm_row, chunk, group, n_chunks, batch):
    wv, wk, qdec, kend, pm, gend = prep
    m = wv.shape[0]
    assert n_chunks % group == 0
    rows = chunk * n_chunks
    steps = m // batch // rows
    blk = lambda b, n: (b * steps + n, 0)
    wide = pl.BlockSpec((rows, DN_WIDTH), blk)
    state = pl.BlockSpec((1, DN_HEADS, DN_KEY_DIM, DN_VAL_DIM), lambda b, n: (b, 0, 0, 0))
    slot = _dn_score_slot(chunk, group)
    assert pm.shape[1] == DN_HEADS * slot
    return pl.pallas_call(
        functools.partial(_dn_scan_kernel, chunk, group, n_chunks),
        grid=(batch, steps),
        in_specs=[
            wide, wide, wide, wide, pl.BlockSpec((rows, DN_HEADS * slot), blk),
            pl.BlockSpec((n_chunks, 1, DN_WIDTH), lambda b, n: (b * steps + n, 0, 0)),
            pl.BlockSpec((rows, DN_WIDTH), lambda b, n: (b * steps + n, P_GATE // DN_WIDTH)),
            state,
            pl.BlockSpec((1, LANES), lambda b, n: (0, 0)),
        ],
        out_specs=[wide, state],
        out_shape=[jax.ShapeDtypeStruct((m, DN_WIDTH), wk.dtype),
                   jax.ShapeDtypeStruct((batch, DN_HEADS, DN_KEY_DIM, DN_VAL_DIM), F32)],
        scratch_shapes=[pltpu.VMEM((DN_HEADS, DN_KEY_DIM, DN_VAL_DIM), F32),
                        pltpu.VMEM((DN_HEADS, slot, DN_VAL_DIM), wk.dtype)],
        compiler_params=_cparams("parallel", "arbitrary"),
        name="dn_scan",
    )(wv, wk, qdec, kend, pm, gend, p, s0, onorm_row)


def _out_proj_kernel(nw, attn_ref, od_ref, *refs):
    w_refs, (x_ref, g_ref, gate_ref, o_ref) = refs[:nw], refs[nw:]
    y = (_mm(attn_ref[...], tuple(r[0:ATTN_WIDTH, :] for r in w_refs))
         + _mm(od_ref[...], tuple(r[ATTN_WIDTH:, :] for r in w_refs)))
    o_ref[...] = x_ref[...] + gate_ref[...] * _rms(y, g_ref[...])


def _out_proj(attn, od, w, x, gain, gate, tm):
    m, d = x.shape
    row = lambda i: (i, 0)
    return pl.pallas_call(
        functools.partial(_out_proj_kernel, len(w)),
        grid=(m // tm,),
        in_specs=[
            pl.BlockSpec((tm, ATTN_WIDTH), row),
            pl.BlockSpec((tm, DN_WIDTH), row),
        ] + [pl.BlockSpec((ATTN_WIDTH + DN_WIDTH, d), lambda i: (0, 0))] * len(w) + [
            pl.BlockSpec((tm, d), row),
            pl.BlockSpec((1, d), lambda i: (0, 0)),
            _mod_spec(gate.shape[0], tm, d),
        ],
        out_specs=pl.BlockSpec((tm, d), row),
        out_shape=jax.ShapeDtypeStruct((m, d), F32),
        compiler_params=_cparams("parallel"),
        name="out_proj",
    )(attn, od, *w, x, gain, gate)


def _ffn_kernel(nw, x_ref, g_ref, sc_ref, sh_ref, *refs):
    wg_refs, wu_refs, wd_refs = refs[:nw], refs[nw:2 * nw], refs[2 * nw:3 * nw]
    g2_ref, gate_ref, o_ref, h_ref, acc_ref = refs[3 * nw:]
    j = pl.program_id(1)

    @pl.when(j == 0)
    def _():
        h = _rms(x_ref[...], g_ref[...]) * (1.0 + sc_ref[...]) + sh_ref[...]
        h_ref[...] = h.astype(h_ref.dtype)
        acc_ref[...] = jnp.zeros_like(acc_ref)

    h = h_ref[...]
    act = _silu(_mm(h, tuple(r[...] for r in wg_refs))) * _mm(h, tuple(r[...] for r in wu_refs))
    acc_ref[...] += _mm(act, tuple(r[...] for r in wd_refs))

    @pl.when(j == pl.num_programs(1) - 1)
    def _():
        o_ref[...] = x_ref[...] + gate_ref[...] * _rms(acc_ref[...], g2_ref[...])


def _ffn(x, gain, scale, shift, wg, wu, wd, gain2, gate, tm, tf):
    m, d = x.shape
    nw = len(wg)
    f = wg[0].shape[1]
    row = lambda i, j: (i, 0)
    vec = pl.BlockSpec((1, d), lambda i, j: (0, 0))
    return pl.pallas_call(
        functools.partial(_ffn_kernel, nw),
        grid=(m // tm, f // tf),
        in_specs=[
            pl.BlockSpec((tm, d), row), vec,
            _mod_spec(scale.shape[0], tm, d), _mod_spec(shift.shape[0], tm, d),
        ] + [pl.BlockSpec((d, tf), lambda i, j: (0, j))] * (2 * nw)
        + [pl.BlockSpec((tf, d), lambda i, j: (j, 0))] * nw
        + [vec, _mod_spec(gate.shape[0], tm, d)],
        out_specs=pl.BlockSpec((tm, d), row),
        out_shape=jax.ShapeDtypeStruct((m, d), F32),
        scratch_shapes=[pltpu.VMEM((tm, d), BF16 if wg[0].dtype == BF16 and nw == 1 else F32),
                        pltpu.VMEM((tm, d), F32)],
        compiler_params=_cparams("parallel", "arbitrary"),
        name="ffn_dense",
    )(x, gain, scale, shift, *wg, *wu, *wd, gain2, gate)


def _router_kernel(x_ref, g_ref, sc_ref, sh_ref, wr_ref, h_ref, gates_ref, idx_ref, w12_ref):
    h = _rms(x_ref[...], g_ref[...]) * (1.0 + sc_ref[...]) + sh_ref[...]
    h_ref[...] = h
    logits = _dot_x3(h, wr_ref[...])
    lane = lax.broadcasted_iota(jnp.int32, logits.shape, 1).astype(F32)
    logits = jnp.where(lane < N_EXPERTS, logits, -jnp.inf)
    m1 = jnp.max(logits, -1, keepdims=True)
    i1 = jnp.min(jnp.where(logits == m1, lane, float(LANES)), -1, keepdims=True)
    rest = jnp.where(lane == i1, -jnp.inf, logits)
    m2 = jnp.max(rest, -1, keepdims=True)
    i2 = jnp.min(jnp.where(rest == m2, lane, float(LANES)), -1, keepdims=True)
    t = jnp.exp(m2 - m1)
    w1 = 1.0 / (1.0 + t)
    w2 = t / (1.0 + t)
    gates_ref[...] = jnp.where(lane == i1, w1, 0.0) + jnp.where(lane == i2, w2, 0.0)
    idx_ref[...] = jnp.where(lane == 0.0, i1, jnp.where(lane == 1.0, i2, 0.0)).astype(jnp.int32)
    w12_ref[...] = jnp.where(lane == 0.0, w1, jnp.where(lane == 1.0, w2, 0.0))


def _router(x, gain, scale, shift, w_router_pad, tm):
    m, d = x.shape
    row = lambda i: (i, 0)
    vec = pl.BlockSpec((1, d), lambda i: (0, 0))
    small = pl.BlockSpec((tm, LANES), row)
    return pl.pallas_call(
        _router_kernel,
        grid=(m // tm,),
        in_specs=[pl.BlockSpec((tm, d), row), vec,
                  _mod_spec(scale.shape[0], tm, d), _mod_spec(shift.shape[0], tm, d),
                  pl.BlockSpec((d, LANES), lambda i: (0, 0))],
        out_specs=[pl.BlockSpec((tm, d), row), small, small, small],
        out_shape=[jax.ShapeDtypeStruct((m, d), F32), jax.ShapeDtypeStruct((m, LANES), F32),
                   jax.ShapeDtypeStruct((m, LANES), jnp.int32), jax.ShapeDtypeStruct((m, LANES), F32)],
        compiler_params=_cparams("parallel"),
        name="moe_router",
    )(x, gain, scale, shift, w_router_pad)


def _moe_gemm_kernel(nj, te_ref, tot_ref, rt_ref, h_hbm, wg_ref, wu_ref, wd_ref, ys_ref, xs_ref, xb_ref, acc_ref,
                     sems):
    r = pl.program_id(0)
    j = pl.program_id(1)
    tm = xb_ref.shape[0]
    total = tot_ref[0]
    active = r < total
    slot = r % 2
    share = -(-tm // nj)

    def row_copy(tile, t, s):
        return pltpu.make_async_copy(h_hbm.at[pl.ds(rt_ref[tile * tm + t], 1), :],
                                     xs_ref.at[s, pl.ds(t, 1), :], sems.at[s])

    def start_rows(tile, s, lo, hi):
        def body(t, carry):
            row_copy(tile, t, s).start()
            return carry

        lax.fori_loop(lo, hi, body, 0)

    @pl.when(jnp.logical_and(r == 0, j == 0))
    def _():
        start_rows(0, 0, 0, tm)

    @pl.when(jnp.logical_and(active, j == 0))
    def _():
        pltpu.make_async_copy(h_hbm.at[pl.ds(0, tm), :], xs_ref.at[slot], sems.at[slot]).wait()
        xb_ref[...] = xs_ref[slot].astype(BF16)
        acc_ref[...] = jnp.zeros_like(acc_ref)

    @pl.when(active)
    def _():
        more = r + 1 < total
        for u in range(share):
            t = j * share + u

            @pl.when(jnp.logical_and(more, t < tm))
            def _():
                row_copy(jnp.minimum(r + 1, pl.num_programs(0) - 1), jnp.minimum(t, tm - 1), 1 - slot).start()

        xb = xb_ref[...]
        act = (_silu(_dot(xb, wg_ref[0])) * _dot(xb, wu_ref[0])).astype(BF16)
        acc_ref[...] += _dot(act, wd_ref[0])

    @pl.when(j == pl.num_programs(1) - 1)
    def _():
        ys_ref[...] = jnp.where(active, acc_ref[...], 0.0)


def _moe_gemm(tile_expert, total_tiles, row_token, h, wg, wu, wd, tm, tf):
    n_tiles = tile_expert.shape[0]
    d = h.shape[1]
    f = wg.shape[2]
    nj = f // tf

    def w_col(r, j, te, tot, rt):
        return (te[r], 0, jnp.where(r < tot[0], j, nj - 1))

    def w_row(r, j, te, tot, rt):
        return (te[r], jnp.where(r < tot[0], j, nj - 1), 0)

    grid_spec = pltpu.PrefetchScalarGridSpec(
        num_scalar_prefetch=3,
        grid=(n_tiles, nj),
        in_specs=[
            pl.BlockSpec(memory_space=pl.ANY),
            pl.BlockSpec((1, d, tf), w_col),
            pl.BlockSpec((1, d, tf), w_col),
            pl.BlockSpec((1, tf, d), w_row),
        ],
        out_specs=pl.BlockSpec((tm, d), lambda r, j, te, tot, rt: (r, 0)),
        scratch_shapes=[pltpu.VMEM((2, tm, d), F32), pltpu.VMEM((tm, d), BF16), pltpu.VMEM((tm, d), F32),
                        pltpu.SemaphoreType.DMA((2,))],
    )
    return pl.pallas_call(
        functools.partial(_moe_gemm_kernel, nj),
        grid_spec=grid_spec,
        out_shape=jax.ShapeDtypeStruct((n_tiles * tm, d), F32),
        compiler_params=_cparams("arbitrary", "arbitrary", row_dma=True),
        name="moe_gemm",
    )(tile_expert, total_tiles, row_token, h, wg, wu, wd)


def _moe_combine_kernel(dest_ref, x_ref, w12_ref, g_ref, gate_ref, ys_hbm, o_ref, buf_ref, sems):
    i = pl.program_id(0)
    tb = x_ref.shape[0]
    slot = i % 2

    def row_copy(blk, t, k, s):
        src = dest_ref[2 * (blk * tb + t) + k]
        return pltpu.make_async_copy(ys_hbm.at[pl.ds(src, 1), :], buf_ref.at[s, k, pl.ds(t, 1), :], sems.at[s])

    def start_block(blk, s):
        def body(t, carry):
            row_copy(blk, t, 0, s).start()
            row_copy(blk, t, 1, s).start()
            return carry

        lax.fori_loop(0, tb, body, 0, unroll=8)

    @pl.when(i == 0)
    def _():
        start_block(0, 0)

    @pl.when(i + 1 < pl.num_programs(0))
    def _():
        start_block(i + 1, 1 - slot)

    for k in range(2):
        pltpu.make_async_copy(ys_hbm.at[pl.ds(0, tb), :], buf_ref.at[slot, k], sems.at[slot]).wait()
    w12 = w12_ref[...]
    y = w12[:, 0:1] * buf_ref[slot, 0] + w12[:, 1:2] * buf_ref[slot, 1]
    o_ref[...] = x_ref[...] + gate_ref[...] * _rms(y, g_ref[...])


def _moe_combine(dest, x, w12, gain, gate, ys, tb):
    m, d = x.shape
    row = lambda i, dst: (i, 0)
    grid_spec = pltpu.PrefetchScalarGridSpec(
        num_scalar_prefetch=1,
        grid=(m // tb,),
        in_specs=[
            pl.BlockSpec((tb, d), row),
            pl.BlockSpec((tb, LANES), row),
            pl.BlockSpec((1, d), lambda i, dst: (0, 0)),
            pl.BlockSpec((1, d), lambda i, dst: (0, 0)),
            pl.BlockSpec(memory_space=pl.ANY),
        ],
        out_specs=pl.BlockSpec((tb, d), row),
        scratch_shapes=[pltpu.VMEM((2, 2, tb, d), F32), pltpu.SemaphoreType.DMA((2,))],
    )
    return pl.pallas_call(
        _moe_combine_kernel,
        grid_spec=grid_spec,
        out_shape=jax.ShapeDtypeStruct((m, d), F32),
        compiler_params=_cparams("arbitrary", row_dma=True),
        name="moe_combine",
    )(dest, x, w12, gain, gate, ys)


def _route_tables(idx2, tm, n_tiles):
    m = idx2.shape[0]
    n_assign = 2 * m
    assert n_tiles * tm == n_assign + N_EXPERTS * tm
    experts = jnp.arange(N_EXPERTS, dtype=jnp.int32)
    e_flat = idx2.reshape(n_assign)
    onehot = (e_flat[:, None] == experts[None, :]).astype(jnp.int32)
    csum = jnp.cumsum(onehot, 0)
    counts = csum[-1]
    padded = ((counts + tm - 1) // tm) * tm
    pend = jnp.cumsum(padded)
    pstart = pend - padded
    dest = jnp.sum(onehot * (pstart[None, :] + csum - 1), -1).astype(jnp.int32)
    total_tiles = (pend[-1] // tm).astype(jnp.int32).reshape(1)
    tile_expert = jnp.minimum(
        jnp.searchsorted(pend // tm, jnp.arange(n_tiles, dtype=jnp.int32), side="right"), N_EXPERTS - 1
    ).astype(jnp.int32)
    filler_key = jnp.where(jnp.arange(tm, dtype=jnp.int32)[None, :] < (padded - counts)[:, None],
                           experts[:, None], N_EXPERTS).reshape(-1)
    keys = jnp.concatenate([e_flat, filler_key])
    tokens = jnp.concatenate([jnp.arange(n_assign, dtype=jnp.int32) // 2,
                              jnp.zeros((N_EXPERTS * tm,), jnp.int32)])
    _, row_token = lax.sort((keys, tokens), num_keys=1, is_stable=True)
    return tile_expert, total_tiles, row_token, dest


def _moe_dense_kernel(h_ref, gates_ref, wg_ref, wu_ref, wd_ref, x_ref, g_ref, gate_ref, o_ref, acc_ref, tot_ref):
    e = pl.program_id(0)
    j = pl.program_id(1)
    nj = pl.num_programs(1)

    @pl.when(jnp.logical_and(e == 0, j == 0))
    def _():
        tot_ref[...] = jnp.zeros_like(tot_ref)

    @pl.when(j == 0)
    def _():
        acc_ref[...] = jnp.zeros_like(acc_ref)

    h = h_ref[...].astype(BF16)
    act = (_silu(_dot(h, wg_ref[0])) * _dot(h, wu_ref[0])).astype(BF16)
    acc_ref[...] += _dot(act, wd_ref[0])

    @pl.when(j == nj - 1)
    def _():
        gates = gates_ref[...]
        lane = lax.broadcasted_iota(jnp.int32, gates.shape, 1)
        ge = jnp.sum(jnp.where(lane == e, gates, 0.0), -1, keepdims=True)
        tot_ref[...] += ge * acc_ref[...]

    @pl.when(jnp.logical_and(e == pl.num_programs(0) - 1, j == nj - 1))
    def _():
        o_ref[...] = x_ref[...] + gate_ref[...] * _rms(tot_ref[...], g_ref[...])


def _moe_dense(h, gates, wg, wu, wd, x, gain, gate, tf):
    m, d = x.shape
    f = wg.shape[2]
    full = pl.BlockSpec((m, d), lambda e, j: (0, 0))
    return pl.pallas_call(
        _moe_dense_kernel,
        grid=(N_EXPERTS, f // tf),
        in_specs=[
            full,
            pl.BlockSpec((m, LANES), lambda e, j: (0, 0)),
            pl.BlockSpec((1, d, tf), lambda e, j: (e, 0, j)),
            pl.BlockSpec((1, d, tf), lambda e, j: (e, 0, j)),
            pl.BlockSpec((1, tf, d), lambda e, j: (e, j, 0)),
            full,
            pl.BlockSpec((1, d), lambda e, j: (0, 0)),
            full,
        ],
        out_specs=full,
        out_shape=jax.ShapeDtypeStruct((m, d), F32),
        scratch_shapes=[pltpu.VMEM((m, d), F32), pltpu.VMEM((m, d), F32)],
        compiler_params=_cparams("arbitrary", "arbitrary"),
        name="moe_dense",
    )(h, gates, wg, wu, wd, x, gain, gate)


def _rope_tables(pos):
    half = ROPE_DIM // 2
    inv_freq = jnp.power(ROPE_THETA, -2.0 * jnp.arange(half, dtype=F32) / ROPE_DIM)
    ang = pos.astype(F32)[:, None] * inv_freq[None, :]
    cos, sin = jnp.cos(ang), jnp.sin(ang)
    t = pos.shape[0]
    rest = ATTN_HEAD_DIM - ROPE_DIM
    cos_h = jnp.concatenate([cos, cos, jnp.ones((t, rest), F32)], 1)
    sa_h = jnp.concatenate([-sin, jnp.zeros((t, half + rest), F32)], 1)
    sb_h = jnp.concatenate([jnp.zeros((t, half), F32), sin, jnp.zeros((t, rest), F32)], 1)
    rep = LANES // ATTN_HEAD_DIM
    return tuple(jnp.tile(a, (1, rep)) for a in (cos_h, sa_h, sb_h))


def _permute_w_in(w):
    o1 = ATTN_WIDTH
    o2 = o1 + KV_WIDTH
    o3 = o2 + KV_WIDTH
    o4 = o3 + DN_CONV_CH
    o5 = o4 + DN_WIDTH
    parts = [w[:, :o1], w[:, o4:o5], w[:, o3:o4], w[:, o1:o2], w[:, o2:o3], w[:, o5:]]
    used = sum(a.shape[1] for a in parts)
    parts.append(jnp.zeros((w.shape[0], P_WIDTH - used), w.dtype))
    return jnp.concatenate(parts, 1)


def _sample_mask_bias(s, lc):
    q_pos = PAST_LEN + np.arange(s)
    k_pos = np.concatenate([PAST_LEN - lc + np.arange(lc), q_pos])
    q_chunk = q_pos[:, None] // CHUNK
    k_chunk = k_pos[None, :] // CHUNK
    mask = (k_pos[None, :] >= 0) & (k_chunk <= q_chunk) & (k_pos[None, :] >= q_chunk * CHUNK - WINDOW)
    bias = np.where(mask, 0.0, -np.inf).astype(np.float32)
    return jnp.asarray(np.tile(bias, (4, 1)))


def _conv_halo(p, init, tb, seq):
    m = p.shape[0]
    batch = m // seq
    nb = seq // tb
    tails = p.reshape(batch, nb, tb, P_WIDTH)[:, :nb - 1, tb - (CONV_WIDTH - 1):, P_CONV:P_CONV + DN_CONV_CH]
    prev = jnp.concatenate([init[:, None], tails], 1)
    prev = prev.reshape(batch * nb, CONV_WIDTH - 1, DN_CONV_CH)
    return jnp.pad(prev, ((0, 0), (8 - (CONV_WIDTH - 1), 0), (0, 0)))


def _trunk(x, mods, layer_w, rope, past, cfg):
    m = x.shape[0]
    batch, seq = cfg["batch"], cfg["seq"]
    precise = cfg["precise"]
    nw = 2 if precise else 1
    ks, vs, ss, bufs = [], [], [], []
    cos, sa, sb = rope
    for l in range(DEPTH):
        w = layer_w[l]
        sh_a, sc_a, g_a, sh_f, sc_f, g_f = mods[l]
        p = _norm_proj(x, w["gain"][0], sc_a, sh_a, w["w_in"][:nw], cfg["tm_proj"], cfg["tn_proj"])
        if past is None:
            attn, k_new = _attn_prompt(p, w["sinks"], cos, sa, sb, cfg["tb_attn"])
            s0 = jnp.zeros((batch, DN_HEADS, DN_KEY_DIM, DN_VAL_DIM), F32)
            conv_init = jnp.zeros((batch, CONV_WIDTH - 1, DN_CONV_CH), F32)
        else:
            ck = past[0][l].reshape(batch, -1, KV_WIDTH)
            cv = past[1][l].reshape(batch, -1, KV_WIDTH)
            attn, k_new = _attn_sample(p, ck, cv, w["sinks"], cos, sa, sb, cfg["bias"], batch, seq)
            s0 = past[2][l]
            conv_init = past[3][l]
        halo = _conv_halo(p, conv_init, cfg["tb_dn"], seq)
        prep = _dn_prep(p, halo, w["w_conv"], w["alog"], w["dtb"], cfg["chunk"], cfg["group"], cfg["tb_dn"],
                        precise)
        od, s_new = _dn_scan(prep, p, s0, w["onorm"], cfg["chunk"], cfg["group"], cfg["scan_chunks"], batch)
        pick = 1 if precise else 0
        x = _out_proj(attn, od, w["w_out"][pick], x, w["gain"][1], g_a, cfg["tm_out"])
        if l % 2 == 0:
            x = _ffn(x, w["gain"][2], sc_f, sh_f, w["ffn_gate"][pick], w["ffn_up"][pick], w["ffn_down"][pick],
                     w["gain"][3], g_f, cfg["tm_ffn"], cfg["tf_ffn"])
        else:
            h, gates, idx, w12 = _router(x, w["gain"][2], sc_f, sh_f, w["router"], cfg["tm_router"])
            if cfg["routed"]:
                tm = cfg["tm_moe"]
                n_tiles = 2 * m // tm + N_EXPERTS
                tile_expert, total_tiles, row_token, dest = _route_tables(idx[:, :2], tm, n_tiles)
                ys = _moe_gemm(tile_expert, total_tiles, row_token, h, w["moe_gate"], w["moe_up"], w["moe_down"],
                               tm, cfg["tf_moe"])
                x = _moe_combine(dest, x, w12, w["gain"][3], g_f, ys, cfg["tb_combine"])
            else:
                x = _moe_dense(h, gates, w["moe_gate"], w["moe_up"], w["moe_down"], x, w["gain"][3], g_f,
                               cfg["tf_moe"])
        pb = p.reshape(batch, seq, P_WIDTH)
        keep = min(WINDOW, seq) if past is None else seq
        ks.append(k_new.reshape(batch, seq, ATTN_KV_HEADS, ATTN_HEAD_DIM)[:, seq - keep:])
        vs.append(pb[:, seq - keep:, P_V:P_V + KV_WIDTH].reshape(batch, keep, ATTN_KV_HEADS, ATTN_HEAD_DIM))
        ss.append(s_new)
        assert seq >= CONV_WIDTH - 1
        bufs.append(pb[:, seq - (CONV_WIDTH - 1):, P_CONV:P_CONV + DN_CONV_CH])
    return x, jnp.stack(ks), jnp.stack(vs), jnp.stack(ss), jnp.stack(bufs)


def kernel(x_prompt, x_sample, cache_attn_k, cache_attn_v, state_delta, state_conv, c_prompt, c_sample, w_in, w_conv, attn_sinks, dn_a_log, dn_dt_bias, dn_norm, w_out, w_mod, b_mod, norm_gains, ffn_gate, ffn_up, ffn_down, moe_router, moe_gate, moe_up, moe_down):
    bp, tp, d = x_prompt.shape
    bs, ts, _ = x_sample.shape
    assert bp == 1 and d == D_MODEL

    c_all = jnp.concatenate([c_prompt, c_sample, jnp.zeros((16 - bp - bs, d), F32)], 0)
    mod = _modulation(c_all, w_mod, b_mod)
    mods_p, mods_s = [], []
    for l in range(DEPTH):
        six = jnp.split(mod[l], 6, -1)
        mods_p.append([a[0:bp] for a in six])
        mods_s.append([jnp.repeat(a[bp:bp + bs], ts, axis=0) for a in six])

    def pad_lanes(v, at):
        return jnp.zeros((1, LANES), F32).at[0, at:at + v.shape[0]].set(v)

    def both(w):
        return ((w.astype(BF16),), (w,))

    layer_w = []
    for l in range(DEPTH):
        w = {
            "gain": [norm_gains[l, i].reshape(1, d) for i in range(4)],
            "w_in": _split_weight(_permute_w_in(w_in[l])),
            "sinks": attn_sinks[l],
            "w_conv": jnp.pad(w_conv[l], ((0, 8 - CONV_WIDTH), (0, 0))),
            "alog": pad_lanes(dn_a_log[l], DN_HEADS),
            "dtb": pad_lanes(dn_dt_bias[l], DN_HEADS),
            "onorm": dn_norm[l].reshape(1, DN_VAL_DIM),
            "w_out": both(w_out[l]),
        }
        if l % 2 == 0:
            w["ffn_gate"] = both(ffn_gate[l // 2])
            w["ffn_up"] = both(ffn_up[l // 2])
            w["ffn_down"] = both(ffn_down[l // 2])
        else:
            w["router"] = jnp.pad(moe_router[l // 2], ((0, 0), (0, LANES - N_EXPERTS)))
            w["moe_gate"] = moe_gate[l // 2].astype(BF16)
            w["moe_up"] = moe_up[l // 2].astype(BF16)
            w["moe_down"] = moe_down[l // 2].astype(BF16)
        layer_w.append(w)

    cfg_p = dict(batch=bp, seq=tp, precise=False, chunk=CHUNK, group=2, scan_chunks=4, tm_proj=1024, tn_proj=512, tb_attn=512,
                 tb_dn=512, tm_out=512, tm_ffn=512, tf_ffn=512, tm_router=512, routed=True, tm_moe=512,
                 tf_moe=256, tb_combine=256)
    rope_p = _rope_tables(jnp.arange(tp, dtype=jnp.int32))
    y_p, k_p, v_p, s_p, conv_p = _trunk(x_prompt.reshape(bp * tp, d), mods_p, layer_w, rope_p, None, cfg_p)

    ms = bs * ts
    cfg_s = dict(batch=bs, seq=ts, precise=True, chunk=ts, group=1, scan_chunks=1, tm_proj=ms, tn_proj=512, tb_dn=ts, tm_out=ms,
                 tm_ffn=ms, tf_ffn=512, tm_router=ms, routed=False, tf_moe=1408,
                 bias=_sample_mask_bias(ts, cache_attn_k.shape[2]))
    rope_s = _rope_tables(PAST_LEN + jnp.arange(ts, dtype=jnp.int32))
    past = (cache_attn_k, cache_attn_v, state_delta, state_conv)
    y_s, k_s, v_s, s_s, conv_s = _trunk(x_sample.reshape(ms, d), mods_s, layer_w, rope_s, past, cfg_s)

    return (y_p.reshape(bp, tp, d), y_s.reshape(bs, ts, d), k_p, v_p, s_p, conv_p, k_s, v_s, s_s, conv_s)
```

```python
import functools
import math

import numpy as np
import jax
import jax.numpy as jnp
from jax import lax
from jax.experimental import pallas as pl
from jax.experimental.pallas import tpu as pltpu

D_MODEL = 2048
DEPTH = 2
PAST_LEN = 1024
CHUNK = 64
ATTN_HEADS = 16
ATTN_KV_HEADS = 2
ATTN_HEAD_DIM = 64
ATTN_WIDTH = 1024
KV_WIDTH = 128
WINDOW = 128
ROPE_THETA = 500000.0
ROPE_DIM = 16
DN_HEADS = 8
DN_KEY_DIM = 128
DN_VAL_DIM = 128
DN_WIDTH = 1024
CONV_WIDTH = 4
DN_CONV_CH = 3072
D_FF = 5632
N_EXPERTS = 8
D_FF_EXPERT = 2816
EPS = 1e-6

F32 = jnp.float32
BF16 = jnp.bfloat16
LANES = 128
V7X_MXU_DEPTH = 256
V7X_VMEM_LIMIT = 56 * 1024 * 1024

P_Q = 0
P_GATE = 1024
P_CONV = 2048
P_K = 5120
P_V = 5248
P_BA = 5376
P_WIDTH = 5632


def _cparams(*sem, row_dma=False):
    return pltpu.CompilerParams(dimension_semantics=sem, vmem_limit_bytes=V7X_VMEM_LIMIT,
                                disable_bounds_checks=row_dma)


def _sigmoid(x):
    return 0.5 * jnp.tanh(0.5 * x) + 0.5


def _silu(x):
    return x * _sigmoid(x)


def _rms(x, gain):
    return x * lax.rsqrt(jnp.mean(x * x, -1, keepdims=True) + EPS) * gain


def _dot(a, b):
    return jnp.dot(a, b, preferred_element_type=F32)


def _dot_nt(a, b):
    return lax.dot_general(a, b, (((1,), (1,)), ((), ())), preferred_element_type=F32)


def _dot_tn(a, b):
    return lax.dot_general(a, b, (((0,), (0,)), ((), ())), preferred_element_type=F32)


def _split_bf16(a):
    hi = a.astype(BF16)
    lo = (a - hi.astype(F32)).astype(BF16)
    return hi, lo


def _dot_x3(a, b, dot=_dot, out_axis=0):
    a_hi, a_lo = _split_bf16(a)
    b_hi, b_lo = _split_bf16(b)
    n = a.shape[out_axis]
    top = dot(jnp.concatenate([a_hi, a_lo], out_axis), b_hi)
    return top[:n] + top[n:] + dot(a_hi, b_lo)


def _dot_any(a, b, precise, dot=_dot, out_axis=0):
    if precise:
        return _dot_x3(a, b, dot, out_axis)
    return dot(a.astype(BF16), b.astype(BF16))


def _mm(a, w):
    if len(w) == 1 and w[0].dtype == F32:
        return _dot_x3(a, w[0])
    if len(w) == 1:
        return _dot(a.astype(BF16), w[0])
    a_hi, a_lo = _split_bf16(a)
    n = a.shape[0]
    top = _dot(jnp.concatenate([a_hi, a_lo], 0), w[0])
    return top[:n] + top[n:] + _dot(a_hi, w[1])


def _split_weight(w):
    hi, lo = _split_bf16(w)
    return (hi, lo)


def _mod_spec(rows, tm, d):
    if rows == 1:
        return pl.BlockSpec((1, d), lambda i, *_: (0, 0))
    return pl.BlockSpec((tm, d), lambda i, *_: (i, 0))


def _mod_kernel(c_ref, w_ref, b_ref, o_ref):
    o_ref[0] = _dot_x3(_silu(c_ref[...]), w_ref[0]) + b_ref[0]


def _modulation(c_all, w_mod, b_mod):
    rows = c_all.shape[0]
    n = w_mod.shape[2]
    tn = 1024
    return pl.pallas_call(
        _mod_kernel,
        grid=(DEPTH, n // tn),
        in_specs=[
            pl.BlockSpec((rows, D_MODEL), lambda l, j: (0, 0)),
            pl.BlockSpec((1, D_MODEL, tn), lambda l, j: (l, 0, j)),
            pl.BlockSpec((1, 1, tn), lambda l, j: (l, 0, j)),
        ],
        out_specs=pl.BlockSpec((1, rows, tn), lambda l, j: (l, 0, j)),
        out_shape=jax.ShapeDtypeStruct((DEPTH, rows, n), F32),
        compiler_params=_cparams("parallel", "parallel"),
        name="modulation",
    )(c_all, w_mod, b_mod.reshape(DEPTH, 1, n))


def _norm_proj_kernel(nw, x_ref, g_ref, sc_ref, sh_ref, *refs):
    w_refs, (o_ref, h_ref) = refs[:nw], refs[nw:]

    @pl.when(pl.program_id(1) == 0)
    def _():
        h = _rms(x_ref[...], g_ref[...]) * (1.0 + sc_ref[...]) + sh_ref[...]
        h_ref[...] = h.astype(h_ref.dtype)

    o_ref[...] = _mm(h_ref[...], tuple(r[...] for r in w_refs))


def _norm_proj(x, gain, scale, shift, w, tm, tn):
    m, d = x.shape
    n = w[0].shape[1]
    return pl.pallas_call(
        functools.partial(_norm_proj_kernel, len(w)),
        grid=(m // tm, n // tn),
        in_specs=[
            pl.BlockSpec((tm, d), lambda i, j: (i, 0)),
            pl.BlockSpec((1, d), lambda i, j: (0, 0)),
            _mod_spec(scale.shape[0], tm, d),
            _mod_spec(shift.shape[0], tm, d),
        ] + [pl.BlockSpec((d, tn), lambda i, j: (0, j))] * len(w),
        out_specs=pl.BlockSpec((tm, tn), lambda i, j: (i, j)),
        out_shape=jax.ShapeDtypeStruct((m, n), F32),
        scratch_shapes=[pltpu.VMEM((tm, d), BF16 if len(w) == 1 else F32)],
        compiler_params=_cparams("parallel", "arbitrary"),
        name="norm_proj",
    )(x, gain, scale, shift, *w)


def _rope(x, cos, sa, sb):
    return x * cos + pltpu.roll(x, LANES - 8, 1) * sa + pltpu.roll(x, 8, 1) * sb


def _kv_variants(k, v):
    lo = lax.broadcasted_iota(jnp.int32, k.shape, 1) < ATTN_HEAD_DIM
    kr = pltpu.roll(k, ATTN_HEAD_DIM, 1)
    vr = pltpu.roll(v, ATTN_HEAD_DIM, 1)
    zero = jnp.zeros_like(k)
    k_lo = (jnp.where(lo, k, zero), jnp.where(lo, kr, zero))
    k_hi = (jnp.where(lo, zero, kr), jnp.where(lo, zero, k))
    v_lo = (jnp.where(lo, v, zero), jnp.where(lo, vr, zero))
    v_hi = (jnp.where(lo, zero, vr), jnp.where(lo, zero, v))
    return k_lo, k_hi, v_lo, v_hi


def _sink_softmax(s, sink):
    m = jnp.maximum(jnp.max(s, -1, keepdims=True), sink)
    p = jnp.exp(s - m)
    den = jnp.sum(p, -1, keepdims=True) + jnp.exp(sink - m)
    return p * (1.0 / den)


def _attn_core(qbs, k_los, k_his, v_los, v_his, bias, sinks, precise=False):
    scores = [(_dot_any(qb, k_lo, precise, _dot_nt) + bias, _dot_any(qb, k_hi, precise, _dot_nt) + bias)
              for qb, k_lo, k_hi in zip(qbs, k_los, k_his)]
    probs = [(_sink_softmax(s_even, sink[0]), _sink_softmax(s_odd, sink[1]))
             for (s_even, s_odd), sink in zip(scores, sinks)]
    return [_dot_any(p_even, v_lo, precise) + _dot_any(p_odd, v_hi, precise)
            for (p_even, p_odd), v_lo, v_hi in zip(probs, v_los, v_his)]


def _sink_columns(sink_ref, rows_per_pair):
    n = 4 * rows_per_pair
    pair = lax.broadcasted_iota(jnp.int32, (n, 1), 0) // rows_per_pair
    out = []
    for j in range(ATTN_KV_HEADS):
        cols = []
        for par in range(2):
            col = jnp.zeros((n, 1), F32)
            for a in range(4):
                col = jnp.where(pair == a, sink_ref[8 * j + 2 * a + par], col)
            cols.append(col)
        out.append(cols)
    return out


def _attn_prompt_kernel(sink_ref, q_ref, kv_ref, cos_ref, sa_ref, sb_ref, o_ref, knew_ref,
                        qs_ref, klo_ref, khi_ref, vlo_ref, vhi_ref):
    i = pl.program_id(0)
    tb = q_ref.shape[0]
    bufs = (klo_ref, khi_ref, vlo_ref, vhi_ref)

    @pl.when(i == 0)
    def _():
        for r in bufs:
            r[:, 0:WINDOW, :] = jnp.zeros((ATTN_KV_HEADS, WINDOW, LANES), BF16)

    @pl.when(i > 0)
    def _():
        for r in bufs:
            r[:, 0:WINDOW, :] = r[:, tb:tb + WINDOW, :]

    cos, sa, sb = cos_ref[...], sa_ref[...], sb_ref[...]
    k = _rope(kv_ref[:, 0:LANES], cos, sa, sb)
    knew_ref[...] = k
    variants = _kv_variants(k, kv_ref[:, LANES:2 * LANES])
    for r, var in zip(bufs, variants):
        for j in range(ATTN_KV_HEADS):
            r[j, WINDOW:, :] = var[j].astype(BF16)
    scale = ATTN_HEAD_DIM ** -0.5
    for a in range(ATTN_WIDTH // LANES):
        cols = slice(a * LANES, (a + 1) * LANES)
        qs_ref[:, cols] = (_rope(q_ref[:, cols], cos, sa, sb) * scale).astype(BF16)

    sinks = _sink_columns(sink_ref, CHUNK)
    nk = WINDOW + CHUNK

    def chunk_body(c, carry):
        r0 = pl.multiple_of(c * CHUNK, CHUNK)
        kpos = i * tb - WINDOW + r0 + lax.broadcasted_iota(jnp.int32, (1, nk), 1)
        bias = jnp.where(kpos >= 0, 0.0, -jnp.inf).astype(F32)
        heads = range(ATTN_KV_HEADS)
        qbs = [jnp.concatenate(
            [qs_ref[pl.ds(r0, CHUNK), (4 * j + a) * LANES:(4 * j + a + 1) * LANES] for a in range(4)], 0)
            for j in heads]
        keys = pl.ds(r0, nk)
        outs = _attn_core(qbs, [klo_ref[j, keys, :] for j in heads], [khi_ref[j, keys, :] for j in heads],
                          [vlo_ref[j, keys, :] for j in heads], [vhi_ref[j, keys, :] for j in heads], bias, sinks)
        for j, o in zip(heads, outs):
            for a in range(4):
                o_ref[pl.ds(r0, CHUNK), (4 * j + a) * LANES:(4 * j + a + 1) * LANES] = (
                    o[a * CHUNK:(a + 1) * CHUNK].astype(BF16))
        return carry

    lax.fori_loop(0, tb // CHUNK, chunk_body, 0, unroll=2)


def _attn_prompt(p, sinks, cos, sa, sb, tb):
    t = p.shape[0]
    kv_blk = P_K // (2 * LANES)
    row = lambda i: (i, 0)
    return pl.pallas_call(
        _attn_prompt_kernel,
        grid=(t // tb,),
        in_specs=[
            pl.BlockSpec(memory_space=pltpu.SMEM),
            pl.BlockSpec((tb, ATTN_WIDTH), row),
            pl.BlockSpec((tb, 2 * LANES), lambda i: (i, kv_blk)),
            pl.BlockSpec((tb, LANES), row),
            pl.BlockSpec((tb, LANES), row),
            pl.BlockSpec((tb, LANES), row),
        ],
        out_specs=[pl.BlockSpec((tb, ATTN_WIDTH), row), pl.BlockSpec((tb, LANES), row)],
        out_shape=[jax.ShapeDtypeStruct((t, ATTN_WIDTH), BF16), jax.ShapeDtypeStruct((t, LANES), F32)],
        scratch_shapes=[pltpu.VMEM((tb, ATTN_WIDTH), BF16)]
        + [pltpu.VMEM((ATTN_KV_HEADS, tb + WINDOW, LANES), BF16) for _ in range(4)],
        compiler_params=_cparams("arbitrary"),
        name="attn_prompt",
    )(sinks, p, p, cos, sa, sb)


def _attn_sample_kernel(sink_ref, q_ref, kv_ref, ck_ref, cv_ref, cos_ref, sa_ref, sb_ref, bias_ref,
                        o_ref, knew_ref):
    s = q_ref.shape[0]
    cos, sa, sb = cos_ref[...], sa_ref[...], sb_ref[...]
    k = _rope(kv_ref[:, 0:LANES], cos, sa, sb)
    knew_ref[...] = k
    kk = jnp.concatenate([ck_ref[0], k], 0)
    vv = jnp.concatenate([cv_ref[0], kv_ref[:, LANES:2 * LANES]], 0)
    k_lo, k_hi, v_lo, v_hi = _kv_variants(kk, vv)
    sinks = _sink_columns(sink_ref, s)
    scale = ATTN_HEAD_DIM ** -0.5
    bias = bias_ref[...]
    qbs = [jnp.concatenate(
        [_rope(q_ref[:, (4 * j + a) * LANES:(4 * j + a + 1) * LANES], cos, sa, sb) * scale for a in range(4)], 0)
        for j in range(ATTN_KV_HEADS)]
    outs = _attn_core(qbs, k_lo, k_hi, v_lo, v_hi, bias, sinks, precise=True)
    for j, o in enumerate(outs):
        for a in range(4):
            o_ref[:, (4 * j + a) * LANES:(4 * j + a + 1) * LANES] = o[a * s:(a + 1) * s]


def _attn_sample(p, cache_k, cache_v, sinks, cos, sa, sb, bias, batch, s):
    lc = cache_k.shape[1]
    kv_blk = P_K // (2 * LANES)
    row = lambda b: (b, 0)
    const = lambda b: (0, 0)
    return pl.pallas_call(
        _attn_sample_kernel,
        grid=(batch,),
        in_specs=[
            pl.BlockSpec(memory_space=pltpu.SMEM),
            pl.BlockSpec((s, ATTN_WIDTH), row),
            pl.BlockSpec((s, 2 * LANES), lambda b: (b, kv_blk)),
            pl.BlockSpec((1, lc, LANES), lambda b: (b, 0, 0)),
            pl.BlockSpec((1, lc, LANES), lambda b: (b, 0, 0)),
            pl.BlockSpec((s, LANES), const),
            pl.BlockSpec((s, LANES), const),
            pl.BlockSpec((s, LANES), const),
            pl.BlockSpec((4 * s, lc + s), const),
        ],
        out_specs=[pl.BlockSpec((s, ATTN_WIDTH), row), pl.BlockSpec((s, LANES), row)],
        out_shape=[jax.ShapeDtypeStruct((batch * s, ATTN_WIDTH), F32),
                   jax.ShapeDtypeStruct((batch * s, LANES), F32)],
        compiler_params=_cparams("parallel"),
        name="attn_sample",
    )(sinks, p, p, cache_k, cache_v, cos, sa, sb, bias)


class _BlockDiag:
    def __init__(self, chunk, group):
        self.chunk, self.group = chunk, group
        n = chunk * group
        lane = lax.broadcasted_iota(jnp.int32, (chunk, n), 1)
        self.lane_block = lane // chunk
        self.eye = (lax.broadcasted_iota(jnp.int32, (chunk, n), 0) == lane % chunk).astype(F32)

    def wide(self, tall):
        c = self.chunk
        out = tall[0:c]
        for b in range(1, self.group):
            out = out + tall[b * c:(b + 1) * c]
        return out

    def expand(self, wide):
        if self.group == 1:
            return wide
        zero = jnp.zeros_like(wide)
        return jnp.concatenate([jnp.where(self.lane_block == b, wide, zero) for b in range(self.group)], 0)

    def rmul(self, lhs, wide):
        l_hi, l_lo = _split_bf16(lhs)
        w_hi, w_lo = _split_bf16(wide)
        m = lhs.shape[0]
        top = _dot(jnp.concatenate([l_hi, l_lo], 0), self.expand(w_hi))
        return top[:m] + top[m:] + _dot(l_hi, self.expand(w_lo))

    def lmul(self, wide, rhs):
        w_hi, w_lo = _split_bf16(wide)
        r_hi, r_lo = _split_bf16(rhs)
        n = self.chunk * self.group
        e_hi = self.expand(w_hi)
        top = _dot(jnp.concatenate([e_hi, self.expand(w_lo)], 0), r_hi)
        return top[:n] + top[n:] + _dot(e_hi, r_lo)

    def unit_lower_inverse(self, a_talls):
        c = self.chunk
        negs = [-self.wide(a) for a in a_talls]
        xs = [self.eye + neg for neg in negs]
        powers = [self.rmul(neg, neg) for neg in negs]
        iters = int(math.log2(c)) - 1
        for it in range(iters):
            last = it == iters - 1
            rs = [self.rmul(x if last else jnp.concatenate([x, p], 0), p) for x, p in zip(xs, powers)]
            xs = [x + r[:c] for x, r in zip(xs, rs)]
            if not last:
                powers = [r[c:] for r in rs]
        return xs


def _softplus(x):
    return jnp.maximum(x, 0.0) + jnp.log1p(jnp.exp(-jnp.abs(x)))


def _dn_prep_kernel(chunk, group, precise, qd_ref, kd_ref, vd_ref, ba_ref, hq_ref, hk_ref, hv_ref,
                    wq_ref, wk_ref, wv_ref, alog_ref, dtb_ref,
                    wv_out, wk_out, qdec_out, kend_out, p_out, gend_out, beta_scr, g_scr):
    h = pl.program_id(1)
    tb = qd_ref.shape[0]
    n = chunk * group

    def conv_silu(x_ref, halo_ref, w_ref):
        xp = jnp.concatenate([halo_ref[0], x_ref[...]], 0)
        w = w_ref[...]
        y = xp[5:5 + tb] * w[0:1]
        for tap in range(1, CONV_WIDTH):
            y = y + xp[5 + tap:5 + tap + tb] * w[tap:tap + 1]
        return _silu(y)

    q = conv_silu(qd_ref, hq_ref, wq_ref)
    k = conv_silu(kd_ref, hk_ref, wk_ref)
    v = conv_silu(vd_ref, hv_ref, wv_ref)
    q = q * lax.rsqrt(jnp.sum(q * q, -1, keepdims=True) + EPS) * (DN_KEY_DIM ** -0.5)
    k = k * lax.rsqrt(jnp.sum(k * k, -1, keepdims=True) + EPS)

    @pl.when(h == 0)
    def _():
        ba = ba_ref[...]
        beta_scr[...] = _sigmoid(ba)
        g_scr[...] = -jnp.exp(alog_ref[...]) * _softplus(ba + dtb_ref[...])

    lane = lax.broadcasted_iota(jnp.int32, ba_ref.shape, 1)
    beta = jnp.sum(jnp.where(lane == h, beta_scr[...], 0.0), -1, keepdims=True)
    g = jnp.sum(jnp.where(lane == h + DN_HEADS, g_scr[...], 0.0), -1, keepdims=True)

    li = lax.broadcasted_iota(jnp.int32, (n, n), 0)
    mi = lax.broadcasted_iota(jnp.int32, (n, n), 1)
    same = (li // chunk) == (mi // chunk)
    upto = jnp.logical_and(same, li <= mi)
    since = jnp.logical_and(same, li >= mi)
    chunk_end = mi == (li // chunk) * chunk + (chunk - 1)
    blocks = _BlockDiag(chunk, group)
    op_dtype = wk_out.dtype
    slot = p_out.shape[1]

    row_groups = [slice(gi * n, (gi + 1) * n) for gi in range(tb // n)]
    a_mats, stash = [], []
    for rows in row_groups:
        qc, kc, bc, gc = q[rows], k[rows], beta[rows], g[rows]
        g_row = jnp.sum(jnp.where(upto, gc, 0.0), 0, keepdims=True)
        g_col = jnp.sum(jnp.where(li == mi, g_row, 0.0), 1, keepdims=True)
        g_end = jnp.sum(jnp.where(chunk_end, g_row, 0.0), 1, keepdims=True)
        decay = jnp.exp(jnp.where(since, g_col - g_row, -jnp.inf))
        qk_kk = _dot_any(jnp.concatenate([qc, kc], 0), kc, precise, _dot_nt)
        a_mats.append(jnp.where(li > mi, bc * decay * qk_kk[n:], 0.0))
        e_g = jnp.exp(g_col)
        qdec_out[rows, :] = (e_g * qc).astype(op_dtype)
        kend_out[rows, :] = (jnp.exp(g_end - g_col) * kc).astype(op_dtype)
        p_out[rows, 0:n] = (qk_kk[:n] * decay).astype(op_dtype)
        if slot > n:
            p_out[rows, n:] = jnp.zeros((n, slot - n), op_dtype)
        stash.append((e_g, g_end))
    t_invs = blocks.unit_lower_inverse(a_mats)
    for gi, (rows, t_inv, (e_g, g_end)) in enumerate(zip(row_groups, t_invs, stash)):
        kc, vc, bc = k[rows], v[rows], beta[rows]
        w = blocks.lmul(t_inv, jnp.concatenate([bc * vc, (bc * e_g) * kc], 1))
        wv_out[rows, :] = w[:, :DN_VAL_DIM]
        wk_out[rows, :] = w[:, DN_VAL_DIM:].astype(op_dtype)
        for c in range(group):
            last = c * chunk + chunk - 1
            gend_out[gi * group + c] = jnp.broadcast_to(jnp.exp(g_end[last:last + 1]), (1, LANES))


def _dn_score_slot(chunk, group):
    return LANES if chunk * group < LANES else max(chunk * group, V7X_MXU_DEPTH)


def _dn_prep(p, halo, w_conv8, alog_row, dtb_row, chunk, group, tb, precise):
    m = p.shape[0]
    op_dtype = F32 if precise else BF16
    slot = _dn_score_slot(chunk, group)
    nh = DN_HEADS
    cq, ck, cv = P_CONV // LANES, P_CONV // LANES + nh, P_CONV // LANES + 2 * nh
    col = lambda base: (lambda i, h: (i, base + h))
    halo_spec = lambda base: pl.BlockSpec((1, 8, LANES), lambda i, h: (i, 0, base + h))
    w_spec = lambda base: pl.BlockSpec((8, LANES), lambda i, h: (0, base + h))
    const = pl.BlockSpec((1, LANES), lambda i, h: (0, 0))
    head_blk = pl.BlockSpec((tb, LANES), lambda i, h: (i, h))
    out_shape = [
        jax.ShapeDtypeStruct((m, DN_WIDTH), F32),
        jax.ShapeDtypeStruct((m, DN_WIDTH), op_dtype),
        jax.ShapeDtypeStruct((m, DN_WIDTH), op_dtype),
        jax.ShapeDtypeStruct((m, DN_WIDTH), op_dtype),
        jax.ShapeDtypeStruct((m, DN_HEADS * slot), op_dtype),
        jax.ShapeDtypeStruct((m // chunk, 1, DN_WIDTH), F32),
    ]
    return pl.pallas_call(
        functools.partial(_dn_prep_kernel, chunk, group, precise),
        grid=(m // tb, nh),
        in_specs=[
            pl.BlockSpec((tb, LANES), col(cq)),
            pl.BlockSpec((tb, LANES), col(ck)),
            pl.BlockSpec((tb, LANES), col(cv)),
            pl.BlockSpec((tb, LANES), lambda i, h: (i, P_BA // LANES)),
            halo_spec(0), halo_spec(nh), halo_spec(2 * nh),
            w_spec(0), w_spec(nh), w_spec(2 * nh),
            const, const,
        ],
        out_specs=[head_blk] * 4 + [pl.BlockSpec((tb, slot), lambda i, h: (i, h)),
                                    pl.BlockSpec((tb // chunk, 1, LANES), lambda i, h: (i, 0, h))],
        out_shape=out_shape,
        scratch_shapes=[pltpu.VMEM((tb, LANES), F32)] * 2,
        compiler_params=_cparams("parallel", "arbitrary"),
        name="dn_prep",
    )(p, p, p, p, halo, halo, halo, w_conv8, w_conv8, w_conv8, alog_row, dtb_row)


def _dn_scan_kernel(chunk, group, n_chunks, wv_ref, wk_ref, qd_ref, ke_ref, p_ref, ge_ref, gate_ref, s0_ref,
                    onorm_ref, od_ref, sout_ref, s_scr, u_scr):
    n = pl.program_id(1)
    precise = wk_ref.dtype == F32
    slot = _dn_score_slot(chunk, group)

    @pl.when(n == 0)
    def _():
        s_scr[...] = s0_ref[0]
        u_scr[...] = jnp.zeros_like(u_scr)

    onorm = onorm_ref[...]
    for c in range(n_chunks):
        rows = slice(c * chunk, (c + 1) * chunk)
        group_rows = slice((c % group) * chunk, (c % group + 1) * chunk)
        for h in range(DN_HEADS):
            cols = slice(h * LANES, (h + 1) * LANES)
            s = s_scr[h]
            if not precise:
                s = s.astype(BF16)
            u = wv_ref[rows, cols] - _dot_any(wk_ref[rows, cols], s, precise)
            if not precise:
                u = u.astype(BF16)
            u_scr[h, group_rows, :] = u
            o = (_dot_any(qd_ref[rows, cols], s, precise)
                 + _dot_any(p_ref[rows, h * slot:(h + 1) * slot], u_scr[h], precise))
            s_scr[h] = ge_ref[c, :, cols] * s_scr[h] + _dot_any(ke_ref[rows, cols], u, precise, _dot_tn, 1)
            gate = gate_ref[rows, cols]
            od_ref[rows, cols] = (_rms(o, onorm) * _silu(gate)).astype(od_ref.dtype)

    @pl.when(n == pl.num_programs(1) - 1)
    def _():
        sout_ref[0] = s_scr[...]


def _dn_scan(prep, p, s0, onorm_row, chunk, group, n_chunks, batch):
    wv, wk, qdec, kend, pm, gend = prep
    m = wv.shape[0]
    assert n_chunks % group == 0
    rows = chunk * n_chunks
    steps = m // batch // rows
    blk = lambda b, n: (b * steps + n, 0)
    wide = pl.BlockSpec((rows, DN_WIDTH), blk)
    state = pl.BlockSpec((1, DN_HEADS, DN_KEY_DIM, DN_VAL_DIM), lambda b, n: (b, 0, 0, 0))
    slot = _dn_score_slot(chunk, group)
    assert pm.shape[1] == DN_HEADS * slot
    return pl.pallas_call(
        functools.partial(_dn_scan_kernel, chunk, group, n_chunks),
        grid=(batch, steps),
        in_specs=[
            wide, wide, wide, wide, pl.BlockSpec((rows, DN_HEADS * slot), blk),
            pl.BlockSpec((n_chunks, 1, DN_WIDTH), lambda b, n: (b * steps + n, 0, 0)),
            pl.BlockSpec((rows, DN_WIDTH), lambda b, n: (b * steps + n, P_GATE // DN_WIDTH)),
            state,
            pl.BlockSpec((1, LANES), lambda b, n: (0, 0)),
        ],
        out_specs=[wide, state],
        out_shape=[jax.ShapeDtypeStruct((m, DN_WIDTH), wk.dtype),
                   jax.ShapeDtypeStruct((batch, DN_HEADS, DN_KEY_DIM, DN_VAL_DIM), F32)],
        scratch_shapes=[pltpu.VMEM((DN_HEADS, DN_KEY_DIM, DN_VAL_DIM), F32),
                        pltpu.VMEM((DN_HEADS, slot, DN_VAL_DIM), wk.dtype)],
        compiler_params=_cparams("parallel", "arbitrary"),
        name="dn_scan",
    )(wv, wk, qdec, kend, pm, gend, p, s0, onorm_row)


def _out_proj_kernel(nw, attn_ref, od_ref, *refs):
    w_refs, (x_ref, g_ref, gate_ref, o_ref) = refs[:nw], refs[nw:]
    y = (_mm(attn_ref[...], tuple(r[0:ATTN_WIDTH, :] for r in w_refs))
         + _mm(od_ref[...], tuple(r[ATTN_WIDTH:, :] for r in w_refs)))
    o_ref[...] = x_ref[...] + gate_ref[...] * _rms(y, g_ref[...])


def _out_proj(attn, od, w, x, gain, gate, tm):
    m, d = x.shape
    row = lambda i: (i, 0)
    return pl.pallas_call(
        functools.partial(_out_proj_kernel, len(w)),
        grid=(m // tm,),
        in_specs=[
            pl.BlockSpec((tm, ATTN_WIDTH), row),
            pl.BlockSpec((tm, DN_WIDTH), row),
        ] + [pl.BlockSpec((ATTN_WIDTH + DN_WIDTH, d), lambda i: (0, 0))] * len(w) + [
            pl.BlockSpec((tm, d), row),
            pl.BlockSpec((1, d), lambda i: (0, 0)),
            _mod_spec(gate.shape[0], tm, d),
        ],
        out_specs=pl.BlockSpec((tm, d), row),
        out_shape=jax.ShapeDtypeStruct((m, d), F32),
        compiler_params=_cparams("parallel"),
        name="out_proj",
    )(attn, od, *w, x, gain, gate)


def _ffn_kernel(nw, x_ref, g_ref, sc_ref, sh_ref, *refs):
    wg_refs, wu_refs, wd_refs = refs[:nw], refs[nw:2 * nw], refs[2 * nw:3 * nw]
    g2_ref, gate_ref, o_ref, h_ref, acc_ref = refs[3 * nw:]
    j = pl.program_id(1)

    @pl.when(j == 0)
    def _():
        h = _rms(x_ref[...], g_ref[...]) * (1.0 + sc_ref[...]) + sh_ref[...]
        h_ref[...] = h.astype(h_ref.dtype)
        acc_ref[...] = jnp.zeros_like(acc_ref)

    h = h_ref[...]
    act = _silu(_mm(h, tuple(r[...] for r in wg_refs))) * _mm(h, tuple(r[...] for r in wu_refs))
    acc_ref[...] += _mm(act, tuple(r[...] for r in wd_refs))

    @pl.when(j == pl.num_programs(1) - 1)
    def _():
        o_ref[...] = x_ref[...] + gate_ref[...] * _rms(acc_ref[...], g2_ref[...])


def _ffn(x, gain, scale, shift, wg, wu, wd, gain2, gate, tm, tf):
    m, d = x.shape
    nw = len(wg)
    f = wg[0].shape[1]
    row = lambda i, j: (i, 0)
    vec = pl.BlockSpec((1, d), lambda i, j: (0, 0))
    return pl.pallas_call(
        functools.partial(_ffn_kernel, nw),
        grid=(m // tm, f // tf),
        in_specs=[
            pl.BlockSpec((tm, d), row), vec,
            _mod_spec(scale.shape[0], tm, d), _mod_spec(shift.shape[0], tm, d),
        ] + [pl.BlockSpec((d, tf), lambda i, j: (0, j))] * (2 * nw)
        + [pl.BlockSpec((tf, d), lambda i, j: (j, 0))] * nw
        + [vec, _mod_spec(gate.shape[0], tm, d)],
        out_specs=pl.BlockSpec((tm, d), row),
        out_shape=jax.ShapeDtypeStruct((m, d), F32),
        scratch_shapes=[pltpu.VMEM((tm, d), BF16 if wg[0].dtype == BF16 and nw == 1 else F32),
                        pltpu.VMEM((tm, d), F32)],
        compiler_params=_cparams("parallel", "arbitrary"),
        name="ffn_dense",
    )(x, gain, scale, shift, *wg, *wu, *wd, gain2, gate)


def _router_kernel(x_ref, g_ref, sc_ref, sh_ref, wr_ref, h_ref, gates_ref, idx_ref, w12_ref):
    h = _rms(x_ref[...], g_ref[...]) * (1.0 + sc_ref[...]) + sh_ref[...]
    h_ref[...] = h
    logits = _dot_x3(h, wr_ref[...])
    lane = lax.broadcasted_iota(jnp.int32, logits.shape, 1).astype(F32)
    logits = jnp.where(lane < N_EXPERTS, logits, -jnp.inf)
    m1 = jnp.max(logits, -1, keepdims=True)
    i1 = jnp.min(jnp.where(logits == m1, lane, float(LANES)), -1, keepdims=True)
    rest = jnp.where(lane == i1, -jnp.inf, logits)
    m2 = jnp.max(rest, -1, keepdims=True)
    i2 = jnp.min(jnp.where(rest == m2, lane, float(LANES)), -1, keepdims=True)
    t = jnp.exp(m2 - m1)
    w1 = 1.0 / (1.0 + t)
    w2 = t / (1.0 + t)
    gates_ref[...] = jnp.where(lane == i1, w1, 0.0) + jnp.where(lane == i2, w2, 0.0)
    idx_ref[...] = jnp.where(lane == 0.0, i1, jnp.where(lane == 1.0, i2, 0.0)).astype(jnp.int32)
    w12_ref[...] = jnp.where(lane == 0.0, w1, jnp.where(lane == 1.0, w2, 0.0))


def _router(x, gain, scale, shift, w_router_pad, tm):
    m, d = x.shape
    row = lambda i: (i, 0)
    vec = pl.BlockSpec((1, d), lambda i: (0, 0))
    small = pl.BlockSpec((tm, LANES), row)
    return pl.pallas_call(
        _router_kernel,
        grid=(m // tm,),
        in_specs=[pl.BlockSpec((tm, d), row), vec,
                  _mod_spec(scale.shape[0], tm, d), _mod_spec(shift.shape[0], tm, d),
                  pl.BlockSpec((d, LANES), lambda i: (0, 0))],
        out_specs=[pl.BlockSpec((tm, d), row), small, small, small],
        out_shape=[jax.ShapeDtypeStruct((m, d), F32), jax.ShapeDtypeStruct((m, LANES), F32),
                   jax.ShapeDtypeStruct((m, LANES), jnp.int32), jax.ShapeDtypeStruct((m, LANES), F32)],
        compiler_params=_cparams("parallel"),
        name="moe_router",
    )(x, gain, scale, shift, w_router_pad)


def _moe_gemm_kernel(nj, te_ref, tot_ref, rt_ref, h_hbm, wg_ref, wu_ref, wd_ref, ys_ref, xs_ref, xb_ref, acc_ref,
                     sems):
    r = pl.program_id(0)
    j = pl.program_id(1)
    tm = xb_ref.shape[0]
    total = tot_ref[0]
    active = r < total
    slot = r % 2
    share = -(-tm // nj)

    def row_copy(tile, t, s):
        return pltpu.make_async_copy(h_hbm.at[pl.ds(rt_ref[tile * tm + t], 1), :],
                                     xs_ref.at[s, pl.ds(t, 1), :], sems.at[s])

    def start_rows(tile, s, lo, hi):
        def body(t, carry):
            row_copy(tile, t, s).start()
            return carry

        lax.fori_loop(lo, hi, body, 0)

    @pl.when(jnp.logical_and(r == 0, j == 0))
    def _():
        start_rows(0, 0, 0, tm)

    @pl.when(jnp.logical_and(active, j == 0))
    def _():
        pltpu.make_async_copy(h_hbm.at[pl.ds(0, tm), :], xs_ref.at[slot], sems.at[slot]).wait()
        xb_ref[...] = xs_ref[slot].astype(BF16)
        acc_ref[...] = jnp.zeros_like(acc_ref)

    @pl.when(active)
    def _():
        more = r + 1 < total
        for u in range(share):
            t = j * share + u

            @pl.when(jnp.logical_and(more, t < tm))
            def _():
                row_copy(jnp.minimum(r + 1, pl.num_programs(0) - 1), jnp.minimum(t, tm - 1), 1 - slot).start()

        xb = xb_ref[...]
        act = (_silu(_dot(xb, wg_ref[0])) * _dot(xb, wu_ref[0])).astype(BF16)
        acc_ref[...] += _dot(act, wd_ref[0])

    @pl.when(j == pl.num_programs(1) - 1)
    def _():
        ys_ref[...] = jnp.where(active, acc_ref[...], 0.0)


def _moe_gemm(tile_expert, total_tiles, row_token, h, wg, wu, wd, tm, tf):
    n_tiles = tile_expert.shape[0]
    d = h.shape[1]
    f = wg.shape[2]
    nj = f // tf

    def w_col(r, j, te, tot, rt):
        return (te[r], 0, jnp.where(r < tot[0], j, nj - 1))

    def w_row(r, j, te, tot, rt):
        return (te[r], jnp.where(r < tot[0], j, nj - 1), 0)

    grid_spec = pltpu.PrefetchScalarGridSpec(
        num_scalar_prefetch=3,
        grid=(n_tiles, nj),
        in_specs=[
            pl.BlockSpec(memory_space=pl.ANY),
            pl.BlockSpec((1, d, tf), w_col),
            pl.BlockSpec((1, d, tf), w_col),
            pl.BlockSpec((1, tf, d), w_row),
        ],
        out_specs=pl.BlockSpec((tm, d), lambda r, j, te, tot, rt: (r, 0)),
        scratch_shapes=[pltpu.VMEM((2, tm, d), F32), pltpu.VMEM((tm, d), BF16), pltpu.VMEM((tm, d), F32),
                        pltpu.SemaphoreType.DMA((2,))],
    )
    return pl.pallas_call(
        functools.partial(_moe_gemm_kernel, nj),
        grid_spec=grid_spec,
        out_shape=jax.ShapeDtypeStruct((n_tiles * tm, d), F32),
        compiler_params=_cparams("arbitrary", "arbitrary", row_dma=True),
        name="moe_gemm",
    )(tile_expert, total_tiles, row_token, h, wg, wu, wd)


def _moe_combine_kernel(dest_ref, x_ref, w12_ref, g_ref, gate_ref, ys_hbm, o_ref, buf_ref, sems):
    i = pl.program_id(0)
    tb = x_ref.shape[0]
    slot = i % 2

    def row_copy(blk, t, k, s):
        src = dest_ref[2 * (blk * tb + t) + k]
        return pltpu.make_async_copy(ys_hbm.at[pl.ds(src, 1), :], buf_ref.at[s, k, pl.ds(t, 1), :], sems.at[s])

    def start_block(blk, s):
        def body(t, carry):
            row_copy(blk, t, 0, s).start()
            row_copy(blk, t, 1, s).start()
            return carry

        lax.fori_loop(0, tb, body, 0, unroll=8)

    @pl.when(i == 0)
    def _():
        start_block(0, 0)

    @pl.when(i + 1 < pl.num_programs(0))
    def _():
        start_block(i + 1, 1 - slot)

    for k in range(2):
        pltpu.make_async_copy(ys_hbm.at[pl.ds(0, tb), :], buf_ref.at[slot, k], sems.at[slot]).wait()
    w12 = w12_ref[...]
    y = w12[:, 0:1] * buf_ref[slot, 0] + w12[:, 1:2] * buf_ref[slot, 1]
    o_ref[...] = x_ref[...] + gate_ref[...] * _rms(y, g_ref[...])


def _moe_combine(dest, x, w12, gain, gate, ys, tb):
    m, d = x.shape
    row = lambda i, dst: (i, 0)
    grid_spec = pltpu.PrefetchScalarGridSpec(
        num_scalar_prefetch=1,
        grid=(m // tb,),
        in_specs=[
            pl.BlockSpec((tb, d), row),
            pl.BlockSpec((tb, LANES), row),
            pl.BlockSpec((1, d), lambda i, dst: (0, 0)),
            pl.BlockSpec((1, d), lambda i, dst: (0, 0)),
            pl.BlockSpec(memory_space=pl.ANY),
        ],
        out_specs=pl.BlockSpec((tb, d), row),
        scratch_shapes=[pltpu.VMEM((2, 2, tb, d), F32), pltpu.SemaphoreType.DMA((2,))],
    )
    return pl.pallas_call(
        _moe_combine_kernel,
        grid_spec=grid_spec,
        out_shape=jax.ShapeDtypeStruct((m, d), F32),
        compiler_params=_cparams("arbitrary", row_dma=True),
        name="moe_combine",
    )(dest, x, w12, gain, gate, ys)


def _route_tables(idx2, tm, n_tiles):
    m = idx2.shape[0]
    n_assign = 2 * m
    assert n_tiles * tm == n_assign + N_EXPERTS * tm
    experts = jnp.arange(N_EXPERTS, dtype=jnp.int32)
    e_flat = idx2.reshape(n_assign)
    onehot = (e_flat[:, None] == experts[None, :]).astype(jnp.int32)
    csum = jnp.cumsum(onehot, 0)
    counts = csum[-1]
    padded = ((counts + tm - 1) // tm) * tm
    pend = jnp.cumsum(padded)
    pstart = pend - padded
    dest = jnp.sum(onehot * (pstart[None, :] + csum - 1), -1).astype(jnp.int32)
    total_tiles = (pend[-1] // tm).astype(jnp.int32).reshape(1)
    tile_expert = jnp.minimum(
        jnp.searchsorted(pend // tm, jnp.arange(n_tiles, dtype=jnp.int32), side="right"), N_EXPERTS - 1
    ).astype(jnp.int32)
    filler_key = jnp.where(jnp.arange(tm, dtype=jnp.int32)[None, :] < (padded - counts)[:, None],
                           experts[:, None], N_EXPERTS).reshape(-1)
    keys = jnp.concatenate([e_flat, filler_key])
    tokens = jnp.concatenate([jnp.arange(n_assign, dtype=jnp.int32) // 2,
                              jnp.zeros((N_EXPERTS * tm,), jnp.int32)])
    _, row_token = lax.sort((keys, tokens), num_keys=1, is_stable=True)
    return tile_expert, total_tiles, row_token, dest


def _moe_dense_kernel(h_ref, gates_ref, wg_ref, wu_ref, wd_ref, x_ref, g_ref, gate_ref, o_ref, acc_ref, tot_ref):
    e = pl.program_id(0)
    j = pl.program_id(1)
    nj = pl.num_programs(1)

    @pl.when(jnp.logical_and(e == 0, j == 0))
    def _():
        tot_ref[...] = jnp.zeros_like(tot_ref)

    @pl.when(j == 0)
    def _():
        acc_ref[...] = jnp.zeros_like(acc_ref)

    h = h_ref[...].astype(BF16)
    act = (_silu(_dot(h, wg_ref[0])) * _dot(h, wu_ref[0])).astype(BF16)
    acc_ref[...] += _dot(act, wd_ref[0])

    @pl.when(j == nj - 1)
    def _():
        gates = gates_ref[...]
        lane = lax.broadcasted_iota(jnp.int32, gates.shape, 1)
        ge = jnp.sum(jnp.where(lane == e, gates, 0.0), -1, keepdims=True)
        tot_ref[...] += ge * acc_ref[...]

    @pl.when(jnp.logical_and(e == pl.num_programs(0) - 1, j == nj - 1))
    def _():
        o_ref[...] = x_ref[...] + gate_ref[...] * _rms(tot_ref[...], g_ref[...])


def _moe_dense(h, gates, wg, wu, wd, x, gain, gate, tf):
    m, d = x.shape
    f = wg.shape[2]
    full = pl.BlockSpec((m, d), lambda e, j: (0, 0))
    return pl.pallas_call(
        _moe_dense_kernel,
        grid=(N_EXPERTS, f // tf),
        in_specs=[
            full,
            pl.BlockSpec((m, LANES), lambda e, j: (0, 0)),
            pl.BlockSpec((1, d, tf), lambda e, j: (e, 0, j)),
            pl.BlockSpec((1, d, tf), lambda e, j: (e, 0, j)),
            pl.BlockSpec((1, tf, d), lambda e, j: (e, j, 0)),
            full,
            pl.BlockSpec((1, d), lambda e, j: (0, 0)),
            full,
        ],
        out_specs=full,
        out_shape=jax.ShapeDtypeStruct((m, d), F32),
        scratch_shapes=[pltpu.VMEM((m, d), F32), pltpu.VMEM((m, d), F32)],
        compiler_params=_cparams("arbitrary", "arbitrary"),
        name="moe_dense",
    )(h, gates, wg, wu, wd, x, gain, gate)


def _rope_tables(pos):
    half = ROPE_DIM // 2
    inv_freq = jnp.power(ROPE_THETA, -2.0 * jnp.arange(half, dtype=F32) / ROPE_DIM)
    ang = pos.astype(F32)[:, None] * inv_freq[None, :]
    cos, sin = jnp.cos(ang), jnp.sin(ang)
    t = pos.shape[0]
    rest = ATTN_HEAD_DIM - ROPE_DIM
    cos_h = jnp.concatenate([cos, cos, jnp.ones((t, rest), F32)], 1)
    sa_h = jnp.concatenate([-sin, jnp.zeros((t, half + rest), F32)], 1)
    sb_h = jnp.concatenate([jnp.zeros((t, half), F32), sin, jnp.zeros((t, rest), F32)], 1)
    rep = LANES // ATTN_HEAD_DIM
    return tuple(jnp.tile(a, (1, rep)) for a in (cos_h, sa_h, sb_h))


def _permute_w_in(w):
    o1 = ATTN_WIDTH
    o2 = o1 + KV_WIDTH
    o3 = o2 + KV_WIDTH
    o4 = o3 + DN_CONV_CH
    o5 = o4 + DN_WIDTH
    parts = [w[:, :o1], w[:, o4:o5], w[:, o3:o4], w[:, o1:o2], w[:, o2:o3], w[:, o5:]]
    used = sum(a.shape[1] for a in parts)
    parts.append(jnp.zeros((w.shape[0], P_WIDTH - used), w.dtype))
    return jnp.concatenate(parts, 1)


def _sample_mask_bias(s, lc):
    q_pos = PAST_LEN + np.arange(s)
    k_pos = np.concatenate([PAST_LEN - lc + np.arange(lc), q_pos])
    q_chunk = q_pos[:, None] // CHUNK
    k_chunk = k_pos[None, :] // CHUNK
    mask = (k_pos[None, :] >= 0) & (k_chunk <= q_chunk) & (k_pos[None, :] >= q_chunk * CHUNK - WINDOW)
    bias = np.where(mask, 0.0, -np.inf).astype(np.float32)
    return jnp.asarray(np.tile(bias, (4, 1)))


def _conv_halo(p, init, tb, seq):
    m = p.shape[0]
    batch = m // seq
    nb = seq // tb
    tails = p.reshape(batch, nb, tb, P_WIDTH)[:, :nb - 1, tb - (CONV_WIDTH - 1):, P_CONV:P_CONV + DN_CONV_CH]
    prev = jnp.concatenate([init[:, None], tails], 1)
    prev = prev.reshape(batch * nb, CONV_WIDTH - 1, DN_CONV_CH)
    return jnp.pad(prev, ((0, 0), (8 - (CONV_WIDTH - 1), 0), (0, 0)))


def _trunk(x, mods, layer_w, rope, past, cfg):
    m = x.shape[0]
    batch, seq = cfg["batch"], cfg["seq"]
    precise = cfg["precise"]
    nw = 2 if precise else 1
    ks, vs, ss, bufs = [], [], [], []
    cos, sa, sb = rope
    for l in range(DEPTH):
        w = layer_w[l]
        sh_a, sc_a, g_a, sh_f, sc_f, g_f = mods[l]
        p = _norm_proj(x, w["gain"][0], sc_a, sh_a, w["w_in"][:nw], cfg["tm_proj"], cfg["tn_proj"])
        if past is None:
            attn, k_new = _attn_prompt(p, w["sinks"], cos, sa, sb, cfg["tb_attn"])
            s0 = jnp.zeros((batch, DN_HEADS, DN_KEY_DIM, DN_VAL_DIM), F32)
            conv_init = jnp.zeros((batch, CONV_WIDTH - 1, DN_CONV_CH), F32)
        else:
            ck = past[0][l].reshape(batch, -1, KV_WIDTH)
            cv = past[1][l].reshape(batch, -1, KV_WIDTH)
            attn, k_new = _attn_sample(p, ck, cv, w["sinks"], cos, sa, sb, cfg["bias"], batch, seq)
            s0 = past[2][l]
            conv_init = past[3][l]
        halo = _conv_halo(p, conv_init, cfg["tb_dn"], seq)
        prep = _dn_prep(p, halo, w["w_conv"], w["alog"], w["dtb"], cfg["chunk"], cfg["group"], cfg["tb_dn"],
                        precise)
        od, s_new = _dn_scan(prep, p, s0, w["onorm"], cfg["chunk"], cfg["group"], cfg["scan_chunks"], batch)
        pick = 1 if precise else 0
        x = _out_proj(attn, od, w["w_out"][pick], x, w["gain"][1], g_a, cfg["tm_out"])
        if l % 2 == 0:
            x = _ffn(x, w["gain"][2], sc_f, sh_f, w["ffn_gate"][pick], w["ffn_up"][pick], w["ffn_down"][pick],
                     w["gain"][3], g_f, cfg["tm_ffn"], cfg["tf_ffn"])
        else:
            h, gates, idx, w12 = _router(x, w["gain"][2], sc_f, sh_f, w["router"], cfg["tm_router"])
            if cfg["routed"]:
                tm = cfg["tm_moe"]
                n_tiles = 2 * m // tm + N_EXPERTS
                tile_expert, total_tiles, row_token, dest = _route_tables(idx[:, :2], tm, n_tiles)
                ys = _moe_gemm(tile_expert, total_tiles, row_token, h, w["moe_gate"], w["moe_up"], w["moe_down"],
                               tm, cfg["tf_moe"])
                x = _moe_combine(dest, x, w12, w["gain"][3], g_f, ys, cfg["tb_combine"])
            else:
                x = _moe_dense(h, gates, w["moe_gate"], w["moe_up"], w["moe_down"], x, w["gain"][3], g_f,
                               cfg["tf_moe"])
        pb = p.reshape(batch, seq, P_WIDTH)
        keep = min(WINDOW, seq) if past is None else seq
        ks.append(k_new.reshape(batch, seq, ATTN_KV_HEADS, ATTN_HEAD_DIM)[:, seq - keep:])
        vs.append(pb[:, seq - keep:, P_V:P_V + KV_WIDTH].reshape(batch, keep, ATTN_KV_HEADS, ATTN_HEAD_DIM))
        ss.append(s_new)
        assert seq >= CONV_WIDTH - 1
        bufs.append(pb[:, seq - (CONV_WIDTH - 1):, P_CONV:P_CONV + DN_CONV_CH])
    return x, jnp.stack(ks), jnp.stack(vs), jnp.stack(ss), jnp.stack(bufs)


def kernel(x_prompt, x_sample, cache_attn_k, cache_attn_v, state_delta, state_conv, c_prompt, c_sample, w_in, w_conv, attn_sinks, dn_a_log, dn_dt_bias, dn_norm, w_out, w_mod, b_mod, norm_gains, ffn_gate, ffn_up, ffn_down, moe_router, moe_gate, moe_up, moe_down):
    bp, tp, d = x_prompt.shape
    bs, ts, _ = x_sample.shape
    assert bp == 1 and d == D_MODEL

    c_all = jnp.concatenate([c_prompt, c_sample, jnp.zeros((16 - bp - bs, d), F32)], 0)
    mod = _modulation(c_all, w_mod, b_mod)
    mods_p, mods_s = [], []
    for l in range(DEPTH):
        six = jnp.split(mod[l], 6, -1)
        mods_p.append([a[0:bp] for a in six])
        mods_s.append([jnp.repeat(a[bp:bp + bs], ts, axis=0) for a in six])

    def pad_lanes(v, at):
        return jnp.zeros((1, LANES), F32).at[0, at:at + v.shape[0]].set(v)

    def both(w):
        return ((w.astype(BF16),), (w,))

    layer_w = []
    for l in range(DEPTH):
        w = {
            "gain": [norm_gains[l, i].reshape(1, d) for i in range(4)],
            "w_in": _split_weight(_permute_w_in(w_in[l])),
            "sinks": attn_sinks[l],
            "w_conv": jnp.pad(w_conv[l], ((0, 8 - CONV_WIDTH), (0, 0))),
            "alog": pad_lanes(dn_a_log[l], DN_HEADS),
            "dtb": pad_lanes(dn_dt_bias[l], DN_HEADS),
            "onorm": dn_norm[l].reshape(1, DN_VAL_DIM),
            "w_out": both(w_out[l]),
        }
        if l % 2 == 0:
            w["ffn_gate"] = both(ffn_gate[l // 2])
            w["ffn_up"] = both(ffn_up[l // 2])
            w["ffn_down"] = both(ffn_down[l // 2])
        else:
            w["router"] = jnp.pad(moe_router[l // 2], ((0, 0), (0, LANES - N_EXPERTS)))
            w["moe_gate"] = moe_gate[l // 2].astype(BF16)
            w["moe_up"] = moe_up[l // 2].astype(BF16)
            w["moe_down"] = moe_down[l // 2].astype(BF16)
        layer_w.append(w)

    cfg_p = dict(batch=bp, seq=tp, precise=False, chunk=CHUNK, group=2, scan_chunks=4, tm_proj=1024, tn_proj=512, tb_attn=512,
                 tb_dn=512, tm_out=512, tm_ffn=512, tf_ffn=512, tm_router=512, routed=True, tm_moe=512,
                 tf_moe=256, tb_combine=256)
    rope_p = _rope_tables(jnp.arange(tp, dtype=jnp.int32))
    y_p, k_p, v_p, s_p, conv_p = _trunk(x_prompt.reshape(bp * tp, d), mods_p, layer_w, rope_p, None, cfg_p)

    ms = bs * ts
    cfg_s = dict(batch=bs, seq=ts, precise=True, chunk=ts, group=1, scan_chunks=1, tm_proj=ms, tn_proj=512, tb_dn=ts, tm_out=ms,
                 tm_ffn=ms, tf_ffn=512, tm_router=ms, routed=False, tf_moe=1408,
                 bias=_sample_mask_bias(ts, cache_attn_k.shape[2]))
    rope_s = _rope_tables(PAST_LEN + jnp.arange(ts, dtype=jnp.int32))
    past = (cache_attn_k, cache_attn_v, state_delta, state_conv)
    y_s, k_s, v_s, s_s, conv_s = _trunk(x_sample.reshape(ms, d), mods_s, layer_w, rope_s, past, cfg_s)

    return (y_p.reshape(bp, tp, d), y_s.reshape(bs, ts, d), k_p, v_p, s_p, conv_p, k_s, v_s, s_s, conv_s)
```

```python
import functools
import math

import numpy as np
import jax
import jax.numpy as jnp
from jax import lax
from jax.experimental import pallas as pl
from jax.experimental.pallas import tpu as pltpu

D_MODEL = 2048
DEPTH = 2
PAST_LEN = 1024
CHUNK = 64
ATTN_HEADS = 16
ATTN_KV_HEADS = 2
ATTN_HEAD_DIM = 64
ATTN_WIDTH = 1024
KV_WIDTH = 128
WINDOW = 128
ROPE_THETA = 500000.0
ROPE_DIM = 16
DN_HEADS = 8
DN_KEY_DIM = 128
DN_VAL_DIM = 128
DN_WIDTH = 1024
CONV_WIDTH = 4
DN_CONV_CH = 3072
D_FF = 5632
N_EXPERTS = 8
D_FF_EXPERT = 2816
EPS = 1e-6

F32 = jnp.float32
BF16 = jnp.bfloat16
LANES = 128
V7X_MXU_DEPTH = 256
V7X_VMEM_LIMIT = 56 * 1024 * 1024

P_Q = 0
P_GATE = 1024
P_CONV = 2048
P_K = 5120
P_V = 5248
P_BA = 5376
P_WIDTH = 5632


def _cparams(*sem, row_dma=False):
    return pltpu.CompilerParams(dimension_semantics=sem, vmem_limit_bytes=V7X_VMEM_LIMIT,
                                disable_bounds_checks=row_dma)


def _sigmoid(x):
    return 0.5 * jnp.tanh(0.5 * x) + 0.5


def _silu(x):
    return x * _sigmoid(x)


def _rms(x, gain):
    return x * lax.rsqrt(jnp.mean(x * x, -1, keepdims=True) + EPS) * gain


def _dot(a, b):
    return jnp.dot(a, b, preferred_element_type=F32)


def _dot_nt(a, b):
    return lax.dot_general(a, b, (((1,), (1,)), ((), ())), preferred_element_type=F32)


def _dot_tn(a, b):
    return lax.dot_general(a, b, (((0,), (0,)), ((), ())), preferred_element_type=F32)


def _split_bf16(a):
    hi = a.astype(BF16)
    lo = (a - hi.astype(F32)).astype(BF16)
    return hi, lo


def _dot_x3(a, b, dot=_dot, out_axis=0):
    a_hi, a_lo = _split_bf16(a)
    b_hi, b_lo = _split_bf16(b)
    n = a.shape[out_axis]
    top = dot(jnp.concatenate([a_hi, a_lo], out_axis), b_hi)
    return top[:n] + top[n:] + dot(a_hi, b_lo)


def _dot_any(a, b, precise, dot=_dot, out_axis=0):
    if precise:
        return _dot_x3(a, b, dot, out_axis)
    return dot(a.astype(BF16), b.astype(BF16))


def _mm(a, w):
    if len(w) == 1 and w[0].dtype == F32:
        return _dot_x3(a, w[0])
    if len(w) == 1:
        return _dot(a.astype(BF16), w[0])
    a_hi, a_lo = _split_bf16(a)
    n = a.shape[0]
    top = _dot(jnp.concatenate([a_hi, a_lo], 0), w[0])
    return top[:n] + top[n:] + _dot(a_hi, w[1])


def _split_weight(w):
    hi, lo = _split_bf16(w)
    return (hi, lo)


def _mod_spec(rows, tm, d):
    if rows == 1:
        return pl.BlockSpec((1, d), lambda i, *_: (0, 0))
    return pl.BlockSpec((tm, d), lambda i, *_: (i, 0))


def _mod_kernel(c_ref, w_ref, b_ref, o_ref):
    o_ref[0] = _dot_x3(_silu(c_ref[...]), w_ref[0]) + b_ref[0]


def _modulation(c_all, w_mod, b_mod):
    rows = c_all.shape[0]
    n = w_mod.shape[2]
    tn = 1024
    return pl.pallas_call(
        _mod_kernel,
        grid=(DEPTH, n // tn),
        in_specs=[
            pl.BlockSpec((rows, D_MODEL), lambda l, j: (0, 0)),
            pl.BlockSpec((1, D_MODEL, tn), lambda l, j: (l, 0, j)),
            pl.BlockSpec((1, 1, tn), lambda l, j: (l, 0, j)),
        ],
        out_specs=pl.BlockSpec((1, rows, tn), lambda l, j: (l, 0, j)),
        out_shape=jax.ShapeDtypeStruct((DEPTH, rows, n), F32),
        compiler_params=_cparams("parallel", "parallel"),
        name="modulation",
    )(c_all, w_mod, b_mod.reshape(DEPTH, 1, n))


def _norm_proj_kernel(nw, x_ref, g_ref, sc_ref, sh_ref, *refs):
    w_refs, (o_ref, h_ref) = refs[:nw], refs[nw:]

    @pl.when(pl.program_id(1) == 0)
    def _():
        h = _rms(x_ref[...], g_ref[...]) * (1.0 + sc_ref[...]) + sh_ref[...]
        h_ref[...] = h.astype(h_ref.dtype)

    o_ref[...] = _mm(h_ref[...], tuple(r[...] for r in w_refs))


def _norm_proj(x, gain, scale, shift, w, tm, tn):
    m, d = x.shape
    n = w[0].shape[1]
    return pl.pallas_call(
        functools.partial(_norm_proj_kernel, len(w)),
        grid=(m // tm, n // tn),
        in_specs=[
            pl.BlockSpec((tm, d), lambda i, j: (i, 0)),
            pl.BlockSpec((1, d), lambda i, j: (0, 0)),
            _mod_spec(scale.shape[0], tm, d),
            _mod_spec(shift.shape[0], tm, d),
        ] + [pl.BlockSpec((d, tn), lambda i, j: (0, j))] * len(w),
        out_specs=pl.BlockSpec((tm, tn), lambda i, j: (i, j)),
        out_shape=jax.ShapeDtypeStruct((m, n), F32),
        scratch_shapes=[pltpu.VMEM((tm, d), BF16 if len(w) == 1 else F32)],
        compiler_params=_cparams("parallel", "arbitrary"),
        name="norm_proj",
    )(x, gain, scale, shift, *w)


def _rope(x, cos, sa, sb):
    return x * cos + pltpu.roll(x, LANES - 8, 1) * sa + pltpu.roll(x, 8, 1) * sb


def _kv_variants(k, v):
    lo = lax.broadcasted_iota(jnp.int32, k.shape, 1) < ATTN_HEAD_DIM
    kr = pltpu.roll(k, ATTN_HEAD_DIM, 1)
    vr = pltpu.roll(v, ATTN_HEAD_DIM, 1)
    zero = jnp.zeros_like(k)
    k_lo = (jnp.where(lo, k, zero), jnp.where(lo, kr, zero))
    k_hi = (jnp.where(lo, zero, kr), jnp.where(lo, zero, k))
    v_lo = (jnp.where(lo, v, zero), jnp.where(lo, vr, zero))
    v_hi = (jnp.where(lo, zero, vr), jnp.where(lo, zero, v))
    return k_lo, k_hi, v_lo, v_hi


def _sink_softmax(s, sink):
    m = jnp.maximum(jnp.max(s, -1, keepdims=True), sink)
    p = jnp.exp(s - m)
    den = jnp.sum(p, -1, keepdims=True) + jnp.exp(sink - m)
    return p * (1.0 / den)


def _attn_core(qbs, k_los, k_his, v_los, v_his, bias, sinks, precise=False):
    scores = [(_dot_any(qb, k_lo, precise, _dot_nt) + bias, _dot_any(qb, k_hi, precise, _dot_nt) + bias)
              for qb, k_lo, k_hi in zip(qbs, k_los, k_his)]
    probs = [(_sink_softmax(s_even, sink[0]), _sink_softmax(s_odd, sink[1]))
             for (s_even, s_odd), sink in zip(scores, sinks)]
    return [_dot_any(p_even, v_lo, precise) + _dot_any(p_odd, v_hi, precise)
            for (p_even, p_odd), v_lo, v_hi in zip(probs, v_los, v_his)]


def _sink_columns(sink_ref, rows_per_pair):
    n = 4 * rows_per_pair
    pair = lax.broadcasted_iota(jnp.int32, (n, 1), 0) // rows_per_pair
    out = []
    for j in range(ATTN_KV_HEADS):
        cols = []
        for par in range(2):
            col = jnp.zeros((n, 1), F32)
            for a in range(4):
                col = jnp.where(pair == a, sink_ref[8 * j + 2 * a + par], col)
            cols.append(col)
        out.append(cols)
    return out


def _attn_prompt_kernel(sink_ref, q_ref, kv_ref, cos_ref, sa_ref, sb_ref, o_ref, knew_ref,
                        qs_ref, klo_ref, khi_ref, vlo_ref, vhi_ref):
    i = pl.program_id(0)
    tb = q_ref.shape[0]
    bufs = (klo_ref, khi_ref, vlo_ref, vhi_ref)

    @pl.when(i == 0)
    def _():
        for r in bufs:
            r[:, 0:WINDOW, :] = jnp.zeros((ATTN_KV_HEADS, WINDOW, LANES), BF16)

    @pl.when(i > 0)
    def _():
        for r in bufs:
            r[:, 0:WINDOW, :] = r[:, tb:tb + WINDOW, :]

    cos, sa, sb = cos_ref[...], sa_ref[...], sb_ref[...]
    k = _rope(kv_ref[:, 0:LANES], cos, sa, sb)
    knew_ref[...] = k
    variants = _kv_variants(k, kv_ref[:, LANES:2 * LANES])
    for r, var in zip(bufs, variants):
        for j in range(ATTN_KV_HEADS):
            r[j, WINDOW:, :] = var[j].astype(BF16)
    scale = ATTN_HEAD_DIM ** -0.5
    for a in range(ATTN_WIDTH // LANES):
        cols = slice(a * LANES, (a + 1) * LANES)
        qs_ref[:, cols] = (_rope(q_ref[:, cols], cos, sa, sb) * scale).astype(BF16)

    sinks = _sink_columns(sink_ref, CHUNK)
    nk = WINDOW + CHUNK

    def chunk_body(c, carry):
        r0 = pl.multiple_of(c * CHUNK, CHUNK)
        kpos = i * tb - WINDOW + r0 + lax.broadcasted_iota(jnp.int32, (1, nk), 1)
        bias = jnp.where(kpos >= 0, 0.0, -jnp.inf).astype(F32)
        heads = range(ATTN_KV_HEADS)
        qbs = [jnp.concatenate(
            [qs_ref[pl.ds(r0, CHUNK), (4 * j + a) * LANES:(4 * j + a + 1) * LANES] for a in range(4)], 0)
            for j in heads]
        keys = pl.ds(r0, nk)
        outs = _attn_core(qbs, [klo_ref[j, keys, :] for j in heads], [khi_ref[j, keys, :] for j in heads],
                          [vlo_ref[j, keys, :] for j in heads], [vhi_ref[j, keys, :] for j in heads], bias, sinks)
        for j, o in zip(heads, outs):
            for a in range(4):
                o_ref[pl.ds(r0, CHUNK), (4 * j + a) * LANES:(4 * j + a + 1) * LANES] = (
                    o[a * CHUNK:(a + 1) * CHUNK].astype(BF16))
        return carry

    lax.fori_loop(0, tb // CHUNK, chunk_body, 0, unroll=2)


def _attn_prompt(p, sinks, cos, sa, sb, tb):
    t = p.shape[0]
    kv_blk = P_K // (2 * LANES)
    row = lambda i: (i, 0)
    return pl.pallas_call(
        _attn_prompt_kernel,
        grid=(t // tb,),
        in_specs=[
            pl.BlockSpec(memory_space=pltpu.SMEM),
            pl.BlockSpec((tb, ATTN_WIDTH), row),
            pl.BlockSpec((tb, 2 * LANES), lambda i: (i, kv_blk)),
            pl.BlockSpec((tb, LANES), row),
            pl.BlockSpec((tb, LANES), row),
            pl.BlockSpec((tb, LANES), row),
        ],
        out_specs=[pl.BlockSpec((tb, ATTN_WIDTH), row), pl.BlockSpec((tb, LANES), row)],
        out_shape=[jax.ShapeDtypeStruct((t, ATTN_WIDTH), BF16), jax.ShapeDtypeStruct((t, LANES), F32)],
        scratch_shapes=[pltpu.VMEM((tb, ATTN_WIDTH), BF16)]
        + [pltpu.VMEM((ATTN_KV_HEADS, tb + WINDOW, LANES), BF16) for _ in range(4)],
        compiler_params=_cparams("arbitrary"),
        name="attn_prompt",
    )(sinks, p, p, cos, sa, sb)


def _attn_sample_kernel(sink_ref, q_ref, kv_ref, ck_ref, cv_ref, cos_ref, sa_ref, sb_ref, bias_ref,
                        o_ref, knew_ref):
    s = q_ref.shape[0]
    cos, sa, sb = cos_ref[...], sa_ref[...], sb_ref[...]
    k = _rope(kv_ref[:, 0:LANES], cos, sa, sb)
    knew_ref[...] = k
    kk = jnp.concatenate([ck_ref[0], k], 0)
    vv = jnp.concatenate([cv_ref[0], kv_ref[:, LANES:2 * LANES]], 0)
    k_lo, k_hi, v_lo, v_hi = _kv_variants(kk, vv)
    sinks = _sink_columns(sink_ref, s)
    scale = ATTN_HEAD_DIM ** -0.5
    bias = bias_ref[...]
    qbs = [jnp.concatenate(
        [_rope(q_ref[:, (4 * j + a) * LANES:(4 * j + a + 1) * LANES], cos, sa, sb) * scale for a in range(4)], 0)
        for j in range(ATTN_KV_HEADS)]
    outs = _attn_core(qbs, k_lo, k_hi, v_lo, v_hi, bias, sinks, precise=True)
    for j, o in enumerate(outs):
        for a in range(4):
            o_ref[:, (4 * j + a) * LANES:(4 * j + a + 1) * LANES] = o[a * s:(a + 1) * s]


def _attn_sample(p, cache_k, cache_v, sinks, cos, sa, sb, bias, batch, s):
    lc = cache_k.shape[1]
    kv_blk = P_K // (2 * LANES)
    row = lambda b: (b, 0)
    const = lambda b: (0, 0)
    return pl.pallas_call(
        _attn_sample_kernel,
        grid=(batch,),
        in_specs=[
            pl.BlockSpec(memory_space=pltpu.SMEM),
            pl.BlockSpec((s, ATTN_WIDTH), row),
            pl.BlockSpec((s, 2 * LANES), lambda b: (b, kv_blk)),
            pl.BlockSpec((1, lc, LANES), lambda b: (b, 0, 0)),
            pl.BlockSpec((1, lc, LANES), lambda b: (b, 0, 0)),
            pl.BlockSpec((s, LANES), const),
            pl.BlockSpec((s, LANES), const),
            pl.BlockSpec((s, LANES), const),
            pl.BlockSpec((4 * s, lc + s), const),
        ],
        out_specs=[pl.BlockSpec((s, ATTN_WIDTH), row), pl.BlockSpec((s, LANES), row)],
        out_shape=[jax.ShapeDtypeStruct((batch * s, ATTN_WIDTH), F32),
                   jax.ShapeDtypeStruct((batch * s, LANES), F32)],
        compiler_params=_cparams("parallel"),
        name="attn_sample",
    )(sinks, p, p, cache_k, cache_v, cos, sa, sb, bias)


class _BlockDiag:
    def __init__(self, chunk, group):
        self.chunk, self.group = chunk, group
        n = chunk * group
        lane = lax.broadcasted_iota(jnp.int32, (chunk, n), 1)
        self.lane_block = lane // chunk
        self.eye = (lax.broadcasted_iota(jnp.int32, (chunk, n), 0) == lane % chunk).astype(F32)

    def wide(self, tall):
        c = self.chunk
        out = tall[0:c]
        for b in range(1, self.group):
            out = out + tall[b * c:(b + 1) * c]
        return out

    def expand(self, wide):
        if self.group == 1:
            return wide
        zero = jnp.zeros_like(wide)
        return jnp.concatenate([jnp.where(self.lane_block == b, wide, zero) for b in range(self.group)], 0)

    def rmul(self, lhs, wide):
        l_hi, l_lo = _split_bf16(lhs)
        w_hi, w_lo = _split_bf16(wide)
        m = lhs.shape[0]
        top = _dot(jnp.concatenate([l_hi, l_lo], 0), self.expand(w_hi))
        return top[:m] + top[m:] + _dot(l_hi, self.expand(w_lo))

    def lmul(self, wide, rhs):
        w_hi, w_lo = _split_bf16(wide)
        r_hi, r_lo = _split_bf16(rhs)
        n = self.chunk * self.group
        e_hi = self.expand(w_hi)
        top = _dot(jnp.concatenate([e_hi, self.expand(w_lo)], 0), r_hi)
        return top[:n] + top[n:] + _dot(e_hi, r_lo)

    def unit_lower_inverse(self, a_talls):
        c = self.chunk
        negs = [-self.wide(a) for a in a_talls]
        xs = [self.eye + neg for neg in negs]
        powers = [self.rmul(neg, neg) for neg in negs]
        iters = int(math.log2(c)) - 1
        for it in range(iters):
            last = it == iters - 1
            rs = [self.rmul(x if last else jnp.concatenate([x, p], 0), p) for x, p in zip(xs, powers)]
            xs = [x + r[:c] for x, r in zip(xs, rs)]
            if not last:
                powers = [r[c:] for r in rs]
        return xs


def _softplus(x):
    return jnp.maximum(x, 0.0) + jnp.log1p(jnp.exp(-jnp.abs(x)))


def _dn_prep_kernel(chunk, group, precise, qd_ref, kd_ref, vd_ref, ba_ref, hq_ref, hk_ref, hv_ref,
                    wq_ref, wk_ref, wv_ref, alog_ref, dtb_ref,
                    wv_out, wk_out, qdec_out, kend_out, p_out, gend_out):
    h = pl.program_id(1)
    tb = qd_ref.shape[0]
    n = chunk * group

    def conv_silu(x_ref, halo_ref, w_ref):
        xp = jnp.concatenate([halo_ref[0], x_ref[...]], 0)
        w = w_ref[...]
        y = xp[5:5 + tb] * w[0:1]
        for tap in range(1, CONV_WIDTH):
            y = y + xp[5 + tap:5 + tap + tb] * w[tap:tap + 1]
        return _silu(y)

    q = conv_silu(qd_ref, hq_ref, wq_ref)
    k = conv_silu(kd_ref, hk_ref, wk_ref)
    v = conv_silu(vd_ref, hv_ref, wv_ref)
    q = q * lax.rsqrt(jnp.sum(q * q, -1, keepdims=True) + EPS) * (DN_KEY_DIM ** -0.5)
    k = k * lax.rsqrt(jnp.sum(k * k, -1, keepdims=True) + EPS)

    ba = ba_ref[...]
    lane = lax.broadcasted_iota(jnp.int32, ba.shape, 1)
    beta_all = _sigmoid(ba)
    g_all = -jnp.exp(alog_ref[...]) * _softplus(ba + dtb_ref[...])
    beta = jnp.sum(jnp.where(lane == h, beta_all, 0.0), -1, keepdims=True)
    g = jnp.sum(jnp.where(lane == h + DN_HEADS, g_all, 0.0), -1, keepdims=True)

    li = lax.broadcasted_iota(jnp.int32, (n, n), 0)
    mi = lax.broadcasted_iota(jnp.int32, (n, n), 1)
    same = (li // chunk) == (mi // chunk)
    upto = jnp.logical_and(same, li <= mi)
    since = jnp.logical_and(same, li >= mi)
    chunk_end = mi == (li // chunk) * chunk + (chunk - 1)
    blocks = _BlockDiag(chunk, group)
    op_dtype = wk_out.dtype
    slot = p_out.shape[1]

    row_groups = [slice(gi * n, (gi + 1) * n) for gi in range(tb // n)]
    a_mats, stash = [], []
    for rows in row_groups:
        qc, kc, bc, gc = q[rows], k[rows], beta[rows], g[rows]
        g_row = jnp.sum(jnp.where(upto, gc, 0.0), 0, keepdims=True)
        g_col = jnp.sum(jnp.where(li == mi, g_row, 0.0), 1, keepdims=True)
        g_end = jnp.sum(jnp.where(chunk_end, g_row, 0.0), 1, keepdims=True)
        decay = jnp.exp(jnp.where(since, g_col - g_row, -jnp.inf))
        qk_kk = _dot_any(jnp.concatenate([qc, kc], 0), kc, precise, _dot_nt)
        a_mats.append(jnp.where(li > mi, bc * decay * qk_kk[n:], 0.0))
        e_g = jnp.exp(g_col)
        qdec_out[rows, :] = (e_g * qc).astype(op_dtype)
        kend_out[rows, :] = (jnp.exp(g_end - g_col) * kc).astype(op_dtype)
        p_out[rows, 0:n] = (qk_kk[:n] * decay).astype(op_dtype)
        if slot > n:
            p_out[rows, n:] = jnp.zeros((n, slot - n), op_dtype)
        stash.append((e_g, g_end))
    t_invs = blocks.unit_lower_inverse(a_mats)
    for gi, (rows, t_inv, (e_g, g_end)) in enumerate(zip(row_groups, t_invs, stash)):
        kc, vc, bc = k[rows], v[rows], beta[rows]
        w = blocks.lmul(t_inv, jnp.concatenate([bc * vc, (bc * e_g) * kc], 1))
        wv_out[rows, :] = w[:, :DN_VAL_DIM]
        wk_out[rows, :] = w[:, DN_VAL_DIM:].astype(op_dtype)
        for c in range(group):
            last = c * chunk + chunk - 1
            gend_out[gi * group + c] = jnp.broadcast_to(jnp.exp(g_end[last:last + 1]), (1, LANES))


def _dn_score_slot(chunk, group):
    return LANES if chunk * group < LANES else max(chunk * group, V7X_MXU_DEPTH)


def _dn_prep(p, halo, w_conv8, alog_row, dtb_row, chunk, group, tb, precise):
    m = p.shape[0]
    op_dtype = F32 if precise else BF16
    slot = _dn_score_slot(chunk, group)
    nh = DN_HEADS
    cq, ck, cv = P_CONV // LANES, P_CONV // LANES + nh, P_CONV // LANES + 2 * nh
    col = lambda base: (lambda i, h: (i, base + h))
    halo_spec = lambda base: pl.BlockSpec((1, 8, LANES), lambda i, h: (i, 0, base + h))
    w_spec = lambda base: pl.BlockSpec((8, LANES), lambda i, h: (0, base + h))
    const = pl.BlockSpec((1, LANES), lambda i, h: (0, 0))
    head_blk = pl.BlockSpec((tb, LANES), lambda i, h: (i, h))
    out_shape = [
        jax.ShapeDtypeStruct((m, DN_WIDTH), F32),
        jax.ShapeDtypeStruct((m, DN_WIDTH), op_dtype),
        jax.ShapeDtypeStruct((m, DN_WIDTH), op_dtype),
        jax.ShapeDtypeStruct((m, DN_WIDTH), op_dtype),
        jax.ShapeDtypeStruct((m, DN_HEADS * slot), op_dtype),
        jax.ShapeDtypeStruct((m // chunk, 1, DN_WIDTH), F32),
    ]
    return pl.pallas_call(
        functools.partial(_dn_prep_kernel, chunk, group, precise),
        grid=(m // tb, nh),
        in_specs=[
            pl.BlockSpec((tb, LANES), col(cq)),
            pl.BlockSpec((tb, LANES), col(ck)),
            pl.BlockSpec((tb, LANES), col(cv)),
            pl.BlockSpec((tb, LANES), lambda i, h: (i, P_BA // LANES)),
            halo_spec(0), halo_spec(nh), halo_spec(2 * nh),
            w_spec(0), w_spec(nh), w_spec(2 * nh),
            const, const,
        ],
        out_specs=[head_blk] * 4 + [pl.BlockSpec((tb, slot), lambda i, h: (i, h)),
                                    pl.BlockSpec((tb // chunk, 1, LANES), lambda i, h: (i, 0, h))],
        out_shape=out_shape,
        compiler_params=_cparams("parallel", "parallel"),
        name="dn_prep",
    )(p, p, p, p, halo, halo, halo, w_conv8, w_conv8, w_conv8, alog_row, dtb_row)


def _dn_scan_kernel(chunk, group, n_chunks, wv_ref, wk_ref, qd_ref, ke_ref, p_ref, ge_ref, gate_ref, s0_ref,
                    onorm_ref, od_ref, sout_ref, s_scr, u_scr):
    n = pl.program_id(1)
    precise = wk_ref.dtype == F32
    slot = _dn_score_slot(chunk, group)

    @pl.when(n == 0)
    def _():
        s_scr[...] = s0_ref[0]
        u_scr[...] = jnp.zeros_like(u_scr)

    onorm = onorm_ref[...]
    for c in range(n_chunks):
        rows = slice(c * chunk, (c + 1) * chunk)
        group_rows = slice((c % group) * chunk, (c % group + 1) * chunk)
        for h in range(DN_HEADS):
            cols = slice(h * LANES, (h + 1) * LANES)
            s = s_scr[h]
            if not precise:
                s = s.astype(BF16)
            u = wv_ref[rows, cols] - _dot_any(wk_ref[rows, cols], s, precise)
            if not precise:
                u = u.astype(BF16)
            u_scr[h, group_rows, :] = u
            o = (_dot_any(qd_ref[rows, cols], s, precise)
                 + _dot_any(p_ref[rows, h * slot:(h + 1) * slot], u_scr[h], precise))
            s_scr[h] = ge_ref[c, :, cols] * s_scr[h] + _dot_any(ke_ref[rows, cols], u, precise, _dot_tn, 1)
            gate = gate_ref[rows, cols]
            od_ref[rows, cols] = (_rms(o, onorm) * _silu(gate)).astype(od_ref.dtype)

    @pl.when(n == pl.num_programs(1) - 1)
    def _():
        sout_ref[0] = s_scr[...]


def _dn_scan(prep, p, s0, onorm_row, chunk, group, n_chunks, batch):
    wv, wk, qdec, kend, pm, gend = prep
    m = wv.shape[0]
    assert n_chunks % group == 0
    rows = chunk * n_chunks
    steps = m // batch // rows
    blk = lambda b, n: (b * steps + n, 0)
    wide = pl.BlockSpec((rows, DN_WIDTH), blk)
    state = pl.BlockSpec((1, DN_HEADS, DN_KEY_DIM, DN_VAL_DIM), lambda b, n: (b, 0, 0, 0))
    slot = _dn_score_slot(chunk, group)
    assert pm.shape[1] == DN_HEADS * slot
    return pl.pallas_call(
        functools.partial(_dn_scan_kernel, chunk, group, n_chunks),
        grid=(batch, steps),
        in_specs=[
            wide, wide, wide, wide, pl.BlockSpec((rows, DN_HEADS * slot), blk),
            pl.BlockSpec((n_chunks, 1, DN_WIDTH), lambda b, n: (b * steps + n, 0, 0)),
            pl.BlockSpec((rows, DN_WIDTH), lambda b, n: (b * steps + n, P_GATE // DN_WIDTH)),
            state,
            pl.BlockSpec((1, LANES), lambda b, n: (0, 0)),
        ],
        out_specs=[wide, state],
        out_shape=[jax.ShapeDtypeStruct((m, DN_WIDTH), wk.dtype),
                   jax.ShapeDtypeStruct((batch, DN_HEADS, DN_KEY_DIM, DN_VAL_DIM), F32)],
        scratch_shapes=[pltpu.VMEM((DN_HEADS, DN_KEY_DIM, DN_VAL_DIM), F32),
                        pltpu.VMEM((DN_HEADS, slot, DN_VAL_DIM), wk.dtype)],
        compiler_params=_cparams("parallel", "arbitrary"),
        name="dn_scan",
    )(wv, wk, qdec, kend, pm, gend, p, s0, onorm_row)


def _out_proj_kernel(nw, attn_ref, od_ref, *refs):
    w_refs, (x_ref, g_ref, gate_ref, o_ref) = refs[:nw], refs[nw:]
    y = (_mm(attn_ref[...], tuple(r[0:ATTN_WIDTH, :] for r in w_refs))
         + _mm(od_ref[...], tuple(r[ATTN_WIDTH:, :] for r in w_refs)))
    o_ref[...] = x_ref[...] + gate_ref[...] * _rms(y, g_ref[...])


def _out_proj(attn, od, w, x, gain, gate, tm):
    m, d = x.shape
    row = lambda i: (i, 0)
    return pl.pallas_call(
        functools.partial(_out_proj_kernel, len(w)),
        grid=(m // tm,),
        in_specs=[
            pl.BlockSpec((tm, ATTN_WIDTH), row),
            pl.BlockSpec((tm, DN_WIDTH), row),
        ] + [pl.BlockSpec((ATTN_WIDTH + DN_WIDTH, d), lambda i: (0, 0))] * len(w) + [
            pl.BlockSpec((tm, d), row),
            pl.BlockSpec((1, d), lambda i: (0, 0)),
            _mod_spec(gate.shape[0], tm, d),
        ],
        out_specs=pl.BlockSpec((tm, d), row),
        out_shape=jax.ShapeDtypeStruct((m, d), F32),
        compiler_params=_cparams("parallel"),
        name="out_proj",
    )(attn, od, *w, x, gain, gate)


def _ffn_kernel(nw, x_ref, g_ref, sc_ref, sh_ref, *refs):
    wg_refs, wu_refs, wd_refs = refs[:nw], refs[nw:2 * nw], refs[2 * nw:3 * nw]
    g2_ref, gate_ref, o_ref, h_ref, acc_ref = refs[3 * nw:]
    j = pl.program_id(1)

    @pl.when(j == 0)
    def _():
        h = _rms(x_ref[...], g_ref[...]) * (1.0 + sc_ref[...]) + sh_ref[...]
        h_ref[...] = h.astype(h_ref.dtype)
        acc_ref[...] = jnp.zeros_like(acc_ref)

    h = h_ref[...]
    act = _silu(_mm(h, tuple(r[...] for r in wg_refs))) * _mm(h, tuple(r[...] for r in wu_refs))
    acc_ref[...] += _mm(act, tuple(r[...] for r in wd_refs))

    @pl.when(j == pl.num_programs(1) - 1)
    def _():
        o_ref[...] = x_ref[...] + gate_ref[...] * _rms(acc_ref[...], g2_ref[...])


def _ffn(x, gain, scale, shift, wg, wu, wd, gain2, gate, tm, tf):
    m, d = x.shape
    nw = len(wg)
    f = wg[0].shape[1]
    row = lambda i, j: (i, 0)
    vec = pl.BlockSpec((1, d), lambda i, j: (0, 0))
    return pl.pallas_call(
        functools.partial(_ffn_kernel, nw),
        grid=(m // tm, f // tf),
        in_specs=[
            pl.BlockSpec((tm, d), row), vec,
            _mod_spec(scale.shape[0], tm, d), _mod_spec(shift.shape[0], tm, d),
        ] + [pl.BlockSpec((d, tf), lambda i, j: (0, j))] * (2 * nw)
        + [pl.BlockSpec((tf, d), lambda i, j: (j, 0))] * nw
        + [vec, _mod_spec(gate.shape[0], tm, d)],
        out_specs=pl.BlockSpec((tm, d), row),
        out_shape=jax.ShapeDtypeStruct((m, d), F32),
        scratch_shapes=[pltpu.VMEM((tm, d), BF16 if wg[0].dtype == BF16 and nw == 1 else F32),
                        pltpu.VMEM((tm, d), F32)],
        compiler_params=_cparams("parallel", "arbitrary"),
        name="ffn_dense",
    )(x, gain, scale, shift, *wg, *wu, *wd, gain2, gate)


def _router_kernel(x_ref, g_ref, sc_ref, sh_ref, wr_ref, h_ref, gates_ref, idx_ref, w12_ref):
    h = _rms(x_ref[...], g_ref[...]) * (1.0 + sc_ref[...]) + sh_ref[...]
    h_ref[...] = h
    logits = _dot_x3(h, wr_ref[...])
    lane = lax.broadcasted_iota(jnp.int32, logits.shape, 1).astype(F32)
    logits = jnp.where(lane < N_EXPERTS, logits, -jnp.inf)
    m1 = jnp.max(logits, -1, keepdims=True)
    i1 = jnp.min(jnp.where(logits == m1, lane, float(LANES)), -1, keepdims=True)
    rest = jnp.where(lane == i1, -jnp.inf, logits)
    m2 = jnp.max(rest, -1, keepdims=True)
    i2 = jnp.min(jnp.where(rest == m2, lane, float(LANES)), -1, keepdims=True)
    t = jnp.exp(m2 - m1)
    w1 = 1.0 / (1.0 + t)
    w2 = t / (1.0 + t)
    gates_ref[...] = jnp.where(lane == i1, w1, 0.0) + jnp.where(lane == i2, w2, 0.0)
    idx_ref[...] = jnp.where(lane == 0.0, i1, jnp.where(lane == 1.0, i2, 0.0)).astype(jnp.int32)
    w12_ref[...] = jnp.where(lane == 0.0, w1, jnp.where(lane == 1.0, w2, 0.0))


def _router(x, gain, scale, shift, w_router_pad, tm):
    m, d = x.shape
    row = lambda i: (i, 0)
    vec = pl.BlockSpec((1, d), lambda i: (0, 0))
    small = pl.BlockSpec((tm, LANES), row)
    return pl.pallas_call(
        _router_kernel,
        grid=(m // tm,),
        in_specs=[pl.BlockSpec((tm, d), row), vec,
                  _mod_spec(scale.shape[0], tm, d), _mod_spec(shift.shape[0], tm, d),
                  pl.BlockSpec((d, LANES), lambda i: (0, 0))],
        out_specs=[pl.BlockSpec((tm, d), row), small, small, small],
        out_shape=[jax.ShapeDtypeStruct((m, d), F32), jax.ShapeDtypeStruct((m, LANES), F32),
                   jax.ShapeDtypeStruct((m, LANES), jnp.int32), jax.ShapeDtypeStruct((m, LANES), F32)],
        compiler_params=_cparams("parallel"),
        name="moe_router",
    )(x, gain, scale, shift, w_router_pad)


def _moe_gemm_kernel(nj, te_ref, tot_ref, rt_ref, h_hbm, wg_ref, wu_ref, wd_ref, ys_ref, xs_ref, xb_ref, acc_ref,
                     sems):
    r = pl.program_id(0)
    j = pl.program_id(1)
    tm = xb_ref.shape[0]
    total = tot_ref[0]
    active = r < total
    slot = r % 2
    share = -(-tm // nj)

    def row_copy(tile, t, s):
        return pltpu.make_async_copy(h_hbm.at[pl.ds(rt_ref[tile * tm + t], 1), :],
                                     xs_ref.at[s, pl.ds(t, 1), :], sems.at[s])

    def start_rows(tile, s, lo, hi):
        def body(t, carry):
            row_copy(tile, t, s).start()
            return carry

        lax.fori_loop(lo, hi, body, 0)

    @pl.when(jnp.logical_and(r == 0, j == 0))
    def _():
        start_rows(0, 0, 0, tm)

    @pl.when(jnp.logical_and(active, j == 0))
    def _():
        pltpu.make_async_copy(h_hbm.at[pl.ds(0, tm), :], xs_ref.at[slot], sems.at[slot]).wait()
        xb_ref[...] = xs_ref[slot].astype(BF16)
        acc_ref[...] = jnp.zeros_like(acc_ref)

    @pl.when(active)
    def _():
        more = r + 1 < total
        for u in range(share):
            t = j * share + u

            @pl.when(jnp.logical_and(more, t < tm))
            def _():
                row_copy(jnp.minimum(r + 1, pl.num_programs(0) - 1), jnp.minimum(t, tm - 1), 1 - slot).start()

        xb = xb_ref[...]
        act = (_silu(_dot(xb, wg_ref[0])) * _dot(xb, wu_ref[0])).astype(BF16)
        acc_ref[...] += _dot(act, wd_ref[0])

    @pl.when(j == pl.num_programs(1) - 1)
    def _():
        ys_ref[...] = jnp.where(active, acc_ref[...], 0.0)


def _moe_gemm(tile_expert, total_tiles, row_token, h, wg, wu, wd, tm, tf):
    n_tiles = tile_expert.shape[0]
    d = h.shape[1]
    f = wg.shape[2]
    nj = f // tf

    def w_col(r, j, te, tot, rt):
        return (te[r], 0, jnp.where(r < tot[0], j, nj - 1))

    def w_row(r, j, te, tot, rt):
        return (te[r], jnp.where(r < tot[0], j, nj - 1), 0)

    grid_spec = pltpu.PrefetchScalarGridSpec(
        num_scalar_prefetch=3,
        grid=(n_tiles, nj),
        in_specs=[
            pl.BlockSpec(memory_space=pl.ANY),
            pl.BlockSpec((1, d, tf), w_col),
            pl.BlockSpec((1, d, tf), w_col),
            pl.BlockSpec((1, tf, d), w_row),
        ],
        out_specs=pl.BlockSpec((tm, d), lambda r, j, te, tot, rt: (r, 0)),
        scratch_shapes=[pltpu.VMEM((2, tm, d), F32), pltpu.VMEM((tm, d), BF16), pltpu.VMEM((tm, d), F32),
                        pltpu.SemaphoreType.DMA((2,))],
    )
    return pl.pallas_call(
        functools.partial(_moe_gemm_kernel, nj),
        grid_spec=grid_spec,
        out_shape=jax.ShapeDtypeStruct((n_tiles * tm, d), F32),
        compiler_params=_cparams("arbitrary", "arbitrary", row_dma=True),
        name="moe_gemm",
    )(tile_expert, total_tiles, row_token, h, wg, wu, wd)


def _moe_combine_kernel(dest_ref, x_ref, w12_ref, g_ref, gate_ref, ys_hbm, o_ref, buf_ref, sems):
    i = pl.program_id(0)
    tb = x_ref.shape[0]
    slot = i % 2

    def row_copy(blk, t, k, s):
        src = dest_ref[2 * (blk * tb + t) + k]
        return pltpu.make_async_copy(ys_hbm.at[pl.ds(src, 1), :], buf_ref.at[s, k, pl.ds(t, 1), :], sems.at[s])

    def start_block(blk, s):
        def body(t, carry):
            row_copy(blk, t, 0, s).start()
            row_copy(blk, t, 1, s).start()
            return carry

        lax.fori_loop(0, tb, body, 0, unroll=8)

    @pl.when(i == 0)
    def _():
        start_block(0, 0)

    @pl.when(i + 1 < pl.num_programs(0))
    def _():
        start_block(i + 1, 1 - slot)

    for k in range(2):
        pltpu.make_async_copy(ys_hbm.at[pl.ds(0, tb), :], buf_ref.at[slot, k], sems.at[slot]).wait()
    w12 = w12_ref[...]
    y = w12[:, 0:1] * buf_ref[slot, 0] + w12[:, 1:2] * buf_ref[slot, 1]
    o_ref[...] = x_ref[...] + gate_ref[...] * _rms(y, g_ref[...])


def _moe_combine(dest, x, w12, gain, gate, ys, tb):
    m, d = x.shape
    row = lambda i, dst: (i, 0)
    grid_spec = pltpu.PrefetchScalarGridSpec(
        num_scalar_prefetch=1,
        grid=(m // tb,),
        in_specs=[
            pl.BlockSpec((tb, d), row),
            pl.BlockSpec((tb, LANES), row),
            pl.BlockSpec((1, d), lambda i, dst: (0, 0)),
            pl.BlockSpec((1, d), lambda i, dst: (0, 0)),
            pl.BlockSpec(memory_space=pl.ANY),
        ],
        out_specs=pl.BlockSpec((tb, d), row),
        scratch_shapes=[pltpu.VMEM((2, 2, tb, d), F32), pltpu.SemaphoreType.DMA((2,))],
    )
    return pl.pallas_call(
        _moe_combine_kernel,
        grid_spec=grid_spec,
        out_shape=jax.ShapeDtypeStruct((m, d), F32),
        compiler_params=_cparams("arbitrary", row_dma=True),
        name="moe_combine",
    )(dest, x, w12, gain, gate, ys)


def _route_tables(idx2, tm, n_tiles):
    m = idx2.shape[0]
    n_assign = 2 * m
    assert n_tiles * tm == n_assign + N_EXPERTS * tm
    experts = jnp.arange(N_EXPERTS, dtype=jnp.int32)
    e_flat = idx2.reshape(n_assign)
    onehot = (e_flat[:, None] == experts[None, :]).astype(jnp.int32)
    csum = jnp.cumsum(onehot, 0)
    counts = csum[-1]
    padded = ((counts + tm - 1) // tm) * tm
    pend = jnp.cumsum(padded)
    pstart = pend - padded
    dest = jnp.sum(onehot * (pstart[None, :] + csum - 1), -1).astype(jnp.int32)
    total_tiles = (pend[-1] // tm).astype(jnp.int32).reshape(1)
    tile_expert = jnp.minimum(
        jnp.searchsorted(pend // tm, jnp.arange(n_tiles, dtype=jnp.int32), side="right"), N_EXPERTS - 1
    ).astype(jnp.int32)
    filler_key = jnp.where(jnp.arange(tm, dtype=jnp.int32)[None, :] < (padded - counts)[:, None],
                           experts[:, None], N_EXPERTS).reshape(-1)
    keys = jnp.concatenate([e_flat, filler_key])
    tokens = jnp.concatenate([jnp.arange(n_assign, dtype=jnp.int32) // 2,
                              jnp.zeros((N_EXPERTS * tm,), jnp.int32)])
    _, row_token = lax.sort((keys, tokens), num_keys=1, is_stable=True)
    return tile_expert, total_tiles, row_token, dest


def _moe_dense_kernel(h_ref, gates_ref, wg_ref, wu_ref, wd_ref, x_ref, g_ref, gate_ref, o_ref, acc_ref, tot_ref):
    e = pl.program_id(0)
    j = pl.program_id(1)
    nj = pl.num_programs(1)

    @pl.when(jnp.logical_and(e == 0, j == 0))
    def _():
        tot_ref[...] = jnp.zeros_like(tot_ref)

    @pl.when(j == 0)
    def _():
        acc_ref[...] = jnp.zeros_like(acc_ref)

    h = h_ref[...].astype(BF16)
    act = (_silu(_dot(h, wg_ref[0])) * _dot(h, wu_ref[0])).astype(BF16)
    acc_ref[...] += _dot(act, wd_ref[0])

    @pl.when(j == nj - 1)
    def _():
        gates = gates_ref[...]
        lane = lax.broadcasted_iota(jnp.int32, gates.shape, 1)
        ge = jnp.sum(jnp.where(lane == e, gates, 0.0), -1, keepdims=True)
        tot_ref[...] += ge * acc_ref[...]

    @pl.when(jnp.logical_and(e == pl.num_programs(0) - 1, j == nj - 1))
    def _():
        o_ref[...] = x_ref[...] + gate_ref[...] * _rms(tot_ref[...], g_ref[...])


def _moe_dense(h, gates, wg, wu, wd, x, gain, gate, tf):
    m, d = x.shape
    f = wg.shape[2]
    full = pl.BlockSpec((m, d), lambda e, j: (0, 0))
    return pl.pallas_call(
        _moe_dense_kernel,
        grid=(N_EXPERTS, f // tf),
        in_specs=[
            full,
            pl.BlockSpec((m, LANES), lambda e, j: (0, 0)),
            pl.BlockSpec((1, d, tf), lambda e, j: (e, 0, j)),
            pl.BlockSpec((1, d, tf), lambda e, j: (e, 0, j)),
            pl.BlockSpec((1, tf, d), lambda e, j: (e, j, 0)),
            full,
            pl.BlockSpec((1, d), lambda e, j: (0, 0)),
            full,
        ],
        out_specs=full,
        out_shape=jax.ShapeDtypeStruct((m, d), F32),
        scratch_shapes=[pltpu.VMEM((m, d), F32), pltpu.VMEM((m, d), F32)],
        compiler_params=_cparams("arbitrary", "arbitrary"),
        name="moe_dense",
    )(h, gates, wg, wu, wd, x, gain, gate)


def _rope_tables(pos):
    half = ROPE_DIM // 2
    inv_freq = jnp.power(ROPE_THETA, -2.0 * jnp.arange(half, dtype=F32) / ROPE_DIM)
    ang = pos.astype(F32)[:, None] * inv_freq[None, :]
    cos, sin = jnp.cos(ang), jnp.sin(ang)
    t = pos.shape[0]
    rest = ATTN_HEAD_DIM - ROPE_DIM
    cos_h = jnp.concatenate([cos, cos, jnp.ones((t, rest), F32)], 1)
    sa_h = jnp.concatenate([-sin, jnp.zeros((t, half + rest), F32)], 1)
    sb_h = jnp.concatenate([jnp.zeros((t, half), F32), sin, jnp.zeros((t, rest), F32)], 1)
    rep = LANES // ATTN_HEAD_DIM
    return tuple(jnp.tile(a, (1, rep)) for a in (cos_h, sa_h, sb_h))


def _permute_w_in(w):
    o1 = ATTN_WIDTH
    o2 = o1 + KV_WIDTH
    o3 = o2 + KV_WIDTH
    o4 = o3 + DN_CONV_CH
    o5 = o4 + DN_WIDTH
    parts = [w[:, :o1], w[:, o4:o5], w[:, o3:o4], w[:, o1:o2], w[:, o2:o3], w[:, o5:]]
    used = sum(a.shape[1] for a in parts)
    parts.append(jnp.zeros((w.shape[0], P_WIDTH - used), w.dtype))
    return jnp.concatenate(parts, 1)


def _sample_mask_bias(s, lc):
    q_pos = PAST_LEN + np.arange(s)
    k_pos = np.concatenate([PAST_LEN - lc + np.arange(lc), q_pos])
    q_chunk = q_pos[:, None] // CHUNK
    k_chunk = k_pos[None, :] // CHUNK
    mask = (k_pos[None, :] >= 0) & (k_chunk <= q_chunk) & (k_pos[None, :] >= q_chunk * CHUNK - WINDOW)
    bias = np.where(mask, 0.0, -np.inf).astype(np.float32)
    return jnp.asarray(np.tile(bias, (4, 1)))


def _conv_halo(p, init, tb, seq):
    m = p.shape[0]
    batch = m // seq
    nb = seq // tb
    tails = p.reshape(batch, nb, tb, P_WIDTH)[:, :nb - 1, tb - (CONV_WIDTH - 1):, P_CONV:P_CONV + DN_CONV_CH]
    prev = jnp.concatenate([init[:, None], tails], 1)
    prev = prev.reshape(batch * nb, CONV_WIDTH - 1, DN_CONV_CH)
    return jnp.pad(prev, ((0, 0), (8 - (CONV_WIDTH - 1), 0), (0, 0)))


def _trunk(x, mods, layer_w, rope, past, cfg):
    m = x.shape[0]
    batch, seq = cfg["batch"], cfg["seq"]
    precise = cfg["precise"]
    nw = 2 if precise else 1
    ks, vs, ss, bufs = [], [], [], []
    cos, sa, sb = rope
    for l in range(DEPTH):
        w = layer_w[l]
        sh_a, sc_a, g_a, sh_f, sc_f, g_f = mods[l]
        p = _norm_proj(x, w["gain"][0], sc_a, sh_a, w["w_in"][:nw], cfg["tm_proj"], cfg["tn_proj"])
        if past is None:
            attn, k_new = _attn_prompt(p, w["sinks"], cos, sa, sb, cfg["tb_attn"])
            s0 = jnp.zeros((batch, DN_HEADS, DN_KEY_DIM, DN_VAL_DIM), F32)
            conv_init = jnp.zeros((batch, CONV_WIDTH - 1, DN_CONV_CH), F32)
        else:
            ck = past[0][l].reshape(batch, -1, KV_WIDTH)
            cv = past[1][l].reshape(batch, -1, KV_WIDTH)
            attn, k_new = _attn_sample(p, ck, cv, w["sinks"], cos, sa, sb, cfg["bias"], batch, seq)
            s0 = past[2][l]
            conv_init = past[3][l]
        halo = _conv_halo(p, conv_init, cfg["tb_dn"], seq)
        prep = _dn_prep(p, halo, w["w_conv"], w["alog"], w["dtb"], cfg["chunk"], cfg["group"], cfg["tb_dn"],
                        precise)
        od, s_new = _dn_scan(prep, p, s0, w["onorm"], cfg["chunk"], cfg["group"], cfg["scan_chunks"], batch)
        pick = 1 if precise else 0
        x = _out_proj(attn, od, w["w_out"][pick], x, w["gain"][1], g_a, cfg["tm_out"])
        if l % 2 == 0:
            x = _ffn(x, w["gain"][2], sc_f, sh_f, w["ffn_gate"][pick], w["ffn_up"][pick], w["ffn_down"][pick],
                     w["gain"][3], g_f, cfg["tm_ffn"], cfg["tf_ffn"])
        else:
            h, gates, idx, w12 = _router(x, w["gain"][2], sc_f, sh_f, w["router"], cfg["tm_router"])
            if cfg["routed"]:
                tm = cfg["tm_moe"]
                n_tiles = 2 * m // tm + N_EXPERTS
                tile_expert, total_tiles, row_token, dest = _route_tables(idx[:, :2], tm, n_tiles)
                ys = _moe_gemm(tile_expert, total_tiles, row_token, h, w["moe_gate"], w["moe_up"], w["moe_down"],
                               tm, cfg["tf_moe"])
                x = _moe_combine(dest, x, w12, w["gain"][3], g_f, ys, cfg["tb_combine"])
            else:
                x = _moe_dense(h, gates, w["moe_gate"], w["moe_up"], w["moe_down"], x, w["gain"][3], g_f,
                               cfg["tf_moe"])
        pb = p.reshape(batch, seq, P_WIDTH)
        keep = min(WINDOW, seq) if past is None else seq
        ks.append(k_new.reshape(batch, seq, ATTN_KV_HEADS, ATTN_HEAD_DIM)[:, seq - keep:])
        vs.append(pb[:, seq - keep:, P_V:P_V + KV_WIDTH].reshape(batch, keep, ATTN_KV_HEADS, ATTN_HEAD_DIM))
        ss.append(s_new)
        assert seq >= CONV_WIDTH - 1
        bufs.append(pb[:, seq - (CONV_WIDTH - 1):, P_CONV:P_CONV + DN_CONV_CH])
    return x, jnp.stack(ks), jnp.stack(vs), jnp.stack(ss), jnp.stack(bufs)


def kernel(x_prompt, x_sample, cache_attn_k, cache_attn_v, state_delta, state_conv, c_prompt, c_sample, w_in, w_conv, attn_sinks, dn_a_log, dn_dt_bias, dn_norm, w_out, w_mod, b_mod, norm_gains, ffn_gate, ffn_up, ffn_down, moe_router, moe_gate, moe_up, moe_down):
    bp, tp, d = x_prompt.shape
    bs, ts, _ = x_sample.shape
    assert bp == 1 and d == D_MODEL

    c_all = jnp.concatenate([c_prompt, c_sample, jnp.zeros((16 - bp - bs, d), F32)], 0)
    mod = _modulation(c_all, w_mod, b_mod)
    mods_p, mods_s = [], []
    for l in range(DEPTH):
        six = jnp.split(mod[l], 6, -1)
        mods_p.append([a[0:bp] for a in six])
        mods_s.append([jnp.repeat(a[bp:bp + bs], ts, axis=0) for a in six])

    def pad_lanes(v, at):
        return jnp.zeros((1, LANES), F32).at[0, at:at + v.shape[0]].set(v)

    def both(w):
        return ((w.astype(BF16),), (w,))

    layer_w = []
    for l in range(DEPTH):
        w = {
            "gain": [norm_gains[l, i].reshape(1, d) for i in range(4)],
            "w_in": _split_weight(_permute_w_in(w_in[l])),
            "sinks": attn_sinks[l],
            "w_conv": jnp.pad(w_conv[l], ((0, 8 - CONV_WIDTH), (0, 0))),
            "alog": pad_lanes(dn_a_log[l], DN_HEADS),
            "dtb": pad_lanes(dn_dt_bias[l], DN_HEADS),
            "onorm": dn_norm[l].reshape(1, DN_VAL_DIM),
            "w_out": both(w_out[l]),
        }
        if l % 2 == 0:
            w["ffn_gate"] = both(ffn_gate[l // 2])
            w["ffn_up"] = both(ffn_up[l // 2])
            w["ffn_down"] = both(ffn_down[l // 2])
        else:
            w["router"] = jnp.pad(moe_router[l // 2], ((0, 0), (0, LANES - N_EXPERTS)))
            w["moe_gate"] = moe_gate[l // 2].astype(BF16)
            w["moe_up"] = moe_up[l // 2].astype(BF16)
            w["moe_down"] = moe_down[l // 2].astype(BF16)
        layer_w.append(w)

    cfg_p = dict(batch=bp, seq=tp, precise=False, chunk=CHUNK, group=2, scan_chunks=4, tm_proj=1024, tn_proj=512, tb_attn=512,
                 tb_dn=1024, tm_out=512, tm_ffn=512, tf_ffn=512, tm_router=512, routed=True, tm_moe=512,
                 tf_moe=256, tb_combine=256)
    rope_p = _rope_tables(jnp.arange(tp, dtype=jnp.int32))
    y_p, k_p, v_p, s_p, conv_p = _trunk(x_prompt.reshape(bp * tp, d), mods_p, layer_w, rope_p, None, cfg_p)

    ms = bs * ts
    cfg_s = dict(batch=bs, seq=ts, precise=True, chunk=ts, group=1, scan_chunks=1, tm_proj=ms, tn_proj=512, tb_dn=ts, tm_out=ms,
                 tm_ffn=ms, tf_ffn=512, tm_router=ms, routed=False, tf_moe=1408,
                 bias=_sample_mask_bias(ts, cache_attn_k.shape[2]))
    rope_s = _rope_tables(PAST_LEN + jnp.arange(ts, dtype=jnp.int32))
    past = (cache_attn_k, cache_attn_v, state_delta, state_conv)
    y_s, k_s, v_s, s_s, conv_s = _trunk(x_sample.reshape(ms, d), mods_s, layer_w, rope_s, past, cfg_s)

    return (y_p.reshape(bp, tp, d), y_s.reshape(bs, ts, d), k_p, v_p, s_p, conv_p, k_s, v_s, s_s, conv_s)
```

```python
import functools
import math

import numpy as np
import jax
import jax.numpy as jnp
from jax import lax
from jax.experimental import pallas as pl
from jax.experimental.pallas import tpu as pltpu

D_MODEL = 2048
DEPTH = 2
PAST_LEN = 1024
CHUNK = 64
ATTN_HEADS = 16
ATTN_KV_HEADS = 2
ATTN_HEAD_DIM = 64
ATTN_WIDTH = 1024
KV_WIDTH = 128
WINDOW = 128
ROPE_THETA = 500000.0
ROPE_DIM = 16
DN_HEADS = 8
DN_KEY_DIM = 128
DN_VAL_DIM = 128
DN_WIDTH = 1024
CONV_WIDTH = 4
DN_CONV_CH = 3072
D_FF = 5632
N_EXPERTS = 8
D_FF_EXPERT = 2816
EPS = 1e-6

F32 = jnp.float32
BF16 = jnp.bfloat16
LANES = 128
V7X_MXU_DEPTH = 256
V7X_VMEM_LIMIT = 56 * 1024 * 1024

P_Q = 0
P_GATE = 1024
P_CONV = 2048
P_K = 5120
P_V = 5248
P_BA = 5376
P_WIDTH = 5632


def _cparams(*sem, row_dma=False):
    return pltpu.CompilerParams(dimension_semantics=sem, vmem_limit_bytes=V7X_VMEM_LIMIT,
                                disable_bounds_checks=row_dma)


def _sigmoid(x):
    return 0.5 * jnp.tanh(0.5 * x) + 0.5


def _silu(x):
    return x * _sigmoid(x)


def _rms(x, gain):
    return x * lax.rsqrt(jnp.mean(x * x, -1, keepdims=True) + EPS) * gain


def _dot(a, b):
    return jnp.dot(a, b, preferred_element_type=F32)


def _dot_nt(a, b):
    return lax.dot_general(a, b, (((1,), (1,)), ((), ())), preferred_element_type=F32)


def _dot_tn(a, b):
    return lax.dot_general(a, b, (((0,), (0,)), ((), ())), preferred_element_type=F32)


def _split_bf16(a):
    hi = a.astype(BF16)
    lo = (a - hi.astype(F32)).astype(BF16)
    return hi, lo


def _dot_x3(a, b, dot=_dot, out_axis=0):
    a_hi, a_lo = _split_bf16(a)
    b_hi, b_lo = _split_bf16(b)
    n = a.shape[out_axis]
    top = dot(jnp.concatenate([a_hi, a_lo], out_axis), b_hi)
    return top[:n] + top[n:] + dot(a_hi, b_lo)


def _dot_any(a, b, precise, dot=_dot, out_axis=0):
    if precise:
        return _dot_x3(a, b, dot, out_axis)
    return dot(a.astype(BF16), b.astype(BF16))


def _mm(a, w):
    if len(w) == 1 and w[0].dtype == F32:
        return _dot_x3(a, w[0])
    if len(w) == 1:
        return _dot(a.astype(BF16), w[0])
    a_hi, a_lo = _split_bf16(a)
    n = a.shape[0]
    top = _dot(jnp.concatenate([a_hi, a_lo], 0), w[0])
    return top[:n] + top[n:] + _dot(a_hi, w[1])


def _split_weight(w):
    hi, lo = _split_bf16(w)
    return (hi, lo)


def _mod_spec(rows, tm, d):
    if rows == 1:
        return pl.BlockSpec((1, d), lambda i, *_: (0, 0))
    return pl.BlockSpec((tm, d), lambda i, *_: (i, 0))


def _mod_kernel(c_ref, w_ref, b_ref, o_ref):
    o_ref[0] = _dot_x3(_silu(c_ref[...]), w_ref[0]) + b_ref[0]


def _modulation(c_all, w_mod, b_mod):
    rows = c_all.shape[0]
    n = w_mod.shape[2]
    tn = 1024
    return pl.pallas_call(
        _mod_kernel,
        grid=(DEPTH, n // tn),
        in_specs=[
            pl.BlockSpec((rows, D_MODEL), lambda l, j: (0, 0)),
            pl.BlockSpec((1, D_MODEL, tn), lambda l, j: (l, 0, j)),
            pl.BlockSpec((1, 1, tn), lambda l, j: (l, 0, j)),
        ],
        out_specs=pl.BlockSpec((1, rows, tn), lambda l, j: (l, 0, j)),
        out_shape=jax.ShapeDtypeStruct((DEPTH, rows, n), F32),
        compiler_params=_cparams("parallel", "parallel"),
        name="modulation",
    )(c_all, w_mod, b_mod.reshape(DEPTH, 1, n))


def _norm_proj_kernel(nw, x_ref, g_ref, sc_ref, sh_ref, *refs):
    w_refs, (o_ref, h_ref) = refs[:nw], refs[nw:]

    @pl.when(pl.program_id(1) == 0)
    def _():
        h = _rms(x_ref[...], g_ref[...]) * (1.0 + sc_ref[...]) + sh_ref[...]
        h_ref[...] = h.astype(h_ref.dtype)

    o_ref[...] = _mm(h_ref[...], tuple(r[...] for r in w_refs))


def _norm_proj(x, gain, scale, shift, w, tm, tn):
    m, d = x.shape
    n = w[0].shape[1]
    return pl.pallas_call(
        functools.partial(_norm_proj_kernel, len(w)),
        grid=(m // tm, n // tn),
        in_specs=[
            pl.BlockSpec((tm, d), lambda i, j: (i, 0)),
            pl.BlockSpec((1, d), lambda i, j: (0, 0)),
            _mod_spec(scale.shape[0], tm, d),
            _mod_spec(shift.shape[0], tm, d),
        ] + [pl.BlockSpec((d, tn), lambda i, j: (0, j))] * len(w),
        out_specs=pl.BlockSpec((tm, tn), lambda i, j: (i, j)),
        out_shape=jax.ShapeDtypeStruct((m, n), F32),
        scratch_shapes=[pltpu.VMEM((tm, d), BF16 if len(w) == 1 else F32)],
        compiler_params=_cparams("parallel", "arbitrary"),
        name="norm_proj",
    )(x, gain, scale, shift, *w)


def _rope(x, cos, sa, sb):
    return x * cos + pltpu.roll(x, LANES - 8, 1) * sa + pltpu.roll(x, 8, 1) * sb


def _kv_variants(k, v):
    lo = lax.broadcasted_iota(jnp.int32, k.shape, 1) < ATTN_HEAD_DIM
    kr = pltpu.roll(k, ATTN_HEAD_DIM, 1)
    vr = pltpu.roll(v, ATTN_HEAD_DIM, 1)
    zero = jnp.zeros_like(k)
    k_lo = (jnp.where(lo, k, zero), jnp.where(lo, kr, zero))
    k_hi = (jnp.where(lo, zero, kr), jnp.where(lo, zero, k))
    v_lo = (jnp.where(lo, v, zero), jnp.where(lo, vr, zero))
    v_hi = (jnp.where(lo, zero, vr), jnp.where(lo, zero, v))
    return k_lo, k_hi, v_lo, v_hi


def _sink_softmax(s, sink):
    m = jnp.maximum(jnp.max(s, -1, keepdims=True), sink)
    p = jnp.exp(s - m)
    den = jnp.sum(p, -1, keepdims=True) + jnp.exp(sink - m)
    return p * (1.0 / den)


def _attn_core(qbs, k_los, k_his, v_los, v_his, bias, sinks, precise=False):
    scores = [(_dot_any(qb, k_lo, precise, _dot_nt) + bias, _dot_any(qb, k_hi, precise, _dot_nt) + bias)
              for qb, k_lo, k_hi in zip(qbs, k_los, k_his)]
    probs = [(_sink_softmax(s_even, sink[0]), _sink_softmax(s_odd, sink[1]))
             for (s_even, s_odd), sink in zip(scores, sinks)]
    return [_dot_any(p_even, v_lo, precise) + _dot_any(p_odd, v_hi, precise)
            for (p_even, p_odd), v_lo, v_hi in zip(probs, v_los, v_his)]


def _sink_columns(sink_ref, rows_per_pair):
    n = 4 * rows_per_pair
    pair = lax.broadcasted_iota(jnp.int32, (n, 1), 0) // rows_per_pair
    out = []
    for j in range(ATTN_KV_HEADS):
        cols = []
        for par in range(2):
            col = jnp.zeros((n, 1), F32)
            for a in range(4):
                col = jnp.where(pair == a, sink_ref[8 * j + 2 * a + par], col)
            cols.append(col)
        out.append(cols)
    return out


def _attn_prompt_kernel(sink_ref, q_ref, kv_ref, cos_ref, sa_ref, sb_ref, o_ref, knew_ref,
                        qs_ref, klo_ref, khi_ref, vlo_ref, vhi_ref):
    i = pl.program_id(0)
    tb = q_ref.shape[0]
    bufs = (klo_ref, khi_ref, vlo_ref, vhi_ref)

    @pl.when(i == 0)
    def _():
        for r in bufs:
            r[:, 0:WINDOW, :] = jnp.zeros((ATTN_KV_HEADS, WINDOW, LANES), BF16)

    @pl.when(i > 0)
    def _():
        for r in bufs:
            r[:, 0:WINDOW, :] = r[:, tb:tb + WINDOW, :]

    cos, sa, sb = cos_ref[...], sa_ref[...], sb_ref[...]
    k = _rope(kv_ref[:, 0:LANES], cos, sa, sb)
    knew_ref[...] = k
    variants = _kv_variants(k, kv_ref[:, LANES:2 * LANES])
    for r, var in zip(bufs, variants):
        for j in range(ATTN_KV_HEADS):
            r[j, WINDOW:, :] = var[j].astype(BF16)
    scale = ATTN_HEAD_DIM ** -0.5
    for a in range(ATTN_WIDTH // LANES):
        cols = slice(a * LANES, (a + 1) * LANES)
        qs_ref[:, cols] = (_rope(q_ref[:, cols], cos, sa, sb) * scale).astype(BF16)

    sinks = _sink_columns(sink_ref, CHUNK)
    nk = WINDOW + CHUNK

    def chunk_body(c, carry):
        r0 = pl.multiple_of(c * CHUNK, CHUNK)
        kpos = i * tb - WINDOW + r0 + lax.broadcasted_iota(jnp.int32, (1, nk), 1)
        bias = jnp.where(kpos >= 0, 0.0, -jnp.inf).astype(F32)
        heads = range(ATTN_KV_HEADS)
        qbs = [jnp.concatenate(
            [qs_ref[pl.ds(r0, CHUNK), (4 * j + a) * LANES:(4 * j + a + 1) * LANES] for a in range(4)], 0)
            for j in heads]
        keys = pl.ds(r0, nk)
        outs = _attn_core(qbs, [klo_ref[j, keys, :] for j in heads], [khi_ref[j, keys, :] for j in heads],
                          [vlo_ref[j, keys, :] for j in heads], [vhi_ref[j, keys, :] for j in heads], bias, sinks)
        for j, o in zip(heads, outs):
            for a in range(4):
                o_ref[pl.ds(r0, CHUNK), (4 * j + a) * LANES:(4 * j + a + 1) * LANES] = (
                    o[a * CHUNK:(a + 1) * CHUNK].astype(BF16))
        return carry

    lax.fori_loop(0, tb // CHUNK, chunk_body, 0, unroll=4)


def _attn_prompt(p, sinks, cos, sa, sb, tb):
    t = p.shape[0]
    kv_blk = P_K // (2 * LANES)
    row = lambda i: (i, 0)
    return pl.pallas_call(
        _attn_prompt_kernel,
        grid=(t // tb,),
        in_specs=[
            pl.BlockSpec(memory_space=pltpu.SMEM),
            pl.BlockSpec((tb, ATTN_WIDTH), row),
            pl.BlockSpec((tb, 2 * LANES), lambda i: (i, kv_blk)),
            pl.BlockSpec((tb, LANES), row),
            pl.BlockSpec((tb, LANES), row),
            pl.BlockSpec((tb, LANES), row),
        ],
        out_specs=[pl.BlockSpec((tb, ATTN_WIDTH), row), pl.BlockSpec((tb, LANES), row)],
        out_shape=[jax.ShapeDtypeStruct((t, ATTN_WIDTH), BF16), jax.ShapeDtypeStruct((t, LANES), F32)],
        scratch_shapes=[pltpu.VMEM((tb, ATTN_WIDTH), BF16)]
        + [pltpu.VMEM((ATTN_KV_HEADS, tb + WINDOW, LANES), BF16) for _ in range(4)],
        compiler_params=_cparams("arbitrary"),
        name="attn_prompt",
    )(sinks, p, p, cos, sa, sb)


def _attn_sample_kernel(sink_ref, q_ref, kv_ref, ck_ref, cv_ref, cos_ref, sa_ref, sb_ref, bias_ref,
                        o_ref, knew_ref):
    s = q_ref.shape[0]
    cos, sa, sb = cos_ref[...], sa_ref[...], sb_ref[...]
    k = _rope(kv_ref[:, 0:LANES], cos, sa, sb)
    knew_ref[...] = k
    kk = jnp.concatenate([ck_ref[0], k], 0)
    vv = jnp.concatenate([cv_ref[0], kv_ref[:, LANES:2 * LANES]], 0)
    k_lo, k_hi, v_lo, v_hi = _kv_variants(kk, vv)
    sinks = _sink_columns(sink_ref, s)
    scale = ATTN_HEAD_DIM ** -0.5
    bias = bias_ref[...]
    qbs = [jnp.concatenate(
        [_rope(q_ref[:, (4 * j + a) * LANES:(4 * j + a + 1) * LANES], cos, sa, sb) * scale for a in range(4)], 0)
        for j in range(ATTN_KV_HEADS)]
    outs = _attn_core(qbs, k_lo, k_hi, v_lo, v_hi, bias, sinks, precise=True)
    for j, o in enumerate(outs):
        for a in range(4):
            o_ref[:, (4 * j + a) * LANES:(4 * j + a + 1) * LANES] = o[a * s:(a + 1) * s]


def _attn_sample(p, cache_k, cache_v, sinks, cos, sa, sb, bias, batch, s):
    lc = cache_k.shape[1]
    kv_blk = P_K // (2 * LANES)
    row = lambda b: (b, 0)
    const = lambda b: (0, 0)
    return pl.pallas_call(
        _attn_sample_kernel,
        grid=(batch,),
        in_specs=[
            pl.BlockSpec(memory_space=pltpu.SMEM),
            pl.BlockSpec((s, ATTN_WIDTH), row),
            pl.BlockSpec((s, 2 * LANES), lambda b: (b, kv_blk)),
            pl.BlockSpec((1, lc, LANES), lambda b: (b, 0, 0)),
            pl.BlockSpec((1, lc, LANES), lambda b: (b, 0, 0)),
            pl.BlockSpec((s, LANES), const),
            pl.BlockSpec((s, LANES), const),
            pl.BlockSpec((s, LANES), const),
            pl.BlockSpec((4 * s, lc + s), const),
        ],
        out_specs=[pl.BlockSpec((s, ATTN_WIDTH), row), pl.BlockSpec((s, LANES), row)],
        out_shape=[jax.ShapeDtypeStruct((batch * s, ATTN_WIDTH), F32),
                   jax.ShapeDtypeStruct((batch * s, LANES), F32)],
        compiler_params=_cparams("parallel"),
        name="attn_sample",
    )(sinks, p, p, cache_k, cache_v, cos, sa, sb, bias)


class _BlockDiag:
    def __init__(self, chunk, group):
        self.chunk, self.group = chunk, group
        n = chunk * group
        lane = lax.broadcasted_iota(jnp.int32, (chunk, n), 1)
        self.lane_block = lane // chunk
        self.eye = (lax.broadcasted_iota(jnp.int32, (chunk, n), 0) == lane % chunk).astype(F32)

    def wide(self, tall):
        c = self.chunk
        out = tall[0:c]
        for b in range(1, self.group):
            out = out + tall[b * c:(b + 1) * c]
        return out

    def expand(self, wide):
        if self.group == 1:
            return wide
        zero = jnp.zeros_like(wide)
        return jnp.concatenate([jnp.where(self.lane_block == b, wide, zero) for b in range(self.group)], 0)

    def rmul(self, lhs, wide):
        l_hi, l_lo = _split_bf16(lhs)
        w_hi, w_lo = _split_bf16(wide)
        m = lhs.shape[0]
        top = _dot(jnp.concatenate([l_hi, l_lo], 0), self.expand(w_hi))
        return top[:m] + top[m:] + _dot(l_hi, self.expand(w_lo))

    def lmul(self, wide, rhs):
        w_hi, w_lo = _split_bf16(wide)
        r_hi, r_lo = _split_bf16(rhs)
        n = self.chunk * self.group
        e_hi = self.expand(w_hi)
        top = _dot(jnp.concatenate([e_hi, self.expand(w_lo)], 0), r_hi)
        return top[:n] + top[n:] + _dot(e_hi, r_lo)

    def unit_lower_inverse(self, a_talls):
        c = self.chunk
        negs = [-self.wide(a) for a in a_talls]
        xs = [self.eye + neg for neg in negs]
        powers = [self.rmul(neg, neg) for neg in negs]
        iters = int(math.log2(c)) - 1
        for it in range(iters):
            last = it == iters - 1
            rs = [self.rmul(x if last else jnp.concatenate([x, p], 0), p) for x, p in zip(xs, powers)]
            xs = [x + r[:c] for x, r in zip(xs, rs)]
            if not last:
                powers = [r[c:] for r in rs]
        return xs


def _softplus(x):
    return jnp.maximum(x, 0.0) + jnp.log1p(jnp.exp(-jnp.abs(x)))


def _dn_prep_kernel(chunk, group, hps, precise, qd_ref, kd_ref, vd_ref, ba_ref, hq_ref, hk_ref, hv_ref,
                    wq_ref, wk_ref, wv_ref, alog_ref, dtb_ref,
                    wv_out, wk_out, qdec_out, kend_out, p_out, gend_out):
    head0 = pl.program_id(1) * hps
    tb = qd_ref.shape[0]
    n = chunk * group

    def conv_silu(x_ref, halo_ref, w_ref, cols):
        xp = jnp.concatenate([halo_ref[0, :, cols], x_ref[:, cols]], 0)
        w = w_ref[:, cols]
        y = xp[5:5 + tb] * w[0:1]
        for tap in range(1, CONV_WIDTH):
            y = y + xp[5 + tap:5 + tap + tb] * w[tap:tap + 1]
        return _silu(y)

    ba = ba_ref[...]
    lane = lax.broadcasted_iota(jnp.int32, ba.shape, 1)
    beta_all = _sigmoid(ba)
    g_all = -jnp.exp(alog_ref[...]) * _softplus(ba + dtb_ref[...])

    li = lax.broadcasted_iota(jnp.int32, (n, n), 0)
    mi = lax.broadcasted_iota(jnp.int32, (n, n), 1)
    same = (li // chunk) == (mi // chunk)
    upto = jnp.logical_and(same, li <= mi)
    since = jnp.logical_and(same, li >= mi)
    chunk_end = mi == (li // chunk) * chunk + (chunk - 1)
    blocks = _BlockDiag(chunk, group)
    op_dtype = wk_out.dtype
    slot = p_out.shape[1] // hps

    items, a_mats = [], []
    for hh in range(hps):
        cols = slice(hh * LANES, (hh + 1) * LANES)
        q = conv_silu(qd_ref, hq_ref, wq_ref, cols)
        k = conv_silu(kd_ref, hk_ref, wk_ref, cols)
        v = conv_silu(vd_ref, hv_ref, wv_ref, cols)
        q = q * lax.rsqrt(jnp.sum(q * q, -1, keepdims=True) + EPS) * (DN_KEY_DIM ** -0.5)
        k = k * lax.rsqrt(jnp.sum(k * k, -1, keepdims=True) + EPS)
        beta = jnp.sum(jnp.where(lane == head0 + hh, beta_all, 0.0), -1, keepdims=True)
        g = jnp.sum(jnp.where(lane == head0 + hh + DN_HEADS, g_all, 0.0), -1, keepdims=True)
        for gi in range(tb // n):
            rows = slice(gi * n, (gi + 1) * n)
            qc, kc, bc, gc = q[rows], k[rows], beta[rows], g[rows]
            g_row = jnp.sum(jnp.where(upto, gc, 0.0), 0, keepdims=True)
            g_col = jnp.sum(jnp.where(li == mi, g_row, 0.0), 1, keepdims=True)
            g_end = jnp.sum(jnp.where(chunk_end, g_row, 0.0), 1, keepdims=True)
            decay = jnp.exp(jnp.where(since, g_col - g_row, -jnp.inf))
            qk_kk = _dot_any(jnp.concatenate([qc, kc], 0), kc, precise, _dot_nt)
            a_mats.append(jnp.where(li > mi, bc * decay * qk_kk[n:], 0.0))
            e_g = jnp.exp(g_col)
            qdec_out[rows, cols] = (e_g * qc).astype(op_dtype)
            kend_out[rows, cols] = (jnp.exp(g_end - g_col) * kc).astype(op_dtype)
            p_out[rows, hh * slot:hh * slot + n] = (qk_kk[:n] * decay).astype(op_dtype)
            if slot > n:
                p_out[rows, hh * slot + n:(hh + 1) * slot] = jnp.zeros((n, slot - n), op_dtype)
            items.append((gi, rows, cols, kc, v[rows], bc, e_g, g_end))
    t_invs = blocks.unit_lower_inverse(a_mats)
    for t_inv, (gi, rows, cols, kc, vc, bc, e_g, g_end) in zip(t_invs, items):
        w = blocks.lmul(t_inv, jnp.concatenate([bc * vc, (bc * e_g) * kc], 1))
        wv_out[rows, cols] = w[:, :DN_VAL_DIM]
        wk_out[rows, cols] = w[:, DN_VAL_DIM:].astype(op_dtype)
        for c in range(group):
            last = c * chunk + chunk - 1
            gend_out[gi * group + c, :, cols] = jnp.broadcast_to(jnp.exp(g_end[last:last + 1]), (1, LANES))


def _dn_score_slot(chunk, group):
    return LANES if chunk * group < LANES else max(chunk * group, V7X_MXU_DEPTH)


def _dn_prep(p, halo, w_conv8, alog_row, dtb_row, chunk, group, tb, hps, precise):
    m = p.shape[0]
    op_dtype = F32 if precise else BF16
    slot = _dn_score_slot(chunk, group)
    nh = DN_HEADS // hps
    hw = hps * LANES
    assert P_CONV % hw == 0 and DN_WIDTH % hw == 0
    cq, ck, cv = P_CONV // hw, P_CONV // hw + nh, P_CONV // hw + 2 * nh
    col = lambda base: (lambda i, h: (i, base + h))
    halo_spec = lambda base: pl.BlockSpec((1, 8, hw), lambda i, h: (i, 0, base + h))
    w_spec = lambda base: pl.BlockSpec((8, hw), lambda i, h: (0, base + h))
    const = pl.BlockSpec((1, LANES), lambda i, h: (0, 0))
    head_blk = pl.BlockSpec((tb, hw), lambda i, h: (i, h))
    out_shape = [
        jax.ShapeDtypeStruct((m, DN_WIDTH), F32),
        jax.ShapeDtypeStruct((m, DN_WIDTH), op_dtype),
        jax.ShapeDtypeStruct((m, DN_WIDTH), op_dtype),
        jax.ShapeDtypeStruct((m, DN_WIDTH), op_dtype),
        jax.ShapeDtypeStruct((m, DN_HEADS * slot), op_dtype),
        jax.ShapeDtypeStruct((m // chunk, 1, DN_WIDTH), F32),
    ]
    return pl.pallas_call(
        functools.partial(_dn_prep_kernel, chunk, group, hps, precise),
        grid=(m // tb, nh),
        in_specs=[
            pl.BlockSpec((tb, hw), col(cq)),
            pl.BlockSpec((tb, hw), col(ck)),
            pl.BlockSpec((tb, hw), col(cv)),
            pl.BlockSpec((tb, LANES), lambda i, h: (i, P_BA // LANES)),
            halo_spec(0), halo_spec(nh), halo_spec(2 * nh),
            w_spec(0), w_spec(nh), w_spec(2 * nh),
            const, const,
        ],
        out_specs=[head_blk] * 4 + [pl.BlockSpec((tb, hps * slot), lambda i, h: (i, h)),
                                    pl.BlockSpec((tb // chunk, 1, hw), lambda i, h: (i, 0, h))],
        out_shape=out_shape,
        compiler_params=_cparams("parallel", "parallel"),
        name="dn_prep",
    )(p, p, p, p, halo, halo, halo, w_conv8, w_conv8, w_conv8, alog_row, dtb_row)


def _dn_scan_kernel(chunk, group, n_chunks, wv_ref, wk_ref, qd_ref, ke_ref, p_ref, ge_ref, gate_ref, s0_ref,
                    onorm_ref, od_ref, sout_ref, s_scr, u_scr):
    n = pl.program_id(1)
    precise = wk_ref.dtype == F32
    slot = _dn_score_slot(chunk, group)

    @pl.when(n == 0)
    def _():
        s_scr[...] = s0_ref[0]
        u_scr[...] = jnp.zeros_like(u_scr)

    onorm = onorm_ref[...]
    for c in range(n_chunks):
        rows = slice(c * chunk, (c + 1) * chunk)
        group_rows = slice((c % group) * chunk, (c % group + 1) * chunk)
        for h in range(DN_HEADS):
            cols = slice(h * LANES, (h + 1) * LANES)
            s = s_scr[h]
            if not precise:
                s = s.astype(BF16)
            u = wv_ref[rows, cols] - _dot_any(wk_ref[rows, cols], s, precise)
            if not precise:
                u = u.astype(BF16)
            u_scr[h, group_rows, :] = u
            o = (_dot_any(qd_ref[rows, cols], s, precise)
                 + _dot_any(p_ref[rows, h * slot:(h + 1) * slot], u_scr[h], precise))
            s_scr[h] = ge_ref[c, :, cols] * s_scr[h] + _dot_any(ke_ref[rows, cols], u, precise, _dot_tn, 1)
            gate = gate_ref[rows, cols]
            od_ref[rows, cols] = (_rms(o, onorm) * _silu(gate)).astype(od_ref.dtype)

    @pl.when(n == pl.num_programs(1) - 1)
    def _():
        sout_ref[0] = s_scr[...]


def _dn_scan(prep, p, s0, onorm_row, chunk, group, n_chunks, batch):
    wv, wk, qdec, kend, pm, gend = prep
    m = wv.shape[0]
    assert n_chunks % group == 0
    rows = chunk * n_chunks
    steps = m // batch // rows
    blk = lambda b, n: (b * steps + n, 0)
    wide = pl.BlockSpec((rows, DN_WIDTH), blk)
    state = pl.BlockSpec((1, DN_HEADS, DN_KEY_DIM, DN_VAL_DIM), lambda b, n: (b, 0, 0, 0))
    slot = _dn_score_slot(chunk, group)
    assert pm.shape[1] == DN_HEADS * slot
    return pl.pallas_call(
        functools.partial(_dn_scan_kernel, chunk, group, n_chunks),
        grid=(batch, steps),
        in_specs=[
            wide, wide, wide, wide, pl.BlockSpec((rows, DN_HEADS * slot), blk),
            pl.BlockSpec((n_chunks, 1, DN_WIDTH), lambda b, n: (b * steps + n, 0, 0)),
            pl.BlockSpec((rows, DN_WIDTH), lambda b, n: (b * steps + n, P_GATE // DN_WIDTH)),
            state,
            pl.BlockSpec((1, LANES), lambda b, n: (0, 0)),
        ],
        out_specs=[wide, state],
        out_shape=[jax.ShapeDtypeStruct((m, DN_WIDTH), wk.dtype),
                   jax.ShapeDtypeStruct((batch, DN_HEADS, DN_KEY_DIM, DN_VAL_DIM), F32)],
        scratch_shapes=[pltpu.VMEM((DN_HEADS, DN_KEY_DIM, DN_VAL_DIM), F32),
                        pltpu.VMEM((DN_HEADS, slot, DN_VAL_DIM), wk.dtype)],
        compiler_params=_cparams("parallel", "arbitrary"),
        name="dn_scan",
    )(wv, wk, qdec, kend, pm, gend, p, s0, onorm_row)


def _out_proj_kernel(nw, attn_ref, od_ref, *refs):
    w_refs, (x_ref, g_ref, gate_ref, o_ref) = refs[:nw], refs[nw:]
    y = (_mm(attn_ref[...], tuple(r[0:ATTN_WIDTH, :] for r in w_refs))
         + _mm(od_ref[...], tuple(r[ATTN_WIDTH:, :] for r in w_refs)))
    o_ref[...] = x_ref[...] + gate_ref[...] * _rms(y, g_ref[...])


def _out_proj(attn, od, w, x, gain, gate, tm):
    m, d = x.shape
    row = lambda i: (i, 0)
    return pl.pallas_call(
        functools.partial(_out_proj_kernel, len(w)),
        grid=(m // tm,),
        in_specs=[
            pl.BlockSpec((tm, ATTN_WIDTH), row),
            pl.BlockSpec((tm, DN_WIDTH), row),
        ] + [pl.BlockSpec((ATTN_WIDTH + DN_WIDTH, d), lambda i: (0, 0))] * len(w) + [
            pl.BlockSpec((tm, d), row),
            pl.BlockSpec((1, d), lambda i: (0, 0)),
            _mod_spec(gate.shape[0], tm, d),
        ],
        out_specs=pl.BlockSpec((tm, d), row),
        out_shape=jax.ShapeDtypeStruct((m, d), F32),
        compiler_params=_cparams("parallel"),
        name="out_proj",
    )(attn, od, *w, x, gain, gate)


def _ffn_kernel(nw, x_ref, g_ref, sc_ref, sh_ref, *refs):
    wg_refs, wu_refs, wd_refs = refs[:nw], refs[nw:2 * nw], refs[2 * nw:3 * nw]
    g2_ref, gate_ref, o_ref, h_ref, acc_ref = refs[3 * nw:]
    j = pl.program_id(1)

    @pl.when(j == 0)
    def _():
        h = _rms(x_ref[...], g_ref[...]) * (1.0 + sc_ref[...]) + sh_ref[...]
        h_ref[...] = h.astype(h_ref.dtype)
        acc_ref[...] = jnp.zeros_like(acc_ref)

    h = h_ref[...]
    act = _silu(_mm(h, tuple(r[...] for r in wg_refs))) * _mm(h, tuple(r[...] for r in wu_refs))
    acc_ref[...] += _mm(act, tuple(r[...] for r in wd_refs))

    @pl.when(j == pl.num_programs(1) - 1)
    def _():
        o_ref[...] = x_ref[...] + gate_ref[...] * _rms(acc_ref[...], g2_ref[...])


def _ffn(x, gain, scale, shift, wg, wu, wd, gain2, gate, tm, tf):
    m, d = x.shape
    nw = len(wg)
    f = wg[0].shape[1]
    row = lambda i, j: (i, 0)
    vec = pl.BlockSpec((1, d), lambda i, j: (0, 0))
    return pl.pallas_call(
        functools.partial(_ffn_kernel, nw),
        grid=(m // tm, f // tf),
        in_specs=[
            pl.BlockSpec((tm, d), row), vec,
            _mod_spec(scale.shape[0], tm, d), _mod_spec(shift.shape[0], tm, d),
        ] + [pl.BlockSpec((d, tf), lambda i, j: (0, j))] * (2 * nw)
        + [pl.BlockSpec((tf, d), lambda i, j: (j, 0))] * nw
        + [vec, _mod_spec(gate.shape[0], tm, d)],
        out_specs=pl.BlockSpec((tm, d), row),
        out_shape=jax.ShapeDtypeStruct((m, d), F32),
        scratch_shapes=[pltpu.VMEM((tm, d), BF16 if wg[0].dtype == BF16 and nw == 1 else F32),
                        pltpu.VMEM((tm, d), F32)],
        compiler_params=_cparams("parallel", "arbitrary"),
        name="ffn_dense",
    )(x, gain, scale, shift, *wg, *wu, *wd, gain2, gate)


def _router_kernel(x_ref, g_ref, sc_ref, sh_ref, wr_ref, h_ref, gates_ref, idx_ref, w12_ref):
    h = _rms(x_ref[...], g_ref[...]) * (1.0 + sc_ref[...]) + sh_ref[...]
    h_ref[...] = h
    logits = _dot_x3(h, wr_ref[...])
    lane = lax.broadcasted_iota(jnp.int32, logits.shape, 1).astype(F32)
    logits = jnp.where(lane < N_EXPERTS, logits, -jnp.inf)
    m1 = jnp.max(logits, -1, keepdims=True)
    i1 = jnp.min(jnp.where(logits == m1, lane, float(LANES)), -1, keepdims=True)
    rest = jnp.where(lane == i1, -jnp.inf, logits)
    m2 = jnp.max(rest, -1, keepdims=True)
    i2 = jnp.min(jnp.where(rest == m2, lane, float(LANES)), -1, keepdims=True)
    t = jnp.exp(m2 - m1)
    w1 = 1.0 / (1.0 + t)
    w2 = t / (1.0 + t)
    gates_ref[...] = jnp.where(lane == i1, w1, 0.0) + jnp.where(lane == i2, w2, 0.0)
    idx_ref[...] = jnp.where(lane == 0.0, i1, jnp.where(lane == 1.0, i2, 0.0)).astype(jnp.int32)
    w12_ref[...] = jnp.where(lane == 0.0, w1, jnp.where(lane == 1.0, w2, 0.0))


def _router(x, gain, scale, shift, w_router_pad, tm):
    m, d = x.shape
    row = lambda i: (i, 0)
    vec = pl.BlockSpec((1, d), lambda i: (0, 0))
    small = pl.BlockSpec((tm, LANES), row)
    return pl.pallas_call(
        _router_kernel,
        grid=(m // tm,),
        in_specs=[pl.BlockSpec((tm, d), row), vec,
                  _mod_spec(scale.shape[0], tm, d), _mod_spec(shift.shape[0], tm, d),
                  pl.BlockSpec((d, LANES), lambda i: (0, 0))],
        out_specs=[pl.BlockSpec((tm, d), row), small, small, small],
        out_shape=[jax.ShapeDtypeStruct((m, d), F32), jax.ShapeDtypeStruct((m, LANES), F32),
                   jax.ShapeDtypeStruct((m, LANES), jnp.int32), jax.ShapeDtypeStruct((m, LANES), F32)],
        compiler_params=_cparams("parallel"),
        name="moe_router",
    )(x, gain, scale, shift, w_router_pad)


def _moe_gemm_kernel(nj, te_ref, tot_ref, rt_ref, h_hbm, wg_ref, wu_ref, wd_ref, ys_ref, xs_ref, xb_ref, acc_ref,
                     sems):
    r = pl.program_id(0)
    j = pl.program_id(1)
    tm = xb_ref.shape[0]
    total = tot_ref[0]
    active = r < total
    slot = r % 2
    share = -(-tm // nj)

    def row_copy(tile, t, s):
        return pltpu.make_async_copy(h_hbm.at[pl.ds(rt_ref[tile * tm + t], 1), :],
                                     xs_ref.at[s, pl.ds(t, 1), :], sems.at[s])

    def start_rows(tile, s, lo, hi):
        def body(t, carry):
            row_copy(tile, t, s).start()
            return carry

        lax.fori_loop(lo, hi, body, 0)

    @pl.when(jnp.logical_and(r == 0, j == 0))
    def _():
        start_rows(0, 0, 0, tm)

    @pl.when(jnp.logical_and(active, j == 0))
    def _():
        pltpu.make_async_copy(h_hbm.at[pl.ds(0, tm), :], xs_ref.at[slot], sems.at[slot]).wait()
        xb_ref[...] = xs_ref[slot].astype(BF16)
        acc_ref[...] = jnp.zeros_like(acc_ref)

    @pl.when(active)
    def _():
        more = r + 1 < total
        for u in range(share):
            t = j * share + u

            @pl.when(jnp.logical_and(more, t < tm))
            def _():
                row_copy(jnp.minimum(r + 1, pl.num_programs(0) - 1), jnp.minimum(t, tm - 1), 1 - slot).start()

        xb = xb_ref[...]
        act = (_silu(_dot(xb, wg_ref[0])) * _dot(xb, wu_ref[0])).astype(BF16)
        acc_ref[...] += _dot(act, wd_ref[0])

    @pl.when(j == pl.num_programs(1) - 1)
    def _():
        ys_ref[...] = jnp.where(active, acc_ref[...], 0.0)


def _moe_gemm(tile_expert, total_tiles, row_token, h, wg, wu, wd, tm, tf):
    n_tiles = tile_expert.shape[0]
    d = h.shape[1]
    f = wg.shape[2]
    nj = f // tf

    def w_col(r, j, te, tot, rt):
        return (te[r], 0, jnp.where(r < tot[0], j, nj - 1))

    def w_row(r, j, te, tot, rt):
        return (te[r], jnp.where(r < tot[0], j, nj - 1), 0)

    grid_spec = pltpu.PrefetchScalarGridSpec(
        num_scalar_prefetch=3,
        grid=(n_tiles, nj),
        in_specs=[
            pl.BlockSpec(memory_space=pl.ANY),
            pl.BlockSpec((1, d, tf), w_col),
            pl.BlockSpec((1, d, tf), w_col),
            pl.BlockSpec((1, tf, d), w_row),
        ],
        out_specs=pl.BlockSpec((tm, d), lambda r, j, te, tot, rt: (r, 0)),
        scratch_shapes=[pltpu.VMEM((2, tm, d), F32), pltpu.VMEM((tm, d), BF16), pltpu.VMEM((tm, d), F32),
                        pltpu.SemaphoreType.DMA((2,))],
    )
    return pl.pallas_call(
        functools.partial(_moe_gemm_kernel, nj),
        grid_spec=grid_spec,
        out_shape=jax.ShapeDtypeStruct((n_tiles * tm, d), F32),
        compiler_params=_cparams("arbitrary", "arbitrary", row_dma=True),
        name="moe_gemm",
    )(tile_expert, total_tiles, row_token, h, wg, wu, wd)


def _moe_combine_kernel(dest_ref, x_ref, w12_ref, g_ref, gate_ref, ys_hbm, o_ref, buf_ref, sems):
    i = pl.program_id(0)
    tb = x_ref.shape[0]
    slot = i % 2

    def row_copy(blk, t, k, s):
        src = dest_ref[2 * (blk * tb + t) + k]
        return pltpu.make_async_copy(ys_hbm.at[pl.ds(src, 1), :], buf_ref.at[s, k, pl.ds(t, 1), :], sems.at[s])

    def start_block(blk, s):
        def body(t, carry):
            row_copy(blk, t, 0, s).start()
            row_copy(blk, t, 1, s).start()
            return carry

        lax.fori_loop(0, tb, body, 0, unroll=8)

    @pl.when(i == 0)
    def _():
        start_block(0, 0)

    @pl.when(i + 1 < pl.num_programs(0))
    def _():
        start_block(i + 1, 1 - slot)

    for k in range(2):
        pltpu.make_async_copy(ys_hbm.at[pl.ds(0, tb), :], buf_ref.at[slot, k], sems.at[slot]).wait()
    w12 = w12_ref[...]
    y = w12[:, 0:1] * buf_ref[slot, 0] + w12[:, 1:2] * buf_ref[slot, 1]
    o_ref[...] = x_ref[...] + gate_ref[...] * _rms(y, g_ref[...])


def _moe_combine(dest, x, w12, gain, gate, ys, tb):
    m, d = x.shape
    row = lambda i, dst: (i, 0)
    grid_spec = pltpu.PrefetchScalarGridSpec(
        num_scalar_prefetch=1,
        grid=(m // tb,),
        in_specs=[
            pl.BlockSpec((tb, d), row),
            pl.BlockSpec((tb, LANES), row),
            pl.BlockSpec((1, d), lambda i, dst: (0, 0)),
            pl.BlockSpec((1, d), lambda i, dst: (0, 0)),
            pl.BlockSpec(memory_space=pl.ANY),
        ],
        out_specs=pl.BlockSpec((tb, d), row),
        scratch_shapes=[pltpu.VMEM((2, 2, tb, d), F32), pltpu.SemaphoreType.DMA((2,))],
    )
    return pl.pallas_call(
        _moe_combine_kernel,
        grid_spec=grid_spec,
        out_shape=jax.ShapeDtypeStruct((m, d), F32),
        compiler_params=_cparams("arbitrary", row_dma=True),
        name="moe_combine",
    )(dest, x, w12, gain, gate, ys)


def _route_tables(idx2, tm, n_tiles):
    m = idx2.shape[0]
    n_assign = 2 * m
    assert n_tiles * tm == n_assign + N_EXPERTS * tm
    experts = jnp.arange(N_EXPERTS, dtype=jnp.int32)
    e_flat = idx2.reshape(n_assign)
    onehot = (e_flat[:, None] == experts[None, :]).astype(jnp.int32)
    csum = jnp.cumsum(onehot, 0)
    counts = csum[-1]
    padded = ((counts + tm - 1) // tm) * tm
    pend = jnp.cumsum(padded)
    pstart = pend - padded
    dest = jnp.sum(onehot * (pstart[None, :] + csum - 1), -1).astype(jnp.int32)
    total_tiles = (pend[-1] // tm).astype(jnp.int32).reshape(1)
    tile_expert = jnp.minimum(
        jnp.searchsorted(pend // tm, jnp.arange(n_tiles, dtype=jnp.int32), side="right"), N_EXPERTS - 1
    ).astype(jnp.int32)
    filler_key = jnp.where(jnp.arange(tm, dtype=jnp.int32)[None, :] < (padded - counts)[:, None],
                           experts[:, None], N_EXPERTS).reshape(-1)
    keys = jnp.concatenate([e_flat, filler_key])
    tokens = jnp.concatenate([jnp.arange(n_assign, dtype=jnp.int32) // 2,
                              jnp.zeros((N_EXPERTS * tm,), jnp.int32)])
    _, row_token = lax.sort((keys, tokens), num_keys=1, is_stable=True)
    return tile_expert, total_tiles, row_token, dest


def _moe_dense_kernel(h_ref, gates_ref, wg_ref, wu_ref, wd_ref, x_ref, g_ref, gate_ref, o_ref, acc_ref, tot_ref):
    e = pl.program_id(0)
    j = pl.program_id(1)
    nj = pl.num_programs(1)

    @pl.when(jnp.logical_and(e == 0, j == 0))
    def _():
        tot_ref[...] = jnp.zeros_like(tot_ref)

    @pl.when(j == 0)
    def _():
        acc_ref[...] = jnp.zeros_like(acc_ref)

    h = h_ref[...].astype(BF16)
    act = (_silu(_dot(h, wg_ref[0])) * _dot(h, wu_ref[0])).astype(BF16)
    acc_ref[...] += _dot(act, wd_ref[0])

    @pl.when(j == nj - 1)
    def _():
        gates = gates_ref[...]
        lane = lax.broadcasted_iota(jnp.int32, gates.shape, 1)
        ge = jnp.sum(jnp.where(lane == e, gates, 0.0), -1, keepdims=True)
        tot_ref[...] += ge * acc_ref[...]

    @pl.when(jnp.logical_and(e == pl.num_programs(0) - 1, j == nj - 1))
    def _():
        o_ref[...] = x_ref[...] + gate_ref[...] * _rms(tot_ref[...], g_ref[...])


def _moe_dense(h, gates, wg, wu, wd, x, gain, gate, tf):
    m, d = x.shape
    f = wg.shape[2]
    full = pl.BlockSpec((m, d), lambda e, j: (0, 0))
    return pl.pallas_call(
        _moe_dense_kernel,
        grid=(N_EXPERTS, f // tf),
        in_specs=[
            full,
            pl.BlockSpec((m, LANES), lambda e, j: (0, 0)),
            pl.BlockSpec((1, d, tf), lambda e, j: (e, 0, j)),
            pl.BlockSpec((1, d, tf), lambda e, j: (e, 0, j)),
            pl.BlockSpec((1, tf, d), lambda e, j: (e, j, 0)),
            full,
            pl.BlockSpec((1, d), lambda e, j: (0, 0)),
            full,
        ],
        out_specs=full,
        out_shape=jax.ShapeDtypeStruct((m, d), F32),
        scratch_shapes=[pltpu.VMEM((m, d), F32), pltpu.VMEM((m, d), F32)],
        compiler_params=_cparams("arbitrary", "arbitrary"),
        name="moe_dense",
    )(h, gates, wg, wu, wd, x, gain, gate)


def _rope_tables(pos):
    half = ROPE_DIM // 2
    inv_freq = jnp.power(ROPE_THETA, -2.0 * jnp.arange(half, dtype=F32) / ROPE_DIM)
    ang = pos.astype(F32)[:, None] * inv_freq[None, :]
    cos, sin = jnp.cos(ang), jnp.sin(ang)
    t = pos.shape[0]
    rest = ATTN_HEAD_DIM - ROPE_DIM
    cos_h = jnp.concatenate([cos, cos, jnp.ones((t, rest), F32)], 1)
    sa_h = jnp.concatenate([-sin, jnp.zeros((t, half + rest), F32)], 1)
    sb_h = jnp.concatenate([jnp.zeros((t, half), F32), sin, jnp.zeros((t, rest), F32)], 1)
    rep = LANES // ATTN_HEAD_DIM
    return tuple(jnp.tile(a, (1, rep)) for a in (cos_h, sa_h, sb_h))


def _permute_w_in(w):
    o1 = ATTN_WIDTH
    o2 = o1 + KV_WIDTH
    o3 = o2 + KV_WIDTH
    o4 = o3 + DN_CONV_CH
    o5 = o4 + DN_WIDTH
    parts = [w[:, :o1], w[:, o4:o5], w[:, o3:o4], w[:, o1:o2], w[:, o2:o3], w[:, o5:]]
    used = sum(a.shape[1] for a in parts)
    parts.append(jnp.zeros((w.shape[0], P_WIDTH - used), w.dtype))
    return jnp.concatenate(parts, 1)


def _sample_mask_bias(s, lc):
    q_pos = PAST_LEN + np.arange(s)
    k_pos = np.concatenate([PAST_LEN - lc + np.arange(lc), q_pos])
    q_chunk = q_pos[:, None] // CHUNK
    k_chunk = k_pos[None, :] // CHUNK
    mask = (k_pos[None, :] >= 0) & (k_chunk <= q_chunk) & (k_pos[None, :] >= q_chunk * CHUNK - WINDOW)
    bias = np.where(mask, 0.0, -np.inf).astype(np.float32)
    return jnp.asarray(np.tile(bias, (4, 1)))


def _conv_halo(p, init, tb, seq):
    m = p.shape[0]
    batch = m // seq
    nb = seq // tb
    tails = p.reshape(batch, nb, tb, P_WIDTH)[:, :nb - 1, tb - (CONV_WIDTH - 1):, P_CONV:P_CONV + DN_CONV_CH]
    prev = jnp.concatenate([init[:, None], tails], 1)
    prev = prev.reshape(batch * nb, CONV_WIDTH - 1, DN_CONV_CH)
    return jnp.pad(prev, ((0, 0), (8 - (CONV_WIDTH - 1), 0), (0, 0)))


def _trunk(x, mods, layer_w, rope, past, cfg):
    m = x.shape[0]
    batch, seq = cfg["batch"], cfg["seq"]
    precise = cfg["precise"]
    nw = 2 if precise else 1
    ks, vs, ss, bufs = [], [], [], []
    cos, sa, sb = rope
    for l in range(DEPTH):
        w = layer_w[l]
        sh_a, sc_a, g_a, sh_f, sc_f, g_f = mods[l]
        p = _norm_proj(x, w["gain"][0], sc_a, sh_a, w["w_in"][:nw], cfg["tm_proj"], cfg["tn_proj"])
        if past is None:
            attn, k_new = _attn_prompt(p, w["sinks"], cos, sa, sb, cfg["tb_attn"])
            s0 = jnp.zeros((batch, DN_HEADS, DN_KEY_DIM, DN_VAL_DIM), F32)
            conv_init = jnp.zeros((batch, CONV_WIDTH - 1, DN_CONV_CH), F32)
        else:
            ck = past[0][l].reshape(batch, -1, KV_WIDTH)
            cv = past[1][l].reshape(batch, -1, KV_WIDTH)
            attn, k_new = _attn_sample(p, ck, cv, w["sinks"], cos, sa, sb, cfg["bias"], batch, seq)
            s0 = past[2][l]
            conv_init = past[3][l]
        halo = _conv_halo(p, conv_init, cfg["tb_dn"], seq)
        prep = _dn_prep(p, halo, w["w_conv"], w["alog"], w["dtb"], cfg["chunk"], cfg["group"], cfg["tb_dn"],
                        cfg["dn_heads_per_step"], precise)
        od, s_new = _dn_scan(prep, p, s0, w["onorm"], cfg["chunk"], cfg["group"], cfg["scan_chunks"], batch)
        pick = 1 if precise else 0
        x = _out_proj(attn, od, w["w_out"][pick], x, w["gain"][1], g_a, cfg["tm_out"])
        if l % 2 == 0:
            x = _ffn(x, w["gain"][2], sc_f, sh_f, w["ffn_gate"][pick], w["ffn_up"][pick], w["ffn_down"][pick],
                     w["gain"][3], g_f, cfg["tm_ffn"], cfg["tf_ffn"])
        else:
            h, gates, idx, w12 = _router(x, w["gain"][2], sc_f, sh_f, w["router"], cfg["tm_router"])
            if cfg["routed"]:
                tm = cfg["tm_moe"]
                n_tiles = 2 * m // tm + N_EXPERTS
                tile_expert, total_tiles, row_token, dest = _route_tables(idx[:, :2], tm, n_tiles)
                ys = _moe_gemm(tile_expert, total_tiles, row_token, h, w["moe_gate"], w["moe_up"], w["moe_down"],
                               tm, cfg["tf_moe"])
                x = _moe_combine(dest, x, w12, w["gain"][3], g_f, ys, cfg["tb_combine"])
            else:
                x = _moe_dense(h, gates, w["moe_gate"], w["moe_up"], w["moe_down"], x, w["gain"][3], g_f,
                               cfg["tf_moe"])
        pb = p.reshape(batch, seq, P_WIDTH)
        keep = min(WINDOW, seq) if past is None else seq
        ks.append(k_new.reshape(batch, seq, ATTN_KV_HEADS, ATTN_HEAD_DIM)[:, seq - keep:])
        vs.append(pb[:, seq - keep:, P_V:P_V + KV_WIDTH].reshape(batch, keep, ATTN_KV_HEADS, ATTN_HEAD_DIM))
        ss.append(s_new)
        assert seq >= CONV_WIDTH - 1
        bufs.append(pb[:, seq - (CONV_WIDTH - 1):, P_CONV:P_CONV + DN_CONV_CH])
    return x, jnp.stack(ks), jnp.stack(vs), jnp.stack(ss), jnp.stack(bufs)


def kernel(x_prompt, x_sample, cache_attn_k, cache_attn_v, state_delta, state_conv, c_prompt, c_sample, w_in, w_conv, attn_sinks, dn_a_log, dn_dt_bias, dn_norm, w_out, w_mod, b_mod, norm_gains, ffn_gate, ffn_up, ffn_down, moe_router, moe_gate, moe_up, moe_down):
    bp, tp, d = x_prompt.shape
    bs, ts, _ = x_sample.shape
    assert bp == 1 and d == D_MODEL

    c_all = jnp.concatenate([c_prompt, c_sample, jnp.zeros((16 - bp - bs, d), F32)], 0)
    mod = _modulation(c_all, w_mod, b_mod)
    mods_p, mods_s = [], []
    for l in range(DEPTH):
        six = jnp.split(mod[l], 6, -1)
        mods_p.append([a[0:bp] for a in six])
        mods_s.append([jnp.repeat(a[bp:bp + bs], ts, axis=0) for a in six])

    def pad_lanes(v, at):
        return jnp.zeros((1, LANES), F32).at[0, at:at + v.shape[0]].set(v)

    def both(w):
        return ((w.astype(BF16),), (w,))

    layer_w = []
    for l in range(DEPTH):
        w = {
            "gain": [norm_gains[l, i].reshape(1, d) for i in range(4)],
            "w_in": _split_weight(_permute_w_in(w_in[l])),
            "sinks": attn_sinks[l],
            "w_conv": jnp.pad(w_conv[l], ((0, 8 - CONV_WIDTH), (0, 0))),
            "alog": pad_lanes(dn_a_log[l], DN_HEADS),
            "dtb": pad_lanes(dn_dt_bias[l], DN_HEADS),
            "onorm": dn_norm[l].reshape(1, DN_VAL_DIM),
            "w_out": both(w_out[l]),
        }
        if l % 2 == 0:
            w["ffn_gate"] = both(ffn_gate[l // 2])
            w["ffn_up"] = both(ffn_up[l // 2])
            w["ffn_down"] = both(ffn_down[l // 2])
        else:
            w["router"] = jnp.pad(moe_router[l // 2], ((0, 0), (0, LANES - N_EXPERTS)))
            w["moe_gate"] = moe_gate[l // 2].astype(BF16)
            w["moe_up"] = moe_up[l // 2].astype(BF16)
            w["moe_down"] = moe_down[l // 2].astype(BF16)
        layer_w.append(w)

    cfg_p = dict(batch=bp, seq=tp, precise=False, chunk=CHUNK, group=2, scan_chunks=4, tm_proj=1024, tn_proj=512, tb_attn=512,
                 tb_dn=1024, dn_heads_per_step=1, tm_out=512, tm_ffn=512, tf_ffn=512, tm_router=512, routed=True, tm_moe=512,
                 tf_moe=256, tb_combine=256)
    rope_p = _rope_tables(jnp.arange(tp, dtype=jnp.int32))
    y_p, k_p, v_p, s_p, conv_p = _trunk(x_prompt.reshape(bp * tp, d), mods_p, layer_w, rope_p, None, cfg_p)

    ms = bs * ts
    cfg_s = dict(batch=bs, seq=ts, precise=True, chunk=ts, group=1, scan_chunks=1, tm_proj=ms, tn_proj=512, tb_dn=ts, dn_heads_per_step=DN_HEADS, tm_out=ms,
                 tm_ffn=ms, tf_ffn=512, tm_router=ms, routed=False, tf_moe=1408,
                 bias=_sample_mask_bias(ts, cache_attn_k.shape[2]))
    rope_s = _rope_tables(PAST_LEN + jnp.arange(ts, dtype=jnp.int32))
    past = (cache_attn_k, cache_attn_v, state_delta, state_conv)
    y_s, k_s, v_s, s_s, conv_s = _trunk(x_sample.reshape(ms, d), mods_s, layer_w, rope_s, past, cfg_s)

    return (y_p.reshape(bp, tp, d), y_s.reshape(bs, ts, d), k_p, v_p, s_p, conv_p, k_s, v_s, s_s, conv_s)
```

```python
import functools
import math

import numpy as np
import jax
import jax.numpy as jnp
from jax import lax
from jax.experimental import pallas as pl
from jax.experimental.pallas import tpu as pltpu

D_MODEL = 2048
DEPTH = 2
PAST_LEN = 1024
CHUNK = 64
ATTN_HEADS = 16
ATTN_KV_HEADS = 2
ATTN_HEAD_DIM = 64
ATTN_WIDTH = 1024
KV_WIDTH = 128
WINDOW = 128
ROPE_THETA = 500000.0
ROPE_DIM = 16
DN_HEADS = 8
DN_KEY_DIM = 128
DN_VAL_DIM = 128
DN_WIDTH = 1024
CONV_WIDTH = 4
DN_CONV_CH = 3072
D_FF = 5632
N_EXPERTS = 8
D_FF_EXPERT = 2816
EPS = 1e-6

F32 = jnp.float32
BF16 = jnp.bfloat16
LANES = 128
V7X_MXU_DEPTH = 256
V7X_VMEM_LIMIT = 56 * 1024 * 1024

P_Q = 0
P_GATE = 1024
P_CONV = 2048
P_K = 5120
P_V = 5248
P_BA = 5376
P_WIDTH = 5632


def _cparams(*sem, row_dma=False):
    return pltpu.CompilerParams(dimension_semantics=sem, vmem_limit_bytes=V7X_VMEM_LIMIT,
                                disable_bounds_checks=row_dma)


def _sigmoid(x):
    return 0.5 * jnp.tanh(0.5 * x) + 0.5


def _silu(x):
    return x * _sigmoid(x)


def _rms(x, gain):
    return x * lax.rsqrt(jnp.mean(x * x, -1, keepdims=True) + EPS) * gain


def _dot(a, b):
    return jnp.dot(a, b, preferred_element_type=F32)


def _dot_nt(a, b):
    return lax.dot_general(a, b, (((1,), (1,)), ((), ())), preferred_element_type=F32)


def _dot_tn(a, b):
    return lax.dot_general(a, b, (((0,), (0,)), ((), ())), preferred_element_type=F32)


def _split_bf16(a):
    hi = a.astype(BF16)
    lo = (a - hi.astype(F32)).astype(BF16)
    return hi, lo


def _dot_x3(a, b, dot=_dot, out_axis=0):
    a_hi, a_lo = _split_bf16(a)
    b_hi, b_lo = _split_bf16(b)
    n = a.shape[out_axis]
    top = dot(jnp.concatenate([a_hi, a_lo], out_axis), b_hi)
    return top[:n] + top[n:] + dot(a_hi, b_lo)


def _dot_any(a, b, precise, dot=_dot, out_axis=0):
    if precise:
        return _dot_x3(a, b, dot, out_axis)
    return dot(a.astype(BF16), b.astype(BF16))


def _mm(a, w):
    if len(w) == 1 and w[0].dtype == F32:
        return _dot_x3(a, w[0])
    if len(w) == 1:
        return _dot(a.astype(BF16), w[0])
    a_hi, a_lo = _split_bf16(a)
    n = a.shape[0]
    top = _dot(jnp.concatenate([a_hi, a_lo], 0), w[0])
    return top[:n] + top[n:] + _dot(a_hi, w[1])


def _split_weight(w):
    hi, lo = _split_bf16(w)
    return (hi, lo)


def _mod_spec(rows, tm, d):
    if rows == 1:
        return pl.BlockSpec((1, d), lambda i, *_: (0, 0))
    return pl.BlockSpec((tm, d), lambda i, *_: (i, 0))


def _mod_kernel(c_ref, w_ref, b_ref, o_ref):
    o_ref[0] = _dot_x3(_silu(c_ref[...]), w_ref[0]) + b_ref[0]


def _modulation(c_all, w_mod, b_mod):
    rows = c_all.shape[0]
    n = w_mod.shape[2]
    tn = 1024
    return pl.pallas_call(
        _mod_kernel,
        grid=(DEPTH, n // tn),
        in_specs=[
            pl.BlockSpec((rows, D_MODEL), lambda l, j: (0, 0)),
            pl.BlockSpec((1, D_MODEL, tn), lambda l, j: (l, 0, j)),
            pl.BlockSpec((1, 1, tn), lambda l, j: (l, 0, j)),
        ],
        out_specs=pl.BlockSpec((1, rows, tn), lambda l, j: (l, 0, j)),
        out_shape=jax.ShapeDtypeStruct((DEPTH, rows, n), F32),
        compiler_params=_cparams("parallel", "parallel"),
        name="modulation",
    )(c_all, w_mod, b_mod.reshape(DEPTH, 1, n))


def _norm_proj_kernel(nw, x_ref, g_ref, sc_ref, sh_ref, *refs):
    w_refs, (o_ref, h_ref) = refs[:nw], refs[nw:]

    @pl.when(pl.program_id(1) == 0)
    def _():
        h = _rms(x_ref[...], g_ref[...]) * (1.0 + sc_ref[...]) + sh_ref[...]
        h_ref[...] = h.astype(h_ref.dtype)

    o_ref[...] = _mm(h_ref[...], tuple(r[...] for r in w_refs))


def _norm_proj(x, gain, scale, shift, w, tm, tn):
    m, d = x.shape
    n = w[0].shape[1]
    return pl.pallas_call(
        functools.partial(_norm_proj_kernel, len(w)),
        grid=(m // tm, n // tn),
        in_specs=[
            pl.BlockSpec((tm, d), lambda i, j: (i, 0)),
            pl.BlockSpec((1, d), lambda i, j: (0, 0)),
            _mod_spec(scale.shape[0], tm, d),
            _mod_spec(shift.shape[0], tm, d),
        ] + [pl.BlockSpec((d, tn), lambda i, j: (0, j))] * len(w),
        out_specs=pl.BlockSpec((tm, tn), lambda i, j: (i, j)),
        out_shape=jax.ShapeDtypeStruct((m, n), F32),
        scratch_shapes=[pltpu.VMEM((tm, d), BF16 if len(w) == 1 else F32)],
        compiler_params=_cparams("parallel", "arbitrary"),
        name="norm_proj",
    )(x, gain, scale, shift, *w)


def _rope(x, cos, sa, sb):
    return x * cos + pltpu.roll(x, LANES - 8, 1) * sa + pltpu.roll(x, 8, 1) * sb


def _kv_variants(k, v):
    lo = lax.broadcasted_iota(jnp.int32, k.shape, 1) < ATTN_HEAD_DIM
    kr = pltpu.roll(k, ATTN_HEAD_DIM, 1)
    vr = pltpu.roll(v, ATTN_HEAD_DIM, 1)
    zero = jnp.zeros_like(k)
    k_lo = (jnp.where(lo, k, zero), jnp.where(lo, kr, zero))
    k_hi = (jnp.where(lo, zero, kr), jnp.where(lo, zero, k))
    v_lo = (jnp.where(lo, v, zero), jnp.where(lo, vr, zero))
    v_hi = (jnp.where(lo, zero, vr), jnp.where(lo, zero, v))
    return k_lo, k_hi, v_lo, v_hi


def _sink_softmax(s, sink):
    m = jnp.maximum(jnp.max(s, -1, keepdims=True), sink)
    p = jnp.exp(s - m)
    den = jnp.sum(p, -1, keepdims=True) + jnp.exp(sink - m)
    return p * (1.0 / den)


def _attn_core(qbs, k_los, k_his, v_los, v_his, bias, sinks, precise=False):
    scores = [(_dot_any(qb, k_lo, precise, _dot_nt) + bias, _dot_any(qb, k_hi, precise, _dot_nt) + bias)
              for qb, k_lo, k_hi in zip(qbs, k_los, k_his)]
    probs = [(_sink_softmax(s_even, sink[0]), _sink_softmax(s_odd, sink[1]))
             for (s_even, s_odd), sink in zip(scores, sinks)]
    return [_dot_any(p_even, v_lo, precise) + _dot_any(p_odd, v_hi, precise)
            for (p_even, p_odd), v_lo, v_hi in zip(probs, v_los, v_his)]


def _sink_columns(sink_ref, rows_per_pair):
    n = 4 * rows_per_pair
    pair = lax.broadcasted_iota(jnp.int32, (n, 1), 0) // rows_per_pair
    out = []
    for j in range(ATTN_KV_HEADS):
        cols = []
        for par in range(2):
            col = jnp.zeros((n, 1), F32)
            for a in range(4):
                col = jnp.where(pair == a, sink_ref[8 * j + 2 * a + par], col)
            cols.append(col)
        out.append(cols)
    return out


def _attn_prompt_kernel(sink_ref, q_ref, kv_ref, cos_ref, sa_ref, sb_ref, o_ref, knew_ref,
                        qs_ref, klo_ref, khi_ref, vlo_ref, vhi_ref):
    i = pl.program_id(0)
    tb = q_ref.shape[0]
    bufs = (klo_ref, khi_ref, vlo_ref, vhi_ref)

    @pl.when(i == 0)
    def _():
        for r in bufs:
            r[:, 0:WINDOW, :] = jnp.zeros((ATTN_KV_HEADS, WINDOW, LANES), BF16)

    @pl.when(i > 0)
    def _():
        for r in bufs:
            r[:, 0:WINDOW, :] = r[:, tb:tb + WINDOW, :]

    cos, sa, sb = cos_ref[...], sa_ref[...], sb_ref[...]
    k = _rope(kv_ref[:, 0:LANES], cos, sa, sb)
    knew_ref[...] = k
    variants = _kv_variants(k, kv_ref[:, LANES:2 * LANES])
    for r, var in zip(bufs, variants):
        for j in range(ATTN_KV_HEADS):
            r[j, WINDOW:, :] = var[j].astype(BF16)
    scale = ATTN_HEAD_DIM ** -0.5
    for a in range(ATTN_WIDTH // LANES):
        cols = slice(a * LANES, (a + 1) * LANES)
        qs_ref[:, cols] = (_rope(q_ref[:, cols], cos, sa, sb) * scale).astype(BF16)

    sinks = _sink_columns(sink_ref, CHUNK)
    nk = WINDOW + CHUNK

    def chunk_body(c, carry):
        r0 = pl.multiple_of(c * CHUNK, CHUNK)
        kpos = i * tb - WINDOW + r0 + lax.broadcasted_iota(jnp.int32, (1, nk), 1)
        bias = jnp.where(kpos >= 0, 0.0, -jnp.inf).astype(F32)
        heads = range(ATTN_KV_HEADS)
        qbs = [jnp.concatenate(
            [qs_ref[pl.ds(r0, CHUNK), (4 * j + a) * LANES:(4 * j + a + 1) * LANES] for a in range(4)], 0)
            for j in heads]
        keys = pl.ds(r0, nk)
        outs = _attn_core(qbs, [klo_ref[j, keys, :] for j in heads], [khi_ref[j, keys, :] for j in heads],
                          [vlo_ref[j, keys, :] for j in heads], [vhi_ref[j, keys, :] for j in heads], bias, sinks)
        for j, o in zip(heads, outs):
            for a in range(4):
                o_ref[pl.ds(r0, CHUNK), (4 * j + a) * LANES:(4 * j + a + 1) * LANES] = (
                    o[a * CHUNK:(a + 1) * CHUNK].astype(BF16))
        return carry

    lax.fori_loop(0, tb // CHUNK, chunk_body, 0, unroll=4)


def _attn_prompt(p, sinks, cos, sa, sb, tb):
    t = p.shape[0]
    kv_blk = P_K // (2 * LANES)
    row = lambda i: (i, 0)
    return pl.pallas_call(
        _attn_prompt_kernel,
        grid=(t // tb,),
        in_specs=[
            pl.BlockSpec(memory_space=pltpu.SMEM),
            pl.BlockSpec((tb, ATTN_WIDTH), row),
            pl.BlockSpec((tb, 2 * LANES), lambda i: (i, kv_blk)),
            pl.BlockSpec((tb, LANES), row),
            pl.BlockSpec((tb, LANES), row),
            pl.BlockSpec((tb, LANES), row),
        ],
        out_specs=[pl.BlockSpec((tb, ATTN_WIDTH), row), pl.BlockSpec((tb, LANES), row)],
        out_shape=[jax.ShapeDtypeStruct((t, ATTN_WIDTH), BF16), jax.ShapeDtypeStruct((t, LANES), F32)],
        scratch_shapes=[pltpu.VMEM((tb, ATTN_WIDTH), BF16)]
        + [pltpu.VMEM((ATTN_KV_HEADS, tb + WINDOW, LANES), BF16) for _ in range(4)],
        compiler_params=_cparams("arbitrary"),
        name="attn_prompt",
    )(sinks, p, p, cos, sa, sb)


def _attn_sample_kernel(sink_ref, q_ref, kv_ref, ck_ref, cv_ref, cos_ref, sa_ref, sb_ref, bias_ref,
                        o_ref, knew_ref):
    s = q_ref.shape[0]
    cos, sa, sb = cos_ref[...], sa_ref[...], sb_ref[...]
    k = _rope(kv_ref[:, 0:LANES], cos, sa, sb)
    knew_ref[...] = k
    kk = jnp.concatenate([ck_ref[0], k], 0)
    vv = jnp.concatenate([cv_ref[0], kv_ref[:, LANES:2 * LANES]], 0)
    k_lo, k_hi, v_lo, v_hi = _kv_variants(kk, vv)
    sinks = _sink_columns(sink_ref, s)
    scale = ATTN_HEAD_DIM ** -0.5
    bias = bias_ref[...]
    qbs = [jnp.concatenate(
        [_rope(q_ref[:, (4 * j + a) * LANES:(4 * j + a + 1) * LANES], cos, sa, sb) * scale for a in range(4)], 0)
        for j in range(ATTN_KV_HEADS)]
    outs = _attn_core(qbs, k_lo, k_hi, v_lo, v_hi, bias, sinks, precise=True)
    for j, o in enumerate(outs):
        for a in range(4):
            o_ref[:, (4 * j + a) * LANES:(4 * j + a + 1) * LANES] = o[a * s:(a + 1) * s]


def _attn_sample(p, cache_k, cache_v, sinks, cos, sa, sb, bias, batch, s):
    lc = cache_k.shape[1]
    kv_blk = P_K // (2 * LANES)
    row = lambda b: (b, 0)
    const = lambda b: (0, 0)
    return pl.pallas_call(
        _attn_sample_kernel,
        grid=(batch,),
        in_specs=[
            pl.BlockSpec(memory_space=pltpu.SMEM),
            pl.BlockSpec((s, ATTN_WIDTH), row),
            pl.BlockSpec((s, 2 * LANES), lambda b: (b, kv_blk)),
            pl.BlockSpec((1, lc, LANES), lambda b: (b, 0, 0)),
            pl.BlockSpec((1, lc, LANES), lambda b: (b, 0, 0)),
            pl.BlockSpec((s, LANES), const),
            pl.BlockSpec((s, LANES), const),
            pl.BlockSpec((s, LANES), const),
            pl.BlockSpec((4 * s, lc + s), const),
        ],
        out_specs=[pl.BlockSpec((s, ATTN_WIDTH), row), pl.BlockSpec((s, LANES), row)],
        out_shape=[jax.ShapeDtypeStruct((batch * s, ATTN_WIDTH), F32),
                   jax.ShapeDtypeStruct((batch * s, LANES), F32)],
        compiler_params=_cparams("parallel"),
        name="attn_sample",
    )(sinks, p, p, cache_k, cache_v, cos, sa, sb, bias)


class _BlockDiag:
    def __init__(self, chunk, group):
        self.chunk, self.group = chunk, group
        n = chunk * group
        lane = lax.broadcasted_iota(jnp.int32, (chunk, n), 1)
        self.lane_block = lane // chunk
        self.eye = (lax.broadcasted_iota(jnp.int32, (chunk, n), 0) == lane % chunk).astype(F32)

    def wide(self, tall):
        c = self.chunk
        out = tall[0:c]
        for b in range(1, self.group):
            out = out + tall[b * c:(b + 1) * c]
        return out

    def expand(self, wide):
        if self.group == 1:
            return wide
        zero = jnp.zeros_like(wide)
        return jnp.concatenate([jnp.where(self.lane_block == b, wide, zero) for b in range(self.group)], 0)

    def rmul(self, lhs, wide):
        l_hi, l_lo = _split_bf16(lhs)
        w_hi, w_lo = _split_bf16(wide)
        m = lhs.shape[0]
        top = _dot(jnp.concatenate([l_hi, l_lo], 0), self.expand(w_hi))
        return top[:m] + top[m:] + _dot(l_hi, self.expand(w_lo))

    def lmul(self, wide, rhs):
        w_hi, w_lo = _split_bf16(wide)
        r_hi, r_lo = _split_bf16(rhs)
        n = self.chunk * self.group
        e_hi = self.expand(w_hi)
        top = _dot(jnp.concatenate([e_hi, self.expand(w_lo)], 0), r_hi)
        return top[:n] + top[n:] + _dot(e_hi, r_lo)

    def unit_lower_inverse(self, a_talls):
        c = self.chunk
        negs = [-self.wide(a) for a in a_talls]
        xs = [self.eye + neg for neg in negs]
        powers = [self.rmul(neg, neg) for neg in negs]
        iters = int(math.log2(c)) - 1
        for it in range(iters):
            last = it == iters - 1
            rs = [self.rmul(x if last else jnp.concatenate([x, p], 0), p) for x, p in zip(xs, powers)]
            xs = [x + r[:c] for x, r in zip(xs, rs)]
            if not last:
                powers = [r[c:] for r in rs]
        return xs


def _softplus(x):
    return jnp.maximum(x, 0.0) + jnp.log1p(jnp.exp(-jnp.abs(x)))


def _dn_prep_kernel(chunk, group, hps, precise, qd_ref, kd_ref, vd_ref, ba_ref, hq_ref, hk_ref, hv_ref,
                    wq_ref, wk_ref, wv_ref, alog_ref, dtb_ref,
                    wv_out, wk_out, qdec_out, kend_out, p_out, gend_out):
    head0 = pl.program_id(1) * hps
    tb = qd_ref.shape[0]
    n = chunk * group

    def conv_silu(x_ref, halo_ref, w_ref, cols):
        xp = jnp.concatenate([halo_ref[0, :, cols], x_ref[:, cols]], 0)
        w = w_ref[:, cols]
        y = xp[5:5 + tb] * w[0:1]
        for tap in range(1, CONV_WIDTH):
            y = y + xp[5 + tap:5 + tap + tb] * w[tap:tap + 1]
        return _silu(y)

    ba = ba_ref[...]
    lane = lax.broadcasted_iota(jnp.int32, ba.shape, 1)
    beta_all = _sigmoid(ba)
    g_all = -jnp.exp(alog_ref[...]) * _softplus(ba + dtb_ref[...])

    li = lax.broadcasted_iota(jnp.int32, (n, n), 0)
    mi = lax.broadcasted_iota(jnp.int32, (n, n), 1)
    same = (li // chunk) == (mi // chunk)
    upto = jnp.logical_and(same, li <= mi)
    since = jnp.logical_and(same, li >= mi)
    chunk_end = mi == (li // chunk) * chunk + (chunk - 1)
    blocks = _BlockDiag(chunk, group)
    op_dtype = wk_out.dtype
    slot = p_out.shape[1] // hps

    items, a_mats = [], []
    for hh in range(hps):
        cols = slice(hh * LANES, (hh + 1) * LANES)
        q = conv_silu(qd_ref, hq_ref, wq_ref, cols)
        k = conv_silu(kd_ref, hk_ref, wk_ref, cols)
        v = conv_silu(vd_ref, hv_ref, wv_ref, cols)
        q = q * lax.rsqrt(jnp.sum(q * q, -1, keepdims=True) + EPS) * (DN_KEY_DIM ** -0.5)
        k = k * lax.rsqrt(jnp.sum(k * k, -1, keepdims=True) + EPS)
        beta = jnp.sum(jnp.where(lane == head0 + hh, beta_all, 0.0), -1, keepdims=True)
        g = jnp.sum(jnp.where(lane == head0 + hh + DN_HEADS, g_all, 0.0), -1, keepdims=True)
        for gi in range(tb // n):
            rows = slice(gi * n, (gi + 1) * n)
            qc, kc, bc, gc = q[rows], k[rows], beta[rows], g[rows]
            g_row = jnp.sum(jnp.where(upto, gc, 0.0), 0, keepdims=True)
            g_col = jnp.sum(jnp.where(li == mi, g_row, 0.0), 1, keepdims=True)
            g_end = jnp.sum(jnp.where(chunk_end, g_row, 0.0), 1, keepdims=True)
            decay = jnp.exp(jnp.where(since, g_col - g_row, -jnp.inf))
            qk_kk = _dot_any(jnp.concatenate([qc, kc], 0), kc, precise, _dot_nt)
            a_mats.append(jnp.where(li > mi, bc * decay * qk_kk[n:], 0.0))
            e_g = jnp.exp(g_col)
            qdec_out[rows, cols] = (e_g * qc).astype(op_dtype)
            kend_out[rows, cols] = (jnp.exp(g_end - g_col) * kc).astype(op_dtype)
            p_out[rows, hh * slot:hh * slot + n] = (qk_kk[:n] * decay).astype(op_dtype)
            if slot > n:
                p_out[rows, hh * slot + n:(hh + 1) * slot] = jnp.zeros((n, slot - n), op_dtype)
            items.append((gi, rows, cols, kc, v[rows], bc, e_g, g_end))
    t_invs = blocks.unit_lower_inverse(a_mats)
    for t_inv, (gi, rows, cols, kc, vc, bc, e_g, g_end) in zip(t_invs, items):
        w = blocks.lmul(t_inv, jnp.concatenate([bc * vc, (bc * e_g) * kc], 1))
        wv_out[rows, cols] = w[:, :DN_VAL_DIM]
        wk_out[rows, cols] = w[:, DN_VAL_DIM:].astype(op_dtype)
        for c in range(group):
            last = c * chunk + chunk - 1
            gend_out[gi * group + c, :, cols] = jnp.broadcast_to(jnp.exp(g_end[last:last + 1]), (1, LANES))


def _dn_score_slot(chunk, group):
    return LANES if chunk * group < LANES else max(chunk * group, V7X_MXU_DEPTH)


def _dn_prep(p, halo, w_conv8, alog_row, dtb_row, chunk, group, tb, hps, precise):
    m = p.shape[0]
    op_dtype = F32 if precise else BF16
    slot = _dn_score_slot(chunk, group)
    nh = DN_HEADS // hps
    hw = hps * LANES
    assert P_CONV % hw == 0 and DN_WIDTH % hw == 0
    cq, ck, cv = P_CONV // hw, P_CONV // hw + nh, P_CONV // hw + 2 * nh
    col = lambda base: (lambda i, h: (i, base + h))
    halo_spec = lambda base: pl.BlockSpec((1, 8, hw), lambda i, h: (i, 0, base + h))
    w_spec = lambda base: pl.BlockSpec((8, hw), lambda i, h: (0, base + h))
    const = pl.BlockSpec((1, LANES), lambda i, h: (0, 0))
    head_blk = pl.BlockSpec((tb, hw), lambda i, h: (i, h))
    out_shape = [
        jax.ShapeDtypeStruct((m, DN_WIDTH), F32),
        jax.ShapeDtypeStruct((m, DN_WIDTH), op_dtype),
        jax.ShapeDtypeStruct((m, DN_WIDTH), op_dtype),
        jax.ShapeDtypeStruct((m, DN_WIDTH), op_dtype),
        jax.ShapeDtypeStruct((m, DN_HEADS * slot), op_dtype),
        jax.ShapeDtypeStruct((m // chunk, 1, DN_WIDTH), F32),
    ]
    return pl.pallas_call(
        functools.partial(_dn_prep_kernel, chunk, group, hps, precise),
        grid=(m // tb, nh),
        in_specs=[
            pl.BlockSpec((tb, hw), col(cq)),
            pl.BlockSpec((tb, hw), col(ck)),
            pl.BlockSpec((tb, hw), col(cv)),
            pl.BlockSpec((tb, LANES), lambda i, h: (i, P_BA // LANES)),
            halo_spec(0), halo_spec(nh), halo_spec(2 * nh),
            w_spec(0), w_spec(nh), w_spec(2 * nh),
            const, const,
        ],
        out_specs=[head_blk] * 4 + [pl.BlockSpec((tb, hps * slot), lambda i, h: (i, h)),
                                    pl.BlockSpec((tb // chunk, 1, hw), lambda i, h: (i, 0, h))],
        out_shape=out_shape,
        compiler_params=_cparams("parallel", "parallel"),
        name="dn_prep",
    )(p, p, p, p, halo, halo, halo, w_conv8, w_conv8, w_conv8, alog_row, dtb_row)


def _dn_scan_kernel(chunk, group, n_chunks, wv_ref, wk_ref, qd_ref, ke_ref, p_ref, ge_ref, gate_ref, s0_ref,
                    onorm_ref, od_ref, sout_ref, s_scr, u_scr):
    n = pl.program_id(1)
    precise = wk_ref.dtype == F32
    slot = _dn_score_slot(chunk, group)

    @pl.when(n == 0)
    def _():
        s_scr[...] = s0_ref[0]
        u_scr[...] = jnp.zeros_like(u_scr)

    onorm = onorm_ref[...]
    for c in range(n_chunks):
        rows = slice(c * chunk, (c + 1) * chunk)
        group_rows = slice((c % group) * chunk, (c % group + 1) * chunk)
        for h in range(DN_HEADS):
            cols = slice(h * LANES, (h + 1) * LANES)
            s = s_scr[h]
            if not precise:
                s = s.astype(BF16)
            u = wv_ref[rows, cols] - _dot_any(wk_ref[rows, cols], s, precise)
            if not precise:
                u = u.astype(BF16)
            u_scr[h, group_rows, :] = u
            o = (_dot_any(qd_ref[rows, cols], s, precise)
                 + _dot_any(p_ref[rows, h * slot:(h + 1) * slot], u_scr[h], precise))
            s_scr[h] = ge_ref[c, :, cols] * s_scr[h] + _dot_any(ke_ref[rows, cols], u, precise, _dot_tn, 1)
            gate = gate_ref[rows, cols]
            od_ref[rows, cols] = (_rms(o, onorm) * _silu(gate)).astype(od_ref.dtype)

    @pl.when(n == pl.num_programs(1) - 1)
    def _():
        sout_ref[0] = s_scr[...]


def _dn_scan(prep, p, s0, onorm_row, chunk, group, n_chunks, batch):
    wv, wk, qdec, kend, pm, gend = prep
    m = wv.shape[0]
    assert n_chunks % group == 0
    rows = chunk * n_chunks
    steps = m // batch // rows
    blk = lambda b, n: (b * steps + n, 0)
    wide = pl.BlockSpec((rows, DN_WIDTH), blk)
    state = pl.BlockSpec((1, DN_HEADS, DN_KEY_DIM, DN_VAL_DIM), lambda b, n: (b, 0, 0, 0))
    slot = _dn_score_slot(chunk, group)
    assert pm.shape[1] == DN_HEADS * slot
    return pl.pallas_call(
        functools.partial(_dn_scan_kernel, chunk, group, n_chunks),
        grid=(batch, steps),
        in_specs=[
            wide, wide, wide, wide, pl.BlockSpec((rows, DN_HEADS * slot), blk),
            pl.BlockSpec((n_chunks, 1, DN_WIDTH), lambda b, n: (b * steps + n, 0, 0)),
            pl.BlockSpec((rows, DN_WIDTH), lambda b, n: (b * steps + n, P_GATE // DN_WIDTH)),
            state,
            pl.BlockSpec((1, LANES), lambda b, n: (0, 0)),
        ],
        out_specs=[wide, state],
        out_shape=[jax.ShapeDtypeStruct((m, DN_WIDTH), wk.dtype),
                   jax.ShapeDtypeStruct((batch, DN_HEADS, DN_KEY_DIM, DN_VAL_DIM), F32)],
        scratch_shapes=[pltpu.VMEM((DN_HEADS, DN_KEY_DIM, DN_VAL_DIM), F32),
                        pltpu.VMEM((DN_HEADS, slot, DN_VAL_DIM), wk.dtype)],
        compiler_params=_cparams("parallel", "arbitrary"),
        name="dn_scan",
    )(wv, wk, qdec, kend, pm, gend, p, s0, onorm_row)


def _out_proj_kernel(nw, attn_ref, od_ref, *refs):
    w_refs, (x_ref, g_ref, gate_ref, o_ref) = refs[:nw], refs[nw:]
    y = (_mm(attn_ref[...], tuple(r[0:ATTN_WIDTH, :] for r in w_refs))
         + _mm(od_ref[...], tuple(r[ATTN_WIDTH:, :] for r in w_refs)))
    o_ref[...] = x_ref[...] + gate_ref[...] * _rms(y, g_ref[...])


def _out_proj(attn, od, w, x, gain, gate, tm):
    m, d = x.shape
    row = lambda i: (i, 0)
    return pl.pallas_call(
        functools.partial(_out_proj_kernel, len(w)),
        grid=(m // tm,),
        in_specs=[
            pl.BlockSpec((tm, ATTN_WIDTH), row),
            pl.BlockSpec((tm, DN_WIDTH), row),
        ] + [pl.BlockSpec((ATTN_WIDTH + DN_WIDTH, d), lambda i: (0, 0))] * len(w) + [
            pl.BlockSpec((tm, d), row),
            pl.BlockSpec((1, d), lambda i: (0, 0)),
            _mod_spec(gate.shape[0], tm, d),
        ],
        out_specs=pl.BlockSpec((tm, d), row),
        out_shape=jax.ShapeDtypeStruct((m, d), F32),
        compiler_params=_cparams("parallel"),
        name="out_proj",
    )(attn, od, *w, x, gain, gate)


def _ffn_kernel(nw, x_ref, g_ref, sc_ref, sh_ref, *refs):
    wg_refs, wu_refs, wd_refs = refs[:nw], refs[nw:2 * nw], refs[2 * nw:3 * nw]
    g2_ref, gate_ref, o_ref, h_ref, acc_ref = refs[3 * nw:]
    j = pl.program_id(1)

    @pl.when(j == 0)
    def _():
        h = _rms(x_ref[...], g_ref[...]) * (1.0 + sc_ref[...]) + sh_ref[...]
        h_ref[...] = h.astype(h_ref.dtype)
        acc_ref[...] = jnp.zeros_like(acc_ref)

    h = h_ref[...]
    act = _silu(_mm(h, tuple(r[...] for r in wg_refs))) * _mm(h, tuple(r[...] for r in wu_refs))
    acc_ref[...] += _mm(act, tuple(r[...] for r in wd_refs))

    @pl.when(j == pl.num_programs(1) - 1)
    def _():
        o_ref[...] = x_ref[...] + gate_ref[...] * _rms(acc_ref[...], g2_ref[...])


def _ffn(x, gain, scale, shift, wg, wu, wd, gain2, gate, tm, tf):
    m, d = x.shape
    nw = len(wg)
    f = wg[0].shape[1]
    row = lambda i, j: (i, 0)
    vec = pl.BlockSpec((1, d), lambda i, j: (0, 0))
    return pl.pallas_call(
        functools.partial(_ffn_kernel, nw),
        grid=(m // tm, f // tf),
        in_specs=[
            pl.BlockSpec((tm, d), row), vec,
            _mod_spec(scale.shape[0], tm, d), _mod_spec(shift.shape[0], tm, d),
        ] + [pl.BlockSpec((d, tf), lambda i, j: (0, j))] * (2 * nw)
        + [pl.BlockSpec((tf, d), lambda i, j: (j, 0))] * nw
        + [vec, _mod_spec(gate.shape[0], tm, d)],
        out_specs=pl.BlockSpec((tm, d), row),
        out_shape=jax.ShapeDtypeStruct((m, d), F32),
        scratch_shapes=[pltpu.VMEM((tm, d), BF16 if wg[0].dtype == BF16 and nw == 1 else F32),
                        pltpu.VMEM((tm, d), F32)],
        compiler_params=_cparams("parallel", "arbitrary"),
        name="ffn_dense",
    )(x, gain, scale, shift, *wg, *wu, *wd, gain2, gate)


def _router_kernel(x_ref, g_ref, sc_ref, sh_ref, wr_ref, h_ref, gates_ref, idx_ref, w12_ref):
    h = _rms(x_ref[...], g_ref[...]) * (1.0 + sc_ref[...]) + sh_ref[...]
    h_ref[...] = h
    logits = _dot_x3(h, wr_ref[...])
    lane = lax.broadcasted_iota(jnp.int32, logits.shape, 1).astype(F32)
    logits = jnp.where(lane < N_EXPERTS, logits, -jnp.inf)
    m1 = jnp.max(logits, -1, keepdims=True)
    i1 = jnp.min(jnp.where(logits == m1, lane, float(LANES)), -1, keepdims=True)
    rest = jnp.where(lane == i1, -jnp.inf, logits)
    m2 = jnp.max(rest, -1, keepdims=True)
    i2 = jnp.min(jnp.where(rest == m2, lane, float(LANES)), -1, keepdims=True)
    t = jnp.exp(m2 - m1)
    w1 = 1.0 / (1.0 + t)
    w2 = t / (1.0 + t)
    gates_ref[...] = jnp.where(lane == i1, w1, 0.0) + jnp.where(lane == i2, w2, 0.0)
    idx_ref[...] = jnp.where(lane == 0.0, i1, jnp.where(lane == 1.0, i2, 0.0)).astype(jnp.int32)
    w12_ref[...] = jnp.where(lane == 0.0, w1, jnp.where(lane == 1.0, w2, 0.0))


def _router(x, gain, scale, shift, w_router_pad, tm):
    m, d = x.shape
    row = lambda i: (i, 0)
    vec = pl.BlockSpec((1, d), lambda i: (0, 0))
    small = pl.BlockSpec((tm, LANES), row)
    return pl.pallas_call(
        _router_kernel,
        grid=(m // tm,),
        in_specs=[pl.BlockSpec((tm, d), row), vec,
                  _mod_spec(scale.shape[0], tm, d), _mod_spec(shift.shape[0], tm, d),
                  pl.BlockSpec((d, LANES), lambda i: (0, 0))],
        out_specs=[pl.BlockSpec((tm, d), row), small, small, small],
        out_shape=[jax.ShapeDtypeStruct((m, d), F32), jax.ShapeDtypeStruct((m, LANES), F32),
                   jax.ShapeDtypeStruct((m, LANES), jnp.int32), jax.ShapeDtypeStruct((m, LANES), F32)],
        compiler_params=_cparams("parallel"),
        name="moe_router",
    )(x, gain, scale, shift, w_router_pad)


def _moe_gemm_kernel(nj, te_ref, tot_ref, rt_ref, h_hbm, wg_ref, wu_ref, wd_ref, ys_ref, xs_ref, xb_ref, acc_ref,
                     sems):
    r = pl.program_id(0)
    j = pl.program_id(1)
    tm = xb_ref.shape[0]
    total = tot_ref[0]
    active = r < total
    slot = r % 2
    share = -(-tm // nj)

    def row_copy(tile, t, s):
        return pltpu.make_async_copy(h_hbm.at[pl.ds(rt_ref[tile * tm + t], 1), :],
                                     xs_ref.at[s, pl.ds(t, 1), :], sems.at[s])

    def start_rows(tile, s, lo, hi):
        def body(t, carry):
            row_copy(tile, t, s).start()
            return carry

        lax.fori_loop(lo, hi, body, 0)

    @pl.when(jnp.logical_and(r == 0, j == 0))
    def _():
        start_rows(0, 0, 0, tm)

    @pl.when(jnp.logical_and(active, j == 0))
    def _():
        pltpu.make_async_copy(h_hbm.at[pl.ds(0, tm), :], xs_ref.at[slot], sems.at[slot]).wait()
        xb_ref[...] = xs_ref[slot].astype(BF16)
        acc_ref[...] = jnp.zeros_like(acc_ref)

    @pl.when(active)
    def _():
        more = r + 1 < total
        for u in range(share):
            t = j * share + u

            @pl.when(jnp.logical_and(more, t < tm))
            def _():
                row_copy(jnp.minimum(r + 1, pl.num_programs(0) - 1), jnp.minimum(t, tm - 1), 1 - slot).start()

        xb = xb_ref[...]
        act = (_silu(_dot(xb, wg_ref[0])) * _dot(xb, wu_ref[0])).astype(BF16)
        acc_ref[...] += _dot(act, wd_ref[0])

    @pl.when(j == pl.num_programs(1) - 1)
    def _():
        ys_ref[...] = jnp.where(active, acc_ref[...], 0.0)


def _moe_gemm(tile_expert, total_tiles, row_token, h, wg, wu, wd, tm, tf):
    n_tiles = tile_expert.shape[0]
    d = h.shape[1]
    f = wg.shape[2]
    nj = f // tf

    def w_col(r, j, te, tot, rt):
        return (te[r], 0, jnp.where(r < tot[0], j, nj - 1))

    def w_row(r, j, te, tot, rt):
        return (te[r], jnp.where(r < tot[0], j, nj - 1), 0)

    grid_spec = pltpu.PrefetchScalarGridSpec(
        num_scalar_prefetch=3,
        grid=(n_tiles, nj),
        in_specs=[
            pl.BlockSpec(memory_space=pl.ANY),
            pl.BlockSpec((1, d, tf), w_col),
            pl.BlockSpec((1, d, tf), w_col),
            pl.BlockSpec((1, tf, d), w_row),
        ],
        out_specs=pl.BlockSpec((tm, d), lambda r, j, te, tot, rt: (r, 0)),
        scratch_shapes=[pltpu.VMEM((2, tm, d), F32), pltpu.VMEM((tm, d), BF16), pltpu.VMEM((tm, d), F32),
                        pltpu.SemaphoreType.DMA((2,))],
    )
    return pl.pallas_call(
        functools.partial(_moe_gemm_kernel, nj),
        grid_spec=grid_spec,
        out_shape=jax.ShapeDtypeStruct((n_tiles * tm, d), F32),
        compiler_params=_cparams("arbitrary", "arbitrary", row_dma=True),
        name="moe_gemm",
    )(tile_expert, total_tiles, row_token, h, wg, wu, wd)


def _moe_combine_kernel(dest_ref, x_ref, w12_ref, g_ref, gate_ref, ys_hbm, o_ref, buf_ref, sems):
    i = pl.program_id(0)
    tb = x_ref.shape[0]
    slot = i % 2

    def row_copy(blk, t, k, s):
        src = dest_ref[2 * (blk * tb + t) + k]
        return pltpu.make_async_copy(ys_hbm.at[pl.ds(src, 1), :], buf_ref.at[s, k, pl.ds(t, 1), :], sems.at[s])

    def start_block(blk, s):
        def body(t, carry):
            row_copy(blk, t, 0, s).start()
            row_copy(blk, t, 1, s).start()
            return carry

        lax.fori_loop(0, tb, body, 0, unroll=8)

    @pl.when(i == 0)
    def _():
        start_block(0, 0)

    @pl.when(i + 1 < pl.num_programs(0))
    def _():
        start_block(i + 1, 1 - slot)

    for k in range(2):
        pltpu.make_async_copy(ys_hbm.at[pl.ds(0, tb), :], buf_ref.at[slot, k], sems.at[slot]).wait()
    w12 = w12_ref[...]
    y = w12[:, 0:1] * buf_ref[slot, 0] + w12[:, 1:2] * buf_ref[slot, 1]
    o_ref[...] = x_ref[...] + gate_ref[...] * _rms(y, g_ref[...])


def _moe_combine(dest, x, w12, gain, gate, ys, tb):
    m, d = x.shape
    row = lambda i, dst: (i, 0)
    grid_spec = pltpu.PrefetchScalarGridSpec(
        num_scalar_prefetch=1,
        grid=(m // tb,),
        in_specs=[
            pl.BlockSpec((tb, d), row),
            pl.BlockSpec((tb, LANES), row),
            pl.BlockSpec((1, d), lambda i, dst: (0, 0)),
            pl.BlockSpec((1, d), lambda i, dst: (0, 0)),
            pl.BlockSpec(memory_space=pl.ANY),
        ],
        out_specs=pl.BlockSpec((tb, d), row),
        scratch_shapes=[pltpu.VMEM((2, 2, tb, d), F32), pltpu.SemaphoreType.DMA((2,))],
    )
    return pl.pallas_call(
        _moe_combine_kernel,
        grid_spec=grid_spec,
        out_shape=jax.ShapeDtypeStruct((m, d), F32),
        compiler_params=_cparams("arbitrary", row_dma=True),
        name="moe_combine",
    )(dest, x, w12, gain, gate, ys)


def _route_tables(idx2, tm, n_tiles):
    m = idx2.shape[0]
    n_assign = 2 * m
    assert n_tiles * tm == n_assign + N_EXPERTS * tm
    experts = jnp.arange(N_EXPERTS, dtype=jnp.int32)
    e_flat = idx2.reshape(n_assign)
    onehot = (e_flat[:, None] == experts[None, :]).astype(jnp.int32)
    csum = jnp.cumsum(onehot, 0)
    counts = csum[-1]
    padded = ((counts + tm - 1) // tm) * tm
    pend = jnp.cumsum(padded)
    pstart = pend - padded
    dest = jnp.sum(onehot * (pstart[None, :] + csum - 1), -1).astype(jnp.int32)
    total_tiles = (pend[-1] // tm).astype(jnp.int32).reshape(1)
    tile_expert = jnp.minimum(
        jnp.searchsorted(pend // tm, jnp.arange(n_tiles, dtype=jnp.int32), side="right"), N_EXPERTS - 1
    ).astype(jnp.int32)
    filler_key = jnp.where(jnp.arange(tm, dtype=jnp.int32)[None, :] < (padded - counts)[:, None],
                           experts[:, None], N_EXPERTS).reshape(-1)
    keys = jnp.concatenate([e_flat, filler_key])
    tokens = jnp.concatenate([jnp.arange(n_assign, dtype=jnp.int32) // 2,
                              jnp.zeros((N_EXPERTS * tm,), jnp.int32)])
    _, row_token = lax.sort((keys, tokens), num_keys=1, is_stable=True)
    return tile_expert, total_tiles, row_token, dest


def _moe_dense_kernel(h_ref, gates_ref, wg_ref, wu_ref, wd_ref, x_ref, g_ref, gate_ref, o_ref, acc_ref, tot_ref):
    e = pl.program_id(0)
    j = pl.program_id(1)
    nj = pl.num_programs(1)

    @pl.when(jnp.logical_and(e == 0, j == 0))
    def _():
        tot_ref[...] = jnp.zeros_like(tot_ref)

    @pl.when(j == 0)
    def _():
        acc_ref[...] = jnp.zeros_like(acc_ref)

    h = h_ref[...].astype(BF16)
    act = (_silu(_dot(h, wg_ref[0])) * _dot(h, wu_ref[0])).astype(BF16)
    acc_ref[...] += _dot(act, wd_ref[0])

    @pl.when(j == nj - 1)
    def _():
        gates = gates_ref[...]
        lane = lax.broadcasted_iota(jnp.int32, gates.shape, 1)
        ge = jnp.sum(jnp.where(lane == e, gates, 0.0), -1, keepdims=True)
        tot_ref[...] += ge * acc_ref[...]

    @pl.when(jnp.logical_and(e == pl.num_programs(0) - 1, j == nj - 1))
    def _():
        o_ref[...] = x_ref[...] + gate_ref[...] * _rms(tot_ref[...], g_ref[...])


def _moe_dense(h, gates, wg, wu, wd, x, gain, gate, tf):
    m, d = x.shape
    f = wg.shape[2]
    full = pl.BlockSpec((m, d), lambda e, j: (0, 0))
    return pl.pallas_call(
        _moe_dense_kernel,
        grid=(N_EXPERTS, f // tf),
        in_specs=[
            full,
            pl.BlockSpec((m, LANES), lambda e, j: (0, 0)),
            pl.BlockSpec((1, d, tf), lambda e, j: (e, 0, j)),
            pl.BlockSpec((1, d, tf), lambda e, j: (e, 0, j)),
            pl.BlockSpec((1, tf, d), lambda e, j: (e, j, 0)),
            full,
            pl.BlockSpec((1, d), lambda e, j: (0, 0)),
            full,
        ],
        out_specs=full,
        out_shape=jax.ShapeDtypeStruct((m, d), F32),
        scratch_shapes=[pltpu.VMEM((m, d), F32), pltpu.VMEM((m, d), F32)],
        compiler_params=_cparams("arbitrary", "arbitrary"),
        name="moe_dense",
    )(h, gates, wg, wu, wd, x, gain, gate)


def _rope_tables(pos):
    half = ROPE_DIM // 2
    inv_freq = jnp.power(ROPE_THETA, -2.0 * jnp.arange(half, dtype=F32) / ROPE_DIM)
    ang = pos.astype(F32)[:, None] * inv_freq[None, :]
    cos, sin = jnp.cos(ang), jnp.sin(ang)
    t = pos.shape[0]
    rest = ATTN_HEAD_DIM - ROPE_DIM
    cos_h = jnp.concatenate([cos, cos, jnp.ones((t, rest), F32)], 1)
    sa_h = jnp.concatenate([-sin, jnp.zeros((t, half + rest), F32)], 1)
    sb_h = jnp.concatenate([jnp.zeros((t, half), F32), sin, jnp.zeros((t, rest), F32)], 1)
    rep = LANES // ATTN_HEAD_DIM
    return tuple(jnp.tile(a, (1, rep)) for a in (cos_h, sa_h, sb_h))


def _permute_w_in(w):
    o1 = ATTN_WIDTH
    o2 = o1 + KV_WIDTH
    o3 = o2 + KV_WIDTH
    o4 = o3 + DN_CONV_CH
    o5 = o4 + DN_WIDTH
    parts = [w[:, :o1], w[:, o4:o5], w[:, o3:o4], w[:, o1:o2], w[:, o2:o3], w[:, o5:]]
    used = sum(a.shape[1] for a in parts)
    parts.append(jnp.zeros((w.shape[0], P_WIDTH - used), w.dtype))
    return jnp.concatenate(parts, 1)


def _sample_mask_bias(s, lc):
    q_pos = PAST_LEN + np.arange(s)
    k_pos = np.concatenate([PAST_LEN - lc + np.arange(lc), q_pos])
    q_chunk = q_pos[:, None] // CHUNK
    k_chunk = k_pos[None, :] // CHUNK
    mask = (k_pos[None, :] >= 0) & (k_chunk <= q_chunk) & (k_pos[None, :] >= q_chunk * CHUNK - WINDOW)
    bias = np.where(mask, 0.0, -np.inf).astype(np.float32)
    return jnp.asarray(np.tile(bias, (4, 1)))


def _conv_halo(p, init, tb, seq):
    m = p.shape[0]
    batch = m // seq
    nb = seq // tb
    tails = p.reshape(batch, nb, tb, P_WIDTH)[:, :nb - 1, tb - (CONV_WIDTH - 1):, P_CONV:P_CONV + DN_CONV_CH]
    prev = jnp.concatenate([init[:, None], tails], 1)
    prev = prev.reshape(batch * nb, CONV_WIDTH - 1, DN_CONV_CH)
    return jnp.pad(prev, ((0, 0), (8 - (CONV_WIDTH - 1), 0), (0, 0)))


def _trunk(x, mods, layer_w, rope, past, cfg):
    m = x.shape[0]
    batch, seq = cfg["batch"], cfg["seq"]
    precise = cfg["precise"]
    nw = 2 if precise else 1
    ks, vs, ss, bufs = [], [], [], []
    cos, sa, sb = rope
    for l in range(DEPTH):
        w = layer_w[l]
        sh_a, sc_a, g_a, sh_f, sc_f, g_f = mods[l]
        p = _norm_proj(x, w["gain"][0], sc_a, sh_a, w["w_in"][:nw], cfg["tm_proj"], cfg["tn_proj"])
        if past is None:
            attn, k_new = _attn_prompt(p, w["sinks"], cos, sa, sb, cfg["tb_attn"])
            s0 = jnp.zeros((batch, DN_HEADS, DN_KEY_DIM, DN_VAL_DIM), F32)
            conv_init = jnp.zeros((batch, CONV_WIDTH - 1, DN_CONV_CH), F32)
        else:
            ck = past[0][l].reshape(batch, -1, KV_WIDTH)
            cv = past[1][l].reshape(batch, -1, KV_WIDTH)
            attn, k_new = _attn_sample(p, ck, cv, w["sinks"], cos, sa, sb, cfg["bias"], batch, seq)
            s0 = past[2][l]
            conv_init = past[3][l]
        halo = _conv_halo(p, conv_init, cfg["tb_dn"], seq)
        prep = _dn_prep(p, halo, w["w_conv"], w["alog"], w["dtb"], cfg["chunk"], cfg["group"], cfg["tb_dn"],
                        cfg["dn_heads_per_step"], precise)
        od, s_new = _dn_scan(prep, p, s0, w["onorm"], cfg["chunk"], cfg["group"], cfg["scan_chunks"], batch)
        pick = 1 if precise else 0
        x = _out_proj(attn, od, w["w_out"][pick], x, w["gain"][1], g_a, cfg["tm_out"])
        if l % 2 == 0:
            x = _ffn(x, w["gain"][2], sc_f, sh_f, w["ffn_gate"][pick], w["ffn_up"][pick], w["ffn_down"][pick],
                     w["gain"][3], g_f, cfg["tm_ffn"], cfg["tf_ffn"])
        else:
            h, gates, idx, w12 = _router(x, w["gain"][2], sc_f, sh_f, w["router"], cfg["tm_router"])
            if cfg["routed"]:
                tm = cfg["tm_moe"]
                n_tiles = 2 * m // tm + N_EXPERTS
                tile_expert, total_tiles, row_token, dest = _route_tables(idx[:, :2], tm, n_tiles)
                ys = _moe_gemm(tile_expert, total_tiles, row_token, h, w["moe_gate"], w["moe_up"], w["moe_down"],
                               tm, cfg["tf_moe"])
                x = _moe_combine(dest, x, w12, w["gain"][3], g_f, ys, cfg["tb_combine"])
            else:
                x = _moe_dense(h, gates, w["moe_gate"], w["moe_up"], w["moe_down"], x, w["gain"][3], g_f,
                               cfg["tf_moe"])
        pb = p.reshape(batch, seq, P_WIDTH)
        keep = min(WINDOW, seq) if past is None else seq
        ks.append(k_new.reshape(batch, seq, ATTN_KV_HEADS, ATTN_HEAD_DIM)[:, seq - keep:])
        vs.append(pb[:, seq - keep:, P_V:P_V + KV_WIDTH].reshape(batch, keep, ATTN_KV_HEADS, ATTN_HEAD_DIM))
        ss.append(s_new)
        assert seq >= CONV_WIDTH - 1
        bufs.append(pb[:, seq - (CONV_WIDTH - 1):, P_CONV:P_CONV + DN_CONV_CH])
    return x, jnp.stack(ks), jnp.stack(vs), jnp.stack(ss), jnp.stack(bufs)


def kernel(x_prompt, x_sample, cache_attn_k, cache_attn_v, state_delta, state_conv, c_prompt, c_sample, w_in, w_conv, attn_sinks, dn_a_log, dn_dt_bias, dn_norm, w_out, w_mod, b_mod, norm_gains, ffn_gate, ffn_up, ffn_down, moe_router, moe_gate, moe_up, moe_down):
    bp, tp, d = x_prompt.shape
    bs, ts, _ = x_sample.shape
    assert bp == 1 and d == D_MODEL

    c_all = jnp.concatenate([c_prompt, c_sample, jnp.zeros((16 - bp - bs, d), F32)], 0)
    mod = _modulation(c_all, w_mod, b_mod)
    mods_p, mods_s = [], []
    for l in range(DEPTH):
        six = jnp.split(mod[l], 6, -1)
        mods_p.append([a[0:bp] for a in six])
        mods_s.append([jnp.repeat(a[bp:bp + bs], ts, axis=0) for a in six])

    def pad_lanes(v, at):
        return jnp.zeros((1, LANES), F32).at[0, at:at + v.shape[0]].set(v)

    def both(w):
        return ((w.astype(BF16),), (w,))

    layer_w = []
    for l in range(DEPTH):
        w = {
            "gain": [norm_gains[l, i].reshape(1, d) for i in range(4)],
            "w_in": _split_weight(_permute_w_in(w_in[l])),
            "sinks": attn_sinks[l],
            "w_conv": jnp.pad(w_conv[l], ((0, 8 - CONV_WIDTH), (0, 0))),
            "alog": pad_lanes(dn_a_log[l], DN_HEADS),
            "dtb": pad_lanes(dn_dt_bias[l], DN_HEADS),
            "onorm": dn_norm[l].reshape(1, DN_VAL_DIM),
            "w_out": both(w_out[l]),
        }
        if l % 2 == 0:
            w["ffn_gate"] = both(ffn_gate[l // 2])
            w["ffn_up"] = both(ffn_up[l // 2])
            w["ffn_down"] = both(ffn_down[l // 2])
        else:
            w["router"] = jnp.pad(moe_router[l // 2], ((0, 0), (0, LANES - N_EXPERTS)))
            w["moe_gate"] = moe_gate[l // 2].astype(BF16)
            w["moe_up"] = moe_up[l // 2].astype(BF16)
            w["moe_down"] = moe_down[l // 2].astype(BF16)
        layer_w.append(w)

    cfg_p = dict(batch=bp, seq=tp, precise=False, chunk=CHUNK, group=2, scan_chunks=4, tm_proj=1024, tn_proj=512, tb_attn=512,
                 tb_dn=1024, dn_heads_per_step=1, tm_out=512, tm_ffn=512, tf_ffn=512, tm_router=512, routed=True, tm_moe=1024,
                 tf_moe=256, tb_combine=256)
    rope_p = _rope_tables(jnp.arange(tp, dtype=jnp.int32))
    y_p, k_p, v_p, s_p, conv_p = _trunk(x_prompt.reshape(bp * tp, d), mods_p, layer_w, rope_p, None, cfg_p)

    ms = bs * ts
    cfg_s = dict(batch=bs, seq=ts, precise=True, chunk=ts, group=1, scan_chunks=1, tm_proj=ms, tn_proj=512, tb_dn=ts, dn_heads_per_step=DN_HEADS, tm_out=ms,
                 tm_ffn=ms, tf_ffn=512, tm_router=ms, routed=False, tf_moe=1408,
                 bias=_sample_mask_bias(ts, cache_attn_k.shape[2]))
    rope_s = _rope_tables(PAST_LEN + jnp.arange(ts, dtype=jnp.int32))
    past = (cache_attn_k, cache_attn_v, state_delta, state_conv)
    y_s, k_s, v_s, s_s, conv_s = _trunk(x_sample.reshape(ms, d), mods_s, layer_w, rope_s, past, cfg_s)

    return (y_p.reshape(bp, tp, d), y_s.reshape(bs, ts, d), k_p, v_p, s_p, conv_p, k_s, v_s, s_s, conv_s)
```

```python
import functools
import math

import numpy as np
import jax
import jax.numpy as jnp
from jax import lax
from jax.experimental import pallas as pl
from jax.experimental.pallas import tpu as pltpu

D_MODEL = 2048
DEPTH = 2
PAST_LEN = 1024
CHUNK = 64
ATTN_HEADS = 16
ATTN_KV_HEADS = 2
ATTN_HEAD_DIM = 64
ATTN_WIDTH = 1024
KV_WIDTH = 128
WINDOW = 128
ROPE_THETA = 500000.0
ROPE_DIM = 16
DN_HEADS = 8
DN_KEY_DIM = 128
DN_VAL_DIM = 128
DN_WIDTH = 1024
CONV_WIDTH = 4
DN_CONV_CH = 3072
D_FF = 5632
N_EXPERTS = 8
D_FF_EXPERT = 2816
EPS = 1e-6

F32 = jnp.float32
BF16 = jnp.bfloat16
LANES = 128
V7X_MXU_DEPTH = 256
V7X_VMEM_LIMIT = 56 * 1024 * 1024

P_Q = 0
P_GATE = 1024
P_CONV = 2048
P_K = 5120
P_V = 5248
P_BA = 5376
P_WIDTH = 5632


def _cparams(*sem, row_dma=False):
    return pltpu.CompilerParams(dimension_semantics=sem, vmem_limit_bytes=V7X_VMEM_LIMIT,
                                disable_bounds_checks=row_dma)


def _sigmoid(x):
    return 0.5 * jnp.tanh(0.5 * x) + 0.5


def _silu(x):
    return x * _sigmoid(x)


def _rms(x, gain):
    return x * lax.rsqrt(jnp.mean(x * x, -1, keepdims=True) + EPS) * gain


def _dot(a, b):
    return jnp.dot(a, b, preferred_element_type=F32)


def _dot_nt(a, b):
    return lax.dot_general(a, b, (((1,), (1,)), ((), ())), preferred_element_type=F32)


def _dot_tn(a, b):
    return lax.dot_general(a, b, (((0,), (0,)), ((), ())), preferred_element_type=F32)


def _split_bf16(a):
    hi = a.astype(BF16)
    lo = (a - hi.astype(F32)).astype(BF16)
    return hi, lo


def _dot_x3(a, b, dot=_dot, out_axis=0):
    a_hi, a_lo = _split_bf16(a)
    b_hi, b_lo = _split_bf16(b)
    n = a.shape[out_axis]
    top = dot(jnp.concatenate([a_hi, a_lo], out_axis), b_hi)
    return top[:n] + top[n:] + dot(a_hi, b_lo)


def _dot_any(a, b, precise, dot=_dot, out_axis=0):
    if precise:
        return _dot_x3(a, b, dot, out_axis)
    return dot(a.astype(BF16), b.astype(BF16))


def _mm(a, w):
    if len(w) == 1 and w[0].dtype == F32:
        return _dot_x3(a, w[0])
    if len(w) == 1:
        return _dot(a.astype(BF16), w[0])
    a_hi, a_lo = _split_bf16(a)
    n = a.shape[0]
    top = _dot(jnp.concatenate([a_hi, a_lo], 0), w[0])
    return top[:n] + top[n:] + _dot(a_hi, w[1])


def _split_weight(w):
    hi, lo = _split_bf16(w)
    return (hi, lo)


def _mod_spec(rows, tm, d):
    if rows == 1:
        return pl.BlockSpec((1, d), lambda i, *_: (0, 0))
    return pl.BlockSpec((tm, d), lambda i, *_: (i, 0))


def _mod_kernel(c_ref, w_ref, b_ref, o_ref):
    o_ref[0] = _dot_x3(_silu(c_ref[...]), w_ref[0]) + b_ref[0]


def _modulation(c_all, w_mod, b_mod):
    rows = c_all.shape[0]
    n = w_mod.shape[2]
    tn = 1024
    return pl.pallas_call(
        _mod_kernel,
        grid=(DEPTH, n // tn),
        in_specs=[
            pl.BlockSpec((rows, D_MODEL), lambda l, j: (0, 0)),
            pl.BlockSpec((1, D_MODEL, tn), lambda l, j: (l, 0, j)),
            pl.BlockSpec((1, 1, tn), lambda l, j: (l, 0, j)),
        ],
        out_specs=pl.BlockSpec((1, rows, tn), lambda l, j: (l, 0, j)),
        out_shape=jax.ShapeDtypeStruct((DEPTH, rows, n), F32),
        compiler_params=_cparams("parallel", "parallel"),
        name="modulation",
    )(c_all, w_mod, b_mod.reshape(DEPTH, 1, n))


def _norm_proj_kernel(nw, x_ref, g_ref, sc_ref, sh_ref, *refs):
    w_refs, (o_ref, h_ref) = refs[:nw], refs[nw:]

    @pl.when(pl.program_id(1) == 0)
    def _():
        h = _rms(x_ref[...], g_ref[...]) * (1.0 + sc_ref[...]) + sh_ref[...]
        h_ref[...] = h.astype(h_ref.dtype)

    o_ref[...] = _mm(h_ref[...], tuple(r[...] for r in w_refs))


def _norm_proj(x, gain, scale, shift, w, tm, tn):
    m, d = x.shape
    n = w[0].shape[1]
    return pl.pallas_call(
        functools.partial(_norm_proj_kernel, len(w)),
        grid=(m // tm, n // tn),
        in_specs=[
            pl.BlockSpec((tm, d), lambda i, j: (i, 0)),
            pl.BlockSpec((1, d), lambda i, j: (0, 0)),
            _mod_spec(scale.shape[0], tm, d),
            _mod_spec(shift.shape[0], tm, d),
        ] + [pl.BlockSpec((d, tn), lambda i, j: (0, j))] * len(w),
        out_specs=pl.BlockSpec((tm, tn), lambda i, j: (i, j)),
        out_shape=jax.ShapeDtypeStruct((m, n), F32),
        scratch_shapes=[pltpu.VMEM((tm, d), BF16 if len(w) == 1 else F32)],
        compiler_params=_cparams("parallel", "arbitrary"),
        name="norm_proj",
    )(x, gain, scale, shift, *w)


def _rope(x, cos, sa, sb):
    return x * cos + pltpu.roll(x, LANES - 8, 1) * sa + pltpu.roll(x, 8, 1) * sb


def _kv_variants(k, v):
    lo = lax.broadcasted_iota(jnp.int32, k.shape, 1) < ATTN_HEAD_DIM
    kr = pltpu.roll(k, ATTN_HEAD_DIM, 1)
    vr = pltpu.roll(v, ATTN_HEAD_DIM, 1)
    zero = jnp.zeros_like(k)
    k_lo = (jnp.where(lo, k, zero), jnp.where(lo, kr, zero))
    k_hi = (jnp.where(lo, zero, kr), jnp.where(lo, zero, k))
    v_lo = (jnp.where(lo, v, zero), jnp.where(lo, vr, zero))
    v_hi = (jnp.where(lo, zero, vr), jnp.where(lo, zero, v))
    return k_lo, k_hi, v_lo, v_hi


def _sink_softmax(s, sink):
    m = jnp.maximum(jnp.max(s, -1, keepdims=True), sink)
    p = jnp.exp(s - m)
    den = jnp.sum(p, -1, keepdims=True) + jnp.exp(sink - m)
    return p * (1.0 / den)


def _attn_core(qbs, k_los, k_his, v_los, v_his, bias, sinks, precise=False):
    scores = [(_dot_any(qb, k_lo, precise, _dot_nt) + bias, _dot_any(qb, k_hi, precise, _dot_nt) + bias)
              for qb, k_lo, k_hi in zip(qbs, k_los, k_his)]
    probs = [(_sink_softmax(s_even, sink[0]), _sink_softmax(s_odd, sink[1]))
             for (s_even, s_odd), sink in zip(scores, sinks)]
    return [_dot_any(p_even, v_lo, precise) + _dot_any(p_odd, v_hi, precise)
            for (p_even, p_odd), v_lo, v_hi in zip(probs, v_los, v_his)]


def _sink_columns(sink_ref, rows_per_pair):
    n = 4 * rows_per_pair
    pair = lax.broadcasted_iota(jnp.int32, (n, 1), 0) // rows_per_pair
    out = []
    for j in range(ATTN_KV_HEADS):
        cols = []
        for par in range(2):
            col = jnp.zeros((n, 1), F32)
            for a in range(4):
                col = jnp.where(pair == a, sink_ref[8 * j + 2 * a + par], col)
            cols.append(col)
        out.append(cols)
    return out


def _attn_prompt_kernel(sink_ref, q_ref, kv_ref, cos_ref, sa_ref, sb_ref, o_ref, knew_ref,
                        qs_ref, klo_ref, khi_ref, vlo_ref, vhi_ref):
    i = pl.program_id(0)
    tb = q_ref.shape[0]
    bufs = (klo_ref, khi_ref, vlo_ref, vhi_ref)

    @pl.when(i == 0)
    def _():
        for r in bufs:
            r[:, 0:WINDOW, :] = jnp.zeros((ATTN_KV_HEADS, WINDOW, LANES), BF16)

    @pl.when(i > 0)
    def _():
        for r in bufs:
            r[:, 0:WINDOW, :] = r[:, tb:tb + WINDOW, :]

    cos, sa, sb = cos_ref[...], sa_ref[...], sb_ref[...]
    k = _rope(kv_ref[:, 0:LANES], cos, sa, sb)
    knew_ref[...] = k
    variants = _kv_variants(k, kv_ref[:, LANES:2 * LANES])
    for r, var in zip(bufs, variants):
        for j in range(ATTN_KV_HEADS):
            r[j, WINDOW:, :] = var[j].astype(BF16)
    scale = ATTN_HEAD_DIM ** -0.5
    for a in range(ATTN_WIDTH // LANES):
        cols = slice(a * LANES, (a + 1) * LANES)
        qs_ref[:, cols] = (_rope(q_ref[:, cols], cos, sa, sb) * scale).astype(BF16)

    sinks = _sink_columns(sink_ref, CHUNK)
    nk = WINDOW + CHUNK

    def chunk_body(c, carry):
        r0 = pl.multiple_of(c * CHUNK, CHUNK)
        kpos = i * tb - WINDOW + r0 + lax.broadcasted_iota(jnp.int32, (1, nk), 1)
        bias = jnp.where(kpos >= 0, 0.0, -jnp.inf).astype(F32)
        heads = range(ATTN_KV_HEADS)
        qbs = [jnp.concatenate(
            [qs_ref[pl.ds(r0, CHUNK), (4 * j + a) * LANES:(4 * j + a + 1) * LANES] for a in range(4)], 0)
            for j in heads]
        keys = pl.ds(r0, nk)
        outs = _attn_core(qbs, [klo_ref[j, keys, :] for j in heads], [khi_ref[j, keys, :] for j in heads],
                          [vlo_ref[j, keys, :] for j in heads], [vhi_ref[j, keys, :] for j in heads], bias, sinks)
        for j, o in zip(heads, outs):
            for a in range(4):
                o_ref[pl.ds(r0, CHUNK), (4 * j + a) * LANES:(4 * j + a + 1) * LANES] = (
                    o[a * CHUNK:(a + 1) * CHUNK].astype(BF16))
        return carry

    lax.fori_loop(0, tb // CHUNK, chunk_body, 0, unroll=4)


def _attn_prompt(p, sinks, cos, sa, sb, tb):
    t = p.shape[0]
    kv_blk = P_K // (2 * LANES)
    row = lambda i: (i, 0)
    return pl.pallas_call(
        _attn_prompt_kernel,
        grid=(t // tb,),
        in_specs=[
            pl.BlockSpec(memory_space=pltpu.SMEM),
            pl.BlockSpec((tb, ATTN_WIDTH), row),
            pl.BlockSpec((tb, 2 * LANES), lambda i: (i, kv_blk)),
            pl.BlockSpec((tb, LANES), row),
            pl.BlockSpec((tb, LANES), row),
            pl.BlockSpec((tb, LANES), row),
        ],
        out_specs=[pl.BlockSpec((tb, ATTN_WIDTH), row), pl.BlockSpec((tb, LANES), row)],
        out_shape=[jax.ShapeDtypeStruct((t, ATTN_WIDTH), BF16), jax.ShapeDtypeStruct((t, LANES), F32)],
        scratch_shapes=[pltpu.VMEM((tb, ATTN_WIDTH), BF16)]
        + [pltpu.VMEM((ATTN_KV_HEADS, tb + WINDOW, LANES), BF16) for _ in range(4)],
        compiler_params=_cparams("arbitrary"),
        name="attn_prompt",
    )(sinks, p, p, cos, sa, sb)


def _attn_sample_kernel(sink_ref, q_ref, kv_ref, ck_ref, cv_ref, cos_ref, sa_ref, sb_ref, bias_ref,
                        o_ref, knew_ref):
    s = q_ref.shape[0]
    cos, sa, sb = cos_ref[...], sa_ref[...], sb_ref[...]
    k = _rope(kv_ref[:, 0:LANES], cos, sa, sb)
    knew_ref[...] = k
    kk = jnp.concatenate([ck_ref[0], k], 0)
    vv = jnp.concatenate([cv_ref[0], kv_ref[:, LANES:2 * LANES]], 0)
    k_lo, k_hi, v_lo, v_hi = _kv_variants(kk, vv)
    sinks = _sink_columns(sink_ref, s)
    scale = ATTN_HEAD_DIM ** -0.5
    bias = bias_ref[...]
    qbs = [jnp.concatenate(
        [_rope(q_ref[:, (4 * j + a) * LANES:(4 * j + a + 1) * LANES], cos, sa, sb) * scale for a in range(4)], 0)
        for j in range(ATTN_KV_HEADS)]
    outs = _attn_core(qbs, k_lo, k_hi, v_lo, v_hi, bias, sinks, precise=True)
    for j, o in enumerate(outs):
        for a in range(4):
            o_ref[:, (4 * j + a) * LANES:(4 * j + a + 1) * LANES] = o[a * s:(a + 1) * s]


def _attn_sample(p, cache_k, cache_v, sinks, cos, sa, sb, bias, batch, s):
    lc = cache_k.shape[1]
    kv_blk = P_K // (2 * LANES)
    row = lambda b: (b, 0)
    const = lambda b: (0, 0)
    return pl.pallas_call(
        _attn_sample_kernel,
        grid=(batch,),
        in_specs=[
            pl.BlockSpec(memory_space=pltpu.SMEM),
            pl.BlockSpec((s, ATTN_WIDTH), row),
            pl.BlockSpec((s, 2 * LANES), lambda b: (b, kv_blk)),
            pl.BlockSpec((1, lc, LANES), lambda b: (b, 0, 0)),
            pl.BlockSpec((1, lc, LANES), lambda b: (b, 0, 0)),
            pl.BlockSpec((s, LANES), const),
            pl.BlockSpec((s, LANES), const),
            pl.BlockSpec((s, LANES), const),
            pl.BlockSpec((4 * s, lc + s), const),
        ],
        out_specs=[pl.BlockSpec((s, ATTN_WIDTH), row), pl.BlockSpec((s, LANES), row)],
        out_shape=[jax.ShapeDtypeStruct((batch * s, ATTN_WIDTH), F32),
                   jax.ShapeDtypeStruct((batch * s, LANES), F32)],
        compiler_params=_cparams("parallel"),
        name="attn_sample",
    )(sinks, p, p, cache_k, cache_v, cos, sa, sb, bias)


class _BlockDiag:
    def __init__(self, chunk, group):
        self.chunk, self.group = chunk, group
        n = chunk * group
        lane = lax.broadcasted_iota(jnp.int32, (chunk, n), 1)
        self.lane_block = lane // chunk
        self.eye = (lax.broadcasted_iota(jnp.int32, (chunk, n), 0) == lane % chunk).astype(F32)

    def wide(self, tall):
        c = self.chunk
        out = tall[0:c]
        for b in range(1, self.group):
            out = out + tall[b * c:(b + 1) * c]
        return out

    def expand(self, wide):
        if self.group == 1:
            return wide
        zero = jnp.zeros_like(wide)
        return jnp.concatenate([jnp.where(self.lane_block == b, wide, zero) for b in range(self.group)], 0)

    def rmul(self, lhs, wide):
        l_hi, l_lo = _split_bf16(lhs)
        w_hi, w_lo = _split_bf16(wide)
        m = lhs.shape[0]
        top = _dot(jnp.concatenate([l_hi, l_lo], 0), self.expand(w_hi))
        return top[:m] + top[m:] + _dot(l_hi, self.expand(w_lo))

    def lmul(self, wide, rhs):
        w_hi, w_lo = _split_bf16(wide)
        r_hi, r_lo = _split_bf16(rhs)
        n = self.chunk * self.group
        e_hi = self.expand(w_hi)
        top = _dot(jnp.concatenate([e_hi, self.expand(w_lo)], 0), r_hi)
        return top[:n] + top[n:] + _dot(e_hi, r_lo)

    def unit_lower_inverse(self, a_talls):
        c = self.chunk
        negs = [-self.wide(a) for a in a_talls]
        xs = [self.eye + neg for neg in negs]
        powers = [self.rmul(neg, neg) for neg in negs]
        iters = int(math.log2(c)) - 1
        for it in range(iters):
            last = it == iters - 1
            rs = [self.rmul(x if last else jnp.concatenate([x, p], 0), p) for x, p in zip(xs, powers)]
            xs = [x + r[:c] for x, r in zip(xs, rs)]
            if not last:
                powers = [r[c:] for r in rs]
        return xs


def _softplus(x):
    return jnp.maximum(x, 0.0) + jnp.log1p(jnp.exp(-jnp.abs(x)))


def _dn_prep_kernel(chunk, group, hps, precise, qd_ref, kd_ref, vd_ref, ba_ref, hq_ref, hk_ref, hv_ref,
                    wq_ref, wk_ref, wv_ref, alog_ref, dtb_ref,
                    wv_out, wk_out, qdec_out, kend_out, p_out, gend_out):
    head0 = pl.program_id(1) * hps
    tb = qd_ref.shape[0]
    n = chunk * group

    def conv_silu(x_ref, halo_ref, w_ref, cols):
        xp = jnp.concatenate([halo_ref[0, :, cols], x_ref[:, cols]], 0)
        w = w_ref[:, cols]
        y = xp[5:5 + tb] * w[0:1]
        for tap in range(1, CONV_WIDTH):
            y = y + xp[5 + tap:5 + tap + tb] * w[tap:tap + 1]
        return _silu(y)

    ba = ba_ref[...]
    lane = lax.broadcasted_iota(jnp.int32, ba.shape, 1)
    beta_all = _sigmoid(ba)
    g_all = -jnp.exp(alog_ref[...]) * _softplus(ba + dtb_ref[...])

    li = lax.broadcasted_iota(jnp.int32, (n, n), 0)
    mi = lax.broadcasted_iota(jnp.int32, (n, n), 1)
    same = (li // chunk) == (mi // chunk)
    upto = jnp.logical_and(same, li <= mi)
    since = jnp.logical_and(same, li >= mi)
    chunk_end = mi == (li // chunk) * chunk + (chunk - 1)
    blocks = _BlockDiag(chunk, group)
    op_dtype = wk_out.dtype
    slot = p_out.shape[1] // hps

    items, a_mats = [], []
    for hh in range(hps):
        cols = slice(hh * LANES, (hh + 1) * LANES)
        q = conv_silu(qd_ref, hq_ref, wq_ref, cols)
        k = conv_silu(kd_ref, hk_ref, wk_ref, cols)
        v = conv_silu(vd_ref, hv_ref, wv_ref, cols)
        q = q * lax.rsqrt(jnp.sum(q * q, -1, keepdims=True) + EPS) * (DN_KEY_DIM ** -0.5)
        k = k * lax.rsqrt(jnp.sum(k * k, -1, keepdims=True) + EPS)
        beta = jnp.sum(jnp.where(lane == head0 + hh, beta_all, 0.0), -1, keepdims=True)
        g = jnp.sum(jnp.where(lane == head0 + hh + DN_HEADS, g_all, 0.0), -1, keepdims=True)
        for gi in range(tb // n):
            rows = slice(gi * n, (gi + 1) * n)
            qc, kc, bc, gc = q[rows], k[rows], beta[rows], g[rows]
            g_row = jnp.sum(jnp.where(upto, gc, 0.0), 0, keepdims=True)
            g_col = jnp.sum(jnp.where(li == mi, g_row, 0.0), 1, keepdims=True)
            g_end = jnp.sum(jnp.where(chunk_end, g_row, 0.0), 1, keepdims=True)
            decay = jnp.exp(jnp.where(since, g_col - g_row, -jnp.inf))
            qk_kk = _dot_any(jnp.concatenate([qc, kc], 0), kc, precise, _dot_nt)
            a_mats.append(jnp.where(li > mi, bc * decay * qk_kk[n:], 0.0))
            e_g = jnp.exp(g_col)
            qdec_out[rows, cols] = (e_g * qc).astype(op_dtype)
            kend_out[rows, cols] = (jnp.exp(g_end - g_col) * kc).astype(op_dtype)
            p_out[rows, hh * slot:hh * slot + n] = (qk_kk[:n] * decay).astype(op_dtype)
            if slot > n:
                p_out[rows, hh * slot + n:(hh + 1) * slot] = jnp.zeros((n, slot - n), op_dtype)
            items.append((gi, rows, cols, kc, v[rows], bc, e_g, g_end))
    t_invs = blocks.unit_lower_inverse(a_mats)
    for t_inv, (gi, rows, cols, kc, vc, bc, e_g, g_end) in zip(t_invs, items):
        w = blocks.lmul(t_inv, jnp.concatenate([bc * vc, (bc * e_g) * kc], 1))
        wv_out[rows, cols] = w[:, :DN_VAL_DIM]
        wk_out[rows, cols] = w[:, DN_VAL_DIM:].astype(op_dtype)
        for c in range(group):
            last = c * chunk + chunk - 1
            gend_out[gi * group + c, :, cols] = jnp.broadcast_to(jnp.exp(g_end[last:last + 1]), (1, LANES))


def _dn_score_slot(chunk, group):
    return LANES if chunk * group < LANES else max(chunk * group, V7X_MXU_DEPTH)


def _dn_prep(p, halo, w_conv8, alog_row, dtb_row, chunk, group, tb, hps, precise):
    m = p.shape[0]
    op_dtype = F32 if precise else BF16
    slot = _dn_score_slot(chunk, group)
    nh = DN_HEADS // hps
    hw = hps * LANES
    assert P_CONV % hw == 0 and DN_WIDTH % hw == 0
    cq, ck, cv = P_CONV // hw, P_CONV // hw + nh, P_CONV // hw + 2 * nh
    col = lambda base: (lambda i, h: (i, base + h))
    halo_spec = lambda base: pl.BlockSpec((1, 8, hw), lambda i, h: (i, 0, base + h))
    w_spec = lambda base: pl.BlockSpec((8, hw), lambda i, h: (0, base + h))
    const = pl.BlockSpec((1, LANES), lambda i, h: (0, 0))
    head_blk = pl.BlockSpec((tb, hw), lambda i, h: (i, h))
    out_shape = [
        jax.ShapeDtypeStruct((m, DN_WIDTH), F32),
        jax.ShapeDtypeStruct((m, DN_WIDTH), op_dtype),
        jax.ShapeDtypeStruct((m, DN_WIDTH), op_dtype),
        jax.ShapeDtypeStruct((m, DN_WIDTH), op_dtype),
        jax.ShapeDtypeStruct((m, DN_HEADS * slot), op_dtype),
        jax.ShapeDtypeStruct((m // chunk, 1, DN_WIDTH), F32),
    ]
    return pl.pallas_call(
        functools.partial(_dn_prep_kernel, chunk, group, hps, precise),
        grid=(m // tb, nh),
        in_specs=[
            pl.BlockSpec((tb, hw), col(cq)),
            pl.BlockSpec((tb, hw), col(ck)),
            pl.BlockSpec((tb, hw), col(cv)),
            pl.BlockSpec((tb, LANES), lambda i, h: (i, P_BA // LANES)),
            halo_spec(0), halo_spec(nh), halo_spec(2 * nh),
            w_spec(0), w_spec(nh), w_spec(2 * nh),
            const, const,
        ],
        out_specs=[head_blk] * 4 + [pl.BlockSpec((tb, hps * slot), lambda i, h: (i, h)),
                                    pl.BlockSpec((tb // chunk, 1, hw), lambda i, h: (i, 0, h))],
        out_shape=out_shape,
        compiler_params=_cparams("parallel", "parallel"),
        name="dn_prep",
    )(p, p, p, p, halo, halo, halo, w_conv8, w_conv8, w_conv8, alog_row, dtb_row)


def _dn_scan_kernel(chunk, group, n_chunks, wv_ref, wk_ref, qd_ref, ke_ref, p_ref, ge_ref, gate_ref, s0_ref,
                    onorm_ref, od_ref, sout_ref, s_scr, u_scr):
    n = pl.program_id(1)
    precise = wk_ref.dtype == F32
    slot = _dn_score_slot(chunk, group)

    @pl.when(n == 0)
    def _():
        s_scr[...] = s0_ref[0]
        u_scr[...] = jnp.zeros_like(u_scr)

    onorm = onorm_ref[...]
    for c in range(n_chunks):
        rows = slice(c * chunk, (c + 1) * chunk)
        group_rows = slice((c % group) * chunk, (c % group + 1) * chunk)
        for h in range(DN_HEADS):
            cols = slice(h * LANES, (h + 1) * LANES)
            s = s_scr[h]
            if not precise:
                s = s.astype(BF16)
            u = wv_ref[rows, cols] - _dot_any(wk_ref[rows, cols], s, precise)
            if not precise:
                u = u.astype(BF16)
            u_scr[h, group_rows, :] = u
            o = (_dot_any(qd_ref[rows, cols], s, precise)
                 + _dot_any(p_ref[rows, h * slot:(h + 1) * slot], u_scr[h], precise))
            s_scr[h] = ge_ref[c, :, cols] * s_scr[h] + _dot_any(ke_ref[rows, cols], u, precise, _dot_tn, 1)
            gate = gate_ref[rows, cols]
            od_ref[rows, cols] = (_rms(o, onorm) * _silu(gate)).astype(od_ref.dtype)

    @pl.when(n == pl.num_programs(1) - 1)
    def _():
        sout_ref[0] = s_scr[...]


def _dn_scan(prep, p, s0, onorm_row, chunk, group, n_chunks, batch):
    wv, wk, qdec, kend, pm, gend = prep
    m = wv.shape[0]
    assert n_chunks % group == 0
    rows = chunk * n_chunks
    steps = m // batch // rows
    blk = lambda b, n: (b * steps + n, 0)
    wide = pl.BlockSpec((rows, DN_WIDTH), blk)
    state = pl.BlockSpec((1, DN_HEADS, DN_KEY_DIM, DN_VAL_DIM), lambda b, n: (b, 0, 0, 0))
    slot = _dn_score_slot(chunk, group)
    assert pm.shape[1] == DN_HEADS * slot
    return pl.pallas_call(
        functools.partial(_dn_scan_kernel, chunk, group, n_chunks),
        grid=(batch, steps),
        in_specs=[
            wide, wide, wide, wide, pl.BlockSpec((rows, DN_HEADS * slot), blk),
            pl.BlockSpec((n_chunks, 1, DN_WIDTH), lambda b, n: (b * steps + n, 0, 0)),
            pl.BlockSpec((rows, DN_WIDTH), lambda b, n: (b * steps + n, P_GATE // DN_WIDTH)),
            state,
            pl.BlockSpec((1, LANES), lambda b, n: (0, 0)),
        ],
        out_specs=[wide, state],
        out_shape=[jax.ShapeDtypeStruct((m, DN_WIDTH), wk.dtype),
                   jax.ShapeDtypeStruct((batch, DN_HEADS, DN_KEY_DIM, DN_VAL_DIM), F32)],
        scratch_shapes=[pltpu.VMEM((DN_HEADS, DN_KEY_DIM, DN_VAL_DIM), F32),
                        pltpu.VMEM((DN_HEADS, slot, DN_VAL_DIM), wk.dtype)],
        compiler_params=_cparams("parallel", "arbitrary"),
        name="dn_scan",
    )(wv, wk, qdec, kend, pm, gend, p, s0, onorm_row)


def _out_proj_kernel(nw, attn_ref, od_ref, *refs):
    w_refs, (x_ref, g_ref, gate_ref, o_ref) = refs[:nw], refs[nw:]
    y = (_mm(attn_ref[...], tuple(r[0:ATTN_WIDTH, :] for r in w_refs))
         + _mm(od_ref[...], tuple(r[ATTN_WIDTH:, :] for r in w_refs)))
    o_ref[...] = x_ref[...] + gate_ref[...] * _rms(y, g_ref[...])


def _out_proj(attn, od, w, x, gain, gate, tm):
    m, d = x.shape
    row = lambda i: (i, 0)
    return pl.pallas_call(
        functools.partial(_out_proj_kernel, len(w)),
        grid=(m // tm,),
        in_specs=[
            pl.BlockSpec((tm, ATTN_WIDTH), row),
            pl.BlockSpec((tm, DN_WIDTH), row),
        ] + [pl.BlockSpec((ATTN_WIDTH + DN_WIDTH, d), lambda i: (0, 0))] * len(w) + [
            pl.BlockSpec((tm, d), row),
            pl.BlockSpec((1, d), lambda i: (0, 0)),
            _mod_spec(gate.shape[0], tm, d),
        ],
        out_specs=pl.BlockSpec((tm, d), row),
        out_shape=jax.ShapeDtypeStruct((m, d), F32),
        compiler_params=_cparams("parallel"),
        name="out_proj",
    )(attn, od, *w, x, gain, gate)


def _ffn_kernel(nw, x_ref, g_ref, sc_ref, sh_ref, *refs):
    wg_refs, wu_refs, wd_refs = refs[:nw], refs[nw:2 * nw], refs[2 * nw:3 * nw]
    g2_ref, gate_ref, o_ref, h_ref, acc_ref = refs[3 * nw:]
    j = pl.program_id(1)

    @pl.when(j == 0)
    def _():
        h = _rms(x_ref[...], g_ref[...]) * (1.0 + sc_ref[...]) + sh_ref[...]
        h_ref[...] = h.astype(h_ref.dtype)
        acc_ref[...] = jnp.zeros_like(acc_ref)

    h = h_ref[...]
    act = _silu(_mm(h, tuple(r[...] for r in wg_refs))) * _mm(h, tuple(r[...] for r in wu_refs))
    acc_ref[...] += _mm(act, tuple(r[...] for r in wd_refs))

    @pl.when(j == pl.num_programs(1) - 1)
    def _():
        o_ref[...] = x_ref[...] + gate_ref[...] * _rms(acc_ref[...], g2_ref[...])


def _ffn(x, gain, scale, shift, wg, wu, wd, gain2, gate, tm, tf):
    m, d = x.shape
    nw = len(wg)
    f = wg[0].shape[1]
    row = lambda i, j: (i, 0)
    vec = pl.BlockSpec((1, d), lambda i, j: (0, 0))
    return pl.pallas_call(
        functools.partial(_ffn_kernel, nw),
        grid=(m // tm, f // tf),
        in_specs=[
            pl.BlockSpec((tm, d), row), vec,
            _mod_spec(scale.shape[0], tm, d), _mod_spec(shift.shape[0], tm, d),
        ] + [pl.BlockSpec((d, tf), lambda i, j: (0, j))] * (2 * nw)
        + [pl.BlockSpec((tf, d), lambda i, j: (j, 0))] * nw
        + [vec, _mod_spec(gate.shape[0], tm, d)],
        out_specs=pl.BlockSpec((tm, d), row),
        out_shape=jax.ShapeDtypeStruct((m, d), F32),
        scratch_shapes=[pltpu.VMEM((tm, d), BF16 if wg[0].dtype == BF16 and nw == 1 else F32),
                        pltpu.VMEM((tm, d), F32)],
        compiler_params=_cparams("parallel", "arbitrary"),
        name="ffn_dense",
    )(x, gain, scale, shift, *wg, *wu, *wd, gain2, gate)


def _router_kernel(x_ref, g_ref, sc_ref, sh_ref, wr_ref, h_ref, gates_ref, idx_ref, w12_ref):
    h = _rms(x_ref[...], g_ref[...]) * (1.0 + sc_ref[...]) + sh_ref[...]
    h_ref[...] = h
    logits = _dot_x3(h, wr_ref[...])
    lane = lax.broadcasted_iota(jnp.int32, logits.shape, 1).astype(F32)
    logits = jnp.where(lane < N_EXPERTS, logits, -jnp.inf)
    m1 = jnp.max(logits, -1, keepdims=True)
    i1 = jnp.min(jnp.where(logits == m1, lane, float(LANES)), -1, keepdims=True)
    rest = jnp.where(lane == i1, -jnp.inf, logits)
    m2 = jnp.max(rest, -1, keepdims=True)
    i2 = jnp.min(jnp.where(rest == m2, lane, float(LANES)), -1, keepdims=True)
    t = jnp.exp(m2 - m1)
    w1 = 1.0 / (1.0 + t)
    w2 = t / (1.0 + t)
    gates_ref[...] = jnp.where(lane == i1, w1, 0.0) + jnp.where(lane == i2, w2, 0.0)
    idx_ref[...] = jnp.where(lane == 0.0, i1, jnp.where(lane == 1.0, i2, 0.0)).astype(jnp.int32)
    w12_ref[...] = jnp.where(lane == 0.0, w1, jnp.where(lane == 1.0, w2, 0.0))


def _router(x, gain, scale, shift, w_router_pad, tm):
    m, d = x.shape
    row = lambda i: (i, 0)
    vec = pl.BlockSpec((1, d), lambda i: (0, 0))
    small = pl.BlockSpec((tm, LANES), row)
    return pl.pallas_call(
        _router_kernel,
        grid=(m // tm,),
        in_specs=[pl.BlockSpec((tm, d), row), vec,
                  _mod_spec(scale.shape[0], tm, d), _mod_spec(shift.shape[0], tm, d),
                  pl.BlockSpec((d, LANES), lambda i: (0, 0))],
        out_specs=[pl.BlockSpec((tm, d), row), small, small, small],
        out_shape=[jax.ShapeDtypeStruct((m, d), F32), jax.ShapeDtypeStruct((m, LANES), F32),
                   jax.ShapeDtypeStruct((m, LANES), jnp.int32), jax.ShapeDtypeStruct((m, LANES), F32)],
        compiler_params=_cparams("parallel"),
        name="moe_router",
    )(x, gain, scale, shift, w_router_pad)


def _moe_gemm_kernel(nj, te_ref, tot_ref, rt_ref, h_hbm, wg_ref, wu_ref, wd_ref, ys_ref, xs_ref, xb_ref, acc_ref,
                     sems):
    r = pl.program_id(0)
    j = pl.program_id(1)
    tm = xb_ref.shape[0]
    total = tot_ref[0]
    active = r < total
    slot = r % 2
    share = -(-tm // nj)

    def row_copy(tile, t, s):
        return pltpu.make_async_copy(h_hbm.at[pl.ds(rt_ref[tile * tm + t], 1), :],
                                     xs_ref.at[s, pl.ds(t, 1), :], sems.at[s])

    def start_rows(tile, s, lo, hi):
        def body(t, carry):
            row_copy(tile, t, s).start()
            return carry

        lax.fori_loop(lo, hi, body, 0)

    @pl.when(jnp.logical_and(r == 0, j == 0))
    def _():
        start_rows(0, 0, 0, tm)

    @pl.when(jnp.logical_and(active, j == 0))
    def _():
        pltpu.make_async_copy(h_hbm.at[pl.ds(0, tm), :], xs_ref.at[slot], sems.at[slot]).wait()
        xb_ref[...] = xs_ref[slot].astype(BF16)
        acc_ref[...] = jnp.zeros_like(acc_ref)

    @pl.when(active)
    def _():
        more = r + 1 < total
        for u in range(share):
            t = j * share + u

            @pl.when(jnp.logical_and(more, t < tm))
            def _():
                row_copy(jnp.minimum(r + 1, pl.num_programs(0) - 1), jnp.minimum(t, tm - 1), 1 - slot).start()

        xb = xb_ref[...]
        act = (_silu(_dot(xb, wg_ref[0])) * _dot(xb, wu_ref[0])).astype(BF16)
        acc_ref[...] += _dot(act, wd_ref[0])

    @pl.when(j == pl.num_programs(1) - 1)
    def _():
        ys_ref[...] = jnp.where(active, acc_ref[...], 0.0)


def _moe_gemm(tile_expert, total_tiles, row_token, h, wg, wu, wd, tm, tf):
    n_tiles = tile_expert.shape[0]
    d = h.shape[1]
    f = wg.shape[2]
    nj = f // tf

    def w_col(r, j, te, tot, rt):
        return (te[r], 0, jnp.where(r < tot[0], j, nj - 1))

    def w_row(r, j, te, tot, rt):
        return (te[r], jnp.where(r < tot[0], j, nj - 1), 0)

    grid_spec = pltpu.PrefetchScalarGridSpec(
        num_scalar_prefetch=3,
        grid=(n_tiles, nj),
        in_specs=[
            pl.BlockSpec(memory_space=pl.ANY),
            pl.BlockSpec((1, d, tf), w_col),
            pl.BlockSpec((1, d, tf), w_col),
            pl.BlockSpec((1, tf, d), w_row),
        ],
        out_specs=pl.BlockSpec((tm, d), lambda r, j, te, tot, rt: (r, 0)),
        scratch_shapes=[pltpu.VMEM((2, tm, d), F32), pltpu.VMEM((tm, d), BF16), pltpu.VMEM((tm, d), F32),
                        pltpu.SemaphoreType.DMA((2,))],
    )
    return pl.pallas_call(
        functools.partial(_moe_gemm_kernel, nj),
        grid_spec=grid_spec,
        out_shape=jax.ShapeDtypeStruct((n_tiles * tm, d), F32),
        compiler_params=_cparams("arbitrary", "arbitrary", row_dma=True),
        name="moe_gemm",
    )(tile_expert, total_tiles, row_token, h, wg, wu, wd)


def _moe_combine_kernel(dest_ref, x_ref, w12_ref, g_ref, gate_ref, ys_hbm, o_ref, buf_ref, sems):
    i = pl.program_id(0)
    tb = x_ref.shape[0]
    slot = i % 2

    def row_copy(blk, t, k, s):
        src = dest_ref[2 * (blk * tb + t) + k]
        return pltpu.make_async_copy(ys_hbm.at[pl.ds(src, 1), :], buf_ref.at[s, k, pl.ds(t, 1), :], sems.at[s])

    def start_block(blk, s):
        def body(t, carry):
            row_copy(blk, t, 0, s).start()
            row_copy(blk, t, 1, s).start()
            return carry

        lax.fori_loop(0, tb, body, 0, unroll=8)

    @pl.when(i == 0)
    def _():
        start_block(0, 0)

    @pl.when(i + 1 < pl.num_programs(0))
    def _():
        start_block(i + 1, 1 - slot)

    for k in range(2):
        pltpu.make_async_copy(ys_hbm.at[pl.ds(0, tb), :], buf_ref.at[slot, k], sems.at[slot]).wait()
    w12 = w12_ref[...]
    y = w12[:, 0:1] * buf_ref[slot, 0] + w12[:, 1:2] * buf_ref[slot, 1]
    o_ref[...] = x_ref[...] + gate_ref[...] * _rms(y, g_ref[...])


def _moe_combine(dest, x, w12, gain, gate, ys, tb):
    m, d = x.shape
    row = lambda i, dst: (i, 0)
    grid_spec = pltpu.PrefetchScalarGridSpec(
        num_scalar_prefetch=1,
        grid=(m // tb,),
        in_specs=[
            pl.BlockSpec((tb, d), row),
            pl.BlockSpec((tb, LANES), row),
            pl.BlockSpec((1, d), lambda i, dst: (0, 0)),
            pl.BlockSpec((1, d), lambda i, dst: (0, 0)),
            pl.BlockSpec(memory_space=pl.ANY),
        ],
        out_specs=pl.BlockSpec((tb, d), row),
        scratch_shapes=[pltpu.VMEM((2, 2, tb, d), F32), pltpu.SemaphoreType.DMA((2,))],
    )
    return pl.pallas_call(
        _moe_combine_kernel,
        grid_spec=grid_spec,
        out_shape=jax.ShapeDtypeStruct((m, d), F32),
        compiler_params=_cparams("arbitrary", row_dma=True),
        name="moe_combine",
    )(dest, x, w12, gain, gate, ys)


def _route_tables(idx2, tm, n_tiles):
    m = idx2.shape[0]
    n_assign = 2 * m
    assert n_tiles * tm == n_assign + N_EXPERTS * tm
    experts = jnp.arange(N_EXPERTS, dtype=jnp.int32)
    e_flat = idx2.reshape(n_assign)
    onehot = (e_flat[:, None] == experts[None, :]).astype(jnp.int32)
    csum = jnp.cumsum(onehot, 0)
    counts = csum[-1]
    padded = ((counts + tm - 1) // tm) * tm
    pend = jnp.cumsum(padded)
    pstart = pend - padded
    dest = jnp.sum(onehot * (pstart[None, :] + csum - 1), -1).astype(jnp.int32)
    total_tiles = (pend[-1] // tm).astype(jnp.int32).reshape(1)
    tile_expert = jnp.minimum(
        jnp.searchsorted(pend // tm, jnp.arange(n_tiles, dtype=jnp.int32), side="right"), N_EXPERTS - 1
    ).astype(jnp.int32)
    filler_key = jnp.where(jnp.arange(tm, dtype=jnp.int32)[None, :] < (padded - counts)[:, None],
                           experts[:, None], N_EXPERTS).reshape(-1)
    keys = jnp.concatenate([e_flat, filler_key])
    tokens = jnp.concatenate([jnp.arange(n_assign, dtype=jnp.int32) // 2,
                              jnp.zeros((N_EXPERTS * tm,), jnp.int32)])
    _, row_token = lax.sort((keys, tokens), num_keys=1, is_stable=True)
    return tile_expert, total_tiles, row_token, dest


def _moe_dense_kernel(h_ref, gates_ref, wg_ref, wu_ref, wd_ref, x_ref, g_ref, gate_ref, o_ref, acc_ref, tot_ref):
    e = pl.program_id(0)
    j = pl.program_id(1)
    nj = pl.num_programs(1)

    @pl.when(jnp.logical_and(e == 0, j == 0))
    def _():
        tot_ref[...] = jnp.zeros_like(tot_ref)

    @pl.when(j == 0)
    def _():
        acc_ref[...] = jnp.zeros_like(acc_ref)

    h = h_ref[...].astype(BF16)
    act = (_silu(_dot(h, wg_ref[0])) * _dot(h, wu_ref[0])).astype(BF16)
    acc_ref[...] += _dot(act, wd_ref[0])

    @pl.when(j == nj - 1)
    def _():
        gates = gates_ref[...]
        lane = lax.broadcasted_iota(jnp.int32, gates.shape, 1)
        ge = jnp.sum(jnp.where(lane == e, gates, 0.0), -1, keepdims=True)
        tot_ref[...] += ge * acc_ref[...]

    @pl.when(jnp.logical_and(e == pl.num_programs(0) - 1, j == nj - 1))
    def _():
        o_ref[...] = x_ref[...] + gate_ref[...] * _rms(tot_ref[...], g_ref[...])


def _moe_dense(h, gates, wg, wu, wd, x, gain, gate, tf):
    m, d = x.shape
    f = wg.shape[2]
    full = pl.BlockSpec((m, d), lambda e, j: (0, 0))
    return pl.pallas_call(
        _moe_dense_kernel,
        grid=(N_EXPERTS, f // tf),
        in_specs=[
            full,
            pl.BlockSpec((m, LANES), lambda e, j: (0, 0)),
            pl.BlockSpec((1, d, tf), lambda e, j: (e, 0, j)),
            pl.BlockSpec((1, d, tf), lambda e, j: (e, 0, j)),
            pl.BlockSpec((1, tf, d), lambda e, j: (e, j, 0)),
            full,
            pl.BlockSpec((1, d), lambda e, j: (0, 0)),
            full,
        ],
        out_specs=full,
        out_shape=jax.ShapeDtypeStruct((m, d), F32),
        scratch_shapes=[pltpu.VMEM((m, d), F32), pltpu.VMEM((m, d), F32)],
        compiler_params=_cparams("arbitrary", "arbitrary"),
        name="moe_dense",
    )(h, gates, wg, wu, wd, x, gain, gate)


def _rope_tables(pos):
    half = ROPE_DIM // 2
    inv_freq = jnp.power(ROPE_THETA, -2.0 * jnp.arange(half, dtype=F32) / ROPE_DIM)
    ang = pos.astype(F32)[:, None] * inv_freq[None, :]
    cos, sin = jnp.cos(ang), jnp.sin(ang)
    t = pos.shape[0]
    rest = ATTN_HEAD_DIM - ROPE_DIM
    cos_h = jnp.concatenate([cos, cos, jnp.ones((t, rest), F32)], 1)
    sa_h = jnp.concatenate([-sin, jnp.zeros((t, half + rest), F32)], 1)
    sb_h = jnp.concatenate([jnp.zeros((t, half), F32), sin, jnp.zeros((t, rest), F32)], 1)
    rep = LANES // ATTN_HEAD_DIM
    return tuple(jnp.tile(a, (1, rep)) for a in (cos_h, sa_h, sb_h))


def _permute_w_in(w):
    o1 = ATTN_WIDTH
    o2 = o1 + KV_WIDTH
    o3 = o2 + KV_WIDTH
    o4 = o3 + DN_CONV_CH
    o5 = o4 + DN_WIDTH
    parts = [w[:, :o1], w[:, o4:o5], w[:, o3:o4], w[:, o1:o2], w[:, o2:o3], w[:, o5:]]
    used = sum(a.shape[1] for a in parts)
    parts.append(jnp.zeros((w.shape[0], P_WIDTH - used), w.dtype))
    return jnp.concatenate(parts, 1)


def _sample_mask_bias(s, lc):
    q_pos = PAST_LEN + np.arange(s)
    k_pos = np.concatenate([PAST_LEN - lc + np.arange(lc), q_pos])
    q_chunk = q_pos[:, None] // CHUNK
    k_chunk = k_pos[None, :] // CHUNK
    mask = (k_pos[None, :] >= 0) & (k_chunk <= q_chunk) & (k_pos[None, :] >= q_chunk * CHUNK - WINDOW)
    bias = np.where(mask, 0.0, -np.inf).astype(np.float32)
    return jnp.asarray(np.tile(bias, (4, 1)))


def _conv_halo(p, init, tb, seq):
    m = p.shape[0]
    batch = m // seq
    nb = seq // tb
    tails = p.reshape(batch, nb, tb, P_WIDTH)[:, :nb - 1, tb - (CONV_WIDTH - 1):, P_CONV:P_CONV + DN_CONV_CH]
    prev = jnp.concatenate([init[:, None], tails], 1)
    prev = prev.reshape(batch * nb, CONV_WIDTH - 1, DN_CONV_CH)
    return jnp.pad(prev, ((0, 0), (8 - (CONV_WIDTH - 1), 0), (0, 0)))


def _trunk(x, mods, layer_w, rope, past, cfg):
    m = x.shape[0]
    batch, seq = cfg["batch"], cfg["seq"]
    precise = cfg["precise"]
    nw = 2 if precise else 1
    ks, vs, ss, bufs = [], [], [], []
    cos, sa, sb = rope
    for l in range(DEPTH):
        w = layer_w[l]
        sh_a, sc_a, g_a, sh_f, sc_f, g_f = mods[l]
        p = _norm_proj(x, w["gain"][0], sc_a, sh_a, w["w_in"][:nw], cfg["tm_proj"], cfg["tn_proj"])
        if past is None:
            attn, k_new = _attn_prompt(p, w["sinks"], cos, sa, sb, cfg["tb_attn"])
            s0 = jnp.zeros((batch, DN_HEADS, DN_KEY_DIM, DN_VAL_DIM), F32)
            conv_init = jnp.zeros((batch, CONV_WIDTH - 1, DN_CONV_CH), F32)
        else:
            ck = past[0][l].reshape(batch, -1, KV_WIDTH)
            cv = past[1][l].reshape(batch, -1, KV_WIDTH)
            attn, k_new = _attn_sample(p, ck, cv, w["sinks"], cos, sa, sb, cfg["bias"], batch, seq)
            s0 = past[2][l]
            conv_init = past[3][l]
        halo = _conv_halo(p, conv_init, cfg["tb_dn"], seq)
        prep = _dn_prep(p, halo, w["w_conv"], w["alog"], w["dtb"], cfg["chunk"], cfg["group"], cfg["tb_dn"],
                        cfg["dn_heads_per_step"], precise)
        od, s_new = _dn_scan(prep, p, s0, w["onorm"], cfg["chunk"], cfg["group"], cfg["scan_chunks"], batch)
        pick = 1 if precise else 0
        x = _out_proj(attn, od, w["w_out"][pick], x, w["gain"][1], g_a, cfg["tm_out"])
        if l % 2 == 0:
            x = _ffn(x, w["gain"][2], sc_f, sh_f, w["ffn_gate"][pick], w["ffn_up"][pick], w["ffn_down"][pick],
                     w["gain"][3], g_f, cfg["tm_ffn"], cfg["tf_ffn"])
        else:
            h, gates, idx, w12 = _router(x, w["gain"][2], sc_f, sh_f, w["router"], cfg["tm_router"])
            if cfg["routed"]:
                tm = cfg["tm_moe"]
                n_tiles = 2 * m // tm + N_EXPERTS
                tile_expert, total_tiles, row_token, dest = _route_tables(idx[:, :2], tm, n_tiles)
                ys = _moe_gemm(tile_expert, total_tiles, row_token, h, w["moe_gate"], w["moe_up"], w["moe_down"],
                               tm, cfg["tf_moe"])
                x = _moe_combine(dest, x, w12, w["gain"][3], g_f, ys, cfg["tb_combine"])
            else:
                x = _moe_dense(h, gates, w["moe_gate"], w["moe_up"], w["moe_down"], x, w["gain"][3], g_f,
                               cfg["tf_moe"])
        pb = p.reshape(batch, seq, P_WIDTH)
        keep = min(WINDOW, seq) if past is None else seq
        ks.append(k_new.reshape(batch, seq, ATTN_KV_HEADS, ATTN_HEAD_DIM)[:, seq - keep:])
        vs.append(pb[:, seq - keep:, P_V:P_V + KV_WIDTH].reshape(batch, keep, ATTN_KV_HEADS, ATTN_HEAD_DIM))
        ss.append(s_new)
        assert seq >= CONV_WIDTH - 1
        bufs.append(pb[:, seq - (CONV_WIDTH - 1):, P_CONV:P_CONV + DN_CONV_CH])
    return x, jnp.stack(ks), jnp.stack(vs), jnp.stack(ss), jnp.stack(bufs)


def kernel(x_prompt, x_sample, cache_attn_k, cache_attn_v, state_delta, state_conv, c_prompt, c_sample, w_in, w_conv, attn_sinks, dn_a_log, dn_dt_bias, dn_norm, w_out, w_mod, b_mod, norm_gains, ffn_gate, ffn_up, ffn_down, moe_router, moe_gate, moe_up, moe_down):
    bp, tp, d = x_prompt.shape
    bs, ts, _ = x_sample.shape
    assert bp == 1 and d == D_MODEL

    c_all = jnp.concatenate([c_prompt, c_sample, jnp.zeros((16 - bp - bs, d), F32)], 0)
    mod = _modulation(c_all, w_mod, b_mod)
    mods_p, mods_s = [], []
    for l in range(DEPTH):
        six = jnp.split(mod[l], 6, -1)
        mods_p.append([a[0:bp] for a in six])
        mods_s.append([jnp.repeat(a[bp:bp + bs], ts, axis=0) for a in six])

    def pad_lanes(v, at):
        return jnp.zeros((1, LANES), F32).at[0, at:at + v.shape[0]].set(v)

    def both(w):
        return ((w.astype(BF16),), (w,))

    layer_w = []
    for l in range(DEPTH):
        w = {
            "gain": [norm_gains[l, i].reshape(1, d) for i in range(4)],
            "w_in": _split_weight(_permute_w_in(w_in[l])),
            "sinks": attn_sinks[l],
            "w_conv": jnp.pad(w_conv[l], ((0, 8 - CONV_WIDTH), (0, 0))),
            "alog": pad_lanes(dn_a_log[l], DN_HEADS),
            "dtb": pad_lanes(dn_dt_bias[l], DN_HEADS),
            "onorm": dn_norm[l].reshape(1, DN_VAL_DIM),
            "w_out": both(w_out[l]),
        }
        if l % 2 == 0:
            w["ffn_gate"] = both(ffn_gate[l // 2])
            w["ffn_up"] = both(ffn_up[l // 2])
            w["ffn_down"] = both(ffn_down[l // 2])
        else:
            w["router"] = jnp.pad(moe_router[l // 2], ((0, 0), (0, LANES - N_EXPERTS)))
            w["moe_gate"] = moe_gate[l // 2].astype(BF16)
            w["moe_up"] = moe_up[l // 2].astype(BF16)
            w["moe_down"] = moe_down[l // 2].astype(BF16)
        layer_w.append(w)

    cfg_p = dict(batch=bp, seq=tp, precise=False, chunk=CHUNK, group=2, scan_chunks=4, tm_proj=1024, tn_proj=512, tb_attn=512,
                 tb_dn=2048, dn_heads_per_step=1, tm_out=512, tm_ffn=512, tf_ffn=512, tm_router=512, routed=True, tm_moe=512,
                 tf_moe=256, tb_combine=256)
    rope_p = _rope_tables(jnp.arange(tp, dtype=jnp.int32))
    y_p, k_p, v_p, s_p, conv_p = _trunk(x_prompt.reshape(bp * tp, d), mods_p, layer_w, rope_p, None, cfg_p)

    ms = bs * ts
    cfg_s = dict(batch=bs, seq=ts, precise=True, chunk=ts, group=1, scan_chunks=1, tm_proj=ms, tn_proj=512, tb_dn=ts, dn_heads_per_step=DN_HEADS, tm_out=ms,
                 tm_ffn=ms, tf_ffn=512, tm_router=ms, routed=False, tf_moe=1408,
                 bias=_sample_mask_bias(ts, cache_attn_k.shape[2]))
    rope_s = _rope_tables(PAST_LEN + jnp.arange(ts, dtype=jnp.int32))
    past = (cache_attn_k, cache_attn_v, state_delta, state_conv)
    y_s, k_s, v_s, s_s, conv_s = _trunk(x_sample.reshape(ms, d), mods_s, layer_w, rope_s, past, cfg_s)

    return (y_p.reshape(bp, tp, d), y_s.reshape(bs, ts, d), k_p, v_p, s_p, conv_p, k_s, v_s, s_s, conv_s)
```

```python
import functools
import math

import numpy as np
import jax
import jax.numpy as jnp
from jax import lax
from jax.experimental import pallas as pl
from jax.experimental.pallas import tpu as pltpu

D_MODEL = 2048
DEPTH = 2
PAST_LEN = 1024
CHUNK = 64
ATTN_HEADS = 16
ATTN_KV_HEADS = 2
ATTN_HEAD_DIM = 64
ATTN_WIDTH = 1024
KV_WIDTH = 128
WINDOW = 128
ROPE_THETA = 500000.0
ROPE_DIM = 16
DN_HEADS = 8
DN_KEY_DIM = 128
DN_VAL_DIM = 128
DN_WIDTH = 1024
CONV_WIDTH = 4
DN_CONV_CH = 3072
D_FF = 5632
N_EXPERTS = 8
D_FF_EXPERT = 2816
EPS = 1e-6

F32 = jnp.float32
BF16 = jnp.bfloat16
LANES = 128
V7X_MXU_DEPTH = 256
V7X_VMEM_LIMIT = 56 * 1024 * 1024

P_Q = 0
P_GATE = 1024
P_CONV = 2048
P_K = 5120
P_V = 5248
P_BA = 5376
P_WIDTH = 5632


def _cparams(*sem, row_dma=False):
    return pltpu.CompilerParams(dimension_semantics=sem, vmem_limit_bytes=V7X_VMEM_LIMIT,
                                disable_bounds_checks=row_dma)


def _sigmoid(x):
    return 0.5 * jnp.tanh(0.5 * x) + 0.5


def _silu(x):
    return x * _sigmoid(x)


def _rms(x, gain):
    return x * lax.rsqrt(jnp.mean(x * x, -1, keepdims=True) + EPS) * gain


def _dot(a, b):
    return jnp.dot(a, b, preferred_element_type=F32)


def _dot_nt(a, b):
    return lax.dot_general(a, b, (((1,), (1,)), ((), ())), preferred_element_type=F32)


def _dot_tn(a, b):
    return lax.dot_general(a, b, (((0,), (0,)), ((), ())), preferred_element_type=F32)


def _split_bf16(a):
    hi = a.astype(BF16)
    lo = (a - hi.astype(F32)).astype(BF16)
    return hi, lo


def _dot_x3(a, b, dot=_dot, out_axis=0):
    a_hi, a_lo = _split_bf16(a)
    b_hi, b_lo = _split_bf16(b)
    n = a.shape[out_axis]
    top = dot(jnp.concatenate([a_hi, a_lo], out_axis), b_hi)
    return top[:n] + top[n:] + dot(a_hi, b_lo)


def _dot_any(a, b, precise, dot=_dot, out_axis=0):
    if precise:
        return _dot_x3(a, b, dot, out_axis)
    return dot(a.astype(BF16), b.astype(BF16))


def _mm(a, w):
    if len(w) == 1 and w[0].dtype == F32:
        return _dot_x3(a, w[0])
    if len(w) == 1:
        return _dot(a.astype(BF16), w[0])
    a_hi, a_lo = _split_bf16(a)
    n = a.shape[0]
    top = _dot(jnp.concatenate([a_hi, a_lo], 0), w[0])
    return top[:n] + top[n:] + _dot(a_hi, w[1])


def _split_weight(w):
    hi, lo = _split_bf16(w)
    return (hi, lo)


def _mod_spec(rows, tm, d):
    if rows == 1:
        return pl.BlockSpec((1, d), lambda i, *_: (0, 0))
    return pl.BlockSpec((tm, d), lambda i, *_: (i, 0))


def _mod_kernel(c_ref, w_ref, b_ref, o_ref):
    o_ref[0] = _dot_x3(_silu(c_ref[...]), w_ref[0]) + b_ref[0]


def _modulation(c_all, w_mod, b_mod):
    rows = c_all.shape[0]
    n = w_mod.shape[2]
    tn = 1024
    return pl.pallas_call(
        _mod_kernel,
        grid=(DEPTH, n // tn),
        in_specs=[
            pl.BlockSpec((rows, D_MODEL), lambda l, j: (0, 0)),
            pl.BlockSpec((1, D_MODEL, tn), lambda l, j: (l, 0, j)),
            pl.BlockSpec((1, 1, tn), lambda l, j: (l, 0, j)),
        ],
        out_specs=pl.BlockSpec((1, rows, tn), lambda l, j: (l, 0, j)),
        out_shape=jax.ShapeDtypeStruct((DEPTH, rows, n), F32),
        compiler_params=_cparams("parallel", "parallel"),
        name="modulation",
    )(c_all, w_mod, b_mod.reshape(DEPTH, 1, n))


def _norm_proj_kernel(nw, x_ref, g_ref, sc_ref, sh_ref, *refs):
    w_refs, (o_ref, h_ref) = refs[:nw], refs[nw:]

    @pl.when(pl.program_id(1) == 0)
    def _():
        h = _rms(x_ref[...], g_ref[...]) * (1.0 + sc_ref[...]) + sh_ref[...]
        h_ref[...] = h.astype(h_ref.dtype)

    o_ref[...] = _mm(h_ref[...], tuple(r[...] for r in w_refs))


def _norm_proj(x, gain, scale, shift, w, tm, tn):
    m, d = x.shape
    n = w[0].shape[1]
    return pl.pallas_call(
        functools.partial(_norm_proj_kernel, len(w)),
        grid=(m // tm, n // tn),
        in_specs=[
            pl.BlockSpec((tm, d), lambda i, j: (i, 0)),
            pl.BlockSpec((1, d), lambda i, j: (0, 0)),
            _mod_spec(scale.shape[0], tm, d),
            _mod_spec(shift.shape[0], tm, d),
        ] + [pl.BlockSpec((d, tn), lambda i, j: (0, j))] * len(w),
        out_specs=pl.BlockSpec((tm, tn), lambda i, j: (i, j)),
        out_shape=jax.ShapeDtypeStruct((m, n), F32),
        scratch_shapes=[pltpu.VMEM((tm, d), BF16 if len(w) == 1 else F32)],
        compiler_params=_cparams("parallel", "arbitrary"),
        name="norm_proj",
    )(x, gain, scale, shift, *w)


def _rope(x, cos, sa, sb):
    return x * cos + pltpu.roll(x, LANES - 8, 1) * sa + pltpu.roll(x, 8, 1) * sb


def _kv_variants(k, v):
    lo = lax.broadcasted_iota(jnp.int32, k.shape, 1) < ATTN_HEAD_DIM
    kr = pltpu.roll(k, ATTN_HEAD_DIM, 1)
    vr = pltpu.roll(v, ATTN_HEAD_DIM, 1)
    zero = jnp.zeros_like(k)
    k_lo = (jnp.where(lo, k, zero), jnp.where(lo, kr, zero))
    k_hi = (jnp.where(lo, zero, kr), jnp.where(lo, zero, k))
    v_lo = (jnp.where(lo, v, zero), jnp.where(lo, vr, zero))
    v_hi = (jnp.where(lo, zero, vr), jnp.where(lo, zero, v))
    return k_lo, k_hi, v_lo, v_hi


def _sink_softmax(s, sink):
    m = jnp.maximum(jnp.max(s, -1, keepdims=True), sink)
    p = jnp.exp(s - m)
    den = jnp.sum(p, -1, keepdims=True) + jnp.exp(sink - m)
    return p * (1.0 / den)


def _attn_core(qbs, k_los, k_his, v_los, v_his, biases, sinks, precise=False):
    scores = [(_dot_any(qb, k_lo, precise, _dot_nt) + bias, _dot_any(qb, k_hi, precise, _dot_nt) + bias)
              for qb, k_lo, k_hi, bias in zip(qbs, k_los, k_his, biases)]
    probs = [(_sink_softmax(s_even, sink[0]), _sink_softmax(s_odd, sink[1]))
             for (s_even, s_odd), sink in zip(scores, sinks)]
    return [_dot_any(p_even, v_lo, precise) + _dot_any(p_odd, v_hi, precise)
            for (p_even, p_odd), v_lo, v_hi in zip(probs, v_los, v_his)]


def _sink_columns(sink_ref, rows_per_pair):
    n = 4 * rows_per_pair
    pair = lax.broadcasted_iota(jnp.int32, (n, 1), 0) // rows_per_pair
    out = []
    for j in range(ATTN_KV_HEADS):
        cols = []
        for par in range(2):
            col = jnp.zeros((n, 1), F32)
            for a in range(4):
                col = jnp.where(pair == a, sink_ref[8 * j + 2 * a + par], col)
            cols.append(col)
        out.append(cols)
    return out


def _attn_prompt_kernel(sink_ref, q_ref, kv_ref, cos_ref, sa_ref, sb_ref, o_ref, knew_ref,
                        qs_ref, klo_ref, khi_ref, vlo_ref, vhi_ref):
    i = pl.program_id(0)
    tb = q_ref.shape[0]
    bufs = (klo_ref, khi_ref, vlo_ref, vhi_ref)

    @pl.when(i == 0)
    def _():
        for r in bufs:
            r[:, 0:WINDOW, :] = jnp.zeros((ATTN_KV_HEADS, WINDOW, LANES), BF16)

    @pl.when(i > 0)
    def _():
        for r in bufs:
            r[:, 0:WINDOW, :] = r[:, tb:tb + WINDOW, :]

    cos, sa, sb = cos_ref[...], sa_ref[...], sb_ref[...]
    k = _rope(kv_ref[:, 0:LANES], cos, sa, sb)
    knew_ref[...] = k
    variants = _kv_variants(k, kv_ref[:, LANES:2 * LANES])
    for r, var in zip(bufs, variants):
        for j in range(ATTN_KV_HEADS):
            r[j, WINDOW:, :] = var[j].astype(BF16)
    scale = ATTN_HEAD_DIM ** -0.5
    for a in range(ATTN_WIDTH // LANES):
        cols = slice(a * LANES, (a + 1) * LANES)
        qs_ref[:, cols] = (_rope(q_ref[:, cols], cos, sa, sb) * scale).astype(BF16)

    sinks = _sink_columns(sink_ref, CHUNK)
    nk = WINDOW + CHUNK

    per_iter = 2

    def chunks_body(c4, carry):
        entries = []
        for c in range(per_iter):
            r0 = pl.multiple_of((c4 * per_iter + c) * CHUNK, CHUNK)
            kpos = i * tb - WINDOW + r0 + lax.broadcasted_iota(jnp.int32, (1, nk), 1)
            bias = jnp.where(kpos >= 0, 0.0, -jnp.inf).astype(F32)
            entries += [(r0, j, bias) for j in range(ATTN_KV_HEADS)]
        qbs = [jnp.concatenate(
            [qs_ref[pl.ds(r0, CHUNK), (4 * j + a) * LANES:(4 * j + a + 1) * LANES] for a in range(4)], 0)
            for r0, j, _ in entries]
        outs = _attn_core(qbs,
                          [klo_ref[j, pl.ds(r0, nk), :] for r0, j, _ in entries],
                          [khi_ref[j, pl.ds(r0, nk), :] for r0, j, _ in entries],
                          [vlo_ref[j, pl.ds(r0, nk), :] for r0, j, _ in entries],
                          [vhi_ref[j, pl.ds(r0, nk), :] for r0, j, _ in entries],
                          [bias for _, _, bias in entries], [sinks[j] for _, j, _ in entries])
        for (r0, j, _), o in zip(entries, outs):
            for a in range(4):
                o_ref[pl.ds(r0, CHUNK), (4 * j + a) * LANES:(4 * j + a + 1) * LANES] = (
                    o[a * CHUNK:(a + 1) * CHUNK].astype(BF16))
        return carry

    lax.fori_loop(0, tb // (CHUNK * per_iter), chunks_body, 0)


def _attn_prompt(p, sinks, cos, sa, sb, tb):
    t = p.shape[0]
    kv_blk = P_K // (2 * LANES)
    row = lambda i: (i, 0)
    return pl.pallas_call(
        _attn_prompt_kernel,
        grid=(t // tb,),
        in_specs=[
            pl.BlockSpec(memory_space=pltpu.SMEM),
            pl.BlockSpec((tb, ATTN_WIDTH), row),
            pl.BlockSpec((tb, 2 * LANES), lambda i: (i, kv_blk)),
            pl.BlockSpec((tb, LANES), row),
            pl.BlockSpec((tb, LANES), row),
            pl.BlockSpec((tb, LANES), row),
        ],
        out_specs=[pl.BlockSpec((tb, ATTN_WIDTH), row), pl.BlockSpec((tb, LANES), row)],
        out_shape=[jax.ShapeDtypeStruct((t, ATTN_WIDTH), BF16), jax.ShapeDtypeStruct((t, LANES), F32)],
        scratch_shapes=[pltpu.VMEM((tb, ATTN_WIDTH), BF16)]
        + [pltpu.VMEM((ATTN_KV_HEADS, tb + WINDOW, LANES), BF16) for _ in range(4)],
        compiler_params=_cparams("arbitrary"),
        name="attn_prompt",
    )(sinks, p, p, cos, sa, sb)


def _attn_sample_kernel(sink_ref, q_ref, kv_ref, ck_ref, cv_ref, cos_ref, sa_ref, sb_ref, bias_ref,
                        o_ref, knew_ref):
    s = q_ref.shape[0]
    cos, sa, sb = cos_ref[...], sa_ref[...], sb_ref[...]
    k = _rope(kv_ref[:, 0:LANES], cos, sa, sb)
    knew_ref[...] = k
    kk = jnp.concatenate([ck_ref[0], k], 0)
    vv = jnp.concatenate([cv_ref[0], kv_ref[:, LANES:2 * LANES]], 0)
    k_lo, k_hi, v_lo, v_hi = _kv_variants(kk, vv)
    sinks = _sink_columns(sink_ref, s)
    scale = ATTN_HEAD_DIM ** -0.5
    bias = bias_ref[...]
    qbs = [jnp.concatenate(
        [_rope(q_ref[:, (4 * j + a) * LANES:(4 * j + a + 1) * LANES], cos, sa, sb) * scale for a in range(4)], 0)
        for j in range(ATTN_KV_HEADS)]
    outs = _attn_core(qbs, k_lo, k_hi, v_lo, v_hi, [bias] * ATTN_KV_HEADS, sinks, precise=True)
    for j, o in enumerate(outs):
        for a in range(4):
            o_ref[:, (4 * j + a) * LANES:(4 * j + a + 1) * LANES] = o[a * s:(a + 1) * s]


def _attn_sample(p, cache_k, cache_v, sinks, cos, sa, sb, bias, batch, s):
    lc = cache_k.shape[1]
    kv_blk = P_K // (2 * LANES)
    row = lambda b: (b, 0)
    const = lambda b: (0, 0)
    return pl.pallas_call(
        _attn_sample_kernel,
        grid=(batch,),
        in_specs=[
            pl.BlockSpec(memory_space=pltpu.SMEM),
            pl.BlockSpec((s, ATTN_WIDTH), row),
            pl.BlockSpec((s, 2 * LANES), lambda b: (b, kv_blk)),
            pl.BlockSpec((1, lc, LANES), lambda b: (b, 0, 0)),
            pl.BlockSpec((1, lc, LANES), lambda b: (b, 0, 0)),
            pl.BlockSpec((s, LANES), const),
            pl.BlockSpec((s, LANES), const),
            pl.BlockSpec((s, LANES), const),
            pl.BlockSpec((4 * s, lc + s), const),
        ],
        out_specs=[pl.BlockSpec((s, ATTN_WIDTH), row), pl.BlockSpec((s, LANES), row)],
        out_shape=[jax.ShapeDtypeStruct((batch * s, ATTN_WIDTH), F32),
                   jax.ShapeDtypeStruct((batch * s, LANES), F32)],
        compiler_params=_cparams("parallel"),
        name="attn_sample",
    )(sinks, p, p, cache_k, cache_v, cos, sa, sb, bias)


class _BlockDiag:
    def __init__(self, chunk, group):
        self.chunk, self.group = chunk, group
        n = chunk * group
        lane = lax.broadcasted_iota(jnp.int32, (chunk, n), 1)
        self.lane_block = lane // chunk
        self.eye = (lax.broadcasted_iota(jnp.int32, (chunk, n), 0) == lane % chunk).astype(F32)

    def wide(self, tall):
        c = self.chunk
        out = tall[0:c]
        for b in range(1, self.group):
            out = out + tall[b * c:(b + 1) * c]
        return out

    def expand(self, wide):
        if self.group == 1:
            return wide
        zero = jnp.zeros_like(wide)
        return jnp.concatenate([jnp.where(self.lane_block == b, wide, zero) for b in range(self.group)], 0)

    def rmul(self, lhs, wide):
        l_hi, l_lo = _split_bf16(lhs)
        w_hi, w_lo = _split_bf16(wide)
        m = lhs.shape[0]
        top = _dot(jnp.concatenate([l_hi, l_lo], 0), self.expand(w_hi))
        return top[:m] + top[m:] + _dot(l_hi, self.expand(w_lo))

    def lmul(self, wide, rhs):
        w_hi, w_lo = _split_bf16(wide)
        r_hi, r_lo = _split_bf16(rhs)
        n = self.chunk * self.group
        e_hi = self.expand(w_hi)
        top = _dot(jnp.concatenate([e_hi, self.expand(w_lo)], 0), r_hi)
        return top[:n] + top[n:] + _dot(e_hi, r_lo)

    def unit_lower_inverse(self, a_talls):
        c = self.chunk
        negs = [-self.wide(a) for a in a_talls]
        xs = [self.eye + neg for neg in negs]
        powers = [self.rmul(neg, neg) for neg in negs]
        iters = int(math.log2(c)) - 1
        for it in range(iters):
            last = it == iters - 1
            rs = [self.rmul(x if last else jnp.concatenate([x, p], 0), p) for x, p in zip(xs, powers)]
            xs = [x + r[:c] for x, r in zip(xs, rs)]
            if not last:
                powers = [r[c:] for r in rs]
        return xs


def _softplus(x):
    return jnp.maximum(x, 0.0) + jnp.log1p(jnp.exp(-jnp.abs(x)))


def _dn_prep_kernel(chunk, group, hps, precise, qd_ref, kd_ref, vd_ref, ba_ref, hq_ref, hk_ref, hv_ref,
                    wq_ref, wk_ref, wv_ref, alog_ref, dtb_ref,
                    wv_out, wk_out, qdec_out, kend_out, p_out, gend_out):
    head0 = pl.program_id(1) * hps
    tb = qd_ref.shape[0]
    n = chunk * group

    def conv_silu(x_ref, halo_ref, w_ref, cols):
        xp = jnp.concatenate([halo_ref[0, :, cols], x_ref[:, cols]], 0)
        w = w_ref[:, cols]
        y = xp[5:5 + tb] * w[0:1]
        for tap in range(1, CONV_WIDTH):
            y = y + xp[5 + tap:5 + tap + tb] * w[tap:tap + 1]
        return _silu(y)

    ba = ba_ref[...]
    lane = lax.broadcasted_iota(jnp.int32, ba.shape, 1)
    beta_all = _sigmoid(ba)
    g_all = -jnp.exp(alog_ref[...]) * _softplus(ba + dtb_ref[...])

    li = lax.broadcasted_iota(jnp.int32, (n, n), 0)
    mi = lax.broadcasted_iota(jnp.int32, (n, n), 1)
    same = (li // chunk) == (mi // chunk)
    upto = jnp.logical_and(same, li <= mi)
    since = jnp.logical_and(same, li >= mi)
    chunk_end = mi == (li // chunk) * chunk + (chunk - 1)
    blocks = _BlockDiag(chunk, group)
    op_dtype = wk_out.dtype
    slot = p_out.shape[1] // hps

    items, a_mats = [], []
    for hh in range(hps):
        cols = slice(hh * LANES, (hh + 1) * LANES)
        q = conv_silu(qd_ref, hq_ref, wq_ref, cols)
        k = conv_silu(kd_ref, hk_ref, wk_ref, cols)
        v = conv_silu(vd_ref, hv_ref, wv_ref, cols)
        q = q * lax.rsqrt(jnp.sum(q * q, -1, keepdims=True) + EPS) * (DN_KEY_DIM ** -0.5)
        k = k * lax.rsqrt(jnp.sum(k * k, -1, keepdims=True) + EPS)
        beta = jnp.sum(jnp.where(lane == head0 + hh, beta_all, 0.0), -1, keepdims=True)
        g = jnp.sum(jnp.where(lane == head0 + hh + DN_HEADS, g_all, 0.0), -1, keepdims=True)
        for gi in range(tb // n):
            rows = slice(gi * n, (gi + 1) * n)
            qc, kc, bc, gc = q[rows], k[rows], beta[rows], g[rows]
            g_row = jnp.sum(jnp.where(upto, gc, 0.0), 0, keepdims=True)
            g_col = jnp.sum(jnp.where(li == mi, g_row, 0.0), 1, keepdims=True)
            g_end = jnp.sum(jnp.where(chunk_end, g_row, 0.0), 1, keepdims=True)
            decay = jnp.exp(jnp.where(since, g_col - g_row, -jnp.inf))
            qk_kk = _dot_any(jnp.concatenate([qc, kc], 0), kc, precise, _dot_nt)
            a_mats.append(jnp.where(li > mi, bc * decay * qk_kk[n:], 0.0))
            e_g = jnp.exp(g_col)
            qdec_out[rows, cols] = (e_g * qc).astype(op_dtype)
            kend_out[rows, cols] = (jnp.exp(g_end - g_col) * kc).astype(op_dtype)
            p_out[rows, hh * slot:hh * slot + n] = (qk_kk[:n] * decay).astype(op_dtype)
            if slot > n:
                p_out[rows, hh * slot + n:(hh + 1) * slot] = jnp.zeros((n, slot - n), op_dtype)
            items.append((gi, rows, cols, kc, v[rows], bc, e_g, g_end))
    t_invs = blocks.unit_lower_inverse(a_mats)
    for t_inv, (gi, rows, cols, kc, vc, bc, e_g, g_end) in zip(t_invs, items):
        w = blocks.lmul(t_inv, jnp.concatenate([bc * vc, (bc * e_g) * kc], 1))
        wv_out[rows, cols] = w[:, :DN_VAL_DIM]
        wk_out[rows, cols] = w[:, DN_VAL_DIM:].astype(op_dtype)
        for c in range(group):
            last = c * chunk + chunk - 1
            gend_out[gi * group + c, :, cols] = jnp.broadcast_to(jnp.exp(g_end[last:last + 1]), (1, LANES))


def _dn_score_slot(chunk, group):
    return LANES if chunk * group < LANES else max(chunk * group, V7X_MXU_DEPTH)


def _dn_prep(p, halo, w_conv8, alog_row, dtb_row, chunk, group, tb, hps, precise):
    m = p.shape[0]
    op_dtype = F32 if precise else BF16
    slot = _dn_score_slot(chunk, group)
    nh = DN_HEADS // hps
    hw = hps * LANES
    assert P_CONV % hw == 0 and DN_WIDTH % hw == 0
    cq, ck, cv = P_CONV // hw, P_CONV // hw + nh, P_CONV // hw + 2 * nh
    col = lambda base: (lambda i, h: (i, base + h))
    halo_spec = lambda base: pl.BlockSpec((1, 8, hw), lambda i, h: (i, 0, base + h))
    w_spec = lambda base: pl.BlockSpec((8, hw), lambda i, h: (0, base + h))
    const = pl.BlockSpec((1, LANES), lambda i, h: (0, 0))
    head_blk = pl.BlockSpec((tb, hw), lambda i, h: (i, h))
    out_shape = [
        jax.ShapeDtypeStruct((m, DN_WIDTH), F32),
        jax.ShapeDtypeStruct((m, DN_WIDTH), op_dtype),
        jax.ShapeDtypeStruct((m, DN_WIDTH), op_dtype),
        jax.ShapeDtypeStruct((m, DN_WIDTH), op_dtype),
        jax.ShapeDtypeStruct((m, DN_HEADS * slot), op_dtype),
        jax.ShapeDtypeStruct((m // chunk, 1, DN_WIDTH), F32),
    ]
    return pl.pallas_call(
        functools.partial(_dn_prep_kernel, chunk, group, hps, precise),
        grid=(m // tb, nh),
        in_specs=[
            pl.BlockSpec((tb, hw), col(cq)),
            pl.BlockSpec((tb, hw), col(ck)),
            pl.BlockSpec((tb, hw), col(cv)),
            pl.BlockSpec((tb, LANES), lambda i, h: (i, P_BA // LANES)),
            halo_spec(0), halo_spec(nh), halo_spec(2 * nh),
            w_spec(0), w_spec(nh), w_spec(2 * nh),
            const, const,
        ],
        out_specs=[head_blk] * 4 + [pl.BlockSpec((tb, hps * slot), lambda i, h: (i, h)),
                                    pl.BlockSpec((tb // chunk, 1, hw), lambda i, h: (i, 0, h))],
        out_shape=out_shape,
        compiler_params=_cparams("parallel", "parallel"),
        name="dn_prep",
    )(p, p, p, p, halo, halo, halo, w_conv8, w_conv8, w_conv8, alog_row, dtb_row)


def _dn_scan_kernel(chunk, group, n_chunks, wv_ref, wk_ref, qd_ref, ke_ref, p_ref, ge_ref, gate_ref, s0_ref,
                    onorm_ref, od_ref, sout_ref, s_scr, u_scr):
    n = pl.program_id(1)
    precise = wk_ref.dtype == F32
    slot = _dn_score_slot(chunk, group)

    @pl.when(n == 0)
    def _():
        s_scr[...] = s0_ref[0]
        u_scr[...] = jnp.zeros_like(u_scr)

    onorm = onorm_ref[...]
    for c in range(n_chunks):
        rows = slice(c * chunk, (c + 1) * chunk)
        group_rows = slice((c % group) * chunk, (c % group + 1) * chunk)
        for h in range(DN_HEADS):
            cols = slice(h * LANES, (h + 1) * LANES)
            s = s_scr[h]
            if not precise:
                s = s.astype(BF16)
            u = wv_ref[rows, cols] - _dot_any(wk_ref[rows, cols], s, precise)
            if not precise:
                u = u.astype(BF16)
            u_scr[h, group_rows, :] = u
            o = (_dot_any(qd_ref[rows, cols], s, precise)
                 + _dot_any(p_ref[rows, h * slot:(h + 1) * slot], u_scr[h], precise))
            s_scr[h] = ge_ref[c, :, cols] * s_scr[h] + _dot_any(ke_ref[rows, cols], u, precise, _dot_tn, 1)
            gate = gate_ref[rows, cols]
            od_ref[rows, cols] = (_rms(o, onorm) * _silu(gate)).astype(od_ref.dtype)

    @pl.when(n == pl.num_programs(1) - 1)
    def _():
        sout_ref[0] = s_scr[...]


def _dn_scan(prep, p, s0, onorm_row, chunk, group, n_chunks, batch):
    wv, wk, qdec, kend, pm, gend = prep
    m = wv.shape[0]
    assert n_chunks % group == 0
    rows = chunk * n_chunks
    steps = m // batch // rows
    blk = lambda b, n: (b * steps + n, 0)
    wide = pl.BlockSpec((rows, DN_WIDTH), blk)
    state = pl.BlockSpec((1, DN_HEADS, DN_KEY_DIM, DN_VAL_DIM), lambda b, n: (b, 0, 0, 0))
    slot = _dn_score_slot(chunk, group)
    assert pm.shape[1] == DN_HEADS * slot
    return pl.pallas_call(
        functools.partial(_dn_scan_kernel, chunk, group, n_chunks),
        grid=(batch, steps),
        in_specs=[
            wide, wide, wide, wide, pl.BlockSpec((rows, DN_HEADS * slot), blk),
            pl.BlockSpec((n_chunks, 1, DN_WIDTH), lambda b, n: (b * steps + n, 0, 0)),
            pl.BlockSpec((rows, DN_WIDTH), lambda b, n: (b * steps + n, P_GATE // DN_WIDTH)),
            state,
            pl.BlockSpec((1, LANES), lambda b, n: (0, 0)),
        ],
        out_specs=[wide, state],
        out_shape=[jax.ShapeDtypeStruct((m, DN_WIDTH), wk.dtype),
                   jax.ShapeDtypeStruct((batch, DN_HEADS, DN_KEY_DIM, DN_VAL_DIM), F32)],
        scratch_shapes=[pltpu.VMEM((DN_HEADS, DN_KEY_DIM, DN_VAL_DIM), F32),
                        pltpu.VMEM((DN_HEADS, slot, DN_VAL_DIM), wk.dtype)],
        compiler_params=_cparams("parallel", "arbitrary"),
        name="dn_scan",
    )(wv, wk, qdec, kend, pm, gend, p, s0, onorm_row)


def _out_proj_kernel(nw, attn_ref, od_ref, *refs):
    w_refs, (x_ref, g_ref, gate_ref, o_ref) = refs[:nw], refs[nw:]
    y = (_mm(attn_ref[...], tuple(r[0:ATTN_WIDTH, :] for r in w_refs))
         + _mm(od_ref[...], tuple(r[ATTN_WIDTH:, :] for r in w_refs)))
    o_ref[...] = x_ref[...] + gate_ref[...] * _rms(y, g_ref[...])


def _out_proj(attn, od, w, x, gain, gate, tm):
    m, d = x.shape
    row = lambda i: (i, 0)
    return pl.pallas_call(
        functools.partial(_out_proj_kernel, len(w)),
        grid=(m // tm,),
        in_specs=[
            pl.BlockSpec((tm, ATTN_WIDTH), row),
            pl.BlockSpec((tm, DN_WIDTH), row),
        ] + [pl.BlockSpec((ATTN_WIDTH + DN_WIDTH, d), lambda i: (0, 0))] * len(w) + [
            pl.BlockSpec((tm, d), row),
            pl.BlockSpec((1, d), lambda i: (0, 0)),
            _mod_spec(gate.shape[0], tm, d),
        ],
        out_specs=pl.BlockSpec((tm, d), row),
        out_shape=jax.ShapeDtypeStruct((m, d), F32),
        compiler_params=_cparams("parallel"),
        name="out_proj",
    )(attn, od, *w, x, gain, gate)


def _ffn_kernel(nw, x_ref, g_ref, sc_ref, sh_ref, *refs):
    wg_refs, wu_refs, wd_refs = refs[:nw], refs[nw:2 * nw], refs[2 * nw:3 * nw]
    g2_ref, gate_ref, o_ref, h_ref, acc_ref = refs[3 * nw:]
    j = pl.program_id(1)

    @pl.when(j == 0)
    def _():
        h = _rms(x_ref[...], g_ref[...]) * (1.0 + sc_ref[...]) + sh_ref[...]
        h_ref[...] = h.astype(h_ref.dtype)
        acc_ref[...] = jnp.zeros_like(acc_ref)

    h = h_ref[...]
    act = _silu(_mm(h, tuple(r[...] for r in wg_refs))) * _mm(h, tuple(r[...] for r in wu_refs))
    acc_ref[...] += _mm(act, tuple(r[...] for r in wd_refs))

    @pl.when(j == pl.num_programs(1) - 1)
    def _():
        o_ref[...] = x_ref[...] + gate_ref[...] * _rms(acc_ref[...], g2_ref[...])


def _ffn(x, gain, scale, shift, wg, wu, wd, gain2, gate, tm, tf):
    m, d = x.shape
    nw = len(wg)
    f = wg[0].shape[1]
    row = lambda i, j: (i, 0)
    vec = pl.BlockSpec((1, d), lambda i, j: (0, 0))
    return pl.pallas_call(
        functools.partial(_ffn_kernel, nw),
        grid=(m // tm, f // tf),
        in_specs=[
            pl.BlockSpec((tm, d), row), vec,
            _mod_spec(scale.shape[0], tm, d), _mod_spec(shift.shape[0], tm, d),
        ] + [pl.BlockSpec((d, tf), lambda i, j: (0, j))] * (2 * nw)
        + [pl.BlockSpec((tf, d), lambda i, j: (j, 0))] * nw
        + [vec, _mod_spec(gate.shape[0], tm, d)],
        out_specs=pl.BlockSpec((tm, d), row),
        out_shape=jax.ShapeDtypeStruct((m, d), F32),
        scratch_shapes=[pltpu.VMEM((tm, d), BF16 if wg[0].dtype == BF16 and nw == 1 else F32),
                        pltpu.VMEM((tm, d), F32)],
        compiler_params=_cparams("parallel", "arbitrary"),
        name="ffn_dense",
    )(x, gain, scale, shift, *wg, *wu, *wd, gain2, gate)


def _router_kernel(x_ref, g_ref, sc_ref, sh_ref, wr_ref, h_ref, gates_ref, idx_ref, w12_ref):
    h = _rms(x_ref[...], g_ref[...]) * (1.0 + sc_ref[...]) + sh_ref[...]
    h_ref[...] = h
    logits = _dot_x3(h, wr_ref[...])
    lane = lax.broadcasted_iota(jnp.int32, logits.shape, 1).astype(F32)
    logits = jnp.where(lane < N_EXPERTS, logits, -jnp.inf)
    m1 = jnp.max(logits, -1, keepdims=True)
    i1 = jnp.min(jnp.where(logits == m1, lane, float(LANES)), -1, keepdims=True)
    rest = jnp.where(lane == i1, -jnp.inf, logits)
    m2 = jnp.max(rest, -1, keepdims=True)
    i2 = jnp.min(jnp.where(rest == m2, lane, float(LANES)), -1, keepdims=True)
    t = jnp.exp(m2 - m1)
    w1 = 1.0 / (1.0 + t)
    w2 = t / (1.0 + t)
    gates_ref[...] = jnp.where(lane == i1, w1, 0.0) + jnp.where(lane == i2, w2, 0.0)
    idx_ref[...] = jnp.where(lane == 0.0, i1, jnp.where(lane == 1.0, i2, 0.0)).astype(jnp.int32)
    w12_ref[...] = jnp.where(lane == 0.0, w1, jnp.where(lane == 1.0, w2, 0.0))


def _router(x, gain, scale, shift, w_router_pad, tm):
    m, d = x.shape
    row = lambda i: (i, 0)
    vec = pl.BlockSpec((1, d), lambda i: (0, 0))
    small = pl.BlockSpec((tm, LANES), row)
    return pl.pallas_call(
        _router_kernel,
        grid=(m // tm,),
        in_specs=[pl.BlockSpec((tm, d), row), vec,
                  _mod_spec(scale.shape[0], tm, d), _mod_spec(shift.shape[0], tm, d),
                  pl.BlockSpec((d, LANES), lambda i: (0, 0))],
        out_specs=[pl.BlockSpec((tm, d), row), small, small, small],
        out_shape=[jax.ShapeDtypeStruct((m, d), F32), jax.ShapeDtypeStruct((m, LANES), F32),
                   jax.ShapeDtypeStruct((m, LANES), jnp.int32), jax.ShapeDtypeStruct((m, LANES), F32)],
        compiler_params=_cparams("parallel"),
        name="moe_router",
    )(x, gain, scale, shift, w_router_pad)


def _moe_gemm_kernel(nj, te_ref, tot_ref, rt_ref, h_hbm, wg_ref, wu_ref, wd_ref, ys_ref, xs_ref, xb_ref, acc_ref,
                     sems):
    r = pl.program_id(0)
    j = pl.program_id(1)
    tm = xb_ref.shape[0]
    total = tot_ref[0]
    active = r < total
    slot = r % 2
    share = -(-tm // nj)

    def row_copy(tile, t, s):
        return pltpu.make_async_copy(h_hbm.at[pl.ds(rt_ref[tile * tm + t], 1), :],
                                     xs_ref.at[s, pl.ds(t, 1), :], sems.at[s])

    def start_rows(tile, s, lo, hi):
        def body(t, carry):
            row_copy(tile, t, s).start()
            return carry

        lax.fori_loop(lo, hi, body, 0)

    @pl.when(jnp.logical_and(r == 0, j == 0))
    def _():
        start_rows(0, 0, 0, tm)

    @pl.when(jnp.logical_and(active, j == 0))
    def _():
        pltpu.make_async_copy(h_hbm.at[pl.ds(0, tm), :], xs_ref.at[slot], sems.at[slot]).wait()
        xb_ref[...] = xs_ref[slot].astype(BF16)
        acc_ref[...] = jnp.zeros_like(acc_ref)

    @pl.when(active)
    def _():
        more = r + 1 < total
        for u in range(share):
            t = j * share + u

            @pl.when(jnp.logical_and(more, t < tm))
            def _():
                row_copy(jnp.minimum(r + 1, pl.num_programs(0) - 1), jnp.minimum(t, tm - 1), 1 - slot).start()

        xb = xb_ref[...]
        act = (_silu(_dot(xb, wg_ref[0])) * _dot(xb, wu_ref[0])).astype(BF16)
        acc_ref[...] += _dot(act, wd_ref[0])

    @pl.when(j == pl.num_programs(1) - 1)
    def _():
        ys_ref[...] = jnp.where(active, acc_ref[...], 0.0)


def _moe_gemm(tile_expert, total_tiles, row_token, h, wg, wu, wd, tm, tf):
    n_tiles = tile_expert.shape[0]
    d = h.shape[1]
    f = wg.shape[2]
    nj = f // tf

    def w_col(r, j, te, tot, rt):
        return (te[r], 0, jnp.where(r < tot[0], j, nj - 1))

    def w_row(r, j, te, tot, rt):
        return (te[r], jnp.where(r < tot[0], j, nj - 1), 0)

    grid_spec = pltpu.PrefetchScalarGridSpec(
        num_scalar_prefetch=3,
        grid=(n_tiles, nj),
        in_specs=[
            pl.BlockSpec(memory_space=pl.ANY),
            pl.BlockSpec((1, d, tf), w_col),
            pl.BlockSpec((1, d, tf), w_col),
            pl.BlockSpec((1, tf, d), w_row),
        ],
        out_specs=pl.BlockSpec((tm, d), lambda r, j, te, tot, rt: (r, 0)),
        scratch_shapes=[pltpu.VMEM((2, tm, d), F32), pltpu.VMEM((tm, d), BF16), pltpu.VMEM((tm, d), F32),
                        pltpu.SemaphoreType.DMA((2,))],
    )
    return pl.pallas_call(
        functools.partial(_moe_gemm_kernel, nj),
        grid_spec=grid_spec,
        out_shape=jax.ShapeDtypeStruct((n_tiles * tm, d), F32),
        compiler_params=_cparams("arbitrary", "arbitrary", row_dma=True),
        name="moe_gemm",
    )(tile_expert, total_tiles, row_token, h, wg, wu, wd)


def _moe_combine_kernel(dest_ref, x_ref, w12_ref, g_ref, gate_ref, ys_hbm, o_ref, buf_ref, sems):
    i = pl.program_id(0)
    tb = x_ref.shape[0]
    slot = i % 2

    def row_copy(blk, t, k, s):
        src = dest_ref[2 * (blk * tb + t) + k]
        return pltpu.make_async_copy(ys_hbm.at[pl.ds(src, 1), :], buf_ref.at[s, k, pl.ds(t, 1), :], sems.at[s])

    def start_block(blk, s):
        def body(t, carry):
            row_copy(blk, t, 0, s).start()
            row_copy(blk, t, 1, s).start()
            return carry

        lax.fori_loop(0, tb, body, 0, unroll=8)

    @pl.when(i == 0)
    def _():
        start_block(0, 0)

    @pl.when(i + 1 < pl.num_programs(0))
    def _():
        start_block(i + 1, 1 - slot)

    for k in range(2):
        pltpu.make_async_copy(ys_hbm.at[pl.ds(0, tb), :], buf_ref.at[slot, k], sems.at[slot]).wait()
    w12 = w12_ref[...]
    y = w12[:, 0:1] * buf_ref[slot, 0] + w12[:, 1:2] * buf_ref[slot, 1]
    o_ref[...] = x_ref[...] + gate_ref[...] * _rms(y, g_ref[...])


def _moe_combine(dest, x, w12, gain, gate, ys, tb):
    m, d = x.shape
    row = lambda i, dst: (i, 0)
    grid_spec = pltpu.PrefetchScalarGridSpec(
        num_scalar_prefetch=1,
        grid=(m // tb,),
        in_specs=[
            pl.BlockSpec((tb, d), row),
            pl.BlockSpec((tb, LANES), row),
            pl.BlockSpec((1, d), lambda i, dst: (0, 0)),
            pl.BlockSpec((1, d), lambda i, dst: (0, 0)),
            pl.BlockSpec(memory_space=pl.ANY),
        ],
        out_specs=pl.BlockSpec((tb, d), row),
        scratch_shapes=[pltpu.VMEM((2, 2, tb, d), F32), pltpu.SemaphoreType.DMA((2,))],
    )
    return pl.pallas_call(
        _moe_combine_kernel,
        grid_spec=grid_spec,
        out_shape=jax.ShapeDtypeStruct((m, d), F32),
        compiler_params=_cparams("arbitrary", row_dma=True),
        name="moe_combine",
    )(dest, x, w12, gain, gate, ys)


def _route_tables(idx2, tm, n_tiles):
    m = idx2.shape[0]
    n_assign = 2 * m
    assert n_tiles * tm == n_assign + N_EXPERTS * tm
    experts = jnp.arange(N_EXPERTS, dtype=jnp.int32)
    e_flat = idx2.reshape(n_assign)
    onehot = (e_flat[:, None] == experts[None, :]).astype(jnp.int32)
    csum = jnp.cumsum(onehot, 0)
    counts = csum[-1]
    padded = ((counts + tm - 1) // tm) * tm
    pend = jnp.cumsum(padded)
    pstart = pend - padded
    dest = jnp.sum(onehot * (pstart[None, :] + csum - 1), -1).astype(jnp.int32)
    total_tiles = (pend[-1] // tm).astype(jnp.int32).reshape(1)
    tile_expert = jnp.minimum(
        jnp.searchsorted(pend // tm, jnp.arange(n_tiles, dtype=jnp.int32), side="right"), N_EXPERTS - 1
    ).astype(jnp.int32)
    filler_key = jnp.where(jnp.arange(tm, dtype=jnp.int32)[None, :] < (padded - counts)[:, None],
                           experts[:, None], N_EXPERTS).reshape(-1)
    keys = jnp.concatenate([e_flat, filler_key])
    tokens = jnp.concatenate([jnp.arange(n_assign, dtype=jnp.int32) // 2,
                              jnp.zeros((N_EXPERTS * tm,), jnp.int32)])
    _, row_token = lax.sort((keys, tokens), num_keys=1, is_stable=True)
    return tile_expert, total_tiles, row_token, dest


def _moe_dense_kernel(h_ref, gates_ref, wg_ref, wu_ref, wd_ref, x_ref, g_ref, gate_ref, o_ref, acc_ref, tot_ref):
    e = pl.program_id(0)
    j = pl.program_id(1)
    nj = pl.num_programs(1)

    @pl.when(jnp.logical_and(e == 0, j == 0))
    def _():
        tot_ref[...] = jnp.zeros_like(tot_ref)

    @pl.when(j == 0)
    def _():
        acc_ref[...] = jnp.zeros_like(acc_ref)

    h = h_ref[...].astype(BF16)
    act = (_silu(_dot(h, wg_ref[0])) * _dot(h, wu_ref[0])).astype(BF16)
    acc_ref[...] += _dot(act, wd_ref[0])

    @pl.when(j == nj - 1)
    def _():
        gates = gates_ref[...]
        lane = lax.broadcasted_iota(jnp.int32, gates.shape, 1)
        ge = jnp.sum(jnp.where(lane == e, gates, 0.0), -1, keepdims=True)
        tot_ref[...] += ge * acc_ref[...]

    @pl.when(jnp.logical_and(e == pl.num_programs(0) - 1, j == nj - 1))
    def _():
        o_ref[...] = x_ref[...] + gate_ref[...] * _rms(tot_ref[...], g_ref[...])


def _moe_dense(h, gates, wg, wu, wd, x, gain, gate, tf):
    m, d = x.shape
    f = wg.shape[2]
    full = pl.BlockSpec((m, d), lambda e, j: (0, 0))
    return pl.pallas_call(
        _moe_dense_kernel,
        grid=(N_EXPERTS, f // tf),
        in_specs=[
            full,
            pl.BlockSpec((m, LANES), lambda e, j: (0, 0)),
            pl.BlockSpec((1, d, tf), lambda e, j: (e, 0, j)),
            pl.BlockSpec((1, d, tf), lambda e, j: (e, 0, j)),
            pl.BlockSpec((1, tf, d), lambda e, j: (e, j, 0)),
            full,
            pl.BlockSpec((1, d), lambda e, j: (0, 0)),
            full,
        ],
        out_specs=full,
        out_shape=jax.ShapeDtypeStruct((m, d), F32),
        scratch_shapes=[pltpu.VMEM((m, d), F32), pltpu.VMEM((m, d), F32)],
        compiler_params=_cparams("arbitrary", "arbitrary"),
        name="moe_dense",
    )(h, gates, wg, wu, wd, x, gain, gate)


def _rope_tables(pos):
    half = ROPE_DIM // 2
    inv_freq = jnp.power(ROPE_THETA, -2.0 * jnp.arange(half, dtype=F32) / ROPE_DIM)
    ang = pos.astype(F32)[:, None] * inv_freq[None, :]
    cos, sin = jnp.cos(ang), jnp.sin(ang)
    t = pos.shape[0]
    rest = ATTN_HEAD_DIM - ROPE_DIM
    cos_h = jnp.concatenate([cos, cos, jnp.ones((t, rest), F32)], 1)
    sa_h = jnp.concatenate([-sin, jnp.zeros((t, half + rest), F32)], 1)
    sb_h = jnp.concatenate([jnp.zeros((t, half), F32), sin, jnp.zeros((t, rest), F32)], 1)
    rep = LANES // ATTN_HEAD_DIM
    return tuple(jnp.tile(a, (1, rep)) for a in (cos_h, sa_h, sb_h))


def _permute_w_in(w):
    o1 = ATTN_WIDTH
    o2 = o1 + KV_WIDTH
    o3 = o2 + KV_WIDTH
    o4 = o3 + DN_CONV_CH
    o5 = o4 + DN_WIDTH
    parts = [w[:, :o1], w[:, o4:o5], w[:, o3:o4], w[:, o1:o2], w[:, o2:o3], w[:, o5:]]
    used = sum(a.shape[1] for a in parts)
    parts.append(jnp.zeros((w.shape[0], P_WIDTH - used), w.dtype))
    return jnp.concatenate(parts, 1)


def _sample_mask_bias(s, lc):
    q_pos = PAST_LEN + np.arange(s)
    k_pos = np.concatenate([PAST_LEN - lc + np.arange(lc), q_pos])
    q_chunk = q_pos[:, None] // CHUNK
    k_chunk = k_pos[None, :] // CHUNK
    mask = (k_pos[None, :] >= 0) & (k_chunk <= q_chunk) & (k_pos[None, :] >= q_chunk * CHUNK - WINDOW)
    bias = np.where(mask, 0.0, -np.inf).astype(np.float32)
    return jnp.asarray(np.tile(bias, (4, 1)))


def _conv_halo(p, init, tb, seq):
    m = p.shape[0]
    batch = m // seq
    nb = seq // tb
    tails = p.reshape(batch, nb, tb, P_WIDTH)[:, :nb - 1, tb - (CONV_WIDTH - 1):, P_CONV:P_CONV + DN_CONV_CH]
    prev = jnp.concatenate([init[:, None], tails], 1)
    prev = prev.reshape(batch * nb, CONV_WIDTH - 1, DN_CONV_CH)
    return jnp.pad(prev, ((0, 0), (8 - (CONV_WIDTH - 1), 0), (0, 0)))


def _trunk(x, mods, layer_w, rope, past, cfg):
    m = x.shape[0]
    batch, seq = cfg["batch"], cfg["seq"]
    precise = cfg["precise"]
    nw = 2 if precise else 1
    ks, vs, ss, bufs = [], [], [], []
    cos, sa, sb = rope
    for l in range(DEPTH):
        w = layer_w[l]
        sh_a, sc_a, g_a, sh_f, sc_f, g_f = mods[l]
        p = _norm_proj(x, w["gain"][0], sc_a, sh_a, w["w_in"][:nw], cfg["tm_proj"], cfg["tn_proj"])
        if past is None:
            attn, k_new = _attn_prompt(p, w["sinks"], cos, sa, sb, cfg["tb_attn"])
            s0 = jnp.zeros((batch, DN_HEADS, DN_KEY_DIM, DN_VAL_DIM), F32)
            conv_init = jnp.zeros((batch, CONV_WIDTH - 1, DN_CONV_CH), F32)
        else:
            ck = past[0][l].reshape(batch, -1, KV_WIDTH)
            cv = past[1][l].reshape(batch, -1, KV_WIDTH)
            attn, k_new = _attn_sample(p, ck, cv, w["sinks"], cos, sa, sb, cfg["bias"], batch, seq)
            s0 = past[2][l]
            conv_init = past[3][l]
        halo = _conv_halo(p, conv_init, cfg["tb_dn"], seq)
        prep = _dn_prep(p, halo, w["w_conv"], w["alog"], w["dtb"], cfg["chunk"], cfg["group"], cfg["tb_dn"],
                        cfg["dn_heads_per_step"], precise)
        od, s_new = _dn_scan(prep, p, s0, w["onorm"], cfg["chunk"], cfg["group"], cfg["scan_chunks"], batch)
        pick = 1 if precise else 0
        x = _out_proj(attn, od, w["w_out"][pick], x, w["gain"][1], g_a, cfg["tm_out"])
        if l % 2 == 0:
            x = _ffn(x, w["gain"][2], sc_f, sh_f, w["ffn_gate"][pick], w["ffn_up"][pick], w["ffn_down"][pick],
                     w["gain"][3], g_f, cfg["tm_ffn"], cfg["tf_ffn"])
        else:
            h, gates, idx, w12 = _router(x, w["gain"][2], sc_f, sh_f, w["router"], cfg["tm_router"])
            if cfg["routed"]:
                tm = cfg["tm_moe"]
                n_tiles = 2 * m // tm + N_EXPERTS
                tile_expert, total_tiles, row_token, dest = _route_tables(idx[:, :2], tm, n_tiles)
                ys = _moe_gemm(tile_expert, total_tiles, row_token, h, w["moe_gate"], w["moe_up"], w["moe_down"],
                               tm, cfg["tf_moe"])
                x = _moe_combine(dest, x, w12, w["gain"][3], g_f, ys, cfg["tb_combine"])
            else:
                x = _moe_dense(h, gates, w["moe_gate"], w["moe_up"], w["moe_down"], x, w["gain"][3], g_f,
                               cfg["tf_moe"])
        pb = p.reshape(batch, seq, P_WIDTH)
        keep = min(WINDOW, seq) if past is None else seq
        ks.append(k_new.reshape(batch, seq, ATTN_KV_HEADS, ATTN_HEAD_DIM)[:, seq - keep:])
        vs.append(pb[:, seq - keep:, P_V:P_V + KV_WIDTH].reshape(batch, keep, ATTN_KV_HEADS, ATTN_HEAD_DIM))
        ss.append(s_new)
        assert seq >= CONV_WIDTH - 1
        bufs.append(pb[:, seq - (CONV_WIDTH - 1):, P_CONV:P_CONV + DN_CONV_CH])
    return x, jnp.stack(ks), jnp.stack(vs), jnp.stack(ss), jnp.stack(bufs)


def kernel(x_prompt, x_sample, cache_attn_k, cache_attn_v, state_delta, state_conv, c_prompt, c_sample, w_in, w_conv, attn_sinks, dn_a_log, dn_dt_bias, dn_norm, w_out, w_mod, b_mod, norm_gains, ffn_gate, ffn_up, ffn_down, moe_router, moe_gate, moe_up, moe_down):
    bp, tp, d = x_prompt.shape
    bs, ts, _ = x_sample.shape
    assert bp == 1 and d == D_MODEL

    c_all = jnp.concatenate([c_prompt, c_sample, jnp.zeros((16 - bp - bs, d), F32)], 0)
    mod = _modulation(c_all, w_mod, b_mod)
    mods_p, mods_s = [], []
    for l in range(DEPTH):
        six = jnp.split(mod[l], 6, -1)
        mods_p.append([a[0:bp] for a in six])
        mods_s.append([jnp.repeat(a[bp:bp + bs], ts, axis=0) for a in six])

    def pad_lanes(v, at):
        return jnp.zeros((1, LANES), F32).at[0, at:at + v.shape[0]].set(v)

    def both(w):
        return ((w.astype(BF16),), (w,))

    layer_w = []
    for l in range(DEPTH):
        w = {
            "gain": [norm_gains[l, i].reshape(1, d) for i in range(4)],
            "w_in": _split_weight(_permute_w_in(w_in[l])),
            "sinks": attn_sinks[l],
            "w_conv": jnp.pad(w_conv[l], ((0, 8 - CONV_WIDTH), (0, 0))),
            "alog": pad_lanes(dn_a_log[l], DN_HEADS),
            "dtb": pad_lanes(dn_dt_bias[l], DN_HEADS),
            "onorm": dn_norm[l].reshape(1, DN_VAL_DIM),
            "w_out": both(w_out[l]),
        }
        if l % 2 == 0:
            w["ffn_gate"] = both(ffn_gate[l // 2])
            w["ffn_up"] = both(ffn_up[l // 2])
            w["ffn_down"] = both(ffn_down[l // 2])
        else:
            w["router"] = jnp.pad(moe_router[l // 2], ((0, 0), (0, LANES - N_EXPERTS)))
            w["moe_gate"] = moe_gate[l // 2].astype(BF16)
            w["moe_up"] = moe_up[l // 2].astype(BF16)
            w["moe_down"] = moe_down[l // 2].astype(BF16)
        layer_w.append(w)

    cfg_p = dict(batch=bp, seq=tp, precise=False, chunk=CHUNK, group=2, scan_chunks=4, tm_proj=1024, tn_proj=512, tb_attn=512,
                 tb_dn=2048, dn_heads_per_step=1, tm_out=512, tm_ffn=512, tf_ffn=512, tm_router=512, routed=True, tm_moe=512,
                 tf_moe=256, tb_combine=256)
    rope_p = _rope_tables(jnp.arange(tp, dtype=jnp.int32))
    y_p, k_p, v_p, s_p, conv_p = _trunk(x_prompt.reshape(bp * tp, d), mods_p, layer_w, rope_p, None, cfg_p)

    ms = bs * ts
    cfg_s = dict(batch=bs, seq=ts, precise=True, chunk=ts, group=1, scan_chunks=1, tm_proj=ms, tn_proj=512, tb_dn=ts, dn_heads_per_step=DN_HEADS, tm_out=ms,
                 tm_ffn=ms, tf_ffn=512, tm_router=ms, routed=False, tf_moe=1408,
                 bias=_sample_mask_bias(ts, cache_attn_k.shape[2]))
    rope_s = _rope_tables(PAST_LEN + jnp.arange(ts, dtype=jnp.int32))
    past = (cache_attn_k, cache_attn_v, state_delta, state_conv)
    y_s, k_s, v_s, s_s, conv_s = _trunk(x_sample.reshape(ms, d), mods_s, layer_w, rope_s, past, cfg_s)

    return (y_p.reshape(bp, tp, d), y_s.reshape(bs, ts, d), k_p, v_p, s_p, conv_p, k_s, v_s, s_s, conv_s)
```

```python
import functools
import math

import numpy as np
import jax
import jax.numpy as jnp
from jax import lax
from jax.experimental import pallas as pl
from jax.experimental.pallas import tpu as pltpu

D_MODEL = 2048
DEPTH = 2
PAST_LEN = 1024
CHUNK = 64
ATTN_HEADS = 16
ATTN_KV_HEADS = 2
ATTN_HEAD_DIM = 64
ATTN_WIDTH = 1024
KV_WIDTH = 128
WINDOW = 128
ROPE_THETA = 500000.0
ROPE_DIM = 16
DN_HEADS = 8
DN_KEY_DIM = 128
DN_VAL_DIM = 128
DN_WIDTH = 1024
CONV_WIDTH = 4
DN_CONV_CH = 3072
D_FF = 5632
N_EXPERTS = 8
D_FF_EXPERT = 2816
EPS = 1e-6

F32 = jnp.float32
BF16 = jnp.bfloat16
LANES = 128
V7X_MXU_DEPTH = 256
V7X_VMEM_LIMIT = 56 * 1024 * 1024

P_Q = 0
P_GATE = 1024
P_CONV = 2048
P_K = 5120
P_V = 5248
P_BA = 5376
P_WIDTH = 5632


def _cparams(*sem, row_dma=False):
    return pltpu.CompilerParams(dimension_semantics=sem, vmem_limit_bytes=V7X_VMEM_LIMIT,
                                disable_bounds_checks=row_dma)


def _sigmoid(x):
    return 0.5 * jnp.tanh(0.5 * x) + 0.5


def _silu(x):
    return x * _sigmoid(x)


def _rms(x, gain):
    return x * lax.rsqrt(jnp.mean(x * x, -1, keepdims=True) + EPS) * gain


def _dot(a, b):
    return jnp.dot(a, b, preferred_element_type=F32)


def _dot_nt(a, b):
    return lax.dot_general(a, b, (((1,), (1,)), ((), ())), preferred_element_type=F32)


def _dot_tn(a, b):
    return lax.dot_general(a, b, (((0,), (0,)), ((), ())), preferred_element_type=F32)


def _split_bf16(a):
    hi = a.astype(BF16)
    lo = (a - hi.astype(F32)).astype(BF16)
    return hi, lo


def _dot_x3(a, b, dot=_dot, out_axis=0):
    a_hi, a_lo = _split_bf16(a)
    b_hi, b_lo = _split_bf16(b)
    n = a.shape[out_axis]
    top = dot(jnp.concatenate([a_hi, a_lo], out_axis), b_hi)
    return top[:n] + top[n:] + dot(a_hi, b_lo)


def _dot_any(a, b, precise, dot=_dot, out_axis=0):
    if precise:
        return _dot_x3(a, b, dot, out_axis)
    return dot(a.astype(BF16), b.astype(BF16))


def _mm(a, w):
    if len(w) == 1 and w[0].dtype == F32:
        return _dot_x3(a, w[0])
    if len(w) == 1:
        return _dot(a.astype(BF16), w[0])
    a_hi, a_lo = _split_bf16(a)
    n = a.shape[0]
    top = _dot(jnp.concatenate([a_hi, a_lo], 0), w[0])
    return top[:n] + top[n:] + _dot(a_hi, w[1])


def _split_weight(w):
    hi, lo = _split_bf16(w)
    return (hi, lo)


def _mod_spec(rows, tm, d):
    if rows == 1:
        return pl.BlockSpec((1, d), lambda i, *_: (0, 0))
    return pl.BlockSpec((tm, d), lambda i, *_: (i, 0))


def _mod_kernel(c_ref, w_ref, b_ref, o_ref):
    o_ref[0] = _dot_x3(_silu(c_ref[...]), w_ref[0]) + b_ref[0]


def _modulation(c_all, w_mod, b_mod):
    rows = c_all.shape[0]
    n = w_mod.shape[2]
    tn = 1024
    return pl.pallas_call(
        _mod_kernel,
        grid=(DEPTH, n // tn),
        in_specs=[
            pl.BlockSpec((rows, D_MODEL), lambda l, j: (0, 0)),
            pl.BlockSpec((1, D_MODEL, tn), lambda l, j: (l, 0, j)),
            pl.BlockSpec((1, 1, tn), lambda l, j: (l, 0, j)),
        ],
        out_specs=pl.BlockSpec((1, rows, tn), lambda l, j: (l, 0, j)),
        out_shape=jax.ShapeDtypeStruct((DEPTH, rows, n), F32),
        compiler_params=_cparams("parallel", "parallel"),
        name="modulation",
    )(c_all, w_mod, b_mod.reshape(DEPTH, 1, n))


def _norm_proj_kernel(nw, x_ref, g_ref, sc_ref, sh_ref, *refs):
    w_refs, (o_ref, h_ref) = refs[:nw], refs[nw:]

    @pl.when(pl.program_id(1) == 0)
    def _():
        h = _rms(x_ref[...], g_ref[...]) * (1.0 + sc_ref[...]) + sh_ref[...]
        h_ref[...] = h.astype(h_ref.dtype)

    o_ref[...] = _mm(h_ref[...], tuple(r[...] for r in w_refs))


def _norm_proj(x, gain, scale, shift, w, tm, tn):
    m, d = x.shape
    n = w[0].shape[1]
    return pl.pallas_call(
        functools.partial(_norm_proj_kernel, len(w)),
        grid=(m // tm, n // tn),
        in_specs=[
            pl.BlockSpec((tm, d), lambda i, j: (i, 0)),
            pl.BlockSpec((1, d), lambda i, j: (0, 0)),
            _mod_spec(scale.shape[0], tm, d),
            _mod_spec(shift.shape[0], tm, d),
        ] + [pl.BlockSpec((d, tn), lambda i, j: (0, j))] * len(w),
        out_specs=pl.BlockSpec((tm, tn), lambda i, j: (i, j)),
        out_shape=jax.ShapeDtypeStruct((m, n), F32),
        scratch_shapes=[pltpu.VMEM((tm, d), BF16 if len(w) == 1 else F32)],
        compiler_params=_cparams("parallel", "arbitrary"),
        name="norm_proj",
    )(x, gain, scale, shift, *w)


def _rope(x, cos, sa, sb):
    return x * cos + pltpu.roll(x, LANES - 8, 1) * sa + pltpu.roll(x, 8, 1) * sb


def _kv_variants(k, v):
    lo = lax.broadcasted_iota(jnp.int32, k.shape, 1) < ATTN_HEAD_DIM
    kr = pltpu.roll(k, ATTN_HEAD_DIM, 1)
    vr = pltpu.roll(v, ATTN_HEAD_DIM, 1)
    zero = jnp.zeros_like(k)
    k_lo = (jnp.where(lo, k, zero), jnp.where(lo, kr, zero))
    k_hi = (jnp.where(lo, zero, kr), jnp.where(lo, zero, k))
    v_lo = (jnp.where(lo, v, zero), jnp.where(lo, vr, zero))
    v_hi = (jnp.where(lo, zero, vr), jnp.where(lo, zero, v))
    return k_lo, k_hi, v_lo, v_hi


def _sink_softmax(s, sink):
    m = jnp.maximum(jnp.max(s, -1, keepdims=True), sink)
    p = jnp.exp(s - m)
    den = jnp.sum(p, -1, keepdims=True) + jnp.exp(sink - m)
    return p * (1.0 / den)


def _attn_core(qbs, k_los, k_his, v_los, v_his, biases, sinks, precise=False):
    scores = [(_dot_any(qb, k_lo, precise, _dot_nt) + bias, _dot_any(qb, k_hi, precise, _dot_nt) + bias)
              for qb, k_lo, k_hi, bias in zip(qbs, k_los, k_his, biases)]
    probs = [(_sink_softmax(s_even, sink[0]), _sink_softmax(s_odd, sink[1]))
             for (s_even, s_odd), sink in zip(scores, sinks)]
    return [_dot_any(p_even, v_lo, precise) + _dot_any(p_odd, v_hi, precise)
            for (p_even, p_odd), v_lo, v_hi in zip(probs, v_los, v_his)]


def _sink_columns(sink_ref, rows_per_pair):
    n = 4 * rows_per_pair
    pair = lax.broadcasted_iota(jnp.int32, (n, 1), 0) // rows_per_pair
    out = []
    for j in range(ATTN_KV_HEADS):
        cols = []
        for par in range(2):
            col = jnp.zeros((n, 1), F32)
            for a in range(4):
                col = jnp.where(pair == a, sink_ref[8 * j + 2 * a + par], col)
            cols.append(col)
        out.append(cols)
    return out


def _attn_prompt_kernel(sink_ref, q_ref, kv_ref, cos_ref, sa_ref, sb_ref, o_ref, knew_ref,
                        qs_ref, klo_ref, khi_ref, vlo_ref, vhi_ref):
    i = pl.program_id(0)
    tb = q_ref.shape[0]
    bufs = (klo_ref, khi_ref, vlo_ref, vhi_ref)

    @pl.when(i == 0)
    def _():
        for r in bufs:
            r[:, 0:WINDOW, :] = jnp.zeros((ATTN_KV_HEADS, WINDOW, LANES), BF16)

    @pl.when(i > 0)
    def _():
        for r in bufs:
            r[:, 0:WINDOW, :] = r[:, tb:tb + WINDOW, :]

    cos, sa, sb = cos_ref[...], sa_ref[...], sb_ref[...]
    k = _rope(kv_ref[:, 0:LANES], cos, sa, sb)
    knew_ref[...] = k
    variants = _kv_variants(k, kv_ref[:, LANES:2 * LANES])
    for r, var in zip(bufs, variants):
        for j in range(ATTN_KV_HEADS):
            r[j, WINDOW:, :] = var[j].astype(BF16)
    scale = ATTN_HEAD_DIM ** -0.5
    for a in range(ATTN_WIDTH // LANES):
        cols = slice(a * LANES, (a + 1) * LANES)
        qs_ref[:, cols] = (_rope(q_ref[:, cols], cos, sa, sb) * scale).astype(BF16)

    sinks = _sink_columns(sink_ref, CHUNK)
    nk = WINDOW + CHUNK

    per_iter = 4

    def chunks_body(c4, carry):
        entries = []
        for c in range(per_iter):
            r0 = pl.multiple_of((c4 * per_iter + c) * CHUNK, CHUNK)
            kpos = i * tb - WINDOW + r0 + lax.broadcasted_iota(jnp.int32, (1, nk), 1)
            bias = jnp.where(kpos >= 0, 0.0, -jnp.inf).astype(F32)
            entries += [(r0, j, bias) for j in range(ATTN_KV_HEADS)]
        qbs = [jnp.concatenate(
            [qs_ref[pl.ds(r0, CHUNK), (4 * j + a) * LANES:(4 * j + a + 1) * LANES] for a in range(4)], 0)
            for r0, j, _ in entries]
        outs = _attn_core(qbs,
                          [klo_ref[j, pl.ds(r0, nk), :] for r0, j, _ in entries],
                          [khi_ref[j, pl.ds(r0, nk), :] for r0, j, _ in entries],
                          [vlo_ref[j, pl.ds(r0, nk), :] for r0, j, _ in entries],
                          [vhi_ref[j, pl.ds(r0, nk), :] for r0, j, _ in entries],
                          [bias for _, _, bias in entries], [sinks[j] for _, j, _ in entries])
        for (r0, j, _), o in zip(entries, outs):
            for a in range(4):
                o_ref[pl.ds(r0, CHUNK), (4 * j + a) * LANES:(4 * j + a + 1) * LANES] = (
                    o[a * CHUNK:(a + 1) * CHUNK].astype(BF16))
        return carry

    lax.fori_loop(0, tb // (CHUNK * per_iter), chunks_body, 0)


def _attn_prompt(p, sinks, cos, sa, sb, tb):
    t = p.shape[0]
    kv_blk = P_K // (2 * LANES)
    row = lambda i: (i, 0)
    return pl.pallas_call(
        _attn_prompt_kernel,
        grid=(t // tb,),
        in_specs=[
            pl.BlockSpec(memory_space=pltpu.SMEM),
            pl.BlockSpec((tb, ATTN_WIDTH), row),
            pl.BlockSpec((tb, 2 * LANES), lambda i: (i, kv_blk)),
            pl.BlockSpec((tb, LANES), row),
            pl.BlockSpec((tb, LANES), row),
            pl.BlockSpec((tb, LANES), row),
        ],
        out_specs=[pl.BlockSpec((tb, ATTN_WIDTH), row), pl.BlockSpec((tb, LANES), row)],
        out_shape=[jax.ShapeDtypeStruct((t, ATTN_WIDTH), BF16), jax.ShapeDtypeStruct((t, LANES), F32)],
        scratch_shapes=[pltpu.VMEM((tb, ATTN_WIDTH), BF16)]
        + [pltpu.VMEM((ATTN_KV_HEADS, tb + WINDOW, LANES), BF16) for _ in range(4)],
        compiler_params=_cparams("arbitrary"),
        name="attn_prompt",
    )(sinks, p, p, cos, sa, sb)


def _attn_sample_kernel(sink_ref, q_ref, kv_ref, ck_ref, cv_ref, cos_ref, sa_ref, sb_ref, bias_ref,
                        o_ref, knew_ref):
    s = q_ref.shape[0]
    cos, sa, sb = cos_ref[...], sa_ref[...], sb_ref[...]
    k = _rope(kv_ref[:, 0:LANES], cos, sa, sb)
    knew_ref[...] = k
    kk = jnp.concatenate([ck_ref[0], k], 0)
    vv = jnp.concatenate([cv_ref[0], kv_ref[:, LANES:2 * LANES]], 0)
    k_lo, k_hi, v_lo, v_hi = _kv_variants(kk, vv)
    sinks = _sink_columns(sink_ref, s)
    scale = ATTN_HEAD_DIM ** -0.5
    bias = bias_ref[...]
    qbs = [jnp.concatenate(
        [_rope(q_ref[:, (4 * j + a) * LANES:(4 * j + a + 1) * LANES], cos, sa, sb) * scale for a in range(4)], 0)
        for j in range(ATTN_KV_HEADS)]
    outs = _attn_core(qbs, k_lo, k_hi, v_lo, v_hi, [bias] * ATTN_KV_HEADS, sinks, precise=True)
    for j, o in enumerate(outs):
        for a in range(4):
            o_ref[:, (4 * j + a) * LANES:(4 * j + a + 1) * LANES] = o[a * s:(a + 1) * s]


def _attn_sample(p, cache_k, cache_v, sinks, cos, sa, sb, bias, batch, s):
    lc = cache_k.shape[1]
    kv_blk = P_K // (2 * LANES)
    row = lambda b: (b, 0)
    const = lambda b: (0, 0)
    return pl.pallas_call(
        _attn_sample_kernel,
        grid=(batch,),
        in_specs=[
            pl.BlockSpec(memory_space=pltpu.SMEM),
            pl.BlockSpec((s, ATTN_WIDTH), row),
            pl.BlockSpec((s, 2 * LANES), lambda b: (b, kv_blk)),
            pl.BlockSpec((1, lc, LANES), lambda b: (b, 0, 0)),
            pl.BlockSpec((1, lc, LANES), lambda b: (b, 0, 0)),
            pl.BlockSpec((s, LANES), const),
            pl.BlockSpec((s, LANES), const),
            pl.BlockSpec((s, LANES), const),
            pl.BlockSpec((4 * s, lc + s), const),
        ],
        out_specs=[pl.BlockSpec((s, ATTN_WIDTH), row), pl.BlockSpec((s, LANES), row)],
        out_shape=[jax.ShapeDtypeStruct((batch * s, ATTN_WIDTH), F32),
                   jax.ShapeDtypeStruct((batch * s, LANES), F32)],
        compiler_params=_cparams("parallel"),
        name="attn_sample",
    )(sinks, p, p, cache_k, cache_v, cos, sa, sb, bias)


class _BlockDiag:
    def __init__(self, chunk, group):
        self.chunk, self.group = chunk, group
        n = chunk * group
        lane = lax.broadcasted_iota(jnp.int32, (chunk, n), 1)
        self.lane_block = lane // chunk
        self.eye = (lax.broadcasted_iota(jnp.int32, (chunk, n), 0) == lane % chunk).astype(F32)

    def wide(self, tall):
        c = self.chunk
        out = tall[0:c]
        for b in range(1, self.group):
            out = out + tall[b * c:(b + 1) * c]
        return out

    def expand(self, wide):
        if self.group == 1:
            return wide
        zero = jnp.zeros_like(wide)
        return jnp.concatenate([jnp.where(self.lane_block == b, wide, zero) for b in range(self.group)], 0)

    def rmul(self, lhs, wide):
        l_hi, l_lo = _split_bf16(lhs)
        w_hi, w_lo = _split_bf16(wide)
        m = lhs.shape[0]
        top = _dot(jnp.concatenate([l_hi, l_lo], 0), self.expand(w_hi))
        return top[:m] + top[m:] + _dot(l_hi, self.expand(w_lo))

    def lmul(self, wide, rhs):
        w_hi, w_lo = _split_bf16(wide)
        r_hi, r_lo = _split_bf16(rhs)
        n = self.chunk * self.group
        e_hi = self.expand(w_hi)
        top = _dot(jnp.concatenate([e_hi, self.expand(w_lo)], 0), r_hi)
        return top[:n] + top[n:] + _dot(e_hi, r_lo)

    def unit_lower_inverse(self, a_talls):
        c = self.chunk
        negs = [-self.wide(a) for a in a_talls]
        xs = [self.eye + neg for neg in negs]
        powers = [self.rmul(neg, neg) for neg in negs]
        iters = int(math.log2(c)) - 1
        for it in range(iters):
            last = it == iters - 1
            rs = [self.rmul(x if last else jnp.concatenate([x, p], 0), p) for x, p in zip(xs, powers)]
            xs = [x + r[:c] for x, r in zip(xs, rs)]
            if not last:
                powers = [r[c:] for r in rs]
        return xs


def _softplus(x):
    return jnp.maximum(x, 0.0) + jnp.log1p(jnp.exp(-jnp.abs(x)))


def _dn_prep_kernel(chunk, group, hps, precise, qd_ref, kd_ref, vd_ref, ba_ref, hq_ref, hk_ref, hv_ref,
                    wq_ref, wk_ref, wv_ref, alog_ref, dtb_ref,
                    wv_out, wk_out, qdec_out, kend_out, p_out, gend_out):
    head0 = pl.program_id(1) * hps
    tb = qd_ref.shape[0]
    n = chunk * group

    def conv_silu(x_ref, halo_ref, w_ref, cols):
        xp = jnp.concatenate([halo_ref[0, :, cols], x_ref[:, cols]], 0)
        w = w_ref[:, cols]
        y = xp[5:5 + tb] * w[0:1]
        for tap in range(1, CONV_WIDTH):
            y = y + xp[5 + tap:5 + tap + tb] * w[tap:tap + 1]
        return _silu(y)

    ba = ba_ref[...]
    lane = lax.broadcasted_iota(jnp.int32, ba.shape, 1)
    beta_all = _sigmoid(ba)
    g_all = -jnp.exp(alog_ref[...]) * _softplus(ba + dtb_ref[...])

    li = lax.broadcasted_iota(jnp.int32, (n, n), 0)
    mi = lax.broadcasted_iota(jnp.int32, (n, n), 1)
    same = (li // chunk) == (mi // chunk)
    upto = jnp.logical_and(same, li <= mi)
    since = jnp.logical_and(same, li >= mi)
    chunk_end = mi == (li // chunk) * chunk + (chunk - 1)
    blocks = _BlockDiag(chunk, group)
    op_dtype = wk_out.dtype
    slot = p_out.shape[1] // hps

    items, a_mats = [], []
    for hh in range(hps):
        cols = slice(hh * LANES, (hh + 1) * LANES)
        q = conv_silu(qd_ref, hq_ref, wq_ref, cols)
        k = conv_silu(kd_ref, hk_ref, wk_ref, cols)
        v = conv_silu(vd_ref, hv_ref, wv_ref, cols)
        q = q * lax.rsqrt(jnp.sum(q * q, -1, keepdims=True) + EPS) * (DN_KEY_DIM ** -0.5)
        k = k * lax.rsqrt(jnp.sum(k * k, -1, keepdims=True) + EPS)
        beta = jnp.sum(jnp.where(lane == head0 + hh, beta_all, 0.0), -1, keepdims=True)
        g = jnp.sum(jnp.where(lane == head0 + hh + DN_HEADS, g_all, 0.0), -1, keepdims=True)
        for gi in range(tb // n):
            rows = slice(gi * n, (gi + 1) * n)
            qc, kc, bc, gc = q[rows], k[rows], beta[rows], g[rows]
            g_row = jnp.sum(jnp.where(upto, gc, 0.0), 0, keepdims=True)
            g_col = jnp.sum(jnp.where(li == mi, g_row, 0.0), 1, keepdims=True)
            g_end = jnp.sum(jnp.where(chunk_end, g_row, 0.0), 1, keepdims=True)
            decay = jnp.exp(jnp.where(since, g_col - g_row, -jnp.inf))
            qk_kk = _dot_any(jnp.concatenate([qc, kc], 0), kc, precise, _dot_nt)
            a_mats.append(jnp.where(li > mi, bc * decay * qk_kk[n:], 0.0))
            e_g = jnp.exp(g_col)
            qdec_out[rows, cols] = (e_g * qc).astype(op_dtype)
            kend_out[rows, cols] = (jnp.exp(g_end - g_col) * kc).astype(op_dtype)
            p_out[rows, hh * slot:hh * slot + n] = (qk_kk[:n] * decay).astype(op_dtype)
            if slot > n:
                p_out[rows, hh * slot + n:(hh + 1) * slot] = jnp.zeros((n, slot - n), op_dtype)
            items.append((gi, rows, cols, kc, v[rows], bc, e_g, g_end))
    t_invs = blocks.unit_lower_inverse(a_mats)
    for t_inv, (gi, rows, cols, kc, vc, bc, e_g, g_end) in zip(t_invs, items):
        w = blocks.lmul(t_inv, jnp.concatenate([bc * vc, (bc * e_g) * kc], 1))
        wv_out[rows, cols] = w[:, :DN_VAL_DIM]
        wk_out[rows, cols] = w[:, DN_VAL_DIM:].astype(op_dtype)
        for c in range(group):
            last = c * chunk + chunk - 1
            gend_out[gi * group + c, :, cols] = jnp.broadcast_to(jnp.exp(g_end[last:last + 1]), (1, LANES))


def _dn_score_slot(chunk, group):
    return LANES if chunk * group < LANES else max(chunk * group, V7X_MXU_DEPTH)


def _dn_prep(p, halo, w_conv8, alog_row, dtb_row, chunk, group, tb, hps, precise):
    m = p.shape[0]
    op_dtype = F32 if precise else BF16
    slot = _dn_score_slot(chunk, group)
    nh = DN_HEADS // hps
    hw = hps * LANES
    assert P_CONV % hw == 0 and DN_WIDTH % hw == 0
    cq, ck, cv = P_CONV // hw, P_CONV // hw + nh, P_CONV // hw + 2 * nh
    col = lambda base: (lambda i, h: (i, base + h))
    halo_spec = lambda base: pl.BlockSpec((1, 8, hw), lambda i, h: (i, 0, base + h))
    w_spec = lambda base: pl.BlockSpec((8, hw), lambda i, h: (0, base + h))
    const = pl.BlockSpec((1, LANES), lambda i, h: (0, 0))
    head_blk = pl.BlockSpec((tb, hw), lambda i, h: (i, h))
    out_shape = [
        jax.ShapeDtypeStruct((m, DN_WIDTH), F32),
        jax.ShapeDtypeStruct((m, DN_WIDTH), op_dtype),
        jax.ShapeDtypeStruct((m, DN_WIDTH), op_dtype),
        jax.ShapeDtypeStruct((m, DN_WIDTH), op_dtype),
        jax.ShapeDtypeStruct((m, DN_HEADS * slot), op_dtype),
        jax.ShapeDtypeStruct((m // chunk, 1, DN_WIDTH), F32),
    ]
    return pl.pallas_call(
        functools.partial(_dn_prep_kernel, chunk, group, hps, precise),
        grid=(m // tb, nh),
        in_specs=[
            pl.BlockSpec((tb, hw), col(cq)),
            pl.BlockSpec((tb, hw), col(ck)),
            pl.BlockSpec((tb, hw), col(cv)),
            pl.BlockSpec((tb, LANES), lambda i, h: (i, P_BA // LANES)),
            halo_spec(0), halo_spec(nh), halo_spec(2 * nh),
            w_spec(0), w_spec(nh), w_spec(2 * nh),
            const, const,
        ],
        out_specs=[head_blk] * 4 + [pl.BlockSpec((tb, hps * slot), lambda i, h: (i, h)),
                                    pl.BlockSpec((tb // chunk, 1, hw), lambda i, h: (i, 0, h))],
        out_shape=out_shape,
        compiler_params=_cparams("parallel", "parallel"),
        name="dn_prep",
    )(p, p, p, p, halo, halo, halo, w_conv8, w_conv8, w_conv8, alog_row, dtb_row)


def _dn_scan_kernel(chunk, group, n_chunks, wv_ref, wk_ref, qd_ref, ke_ref, p_ref, ge_ref, gate_ref, s0_ref,
                    onorm_ref, od_ref, sout_ref, s_scr, u_scr):
    n = pl.program_id(1)
    precise = wk_ref.dtype == F32
    slot = _dn_score_slot(chunk, group)

    @pl.when(n == 0)
    def _():
        s_scr[...] = s0_ref[0]
        u_scr[...] = jnp.zeros_like(u_scr)

    onorm = onorm_ref[...]
    for c in range(n_chunks):
        rows = slice(c * chunk, (c + 1) * chunk)
        group_rows = slice((c % group) * chunk, (c % group + 1) * chunk)
        for h in range(DN_HEADS):
            cols = slice(h * LANES, (h + 1) * LANES)
            s = s_scr[h]
            if not precise:
                s = s.astype(BF16)
            u = wv_ref[rows, cols] - _dot_any(wk_ref[rows, cols], s, precise)
            if not precise:
                u = u.astype(BF16)
            u_scr[h, group_rows, :] = u
            o = (_dot_any(qd_ref[rows, cols], s, precise)
                 + _dot_any(p_ref[rows, h * slot:(h + 1) * slot], u_scr[h], precise))
            s_scr[h] = ge_ref[c, :, cols] * s_scr[h] + _dot_any(ke_ref[rows, cols], u, precise, _dot_tn, 1)
            gate = gate_ref[rows, cols]
            od_ref[rows, cols] = (_rms(o, onorm) * _silu(gate)).astype(od_ref.dtype)

    @pl.when(n == pl.num_programs(1) - 1)
    def _():
        sout_ref[0] = s_scr[...]


def _dn_scan(prep, p, s0, onorm_row, chunk, group, n_chunks, batch):
    wv, wk, qdec, kend, pm, gend = prep
    m = wv.shape[0]
    assert n_chunks % group == 0
    rows = chunk * n_chunks
    steps = m // batch // rows
    blk = lambda b, n: (b * steps + n, 0)
    wide = pl.BlockSpec((rows, DN_WIDTH), blk)
    state = pl.BlockSpec((1, DN_HEADS, DN_KEY_DIM, DN_VAL_DIM), lambda b, n: (b, 0, 0, 0))
    slot = _dn_score_slot(chunk, group)
    assert pm.shape[1] == DN_HEADS * slot
    return pl.pallas_call(
        functools.partial(_dn_scan_kernel, chunk, group, n_chunks),
        grid=(batch, steps),
        in_specs=[
            wide, wide, wide, wide, pl.BlockSpec((rows, DN_HEADS * slot), blk),
            pl.BlockSpec((n_chunks, 1, DN_WIDTH), lambda b, n: (b * steps + n, 0, 0)),
            pl.BlockSpec((rows, DN_WIDTH), lambda b, n: (b * steps + n, P_GATE // DN_WIDTH)),
            state,
            pl.BlockSpec((1, LANES), lambda b, n: (0, 0)),
        ],
        out_specs=[wide, state],
        out_shape=[jax.ShapeDtypeStruct((m, DN_WIDTH), wk.dtype),
                   jax.ShapeDtypeStruct((batch, DN_HEADS, DN_KEY_DIM, DN_VAL_DIM), F32)],
        scratch_shapes=[pltpu.VMEM((DN_HEADS, DN_KEY_DIM, DN_VAL_DIM), F32),
                        pltpu.VMEM((DN_HEADS, slot, DN_VAL_DIM), wk.dtype)],
        compiler_params=_cparams("parallel", "arbitrary"),
        name="dn_scan",
    )(wv, wk, qdec, kend, pm, gend, p, s0, onorm_row)


def _out_proj_kernel(nw, attn_ref, od_ref, *refs):
    w_refs, (x_ref, g_ref, gate_ref, o_ref) = refs[:nw], refs[nw:]
    y = (_mm(attn_ref[...], tuple(r[0:ATTN_WIDTH, :] for r in w_refs))
         + _mm(od_ref[...], tuple(r[ATTN_WIDTH:, :] for r in w_refs)))
    o_ref[...] = x_ref[...] + gate_ref[...] * _rms(y, g_ref[...])


def _out_proj(attn, od, w, x, gain, gate, tm):
    m, d = x.shape
    row = lambda i: (i, 0)
    return pl.pallas_call(
        functools.partial(_out_proj_kernel, len(w)),
        grid=(m // tm,),
        in_specs=[
            pl.BlockSpec((tm, ATTN_WIDTH), row),
            pl.BlockSpec((tm, DN_WIDTH), row),
        ] + [pl.BlockSpec((ATTN_WIDTH + DN_WIDTH, d), lambda i: (0, 0))] * len(w) + [
            pl.BlockSpec((tm, d), row),
            pl.BlockSpec((1, d), lambda i: (0, 0)),
            _mod_spec(gate.shape[0], tm, d),
        ],
        out_specs=pl.BlockSpec((tm, d), row),
        out_shape=jax.ShapeDtypeStruct((m, d), F32),
        compiler_params=_cparams("parallel"),
        name="out_proj",
    )(attn, od, *w, x, gain, gate)


def _ffn_kernel(nw, x_ref, g_ref, sc_ref, sh_ref, *refs):
    wg_refs, wu_refs, wd_refs = refs[:nw], refs[nw:2 * nw], refs[2 * nw:3 * nw]
    g2_ref, gate_ref, o_ref, h_ref, acc_ref = refs[3 * nw:]
    j = pl.program_id(1)

    @pl.when(j == 0)
    def _():
        h = _rms(x_ref[...], g_ref[...]) * (1.0 + sc_ref[...]) + sh_ref[...]
        h_ref[...] = h.astype(h_ref.dtype)
        acc_ref[...] = jnp.zeros_like(acc_ref)

    h = h_ref[...]
    act = _silu(_mm(h, tuple(r[...] for r in wg_refs))) * _mm(h, tuple(r[...] for r in wu_refs))
    acc_ref[...] += _mm(act, tuple(r[...] for r in wd_refs))

    @pl.when(j == pl.num_programs(1) - 1)
    def _():
        o_ref[...] = x_ref[...] + gate_ref[...] * _rms(acc_ref[...], g2_ref[...])


def _ffn(x, gain, scale, shift, wg, wu, wd, gain2, gate, tm, tf):
    m, d = x.shape
    nw = len(wg)
    f = wg[0].shape[1]
    row = lambda i, j: (i, 0)
    vec = pl.BlockSpec((1, d), lambda i, j: (0, 0))
    return pl.pallas_call(
        functools.partial(_ffn_kernel, nw),
        grid=(m // tm, f // tf),
        in_specs=[
            pl.BlockSpec((tm, d), row), vec,
            _mod_spec(scale.shape[0], tm, d), _mod_spec(shift.shape[0], tm, d),
        ] + [pl.BlockSpec((d, tf), lambda i, j: (0, j))] * (2 * nw)
        + [pl.BlockSpec((tf, d), lambda i, j: (j, 0))] * nw
        + [vec, _mod_spec(gate.shape[0], tm, d)],
        out_specs=pl.BlockSpec((tm, d), row),
        out_shape=jax.ShapeDtypeStruct((m, d), F32),
        scratch_shapes=[pltpu.VMEM((tm, d), BF16 if wg[0].dtype == BF16 and nw == 1 else F32),
                        pltpu.VMEM((tm, d), F32)],
        compiler_params=_cparams("parallel", "arbitrary"),
        name="ffn_dense",
    )(x, gain, scale, shift, *wg, *wu, *wd, gain2, gate)


def _router_kernel(x_ref, g_ref, sc_ref, sh_ref, wr_ref, h_ref, gates_ref, idx_ref, w12_ref):
    h = _rms(x_ref[...], g_ref[...]) * (1.0 + sc_ref[...]) + sh_ref[...]
    h_ref[...] = h
    logits = _dot_x3(h, wr_ref[...])
    lane = lax.broadcasted_iota(jnp.int32, logits.shape, 1).astype(F32)
    logits = jnp.where(lane < N_EXPERTS, logits, -jnp.inf)
    m1 = jnp.max(logits, -1, keepdims=True)
    i1 = jnp.min(jnp.where(logits == m1, lane, float(LANES)), -1, keepdims=True)
    rest = jnp.where(lane == i1, -jnp.inf, logits)
    m2 = jnp.max(rest, -1, keepdims=True)
    i2 = jnp.min(jnp.where(rest == m2, lane, float(LANES)), -1, keepdims=True)
    t = jnp.exp(m2 - m1)
    w1 = 1.0 / (1.0 + t)
    w2 = t / (1.0 + t)
    gates_ref[...] = jnp.where(lane == i1, w1, 0.0) + jnp.where(lane == i2, w2, 0.0)
    idx_ref[...] = jnp.where(lane == 0.0, i1, jnp.where(lane == 1.0, i2, 0.0)).astype(jnp.int32)
    w12_ref[...] = jnp.where(lane == 0.0, w1, jnp.where(lane == 1.0, w2, 0.0))


def _router(x, gain, scale, shift, w_router_pad, tm):
    m, d = x.shape
    row = lambda i: (i, 0)
    vec = pl.BlockSpec((1, d), lambda i: (0, 0))
    small = pl.BlockSpec((tm, LANES), row)
    return pl.pallas_call(
        _router_kernel,
        grid=(m // tm,),
        in_specs=[pl.BlockSpec((tm, d), row), vec,
                  _mod_spec(scale.shape[0], tm, d), _mod_spec(shift.shape[0], tm, d),
                  pl.BlockSpec((d, LANES), lambda i: (0, 0))],
        out_specs=[pl.BlockSpec((tm, d), row), small, small, small],
        out_shape=[jax.ShapeDtypeStruct((m, d), F32), jax.ShapeDtypeStruct((m, LANES), F32),
                   jax.ShapeDtypeStruct((m, LANES), jnp.int32), jax.ShapeDtypeStruct((m, LANES), F32)],
        compiler_params=_cparams("parallel"),
        name="moe_router",
    )(x, gain, scale, shift, w_router_pad)


def _moe_gemm_kernel(nj, te_ref, tot_ref, rt_ref, h_hbm, wg_ref, wu_ref, wd_ref, ys_ref, xs_ref, xb_ref, acc_ref,
                     sems):
    r = pl.program_id(0)
    j = pl.program_id(1)
    tm = xb_ref.shape[0]
    total = tot_ref[0]
    active = r < total
    slot = r % 2
    share = -(-tm // nj)

    def row_copy(tile, t, s):
        return pltpu.make_async_copy(h_hbm.at[pl.ds(rt_ref[tile * tm + t], 1), :],
                                     xs_ref.at[s, pl.ds(t, 1), :], sems.at[s])

    def start_rows(tile, s, lo, hi):
        def body(t, carry):
            row_copy(tile, t, s).start()
            return carry

        lax.fori_loop(lo, hi, body, 0)

    @pl.when(jnp.logical_and(r == 0, j == 0))
    def _():
        start_rows(0, 0, 0, tm)

    @pl.when(jnp.logical_and(active, j == 0))
    def _():
        pltpu.make_async_copy(h_hbm.at[pl.ds(0, tm), :], xs_ref.at[slot], sems.at[slot]).wait()
        xb_ref[...] = xs_ref[slot].astype(BF16)
        acc_ref[...] = jnp.zeros_like(acc_ref)

    @pl.when(active)
    def _():
        more = r + 1 < total
        for u in range(share):
            t = j * share + u

            @pl.when(jnp.logical_and(more, t < tm))
            def _():
                row_copy(jnp.minimum(r + 1, pl.num_programs(0) - 1), jnp.minimum(t, tm - 1), 1 - slot).start()

        xb = xb_ref[...]
        act = (_silu(_dot(xb, wg_ref[0])) * _dot(xb, wu_ref[0])).astype(BF16)
        acc_ref[...] += _dot(act, wd_ref[0])

    @pl.when(j == pl.num_programs(1) - 1)
    def _():
        ys_ref[...] = jnp.where(active, acc_ref[...], 0.0)


def _moe_gemm(tile_expert, total_tiles, row_token, h, wg, wu, wd, tm, tf):
    n_tiles = tile_expert.shape[0]
    d = h.shape[1]
    f = wg.shape[2]
    nj = f // tf

    def w_col(r, j, te, tot, rt):
        return (te[r], 0, jnp.where(r < tot[0], j, nj - 1))

    def w_row(r, j, te, tot, rt):
        return (te[r], jnp.where(r < tot[0], j, nj - 1), 0)

    grid_spec = pltpu.PrefetchScalarGridSpec(
        num_scalar_prefetch=3,
        grid=(n_tiles, nj),
        in_specs=[
            pl.BlockSpec(memory_space=pl.ANY),
            pl.BlockSpec((1, d, tf), w_col),
            pl.BlockSpec((1, d, tf), w_col),
            pl.BlockSpec((1, tf, d), w_row),
        ],
        out_specs=pl.BlockSpec((tm, d), lambda r, j, te, tot, rt: (r, 0)),
        scratch_shapes=[pltpu.VMEM((2, tm, d), F32), pltpu.VMEM((tm, d), BF16), pltpu.VMEM((tm, d), F32),
                        pltpu.SemaphoreType.DMA((2,))],
    )
    return pl.pallas_call(
        functools.partial(_moe_gemm_kernel, nj),
        grid_spec=grid_spec,
        out_shape=jax.ShapeDtypeStruct((n_tiles * tm, d), F32),
        compiler_params=_cparams("arbitrary", "arbitrary", row_dma=True),
        name="moe_gemm",
    )(tile_expert, total_tiles, row_token, h, wg, wu, wd)


def _moe_combine_kernel(dest_ref, x_ref, w12_ref, g_ref, gate_ref, ys_hbm, o_ref, buf_ref, sems):
    i = pl.program_id(0)
    tb = x_ref.shape[0]
    slot = i % 2

    def row_copy(blk, t, k, s):
        src = dest_ref[2 * (blk * tb + t) + k]
        return pltpu.make_async_copy(ys_hbm.at[pl.ds(src, 1), :], buf_ref.at[s, k, pl.ds(t, 1), :], sems.at[s])

    def start_block(blk, s):
        def body(t, carry):
            row_copy(blk, t, 0, s).start()
            row_copy(blk, t, 1, s).start()
            return carry

        lax.fori_loop(0, tb, body, 0, unroll=8)

    @pl.when(i == 0)
    def _():
        start_block(0, 0)

    @pl.when(i + 1 < pl.num_programs(0))
    def _():
        start_block(i + 1, 1 - slot)

    for k in range(2):
        pltpu.make_async_copy(ys_hbm.at[pl.ds(0, tb), :], buf_ref.at[slot, k], sems.at[slot]).wait()
    w12 = w12_ref[...]
    y = w12[:, 0:1] * buf_ref[slot, 0] + w12[:, 1:2] * buf_ref[slot, 1]
    o_ref[...] = x_ref[...] + gate_ref[...] * _rms(y, g_ref[...])


def _moe_combine(dest, x, w12, gain, gate, ys, tb):
    m, d = x.shape
    row = lambda i, dst: (i, 0)
    grid_spec = pltpu.PrefetchScalarGridSpec(
        num_scalar_prefetch=1,
        grid=(m // tb,),
        in_specs=[
            pl.BlockSpec((tb, d), row),
            pl.BlockSpec((tb, LANES), row),
            pl.BlockSpec((1, d), lambda i, dst: (0, 0)),
            pl.BlockSpec((1, d), lambda i, dst: (0, 0)),
            pl.BlockSpec(memory_space=pl.ANY),
        ],
        out_specs=pl.BlockSpec((tb, d), row),
        scratch_shapes=[pltpu.VMEM((2, 2, tb, d), F32), pltpu.SemaphoreType.DMA((2,))],
    )
    return pl.pallas_call(
        _moe_combine_kernel,
        grid_spec=grid_spec,
        out_shape=jax.ShapeDtypeStruct((m, d), F32),
        compiler_params=_cparams("arbitrary", row_dma=True),
        name="moe_combine",
    )(dest, x, w12, gain, gate, ys)


def _route_tables(idx2, tm, n_tiles):
    m = idx2.shape[0]
    n_assign = 2 * m
    assert n_tiles * tm == n_assign + N_EXPERTS * tm
    experts = jnp.arange(N_EXPERTS, dtype=jnp.int32)
    e_flat = idx2.reshape(n_assign)
    onehot = (e_flat[:, None] == experts[None, :]).astype(jnp.int32)
    csum = jnp.cumsum(onehot, 0)
    counts = csum[-1]
    padded = ((counts + tm - 1) // tm) * tm
    pend = jnp.cumsum(padded)
    pstart = pend - padded
    dest = jnp.sum(onehot * (pstart[None, :] + csum - 1), -1).astype(jnp.int32)
    total_tiles = (pend[-1] // tm).astype(jnp.int32).reshape(1)
    tile_expert = jnp.minimum(
        jnp.searchsorted(pend // tm, jnp.arange(n_tiles, dtype=jnp.int32), side="right"), N_EXPERTS - 1
    ).astype(jnp.int32)
    filler_key = jnp.where(jnp.arange(tm, dtype=jnp.int32)[None, :] < (padded - counts)[:, None],
                           experts[:, None], N_EXPERTS).reshape(-1)
    keys = jnp.concatenate([e_flat, filler_key])
    tokens = jnp.concatenate([jnp.arange(n_assign, dtype=jnp.int32) // 2,
                              jnp.zeros((N_EXPERTS * tm,), jnp.int32)])
    _, row_token = lax.sort((keys, tokens), num_keys=1, is_stable=True)
    return tile_expert, total_tiles, row_token, dest


def _moe_dense_kernel(h_ref, gates_ref, wg_ref, wu_ref, wd_ref, x_ref, g_ref, gate_ref, o_ref, acc_ref, tot_ref):
    e = pl.program_id(0)
    j = pl.program_id(1)
    nj = pl.num_programs(1)

    @pl.when(jnp.logical_and(e == 0, j == 0))
    def _():
        tot_ref[...] = jnp.zeros_like(tot_ref)

    @pl.when(j == 0)
    def _():
        acc_ref[...] = jnp.zeros_like(acc_ref)

    h = h_ref[...].astype(BF16)
    act = (_silu(_dot(h, wg_ref[0])) * _dot(h, wu_ref[0])).astype(BF16)
    acc_ref[...] += _dot(act, wd_ref[0])

    @pl.when(j == nj - 1)
    def _():
        gates = gates_ref[...]
        lane = lax.broadcasted_iota(jnp.int32, gates.shape, 1)
        ge = jnp.sum(jnp.where(lane == e, gates, 0.0), -1, keepdims=True)
        tot_ref[...] += ge * acc_ref[...]

    @pl.when(jnp.logical_and(e == pl.num_programs(0) - 1, j == nj - 1))
    def _():
        o_ref[...] = x_ref[...] + gate_ref[...] * _rms(tot_ref[...], g_ref[...])


def _moe_dense(h, gates, wg, wu, wd, x, gain, gate, tf):
    m, d = x.shape
    f = wg.shape[2]
    full = pl.BlockSpec((m, d), lambda e, j: (0, 0))
    return pl.pallas_call(
        _moe_dense_kernel,
        grid=(N_EXPERTS, f // tf),
        in_specs=[
            full,
            pl.BlockSpec((m, LANES), lambda e, j: (0, 0)),
            pl.BlockSpec((1, d, tf), lambda e, j: (e, 0, j)),
            pl.BlockSpec((1, d, tf), lambda e, j: (e, 0, j)),
            pl.BlockSpec((1, tf, d), lambda e, j: (e, j, 0)),
            full,
            pl.BlockSpec((1, d), lambda e, j: (0, 0)),
            full,
        ],
        out_specs=full,
        out_shape=jax.ShapeDtypeStruct((m, d), F32),
        scratch_shapes=[pltpu.VMEM((m, d), F32), pltpu.VMEM((m, d), F32)],
        compiler_params=_cparams("arbitrary", "arbitrary"),
        name="moe_dense",
    )(h, gates, wg, wu, wd, x, gain, gate)


def _rope_tables(pos):
    half = ROPE_DIM // 2
    inv_freq = jnp.power(ROPE_THETA, -2.0 * jnp.arange(half, dtype=F32) / ROPE_DIM)
    ang = pos.astype(F32)[:, None] * inv_freq[None, :]
    cos, sin = jnp.cos(ang), jnp.sin(ang)
    t = pos.shape[0]
    rest = ATTN_HEAD_DIM - ROPE_DIM
    cos_h = jnp.concatenate([cos, cos, jnp.ones((t, rest), F32)], 1)
    sa_h = jnp.concatenate([-sin, jnp.zeros((t, half + rest), F32)], 1)
    sb_h = jnp.concatenate([jnp.zeros((t, half), F32), sin, jnp.zeros((t, rest), F32)], 1)
    rep = LANES // ATTN_HEAD_DIM
    return tuple(jnp.tile(a, (1, rep)) for a in (cos_h, sa_h, sb_h))


def _permute_w_in(w):
    o1 = ATTN_WIDTH
    o2 = o1 + KV_WIDTH
    o3 = o2 + KV_WIDTH
    o4 = o3 + DN_CONV_CH
    o5 = o4 + DN_WIDTH
    parts = [w[:, :o1], w[:, o4:o5], w[:, o3:o4], w[:, o1:o2], w[:, o2:o3], w[:, o5:]]
    used = sum(a.shape[1] for a in parts)
    parts.append(jnp.zeros((w.shape[0], P_WIDTH - used), w.dtype))
    return jnp.concatenate(parts, 1)


def _sample_mask_bias(s, lc):
    q_pos = PAST_LEN + np.arange(s)
    k_pos = np.concatenate([PAST_LEN - lc + np.arange(lc), q_pos])
    q_chunk = q_pos[:, None] // CHUNK
    k_chunk = k_pos[None, :] // CHUNK
    mask = (k_pos[None, :] >= 0) & (k_chunk <= q_chunk) & (k_pos[None, :] >= q_chunk * CHUNK - WINDOW)
    bias = np.where(mask, 0.0, -np.inf).astype(np.float32)
    return jnp.asarray(np.tile(bias, (4, 1)))


def _conv_halo(p, init, tb, seq):
    m = p.shape[0]
    batch = m // seq
    nb = seq // tb
    tails = p.reshape(batch, nb, tb, P_WIDTH)[:, :nb - 1, tb - (CONV_WIDTH - 1):, P_CONV:P_CONV + DN_CONV_CH]
    prev = jnp.concatenate([init[:, None], tails], 1)
    prev = prev.reshape(batch * nb, CONV_WIDTH - 1, DN_CONV_CH)
    return jnp.pad(prev, ((0, 0), (8 - (CONV_WIDTH - 1), 0), (0, 0)))


def _trunk(x, mods, layer_w, rope, past, cfg):
    m = x.shape[0]
    batch, seq = cfg["batch"], cfg["seq"]
    precise = cfg["precise"]
    nw = 2 if precise else 1
    ks, vs, ss, bufs = [], [], [], []
    cos, sa, sb = rope
    for l in range(DEPTH):
        w = layer_w[l]
        sh_a, sc_a, g_a, sh_f, sc_f, g_f = mods[l]
        p = _norm_proj(x, w["gain"][0], sc_a, sh_a, w["w_in"][:nw], cfg["tm_proj"], cfg["tn_proj"])
        if past is None:
            attn, k_new = _attn_prompt(p, w["sinks"], cos, sa, sb, cfg["tb_attn"])
            s0 = jnp.zeros((batch, DN_HEADS, DN_KEY_DIM, DN_VAL_DIM), F32)
            conv_init = jnp.zeros((batch, CONV_WIDTH - 1, DN_CONV_CH), F32)
        else:
            ck = past[0][l].reshape(batch, -1, KV_WIDTH)
            cv = past[1][l].reshape(batch, -1, KV_WIDTH)
            attn, k_new = _attn_sample(p, ck, cv, w["sinks"], cos, sa, sb, cfg["bias"], batch, seq)
            s0 = past[2][l]
            conv_init = past[3][l]
        halo = _conv_halo(p, conv_init, cfg["tb_dn"], seq)
        prep = _dn_prep(p, halo, w["w_conv"], w["alog"], w["dtb"], cfg["chunk"], cfg["group"], cfg["tb_dn"],
                        cfg["dn_heads_per_step"], precise)
        od, s_new = _dn_scan(prep, p, s0, w["onorm"], cfg["chunk"], cfg["group"], cfg["scan_chunks"], batch)
        pick = 1 if precise else 0
        x = _out_proj(attn, od, w["w_out"][pick], x, w["gain"][1], g_a, cfg["tm_out"])
        if l % 2 == 0:
            x = _ffn(x, w["gain"][2], sc_f, sh_f, w["ffn_gate"][pick], w["ffn_up"][pick], w["ffn_down"][pick],
                     w["gain"][3], g_f, cfg["tm_ffn"], cfg["tf_ffn"])
        else:
            h, gates, idx, w12 = _router(x, w["gain"][2], sc_f, sh_f, w["router"], cfg["tm_router"])
            if cfg["routed"]:
                tm = cfg["tm_moe"]
                n_tiles = 2 * m // tm + N_EXPERTS
                tile_expert, total_tiles, row_token, dest = _route_tables(idx[:, :2], tm, n_tiles)
                ys = _moe_gemm(tile_expert, total_tiles, row_token, h, w["moe_gate"], w["moe_up"], w["moe_down"],
                               tm, cfg["tf_moe"])
                x = _moe_combine(dest, x, w12, w["gain"][3], g_f, ys, cfg["tb_combine"])
            else:
                x = _moe_dense(h, gates, w["moe_gate"], w["moe_up"], w["moe_down"], x, w["gain"][3], g_f,
                               cfg["tf_moe"])
        pb = p.reshape(batch, seq, P_WIDTH)
        keep = min(WINDOW, seq) if past is None else seq
        ks.append(k_new.reshape(batch, seq, ATTN_KV_HEADS, ATTN_HEAD_DIM)[:, seq - keep:])
        vs.append(pb[:, seq - keep:, P_V:P_V + KV_WIDTH].reshape(batch, keep, ATTN_KV_HEADS, ATTN_HEAD_DIM))
        ss.append(s_new)
        assert seq >= CONV_WIDTH - 1
        bufs.append(pb[:, seq - (CONV_WIDTH - 1):, P_CONV:P_CONV + DN_CONV_CH])
    return x, jnp.stack(ks), jnp.stack(vs), jnp.stack(ss), jnp.stack(bufs)


def kernel(x_prompt, x_sample, cache_attn_k, cache_attn_v, state_delta, state_conv, c_prompt, c_sample, w_in, w_conv, attn_sinks, dn_a_log, dn_dt_bias, dn_norm, w_out, w_mod, b_mod, norm_gains, ffn_gate, ffn_up, ffn_down, moe_router, moe_gate, moe_up, moe_down):
    bp, tp, d = x_prompt.shape
    bs, ts, _ = x_sample.shape
    assert bp == 1 and d == D_MODEL

    c_all = jnp.concatenate([c_prompt, c_sample, jnp.zeros((16 - bp - bs, d), F32)], 0)
    mod = _modulation(c_all, w_mod, b_mod)
    mods_p, mods_s = [], []
    for l in range(DEPTH):
        six = jnp.split(mod[l], 6, -1)
        mods_p.append([a[0:bp] for a in six])
        mods_s.append([jnp.repeat(a[bp:bp + bs], ts, axis=0) for a in six])

    def pad_lanes(v, at):
        return jnp.zeros((1, LANES), F32).at[0, at:at + v.shape[0]].set(v)

    def both(w):
        return ((w.astype(BF16),), (w,))

    layer_w = []
    for l in range(DEPTH):
        w = {
            "gain": [norm_gains[l, i].reshape(1, d) for i in range(4)],
            "w_in": _split_weight(_permute_w_in(w_in[l])),
            "sinks": attn_sinks[l],
            "w_conv": jnp.pad(w_conv[l], ((0, 8 - CONV_WIDTH), (0, 0))),
            "alog": pad_lanes(dn_a_log[l], DN_HEADS),
            "dtb": pad_lanes(dn_dt_bias[l], DN_HEADS),
            "onorm": dn_norm[l].reshape(1, DN_VAL_DIM),
            "w_out": both(w_out[l]),
        }
        if l % 2 == 0:
            w["ffn_gate"] = both(ffn_gate[l // 2])
            w["ffn_up"] = both(ffn_up[l // 2])
            w["ffn_down"] = both(ffn_down[l // 2])
        else:
            w["router"] = jnp.pad(moe_router[l // 2], ((0, 0), (0, LANES - N_EXPERTS)))
            w["moe_gate"] = moe_gate[l // 2].astype(BF16)
            w["moe_up"] = moe_up[l // 2].astype(BF16)
            w["moe_down"] = moe_down[l // 2].astype(BF16)
        layer_w.append(w)

    cfg_p = dict(batch=bp, seq=tp, precise=False, chunk=CHUNK, group=2, scan_chunks=4, tm_proj=1024, tn_proj=512, tb_attn=512,
                 tb_dn=2048, dn_heads_per_step=1, tm_out=512, tm_ffn=512, tf_ffn=512, tm_router=512, routed=True, tm_moe=512,
                 tf_moe=256, tb_combine=256)
    rope_p = _rope_tables(jnp.arange(tp, dtype=jnp.int32))
    y_p, k_p, v_p, s_p, conv_p = _trunk(x_prompt.reshape(bp * tp, d), mods_p, layer_w, rope_p, None, cfg_p)

    ms = bs * ts
    cfg_s = dict(batch=bs, seq=ts, precise=True, chunk=ts, group=1, scan_chunks=1, tm_proj=ms, tn_proj=512, tb_dn=ts, dn_heads_per_step=DN_HEADS, tm_out=ms,
                 tm_ffn=ms, tf_ffn=512, tm_router=ms, routed=False, tf_moe=1408,
                 bias=_sample_mask_bias(ts, cache_attn_k.shape[2]))
    rope_s = _rope_tables(PAST_LEN + jnp.arange(ts, dtype=jnp.int32))
    past = (cache_attn_k, cache_attn_v, state_delta, state_conv)
    y_s, k_s, v_s, s_s, conv_s = _trunk(x_sample.reshape(ms, d), mods_s, layer_w, rope_s, past, cfg_s)

    return (y_p.reshape(bp, tp, d), y_s.reshape(bs, ts, d), k_p, v_p, s_p, conv_p, k_s, v_s, s_s, conv_s)
```

```python
import functools
import math

import numpy as np
import jax
import jax.numpy as jnp
from jax import lax
from jax.experimental import pallas as pl
from jax.experimental.pallas import tpu as pltpu

D_MODEL = 2048
DEPTH = 2
PAST_LEN = 1024
CHUNK = 64
ATTN_HEADS = 16
ATTN_KV_HEADS = 2
ATTN_HEAD_DIM = 64
ATTN_WIDTH = 1024
KV_WIDTH = 128
WINDOW = 128
ROPE_THETA = 500000.0
ROPE_DIM = 16
DN_HEADS = 8
DN_KEY_DIM = 128
DN_VAL_DIM = 128
DN_WIDTH = 1024
CONV_WIDTH = 4
DN_CONV_CH = 3072
D_FF = 5632
N_EXPERTS = 8
D_FF_EXPERT = 2816
EPS = 1e-6

F32 = jnp.float32
BF16 = jnp.bfloat16
LANES = 128
V7X_MXU_DEPTH = 256
V7X_VMEM_LIMIT = 56 * 1024 * 1024

P_Q = 0
P_GATE = 1024
P_CONV = 2048
P_K = 5120
P_V = 5248
P_BA = 5376
P_WIDTH = 5632


def _cparams(*sem, row_dma=False):
    return pltpu.CompilerParams(dimension_semantics=sem, vmem_limit_bytes=V7X_VMEM_LIMIT,
                                disable_bounds_checks=row_dma)


def _sigmoid(x):
    return 0.5 * jnp.tanh(0.5 * x) + 0.5


def _silu(x):
    return x * _sigmoid(x)


def _rms(x, gain):
    return x * lax.rsqrt(jnp.mean(x * x, -1, keepdims=True) + EPS) * gain


def _dot(a, b):
    return jnp.dot(a, b, preferred_element_type=F32)


def _dot_nt(a, b):
    return lax.dot_general(a, b, (((1,), (1,)), ((), ())), preferred_element_type=F32)


def _dot_tn(a, b):
    return lax.dot_general(a, b, (((0,), (0,)), ((), ())), preferred_element_type=F32)


def _split_bf16(a):
    hi = a.astype(BF16)
    lo = (a - hi.astype(F32)).astype(BF16)
    return hi, lo


def _dot_x3(a, b, dot=_dot, out_axis=0):
    a_hi, a_lo = _split_bf16(a)
    b_hi, b_lo = _split_bf16(b)
    n = a.shape[out_axis]
    top = dot(jnp.concatenate([a_hi, a_lo], out_axis), b_hi)
    return top[:n] + top[n:] + dot(a_hi, b_lo)


def _dot_any(a, b, precise, dot=_dot, out_axis=0):
    if precise:
        return _dot_x3(a, b, dot, out_axis)
    return dot(a.astype(BF16), b.astype(BF16))


def _mm(a, w):
    if len(w) == 1 and w[0].dtype == F32:
        return _dot_x3(a, w[0])
    if len(w) == 1:
        return _dot(a.astype(BF16), w[0])
    a_hi, a_lo = _split_bf16(a)
    n = a.shape[0]
    top = _dot(jnp.concatenate([a_hi, a_lo], 0), w[0])
    return top[:n] + top[n:] + _dot(a_hi, w[1])


def _split_weight(w):
    hi, lo = _split_bf16(w)
    return (hi, lo)


def _mod_spec(rows, tm, d):
    if rows == 1:
        return pl.BlockSpec((1, d), lambda i, *_: (0, 0))
    return pl.BlockSpec((tm, d), lambda i, *_: (i, 0))


def _mod_kernel(c_ref, w_ref, b_ref, o_ref):
    o_ref[0] = _dot_x3(_silu(c_ref[...]), w_ref[0]) + b_ref[0]


def _modulation(c_all, w_mod, b_mod):
    rows = c_all.shape[0]
    n = w_mod.shape[2]
    tn = 1024
    return pl.pallas_call(
        _mod_kernel,
        grid=(DEPTH, n // tn),
        in_specs=[
            pl.BlockSpec((rows, D_MODEL), lambda l, j: (0, 0)),
            pl.BlockSpec((1, D_MODEL, tn), lambda l, j: (l, 0, j)),
            pl.BlockSpec((1, 1, tn), lambda l, j: (l, 0, j)),
        ],
        out_specs=pl.BlockSpec((1, rows, tn), lambda l, j: (l, 0, j)),
        out_shape=jax.ShapeDtypeStruct((DEPTH, rows, n), F32),
        compiler_params=_cparams("parallel", "parallel"),
        name="modulation",
    )(c_all, w_mod, b_mod.reshape(DEPTH, 1, n))


def _norm_proj_kernel(nw, x_ref, g_ref, sc_ref, sh_ref, *refs):
    w_refs, (o_ref, h_ref) = refs[:nw], refs[nw:]

    @pl.when(pl.program_id(1) == 0)
    def _():
        h = _rms(x_ref[...], g_ref[...]) * (1.0 + sc_ref[...]) + sh_ref[...]
        h_ref[...] = h.astype(h_ref.dtype)

    o_ref[...] = _mm(h_ref[...], tuple(r[...] for r in w_refs))


def _norm_proj(x, gain, scale, shift, w, tm, tn):
    m, d = x.shape
    n = w[0].shape[1]
    return pl.pallas_call(
        functools.partial(_norm_proj_kernel, len(w)),
        grid=(m // tm, n // tn),
        in_specs=[
            pl.BlockSpec((tm, d), lambda i, j: (i, 0)),
            pl.BlockSpec((1, d), lambda i, j: (0, 0)),
            _mod_spec(scale.shape[0], tm, d),
            _mod_spec(shift.shape[0], tm, d),
        ] + [pl.BlockSpec((d, tn), lambda i, j: (0, j))] * len(w),
        out_specs=pl.BlockSpec((tm, tn), lambda i, j: (i, j)),
        out_shape=jax.ShapeDtypeStruct((m, n), F32),
        scratch_shapes=[pltpu.VMEM((tm, d), BF16 if len(w) == 1 else F32)],
        compiler_params=_cparams("parallel", "arbitrary"),
        name="norm_proj",
    )(x, gain, scale, shift, *w)


def _rope(x, cos, sa, sb):
    return x * cos + pltpu.roll(x, LANES - 8, 1) * sa + pltpu.roll(x, 8, 1) * sb


def _kv_variants(k, v):
    lo = lax.broadcasted_iota(jnp.int32, k.shape, 1) < ATTN_HEAD_DIM
    kr = pltpu.roll(k, ATTN_HEAD_DIM, 1)
    vr = pltpu.roll(v, ATTN_HEAD_DIM, 1)
    zero = jnp.zeros_like(k)
    k_lo = (jnp.where(lo, k, zero), jnp.where(lo, kr, zero))
    k_hi = (jnp.where(lo, zero, kr), jnp.where(lo, zero, k))
    v_lo = (jnp.where(lo, v, zero), jnp.where(lo, vr, zero))
    v_hi = (jnp.where(lo, zero, vr), jnp.where(lo, zero, v))
    return k_lo, k_hi, v_lo, v_hi


def _sink_softmax(s, sink):
    m = jnp.maximum(jnp.max(s, -1, keepdims=True), sink)
    p = jnp.exp(s - m)
    den = jnp.sum(p, -1, keepdims=True) + jnp.exp(sink - m)
    return p * (1.0 / den)


def _attn_core(qbs, k_los, k_his, v_los, v_his, biases, sinks, precise=False):
    scores = [(_dot_any(qb, k_lo, precise, _dot_nt) + bias, _dot_any(qb, k_hi, precise, _dot_nt) + bias)
              for qb, k_lo, k_hi, bias in zip(qbs, k_los, k_his, biases)]
    probs = [(_sink_softmax(s_even, sink[0]), _sink_softmax(s_odd, sink[1]))
             for (s_even, s_odd), sink in zip(scores, sinks)]
    return [_dot_any(p_even, v_lo, precise) + _dot_any(p_odd, v_hi, precise)
            for (p_even, p_odd), v_lo, v_hi in zip(probs, v_los, v_his)]


def _sink_columns(sink_ref, rows_per_pair):
    n = 4 * rows_per_pair
    pair = lax.broadcasted_iota(jnp.int32, (n, 1), 0) // rows_per_pair
    out = []
    for j in range(ATTN_KV_HEADS):
        cols = []
        for par in range(2):
            col = jnp.zeros((n, 1), F32)
            for a in range(4):
                col = jnp.where(pair == a, sink_ref[8 * j + 2 * a + par], col)
            cols.append(col)
        out.append(cols)
    return out


def _attn_prompt_kernel(sink_ref, q_ref, kv_ref, cos_ref, sa_ref, sb_ref, o_ref, knew_ref,
                        qs_ref, klo_ref, khi_ref, vlo_ref, vhi_ref):
    i = pl.program_id(0)
    tb = q_ref.shape[0]
    bufs = (klo_ref, khi_ref, vlo_ref, vhi_ref)

    @pl.when(i == 0)
    def _():
        for r in bufs:
            r[:, 0:WINDOW, :] = jnp.zeros((ATTN_KV_HEADS, WINDOW, LANES), BF16)

    @pl.when(i > 0)
    def _():
        for r in bufs:
            r[:, 0:WINDOW, :] = r[:, tb:tb + WINDOW, :]

    cos, sa, sb = cos_ref[...], sa_ref[...], sb_ref[...]
    k = _rope(kv_ref[:, 0:LANES], cos, sa, sb)
    knew_ref[...] = k
    variants = _kv_variants(k, kv_ref[:, LANES:2 * LANES])
    for r, var in zip(bufs, variants):
        for j in range(ATTN_KV_HEADS):
            r[j, WINDOW:, :] = var[j].astype(BF16)
    scale = ATTN_HEAD_DIM ** -0.5
    for a in range(ATTN_WIDTH // LANES):
        cols = slice(a * LANES, (a + 1) * LANES)
        qs_ref[:, cols] = (_rope(q_ref[:, cols], cos, sa, sb) * scale).astype(BF16)

    sinks = _sink_columns(sink_ref, CHUNK)
    nk = WINDOW + CHUNK

    per_iter = 8

    def chunks_body(c4, carry):
        entries = []
        for c in range(per_iter):
            r0 = pl.multiple_of((c4 * per_iter + c) * CHUNK, CHUNK)
            kpos = i * tb - WINDOW + r0 + lax.broadcasted_iota(jnp.int32, (1, nk), 1)
            bias = jnp.where(kpos >= 0, 0.0, -jnp.inf).astype(F32)
            entries += [(r0, j, bias) for j in range(ATTN_KV_HEADS)]
        qbs = [jnp.concatenate(
            [qs_ref[pl.ds(r0, CHUNK), (4 * j + a) * LANES:(4 * j + a + 1) * LANES] for a in range(4)], 0)
            for r0, j, _ in entries]
        outs = _attn_core(qbs,
                          [klo_ref[j, pl.ds(r0, nk), :] for r0, j, _ in entries],
                          [khi_ref[j, pl.ds(r0, nk), :] for r0, j, _ in entries],
                          [vlo_ref[j, pl.ds(r0, nk), :] for r0, j, _ in entries],
                          [vhi_ref[j, pl.ds(r0, nk), :] for r0, j, _ in entries],
                          [bias for _, _, bias in entries], [sinks[j] for _, j, _ in entries])
        for (r0, j, _), o in zip(entries, outs):
            for a in range(4):
                o_ref[pl.ds(r0, CHUNK), (4 * j + a) * LANES:(4 * j + a + 1) * LANES] = (
                    o[a * CHUNK:(a + 1) * CHUNK].astype(BF16))
        return carry

    lax.fori_loop(0, tb // (CHUNK * per_iter), chunks_body, 0)


def _attn_prompt(p, sinks, cos, sa, sb, tb):
    t = p.shape[0]
    kv_blk = P_K // (2 * LANES)
    row = lambda i: (i, 0)
    return pl.pallas_call(
        _attn_prompt_kernel,
        grid=(t // tb,),
        in_specs=[
            pl.BlockSpec(memory_space=pltpu.SMEM),
            pl.BlockSpec((tb, ATTN_WIDTH), row),
            pl.BlockSpec((tb, 2 * LANES), lambda i: (i, kv_blk)),
            pl.BlockSpec((tb, LANES), row),
            pl.BlockSpec((tb, LANES), row),
            pl.BlockSpec((tb, LANES), row),
        ],
        out_specs=[pl.BlockSpec((tb, ATTN_WIDTH), row), pl.BlockSpec((tb, LANES), row)],
        out_shape=[jax.ShapeDtypeStruct((t, ATTN_WIDTH), BF16), jax.ShapeDtypeStruct((t, LANES), F32)],
        scratch_shapes=[pltpu.VMEM((tb, ATTN_WIDTH), BF16)]
        + [pltpu.VMEM((ATTN_KV_HEADS, tb + WINDOW, LANES), BF16) for _ in range(4)],
        compiler_params=_cparams("arbitrary"),
        name="attn_prompt",
    )(sinks, p, p, cos, sa, sb)


def _attn_sample_kernel(sink_ref, q_ref, kv_ref, ck_ref, cv_ref, cos_ref, sa_ref, sb_ref, bias_ref,
                        o_ref, knew_ref):
    s = q_ref.shape[0]
    cos, sa, sb = cos_ref[...], sa_ref[...], sb_ref[...]
    k = _rope(kv_ref[:, 0:LANES], cos, sa, sb)
    knew_ref[...] = k
    kk = jnp.concatenate([ck_ref[0], k], 0)
    vv = jnp.concatenate([cv_ref[0], kv_ref[:, LANES:2 * LANES]], 0)
    k_lo, k_hi, v_lo, v_hi = _kv_variants(kk, vv)
    sinks = _sink_columns(sink_ref, s)
    scale = ATTN_HEAD_DIM ** -0.5
    bias = bias_ref[...]
    qbs = [jnp.concatenate(
        [_rope(q_ref[:, (4 * j + a) * LANES:(4 * j + a + 1) * LANES], cos, sa, sb) * scale for a in range(4)], 0)
        for j in range(ATTN_KV_HEADS)]
    outs = _attn_core(qbs, k_lo, k_hi, v_lo, v_hi, [bias] * ATTN_KV_HEADS, sinks, precise=True)
    for j, o in enumerate(outs):
        for a in range(4):
            o_ref[:, (4 * j + a) * LANES:(4 * j + a + 1) * LANES] = o[a * s:(a + 1) * s]


def _attn_sample(p, cache_k, cache_v, sinks, cos, sa, sb, bias, batch, s):
    lc = cache_k.shape[1]
    kv_blk = P_K // (2 * LANES)
    row = lambda b: (b, 0)
    const = lambda b: (0, 0)
    return pl.pallas_call(
        _attn_sample_kernel,
        grid=(batch,),
        in_specs=[
            pl.BlockSpec(memory_space=pltpu.SMEM),
            pl.BlockSpec((s, ATTN_WIDTH), row),
            pl.BlockSpec((s, 2 * LANES), lambda b: (b, kv_blk)),
            pl.BlockSpec((1, lc, LANES), lambda b: (b, 0, 0)),
            pl.BlockSpec((1, lc, LANES), lambda b: (b, 0, 0)),
            pl.BlockSpec((s, LANES), const),
            pl.BlockSpec((s, LANES), const),
            pl.BlockSpec((s, LANES), const),
            pl.BlockSpec((4 * s, lc + s), const),
        ],
        out_specs=[pl.BlockSpec((s, ATTN_WIDTH), row), pl.BlockSpec((s, LANES), row)],
        out_shape=[jax.ShapeDtypeStruct((batch * s, ATTN_WIDTH), F32),
                   jax.ShapeDtypeStruct((batch * s, LANES), F32)],
        compiler_params=_cparams("parallel"),
        name="attn_sample",
    )(sinks, p, p, cache_k, cache_v, cos, sa, sb, bias)


class _BlockDiag:
    def __init__(self, chunk, group):
        self.chunk, self.group = chunk, group
        n = chunk * group
        lane = lax.broadcasted_iota(jnp.int32, (chunk, n), 1)
        self.lane_block = lane // chunk
        self.eye = (lax.broadcasted_iota(jnp.int32, (chunk, n), 0) == lane % chunk).astype(F32)

    def wide(self, tall):
        c = self.chunk
        out = tall[0:c]
        for b in range(1, self.group):
            out = out + tall[b * c:(b + 1) * c]
        return out

    def expand(self, wide):
        if self.group == 1:
            return wide
        zero = jnp.zeros_like(wide)
        return jnp.concatenate([jnp.where(self.lane_block == b, wide, zero) for b in range(self.group)], 0)

    def rmul(self, lhs, wide):
        l_hi, l_lo = _split_bf16(lhs)
        w_hi, w_lo = _split_bf16(wide)
        m = lhs.shape[0]
        top = _dot(jnp.concatenate([l_hi, l_lo], 0), self.expand(w_hi))
        return top[:m] + top[m:] + _dot(l_hi, self.expand(w_lo))

    def lmul(self, wide, rhs):
        w_hi, w_lo = _split_bf16(wide)
        r_hi, r_lo = _split_bf16(rhs)
        n = self.chunk * self.group
        e_hi = self.expand(w_hi)
        top = _dot(jnp.concatenate([e_hi, self.expand(w_lo)], 0), r_hi)
        return top[:n] + top[n:] + _dot(e_hi, r_lo)

    def unit_lower_inverse(self, a_talls):
        c = self.chunk
        negs = [-self.wide(a) for a in a_talls]
        xs = [self.eye + neg for neg in negs]
        powers = [self.rmul(neg, neg) for neg in negs]
        iters = int(math.log2(c)) - 1
        for it in range(iters):
            last = it == iters - 1
            rs = [self.rmul(x if last else jnp.concatenate([x, p], 0), p) for x, p in zip(xs, powers)]
            xs = [x + r[:c] for x, r in zip(xs, rs)]
            if not last:
                powers = [r[c:] for r in rs]
        return xs


def _softplus(x):
    return jnp.maximum(x, 0.0) + jnp.log1p(jnp.exp(-jnp.abs(x)))


def _dn_prep_kernel(chunk, group, hps, precise, qd_ref, kd_ref, vd_ref, ba_ref, hq_ref, hk_ref, hv_ref,
                    wq_ref, wk_ref, wv_ref, alog_ref, dtb_ref,
                    wv_out, wk_out, qdec_out, kend_out, p_out, gend_out):
    head0 = pl.program_id(1) * hps
    tb = qd_ref.shape[0]
    n = chunk * group

    def conv_silu(x_ref, halo_ref, w_ref, cols):
        xp = jnp.concatenate([halo_ref[0, :, cols], x_ref[:, cols]], 0)
        w = w_ref[:, cols]
        y = xp[5:5 + tb] * w[0:1]
        for tap in range(1, CONV_WIDTH):
            y = y + xp[5 + tap:5 + tap + tb] * w[tap:tap + 1]
        return _silu(y)

    ba = ba_ref[...]
    lane = lax.broadcasted_iota(jnp.int32, ba.shape, 1)
    beta_all = _sigmoid(ba)
    g_all = -jnp.exp(alog_ref[...]) * _softplus(ba + dtb_ref[...])

    li = lax.broadcasted_iota(jnp.int32, (n, n), 0)
    mi = lax.broadcasted_iota(jnp.int32, (n, n), 1)
    same = (li // chunk) == (mi // chunk)
    upto = jnp.logical_and(same, li <= mi)
    since = jnp.logical_and(same, li >= mi)
    chunk_end = mi == (li // chunk) * chunk + (chunk - 1)
    blocks = _BlockDiag(chunk, group)
    op_dtype = wk_out.dtype
    slot = p_out.shape[1] // hps

    items, a_mats = [], []
    for hh in range(hps):
        cols = slice(hh * LANES, (hh + 1) * LANES)
        q = conv_silu(qd_ref, hq_ref, wq_ref, cols)
        k = conv_silu(kd_ref, hk_ref, wk_ref, cols)
        v = conv_silu(vd_ref, hv_ref, wv_ref, cols)
        q = q * lax.rsqrt(jnp.sum(q * q, -1, keepdims=True) + EPS) * (DN_KEY_DIM ** -0.5)
        k = k * lax.rsqrt(jnp.sum(k * k, -1, keepdims=True) + EPS)
        beta = jnp.sum(jnp.where(lane == head0 + hh, beta_all, 0.0), -1, keepdims=True)
        g = jnp.sum(jnp.where(lane == head0 + hh + DN_HEADS, g_all, 0.0), -1, keepdims=True)
        for gi in range(tb // n):
            rows = slice(gi * n, (gi + 1) * n)
            qc, kc, bc, gc = q[rows], k[rows], beta[rows], g[rows]
            g_row = jnp.sum(jnp.where(upto, gc, 0.0), 0, keepdims=True)
            g_col = jnp.sum(jnp.where(li == mi, g_row, 0.0), 1, keepdims=True)
            g_end = jnp.sum(jnp.where(chunk_end, g_row, 0.0), 1, keepdims=True)
            decay = jnp.exp(jnp.where(since, g_col - g_row, -jnp.inf))
            qk_kk = _dot_any(jnp.concatenate([qc, kc], 0), kc, precise, _dot_nt)
            a_mats.append(jnp.where(li > mi, bc * decay * qk_kk[n:], 0.0))
            e_g = jnp.exp(g_col)
            qdec_out[rows, cols] = (e_g * qc).astype(op_dtype)
            kend_out[rows, cols] = (jnp.exp(g_end - g_col) * kc).astype(op_dtype)
            p_out[rows, hh * slot:hh * slot + n] = (qk_kk[:n] * decay).astype(op_dtype)
            if slot > n:
                p_out[rows, hh * slot + n:(hh + 1) * slot] = jnp.zeros((n, slot - n), op_dtype)
            items.append((gi, rows, cols, kc, v[rows], bc, e_g, g_end))
    t_invs = blocks.unit_lower_inverse(a_mats)
    for t_inv, (gi, rows, cols, kc, vc, bc, e_g, g_end) in zip(t_invs, items):
        w = blocks.lmul(t_inv, jnp.concatenate([bc * vc, (bc * e_g) * kc], 1))
        wv_out[rows, cols] = w[:, :DN_VAL_DIM]
        wk_out[rows, cols] = w[:, DN_VAL_DIM:].astype(op_dtype)
        for c in range(group):
            last = c * chunk + chunk - 1
            gend_out[gi * group + c, :, cols] = jnp.broadcast_to(jnp.exp(g_end[last:last + 1]), (1, LANES))


def _dn_score_slot(chunk, group):
    return LANES if chunk * group < LANES else max(chunk * group, V7X_MXU_DEPTH)


def _dn_prep(p, halo, w_conv8, alog_row, dtb_row, chunk, group, tb, hps, precise):
    m = p.shape[0]
    op_dtype = F32 if precise else BF16
    slot = _dn_score_slot(chunk, group)
    nh = DN_HEADS // hps
    hw = hps * LANES
    assert P_CONV % hw == 0 and DN_WIDTH % hw == 0
    cq, ck, cv = P_CONV // hw, P_CONV // hw + nh, P_CONV // hw + 2 * nh
    col = lambda base: (lambda i, h: (i, base + h))
    halo_spec = lambda base: pl.BlockSpec((1, 8, hw), lambda i, h: (i, 0, base + h))
    w_spec = lambda base: pl.BlockSpec((8, hw), lambda i, h: (0, base + h))
    const = pl.BlockSpec((1, LANES), lambda i, h: (0, 0))
    head_blk = pl.BlockSpec((tb, hw), lambda i, h: (i, h))
    out_shape = [
        jax.ShapeDtypeStruct((m, DN_WIDTH), F32),
        jax.ShapeDtypeStruct((m, DN_WIDTH), op_dtype),
        jax.ShapeDtypeStruct((m, DN_WIDTH), op_dtype),
        jax.ShapeDtypeStruct((m, DN_WIDTH), op_dtype),
        jax.ShapeDtypeStruct((m, DN_HEADS * slot), op_dtype),
        jax.ShapeDtypeStruct((m // chunk, 1, DN_WIDTH), F32),
    ]
    return pl.pallas_call(
        functools.partial(_dn_prep_kernel, chunk, group, hps, precise),
        grid=(m // tb, nh),
        in_specs=[
            pl.BlockSpec((tb, hw), col(cq)),
            pl.BlockSpec((tb, hw), col(ck)),
            pl.BlockSpec((tb, hw), col(cv)),
            pl.BlockSpec((tb, LANES), lambda i, h: (i, P_BA // LANES)),
            halo_spec(0), halo_spec(nh), halo_spec(2 * nh),
            w_spec(0), w_spec(nh), w_spec(2 * nh),
            const, const,
        ],
        out_specs=[head_blk] * 4 + [pl.BlockSpec((tb, hps * slot), lambda i, h: (i, h)),
                                    pl.BlockSpec((tb // chunk, 1, hw), lambda i, h: (i, 0, h))],
        out_shape=out_shape,
        compiler_params=_cparams("parallel", "parallel"),
        name="dn_prep",
    )(p, p, p, p, halo, halo, halo, w_conv8, w_conv8, w_conv8, alog_row, dtb_row)


def _dn_scan_kernel(chunk, group, n_chunks, wv_ref, wk_ref, qd_ref, ke_ref, p_ref, ge_ref, gate_ref, s0_ref,
                    onorm_ref, od_ref, sout_ref, s_scr, u_scr):
    n = pl.program_id(1)
    precise = wk_ref.dtype == F32
    slot = _dn_score_slot(chunk, group)

    @pl.when(n == 0)
    def _():
        s_scr[...] = s0_ref[0]
        u_scr[...] = jnp.zeros_like(u_scr)

    onorm = onorm_ref[...]
    for c in range(n_chunks):
        rows = slice(c * chunk, (c + 1) * chunk)
        group_rows = slice((c % group) * chunk, (c % group + 1) * chunk)
        for h in range(DN_HEADS):
            cols = slice(h * LANES, (h + 1) * LANES)
            s = s_scr[h]
            if not precise:
                s = s.astype(BF16)
            u = wv_ref[rows, cols] - _dot_any(wk_ref[rows, cols], s, precise)
            if not precise:
                u = u.astype(BF16)
            u_scr[h, group_rows, :] = u
            o = (_dot_any(qd_ref[rows, cols], s, precise)
                 + _dot_any(p_ref[rows, h * slot:(h + 1) * slot], u_scr[h], precise))
            s_scr[h] = ge_ref[c, :, cols] * s_scr[h] + _dot_any(ke_ref[rows, cols], u, precise, _dot_tn, 1)
            gate = gate_ref[rows, cols]
            od_ref[rows, cols] = (_rms(o, onorm) * _silu(gate)).astype(od_ref.dtype)

    @pl.when(n == pl.num_programs(1) - 1)
    def _():
        sout_ref[0] = s_scr[...]


def _dn_scan(prep, p, s0, onorm_row, chunk, group, n_chunks, batch):
    wv, wk, qdec, kend, pm, gend = prep
    m = wv.shape[0]
    assert n_chunks % group == 0
    rows = chunk * n_chunks
    steps = m // batch // rows
    blk = lambda b, n: (b * steps + n, 0)
    wide = pl.BlockSpec((rows, DN_WIDTH), blk)
    state = pl.BlockSpec((1, DN_HEADS, DN_KEY_DIM, DN_VAL_DIM), lambda b, n: (b, 0, 0, 0))
    slot = _dn_score_slot(chunk, group)
    assert pm.shape[1] == DN_HEADS * slot
    return pl.pallas_call(
        functools.partial(_dn_scan_kernel, chunk, group, n_chunks),
        grid=(batch, steps),
        in_specs=[
            wide, wide, wide, wide, pl.BlockSpec((rows, DN_HEADS * slot), blk),
            pl.BlockSpec((n_chunks, 1, DN_WIDTH), lambda b, n: (b * steps + n, 0, 0)),
            pl.BlockSpec((rows, DN_WIDTH), lambda b, n: (b * steps + n, P_GATE // DN_WIDTH)),
            state,
            pl.BlockSpec((1, LANES), lambda b, n: (0, 0)),
        ],
        out_specs=[wide, state],
        out_shape=[jax.ShapeDtypeStruct((m, DN_WIDTH), wk.dtype),
                   jax.ShapeDtypeStruct((batch, DN_HEADS, DN_KEY_DIM, DN_VAL_DIM), F32)],
        scratch_shapes=[pltpu.VMEM((DN_HEADS, DN_KEY_DIM, DN_VAL_DIM), F32),
                        pltpu.VMEM((DN_HEADS, slot, DN_VAL_DIM), wk.dtype)],
        compiler_params=_cparams("parallel", "arbitrary"),
        name="dn_scan",
    )(wv, wk, qdec, kend, pm, gend, p, s0, onorm_row)


def _out_proj_kernel(nw, attn_ref, od_ref, *refs):
    w_refs, (x_ref, g_ref, gate_ref, o_ref) = refs[:nw], refs[nw:]
    y = (_mm(attn_ref[...], tuple(r[0:ATTN_WIDTH, :] for r in w_refs))
         + _mm(od_ref[...], tuple(r[ATTN_WIDTH:, :] for r in w_refs)))
    o_ref[...] = x_ref[...] + gate_ref[...] * _rms(y, g_ref[...])


def _out_proj(attn, od, w, x, gain, gate, tm):
    m, d = x.shape
    row = lambda i: (i, 0)
    return pl.pallas_call(
        functools.partial(_out_proj_kernel, len(w)),
        grid=(m // tm,),
        in_specs=[
            pl.BlockSpec((tm, ATTN_WIDTH), row),
            pl.BlockSpec((tm, DN_WIDTH), row),
        ] + [pl.BlockSpec((ATTN_WIDTH + DN_WIDTH, d), lambda i: (0, 0))] * len(w) + [
            pl.BlockSpec((tm, d), row),
            pl.BlockSpec((1, d), lambda i: (0, 0)),
            _mod_spec(gate.shape[0], tm, d),
        ],
        out_specs=pl.BlockSpec((tm, d), row),
        out_shape=jax.ShapeDtypeStruct((m, d), F32),
        compiler_params=_cparams("parallel"),
        name="out_proj",
    )(attn, od, *w, x, gain, gate)


def _ffn_kernel(nw, x_ref, g_ref, sc_ref, sh_ref, *refs):
    wg_refs, wu_refs, wd_refs = refs[:nw], refs[nw:2 * nw], refs[2 * nw:3 * nw]
    g2_ref, gate_ref, o_ref, h_ref, acc_ref = refs[3 * nw:]
    j = pl.program_id(1)

    @pl.when(j == 0)
    def _():
        h = _rms(x_ref[...], g_ref[...]) * (1.0 + sc_ref[...]) + sh_ref[...]
        h_ref[...] = h.astype(h_ref.dtype)
        acc_ref[...] = jnp.zeros_like(acc_ref)

    h = h_ref[...]
    act = _silu(_mm(h, tuple(r[...] for r in wg_refs))) * _mm(h, tuple(r[...] for r in wu_refs))
    acc_ref[...] += _mm(act, tuple(r[...] for r in wd_refs))

    @pl.when(j == pl.num_programs(1) - 1)
    def _():
        o_ref[...] = x_ref[...] + gate_ref[...] * _rms(acc_ref[...], g2_ref[...])


def _ffn(x, gain, scale, shift, wg, wu, wd, gain2, gate, tm, tf):
    m, d = x.shape
    nw = len(wg)
    f = wg[0].shape[1]
    row = lambda i, j: (i, 0)
    vec = pl.BlockSpec((1, d), lambda i, j: (0, 0))
    return pl.pallas_call(
        functools.partial(_ffn_kernel, nw),
        grid=(m // tm, f // tf),
        in_specs=[
            pl.BlockSpec((tm, d), row), vec,
            _mod_spec(scale.shape[0], tm, d), _mod_spec(shift.shape[0], tm, d),
        ] + [pl.BlockSpec((d, tf), lambda i, j: (0, j))] * (2 * nw)
        + [pl.BlockSpec((tf, d), lambda i, j: (j, 0))] * nw
        + [vec, _mod_spec(gate.shape[0], tm, d)],
        out_specs=pl.BlockSpec((tm, d), row),
        out_shape=jax.ShapeDtypeStruct((m, d), F32),
        scratch_shapes=[pltpu.VMEM((tm, d), BF16 if wg[0].dtype == BF16 and nw == 1 else F32),
                        pltpu.VMEM((tm, d), F32)],
        compiler_params=_cparams("parallel", "arbitrary"),
        name="ffn_dense",
    )(x, gain, scale, shift, *wg, *wu, *wd, gain2, gate)


def _router_kernel(x_ref, g_ref, sc_ref, sh_ref, wr_ref, h_ref, gates_ref, idx_ref, w12_ref):
    h = _rms(x_ref[...], g_ref[...]) * (1.0 + sc_ref[...]) + sh_ref[...]
    h_ref[...] = h
    logits = _dot_x3(h, wr_ref[...])
    lane = lax.broadcasted_iota(jnp.int32, logits.shape, 1).astype(F32)
    logits = jnp.where(lane < N_EXPERTS, logits, -jnp.inf)
    m1 = jnp.max(logits, -1, keepdims=True)
    i1 = jnp.min(jnp.where(logits == m1, lane, float(LANES)), -1, keepdims=True)
    rest = jnp.where(lane == i1, -jnp.inf, logits)
    m2 = jnp.max(rest, -1, keepdims=True)
    i2 = jnp.min(jnp.where(rest == m2, lane, float(LANES)), -1, keepdims=True)
    t = jnp.exp(m2 - m1)
    w1 = 1.0 / (1.0 + t)
    w2 = t / (1.0 + t)
    gates_ref[...] = jnp.where(lane == i1, w1, 0.0) + jnp.where(lane == i2, w2, 0.0)
    idx_ref[...] = jnp.where(lane == 0.0, i1, jnp.where(lane == 1.0, i2, 0.0)).astype(jnp.int32)
    w12_ref[...] = jnp.where(lane == 0.0, w1, jnp.where(lane == 1.0, w2, 0.0))


def _router(x, gain, scale, shift, w_router_pad, tm):
    m, d = x.shape
    row = lambda i: (i, 0)
    vec = pl.BlockSpec((1, d), lambda i: (0, 0))
    small = pl.BlockSpec((tm, LANES), row)
    return pl.pallas_call(
        _router_kernel,
        grid=(m // tm,),
        in_specs=[pl.BlockSpec((tm, d), row), vec,
                  _mod_spec(scale.shape[0], tm, d), _mod_spec(shift.shape[0], tm, d),
                  pl.BlockSpec((d, LANES), lambda i: (0, 0))],
        out_specs=[pl.BlockSpec((tm, d), row), small, small, small],
        out_shape=[jax.ShapeDtypeStruct((m, d), F32), jax.ShapeDtypeStruct((m, LANES), F32),
                   jax.ShapeDtypeStruct((m, LANES), jnp.int32), jax.ShapeDtypeStruct((m, LANES), F32)],
        compiler_params=_cparams("parallel"),
        name="moe_router",
    )(x, gain, scale, shift, w_router_pad)


def _moe_gemm_kernel(nj, te_ref, tot_ref, rt_ref, h_hbm, wg_ref, wu_ref, wd_ref, ys_ref, xs_ref, xb_ref, acc_ref,
                     sems):
    r = pl.program_id(0)
    j = pl.program_id(1)
    tm = xb_ref.shape[0]
    total = tot_ref[0]
    active = r < total
    slot = r % 2
    share = -(-tm // nj)

    def row_copy(tile, t, s):
        return pltpu.make_async_copy(h_hbm.at[pl.ds(rt_ref[tile * tm + t], 1), :],
                                     xs_ref.at[s, pl.ds(t, 1), :], sems.at[s])

    def start_rows(tile, s, lo, hi):
        def body(t, carry):
            row_copy(tile, t, s).start()
            return carry

        lax.fori_loop(lo, hi, body, 0)

    @pl.when(jnp.logical_and(r == 0, j == 0))
    def _():
        start_rows(0, 0, 0, tm)

    @pl.when(jnp.logical_and(active, j == 0))
    def _():
        pltpu.make_async_copy(h_hbm.at[pl.ds(0, tm), :], xs_ref.at[slot], sems.at[slot]).wait()
        xb_ref[...] = xs_ref[slot].astype(BF16)
        acc_ref[...] = jnp.zeros_like(acc_ref)

    @pl.when(active)
    def _():
        more = r + 1 < total
        for u in range(share):
            t = j * share + u

            @pl.when(jnp.logical_and(more, t < tm))
            def _():
                row_copy(jnp.minimum(r + 1, pl.num_programs(0) - 1), jnp.minimum(t, tm - 1), 1 - slot).start()

        xb = xb_ref[...]
        act = (_silu(_dot(xb, wg_ref[0])) * _dot(xb, wu_ref[0])).astype(BF16)
        acc_ref[...] += _dot(act, wd_ref[0])

    @pl.when(j == pl.num_programs(1) - 1)
    def _():
        ys_ref[...] = jnp.where(active, acc_ref[...], 0.0)


def _moe_gemm(tile_expert, total_tiles, row_token, h, wg, wu, wd, tm, tf):
    n_tiles = tile_expert.shape[0]
    d = h.shape[1]
    f = wg.shape[2]
    nj = f // tf

    def w_col(r, j, te, tot, rt):
        return (te[r], 0, jnp.where(r < tot[0], j, nj - 1))

    def w_row(r, j, te, tot, rt):
        return (te[r], jnp.where(r < tot[0], j, nj - 1), 0)

    grid_spec = pltpu.PrefetchScalarGridSpec(
        num_scalar_prefetch=3,
        grid=(n_tiles, nj),
        in_specs=[
            pl.BlockSpec(memory_space=pl.ANY),
            pl.BlockSpec((1, d, tf), w_col),
            pl.BlockSpec((1, d, tf), w_col),
            pl.BlockSpec((1, tf, d), w_row),
        ],
        out_specs=pl.BlockSpec((tm, d), lambda r, j, te, tot, rt: (r, 0)),
        scratch_shapes=[pltpu.VMEM((2, tm, d), F32), pltpu.VMEM((tm, d), BF16), pltpu.VMEM((tm, d), F32),
                        pltpu.SemaphoreType.DMA((2,))],
    )
    return pl.pallas_call(
        functools.partial(_moe_gemm_kernel, nj),
        grid_spec=grid_spec,
        out_shape=jax.ShapeDtypeStruct((n_tiles * tm, d), F32),
        compiler_params=_cparams("arbitrary", "arbitrary", row_dma=True),
        name="moe_gemm",
    )(tile_expert, total_tiles, row_token, h, wg, wu, wd)


def _moe_combine_kernel(dest_ref, x_ref, w12_ref, g_ref, gate_ref, ys_hbm, o_ref, buf_ref, sems):
    i = pl.program_id(0)
    tb = x_ref.shape[0]
    slot = i % 2

    def row_copy(blk, t, k, s):
        src = dest_ref[2 * (blk * tb + t) + k]
        return pltpu.make_async_copy(ys_hbm.at[pl.ds(src, 1), :], buf_ref.at[s, k, pl.ds(t, 1), :], sems.at[s])

    def start_block(blk, s):
        def body(t, carry):
            row_copy(blk, t, 0, s).start()
            row_copy(blk, t, 1, s).start()
            return carry

        lax.fori_loop(0, tb, body, 0, unroll=8)

    @pl.when(i == 0)
    def _():
        start_block(0, 0)

    @pl.when(i + 1 < pl.num_programs(0))
    def _():
        start_block(i + 1, 1 - slot)

    for k in range(2):
        pltpu.make_async_copy(ys_hbm.at[pl.ds(0, tb), :], buf_ref.at[slot, k], sems.at[slot]).wait()
    w12 = w12_ref[...]
    y = w12[:, 0:1] * buf_ref[slot, 0] + w12[:, 1:2] * buf_ref[slot, 1]
    o_ref[...] = x_ref[...] + gate_ref[...] * _rms(y, g_ref[...])


def _moe_combine(dest, x, w12, gain, gate, ys, tb):
    m, d = x.shape
    row = lambda i, dst: (i, 0)
    grid_spec = pltpu.PrefetchScalarGridSpec(
        num_scalar_prefetch=1,
        grid=(m // tb,),
        in_specs=[
            pl.BlockSpec((tb, d), row),
            pl.BlockSpec((tb, LANES), row),
            pl.BlockSpec((1, d), lambda i, dst: (0, 0)),
            pl.BlockSpec((1, d), lambda i, dst: (0, 0)),
            pl.BlockSpec(memory_space=pl.ANY),
        ],
        out_specs=pl.BlockSpec((tb, d), row),
        scratch_shapes=[pltpu.VMEM((2, 2, tb, d), F32), pltpu.SemaphoreType.DMA((2,))],
    )
    return pl.pallas_call(
        _moe_combine_kernel,
        grid_spec=grid_spec,
        out_shape=jax.ShapeDtypeStruct((m, d), F32),
        compiler_params=_cparams("arbitrary", row_dma=True),
        name="moe_combine",
    )(dest, x, w12, gain, gate, ys)


def _route_tables(idx2, tm, n_tiles):
    m = idx2.shape[0]
    n_assign = 2 * m
    assert n_tiles * tm == n_assign + N_EXPERTS * tm
    experts = jnp.arange(N_EXPERTS, dtype=jnp.int32)
    e_flat = idx2.reshape(n_assign)
    onehot = (e_flat[:, None] == experts[None, :]).astype(jnp.int32)
    csum = jnp.cumsum(onehot, 0)
    counts = csum[-1]
    padded = ((counts + tm - 1) // tm) * tm
    pend = jnp.cumsum(padded)
    pstart = pend - padded
    dest = jnp.sum(onehot * (pstart[None, :] + csum - 1), -1).astype(jnp.int32)
    total_tiles = (pend[-1] // tm).astype(jnp.int32).reshape(1)
    tile_expert = jnp.minimum(
        jnp.searchsorted(pend // tm, jnp.arange(n_tiles, dtype=jnp.int32), side="right"), N_EXPERTS - 1
    ).astype(jnp.int32)
    filler_key = jnp.where(jnp.arange(tm, dtype=jnp.int32)[None, :] < (padded - counts)[:, None],
                           experts[:, None], N_EXPERTS).reshape(-1)
    keys = jnp.concatenate([e_flat, filler_key])
    tokens = jnp.concatenate([jnp.arange(n_assign, dtype=jnp.int32) // 2,
                              jnp.zeros((N_EXPERTS * tm,), jnp.int32)])
    _, row_token = lax.sort((keys, tokens), num_keys=1, is_stable=True)
    return tile_expert, total_tiles, row_token, dest


def _moe_dense_kernel(h_ref, gates_ref, wg_ref, wu_ref, wd_ref, x_ref, g_ref, gate_ref, o_ref, acc_ref, tot_ref):
    e = pl.program_id(0)
    j = pl.program_id(1)
    nj = pl.num_programs(1)

    @pl.when(jnp.logical_and(e == 0, j == 0))
    def _():
        tot_ref[...] = jnp.zeros_like(tot_ref)

    @pl.when(j == 0)
    def _():
        acc_ref[...] = jnp.zeros_like(acc_ref)

    h = h_ref[...].astype(BF16)
    act = (_silu(_dot(h, wg_ref[0])) * _dot(h, wu_ref[0])).astype(BF16)
    acc_ref[...] += _dot(act, wd_ref[0])

    @pl.when(j == nj - 1)
    def _():
        gates = gates_ref[...]
        lane = lax.broadcasted_iota(jnp.int32, gates.shape, 1)
        ge = jnp.sum(jnp.where(lane == e, gates, 0.0), -1, keepdims=True)
        tot_ref[...] += ge * acc_ref[...]

    @pl.when(jnp.logical_and(e == pl.num_programs(0) - 1, j == nj - 1))
    def _():
        o_ref[...] = x_ref[...] + gate_ref[...] * _rms(tot_ref[...], g_ref[...])


def _moe_dense(h, gates, wg, wu, wd, x, gain, gate, tf):
    m, d = x.shape
    f = wg.shape[2]
    full = pl.BlockSpec((m, d), lambda e, j: (0, 0))
    return pl.pallas_call(
        _moe_dense_kernel,
        grid=(N_EXPERTS, f // tf),
        in_specs=[
            full,
            pl.BlockSpec((m, LANES), lambda e, j: (0, 0)),
            pl.BlockSpec((1, d, tf), lambda e, j: (e, 0, j)),
            pl.BlockSpec((1, d, tf), lambda e, j: (e, 0, j)),
            pl.BlockSpec((1, tf, d), lambda e, j: (e, j, 0)),
            full,
            pl.BlockSpec((1, d), lambda e, j: (0, 0)),
            full,
        ],
        out_specs=full,
        out_shape=jax.ShapeDtypeStruct((m, d), F32),
        scratch_shapes=[pltpu.VMEM((m, d), F32), pltpu.VMEM((m, d), F32)],
        compiler_params=_cparams("arbitrary", "arbitrary"),
        name="moe_dense",
    )(h, gates, wg, wu, wd, x, gain, gate)


def _rope_tables(pos):
    half = ROPE_DIM // 2
    inv_freq = jnp.power(ROPE_THETA, -2.0 * jnp.arange(half, dtype=F32) / ROPE_DIM)
    ang = pos.astype(F32)[:, None] * inv_freq[None, :]
    cos, sin = jnp.cos(ang), jnp.sin(ang)
    t = pos.shape[0]
    rest = ATTN_HEAD_DIM - ROPE_DIM
    cos_h = jnp.concatenate([cos, cos, jnp.ones((t, rest), F32)], 1)
    sa_h = jnp.concatenate([-sin, jnp.zeros((t, half + rest), F32)], 1)
    sb_h = jnp.concatenate([jnp.zeros((t, half), F32), sin, jnp.zeros((t, rest), F32)], 1)
    rep = LANES // ATTN_HEAD_DIM
    return tuple(jnp.tile(a, (1, rep)) for a in (cos_h, sa_h, sb_h))


def _permute_w_in(w):
    o1 = ATTN_WIDTH
    o2 = o1 + KV_WIDTH
    o3 = o2 + KV_WIDTH
    o4 = o3 + DN_CONV_CH
    o5 = o4 + DN_WIDTH
    parts = [w[:, :o1], w[:, o4:o5], w[:, o3:o4], w[:, o1:o2], w[:, o2:o3], w[:, o5:]]
    used = sum(a.shape[1] for a in parts)
    parts.append(jnp.zeros((w.shape[0], P_WIDTH - used), w.dtype))
    return jnp.concatenate(parts, 1)


def _sample_mask_bias(s, lc):
    q_pos = PAST_LEN + np.arange(s)
    k_pos = np.concatenate([PAST_LEN - lc + np.arange(lc), q_pos])
    q_chunk = q_pos[:, None] // CHUNK
    k_chunk = k_pos[None, :] // CHUNK
    mask = (k_pos[None, :] >= 0) & (k_chunk <= q_chunk) & (k_pos[None, :] >= q_chunk * CHUNK - WINDOW)
    bias = np.where(mask, 0.0, -np.inf).astype(np.float32)
    return jnp.asarray(np.tile(bias, (4, 1)))


def _conv_halo(p, init, tb, seq):
    m = p.shape[0]
    batch = m // seq
    nb = seq // tb
    tails = p.reshape(batch, nb, tb, P_WIDTH)[:, :nb - 1, tb - (CONV_WIDTH - 1):, P_CONV:P_CONV + DN_CONV_CH]
    prev = jnp.concatenate([init[:, None], tails], 1)
    prev = prev.reshape(batch * nb, CONV_WIDTH - 1, DN_CONV_CH)
    return jnp.pad(prev, ((0, 0), (8 - (CONV_WIDTH - 1), 0), (0, 0)))


def _trunk(x, mods, layer_w, rope, past, cfg):
    m = x.shape[0]
    batch, seq = cfg["batch"], cfg["seq"]
    precise = cfg["precise"]
    nw = 2 if precise else 1
    ks, vs, ss, bufs = [], [], [], []
    cos, sa, sb = rope
    for l in range(DEPTH):
        w = layer_w[l]
        sh_a, sc_a, g_a, sh_f, sc_f, g_f = mods[l]
        p = _norm_proj(x, w["gain"][0], sc_a, sh_a, w["w_in"][:nw], cfg["tm_proj"], cfg["tn_proj"])
        if past is None:
            attn, k_new = _attn_prompt(p, w["sinks"], cos, sa, sb, cfg["tb_attn"])
            s0 = jnp.zeros((batch, DN_HEADS, DN_KEY_DIM, DN_VAL_DIM), F32)
            conv_init = jnp.zeros((batch, CONV_WIDTH - 1, DN_CONV_CH), F32)
        else:
            ck = past[0][l].reshape(batch, -1, KV_WIDTH)
            cv = past[1][l].reshape(batch, -1, KV_WIDTH)
            attn, k_new = _attn_sample(p, ck, cv, w["sinks"], cos, sa, sb, cfg["bias"], batch, seq)
            s0 = past[2][l]
            conv_init = past[3][l]
        halo = _conv_halo(p, conv_init, cfg["tb_dn"], seq)
        prep = _dn_prep(p, halo, w["w_conv"], w["alog"], w["dtb"], cfg["chunk"], cfg["group"], cfg["tb_dn"],
                        cfg["dn_heads_per_step"], precise)
        od, s_new = _dn_scan(prep, p, s0, w["onorm"], cfg["chunk"], cfg["group"], cfg["scan_chunks"], batch)
        pick = 1 if precise else 0
        x = _out_proj(attn, od, w["w_out"][pick], x, w["gain"][1], g_a, cfg["tm_out"])
        if l % 2 == 0:
            x = _ffn(x, w["gain"][2], sc_f, sh_f, w["ffn_gate"][pick], w["ffn_up"][pick], w["ffn_down"][pick],
                     w["gain"][3], g_f, cfg["tm_ffn"], cfg["tf_ffn"])
        else:
            h, gates, idx, w12 = _router(x, w["gain"][2], sc_f, sh_f, w["router"], cfg["tm_router"])
            if cfg["routed"]:
                tm = cfg["tm_moe"]
                n_tiles = 2 * m // tm + N_EXPERTS
                tile_expert, total_tiles, row_token, dest = _route_tables(idx[:, :2], tm, n_tiles)
                ys = _moe_gemm(tile_expert, total_tiles, row_token, h, w["moe_gate"], w["moe_up"], w["moe_down"],
                               tm, cfg["tf_moe"])
                x = _moe_combine(dest, x, w12, w["gain"][3], g_f, ys, cfg["tb_combine"])
            else:
                x = _moe_dense(h, gates, w["moe_gate"], w["moe_up"], w["moe_down"], x, w["gain"][3], g_f,
                               cfg["tf_moe"])
        pb = p.reshape(batch, seq, P_WIDTH)
        keep = min(WINDOW, seq) if past is None else seq
        ks.append(k_new.reshape(batch, seq, ATTN_KV_HEADS, ATTN_HEAD_DIM)[:, seq - keep:])
        vs.append(pb[:, seq - keep:, P_V:P_V + KV_WIDTH].reshape(batch, keep, ATTN_KV_HEADS, ATTN_HEAD_DIM))
        ss.append(s_new)
        assert seq >= CONV_WIDTH - 1
        bufs.append(pb[:, seq - (CONV_WIDTH - 1):, P_CONV:P_CONV + DN_CONV_CH])
    return x, jnp.stack(ks), jnp.stack(vs), jnp.stack(ss), jnp.stack(bufs)


def kernel(x_prompt, x_sample, cache_attn_k, cache_attn_v, state_delta, state_conv, c_prompt, c_sample, w_in, w_conv, attn_sinks, dn_a_log, dn_dt_bias, dn_norm, w_out, w_mod, b_mod, norm_gains, ffn_gate, ffn_up, ffn_down, moe_router, moe_gate, moe_up, moe_down):
    bp, tp, d = x_prompt.shape
    bs, ts, _ = x_sample.shape
    assert bp == 1 and d == D_MODEL

    c_all = jnp.concatenate([c_prompt, c_sample, jnp.zeros((16 - bp - bs, d), F32)], 0)
    mod = _modulation(c_all, w_mod, b_mod)
    mods_p, mods_s = [], []
    for l in range(DEPTH):
        six = jnp.split(mod[l], 6, -1)
        mods_p.append([a[0:bp] for a in six])
        mods_s.append([jnp.repeat(a[bp:bp + bs], ts, axis=0) for a in six])

    def pad_lanes(v, at):
        return jnp.zeros((1, LANES), F32).at[0, at:at + v.shape[0]].set(v)

    def both(w):
        return ((w.astype(BF16),), (w,))

    layer_w = []
    for l in range(DEPTH):
        w = {
            "gain": [norm_gains[l, i].reshape(1, d) for i in range(4)],
            "w_in": _split_weight(_permute_w_in(w_in[l])),
            "sinks": attn_sinks[l],
            "w_conv": jnp.pad(w_conv[l], ((0, 8 - CONV_WIDTH), (0, 0))),
            "alog": pad_lanes(dn_a_log[l], DN_HEADS),
            "dtb": pad_lanes(dn_dt_bias[l], DN_HEADS),
            "onorm": dn_norm[l].reshape(1, DN_VAL_DIM),
            "w_out": both(w_out[l]),
        }
        if l % 2 == 0:
            w["ffn_gate"] = both(ffn_gate[l // 2])
            w["ffn_up"] = both(ffn_up[l // 2])
            w["ffn_down"] = both(ffn_down[l // 2])
        else:
            w["router"] = jnp.pad(moe_router[l // 2], ((0, 0), (0, LANES - N_EXPERTS)))
            w["moe_gate"] = moe_gate[l // 2].astype(BF16)
            w["moe_up"] = moe_up[l // 2].astype(BF16)
            w["moe_down"] = moe_down[l // 2].astype(BF16)
        layer_w.append(w)

    cfg_p = dict(batch=bp, seq=tp, precise=False, chunk=CHUNK, group=2, scan_chunks=4, tm_proj=1024, tn_proj=512, tb_attn=512,
                 tb_dn=2048, dn_heads_per_step=1, tm_out=512, tm_ffn=512, tf_ffn=512, tm_router=512, routed=True, tm_moe=512,
                 tf_moe=256, tb_combine=256)
    rope_p = _rope_tables(jnp.arange(tp, dtype=jnp.int32))
    y_p, k_p, v_p, s_p, conv_p = _trunk(x_prompt.reshape(bp * tp, d), mods_p, layer_w, rope_p, None, cfg_p)

    ms = bs * ts
    cfg_s = dict(batch=bs, seq=ts, precise=True, chunk=ts, group=1, scan_chunks=1, tm_proj=ms, tn_proj=512, tb_dn=ts, dn_heads_per_step=DN_HEADS, tm_out=ms,
                 tm_ffn=ms, tf_ffn=512, tm_router=ms, routed=False, tf_moe=1408,
                 bias=_sample_mask_bias(ts, cache_attn_k.shape[2]))
    rope_s = _rope_tables(PAST_LEN + jnp.arange(ts, dtype=jnp.int32))
    past = (cache_attn_k, cache_attn_v, state_delta, state_conv)
    y_s, k_s, v_s, s_s, conv_s = _trunk(x_sample.reshape(ms, d), mods_s, layer_w, rope_s, past, cfg_s)

    return (y_p.reshape(bp, tp, d), y_s.reshape(bs, ts, d), k_p, v_p, s_p, conv_p, k_s, v_s, s_s, conv_s)
```
